```python
import math
import jax, jax.numpy as jnp
from jax import lax
import numpy as np


D_MODEL = 1024
BATCH = 8
SEQ = 4096
DEPTH = 2

LRU_WIDTH = 1024
LRU_BLOCKS = 8
LRU_C = 8.0
CONV_WIDTH = 4
ML_HEADS = 4
ML_DQK = 64
ML_DV = 128
GD_HEADS = 8
GD_DK = 64
GD_DV = 64
CHUNK = 64
N_EXPERTS = 32
TOP_K = 4
D_FF = 1024
SWIGLU_LIMIT = 7.0
SWIGLU_ALPHA = 1.702
MOE_BLOCK = 256
N_BRANCH = 3
DN_ALPHA = (2.0 * DEPTH) ** 0.25
DN_BETA = (8.0 * DEPTH) ** -0.25
LN_EPS = 1e-5
RMS_EPS = 1e-6

ML_QK_W = ML_HEADS * ML_DQK
ML_V_W = ML_HEADS * ML_DV
GD_K_W = GD_HEADS * GD_DK
GD_V_W = GD_HEADS * GD_DV
IN_SPLITS = (LRU_WIDTH, LRU_WIDTH, ML_QK_W, ML_QK_W, ML_V_W, ML_HEADS, ML_HEADS, ML_V_W,
             GD_K_W, GD_K_W, GD_V_W, GD_HEADS, GD_HEADS, GD_V_W, N_BRANCH * D_MODEL)
D_IN = sum(IN_SPLITS)
SPLIT_POINTS = [int(c) for c in np.cumsum(IN_SPLITS)[:-1]]

kernel_name = 'hybrid_lru_mlstm_gdn_moe_deepnorm'


def layer_norm(x, g, b):
    xf = x.astype(jnp.float32)
    mu = jnp.mean(xf, -1, keepdims=True)
    xc = xf - mu
    var = jnp.mean(xc * xc, -1, keepdims=True)
    return (xc * lax.rsqrt(var + LN_EPS) * g + b).astype(x.dtype)


def rms_norm(x, g):
    xf = x.astype(jnp.float32)
    return xf * lax.rsqrt(jnp.mean(xf * xf, -1, keepdims=True) + RMS_EPS) * g


def l2_norm(x):
    xf = x.astype(jnp.float32)
    return xf * lax.rsqrt(jnp.sum(xf * xf, -1, keepdims=True) + RMS_EPS)


def causal_conv(x, w):
    S = x.shape[1]
    W = w.shape[0]
    xp = jnp.pad(x, ((0, 0), (W - 1, 0), (0, 0)))
    y = xp[:, W - 1:W - 1 + S] * w[W - 1]
    for j in range(W - 1):
        y = y + xp[:, j:j + S] * w[j]
    return y


def to_chunks(t):
    B, S, H = t.shape[:3]
    t = t.reshape((B, S // CHUNK, CHUNK, H) + t.shape[3:])
    return jnp.moveaxis(t, 3, 1)


def from_chunks(t):
    B, H, NC, L, d = t.shape
    return jnp.moveaxis(t, 1, 3).reshape(B, NC * L, H, d)


def rg_lru(x, wa, ba, wx, bx, lam):
    B, S, W = x.shape
    xf = x.astype(jnp.float32)
    xb = xf.reshape(B, S, LRU_BLOCKS, W // LRU_BLOCKS)
    r = jax.nn.sigmoid(jnp.einsum('bshi,hij->bshj', xb, wa.astype(jnp.float32)) + ba).reshape(B, S, W)
    ig = jax.nn.sigmoid(jnp.einsum('bshi,hij->bshj', xb, wx.astype(jnp.float32)) + bx).reshape(B, S, W)
    log_a = -LRU_C * r * jax.nn.softplus(-lam.astype(jnp.float32))
    a = jnp.exp(log_a)
    u = jnp.sqrt(-jnp.expm1(2.0 * log_a)) * (ig * xf)
    _, h = lax.associative_scan(lambda e1, e2: (e1[0] * e2[0], e2[0] * e1[1] + e2[1]), (a, u), axis=1)
    return h


def mlstm_chunkwise(q, k, v, i_pre, f_pre):
    f32 = jnp.float32
    q = to_chunks(q.astype(f32)) * (q.shape[-1] ** -0.5)
    k = to_chunks(k.astype(f32))
    v = to_chunks(v.astype(f32))
    ig = to_chunks(i_pre.astype(f32))
    logf = to_chunks(jax.nn.log_sigmoid(f_pre.astype(f32)))
    Bn, H, NC, L, dqk = q.shape
    dv = v.shape[-1]
    b = jnp.cumsum(logf, axis=-1)
    b_last = b[..., -1]
    g_end = b_last[..., None] - b + ig

    def step(carry, inp):
        C, n, m = carry
        bl, ge, kc, vc = inp
        m_new = jnp.maximum(bl + m, jnp.max(ge, axis=-1))
        decay = jnp.exp(bl + m - m_new)
        w = jnp.exp(ge - m_new[..., None])
        C_new = decay[..., None, None] * C + jnp.einsum('bhl,bhld,bhle->bhde', w, kc, vc)
        n_new = decay[..., None] * n + jnp.einsum('bhl,bhld->bhd', w, kc)
        return (C_new, n_new, m_new), (C, n, m)

    init = (jnp.zeros((Bn, H, dqk, dv), f32), jnp.zeros((Bn, H, dqk), f32), jnp.zeros((Bn, H), f32))
    xs = tuple(jnp.moveaxis(t, 2, 0) for t in (b_last, g_end, k, v))
    _, (C_prev, n_prev, m_prev) = lax.scan(step, init, xs)
    C_prev = jnp.moveaxis(C_prev, 0, 2)
    n_prev = jnp.moveaxis(n_prev, 0, 2)
    m_prev = jnp.moveaxis(m_prev, 0, 2)

    causal = jnp.tril(jnp.ones((L, L), bool))
    D = jnp.where(causal, b[..., :, None] - b[..., None, :] + ig[..., None, :], -jnp.inf)
    inter = b + m_prev[..., None]
    m_t = jnp.maximum(inter, jnp.max(D, axis=-1))
    Dw = jnp.exp(D - m_t[..., None])
    iw = jnp.exp(inter - m_t)
    s = jnp.einsum('bhcld,bhcsd->bhcls', q, k) * Dw
    num = jnp.einsum('bhcls,bhcse->bhcle', s, v) + iw[..., None] * jnp.einsum('bhcld,bhcde->bhcle', q, C_prev)
    den = jnp.sum(s, axis=-1) + iw * jnp.einsum('bhcld,bhcd->bhcl', q, n_prev)
    h = num / jnp.maximum(jnp.abs(den), jnp.exp(-m_t))[..., None]
    return from_chunks(h)


def gated_delta_chunked(q, k, v, g, beta):
    f32 = jnp.float32
    q = to_chunks(q.astype(f32)) * (q.shape[-1] ** -0.5)
    k = to_chunks(k.astype(f32))
    v = to_chunks(v.astype(f32))
    g = to_chunks(g.astype(f32))
    beta = to_chunks(beta.astype(f32))
    Bn, H, NC, L, dk = q.shape
    dv = v.shape[-1]
    gc = jnp.cumsum(g, axis=-1)
    causal = jnp.tril(jnp.ones((L, L), bool))
    strict = jnp.tril(jnp.ones((L, L), bool), k=-1)
    gam = jnp.exp(jnp.where(causal, gc[..., :, None] - gc[..., None, :], -jnp.inf))
    a_mat = jnp.where(strict, beta[..., :, None] * gam * jnp.einsum('bhcld,bhcsd->bhcls', k, k), 0.0)
    eye = jnp.eye(L, dtype=f32)
    rhs = jnp.concatenate([beta[..., None] * v, (beta * jnp.exp(gc))[..., None] * k], axis=-1)
    sol = lax.linalg.triangular_solve(eye + a_mat, rhs, left_side=True, lower=True, unit_diagonal=True)
    u, w = sol[..., :dv], sol[..., dv:]
    p = gam * jnp.einsum('bhcld,bhcsd->bhcls', q, k)
    qg = q * jnp.exp(gc)[..., None]
    kd = k * jnp.exp(gc[..., -1:] - gc)[..., None]
    gl = jnp.exp(gc[..., -1])

    def step(S, inp):
        qg_c, p_c, u_c, w_c, kd_c, gl_c = inp
        delta = u_c - jnp.einsum('bhld,bhde->bhle', w_c, S)
        o = jnp.einsum('bhld,bhde->bhle', qg_c, S) + jnp.einsum('bhls,bhse->bhle', p_c, delta)
        S_new = gl_c[..., None, None] * S + jnp.einsum('bhld,bhle->bhde', kd_c, delta)
        return S_new, o

    xs = tuple(jnp.moveaxis(t, 2, 0) for t in (qg, p, u, w, kd, gl))
    _, o = lax.scan(step, jnp.zeros((Bn, H, dk, dv), f32), xs)
    return from_chunks(jnp.moveaxis(o, 0, 2))


def token_mixers(x, w_in, lru_conv_w, lru_conv_b, lru_wa, lru_ba, lru_wx, lru_bx, lru_lambda,
                 ml_i_bias, ml_f_bias, ml_norm_g, gd_conv_w, gd_a_log, gd_dt_bias, gd_norm_g,
                 w_br_lru, w_br_ml, w_br_gd, w_out):
    B, S, _ = x.shape
    f32 = jnp.float32
    proj = x @ w_in
    (lru_x, lru_y, ml_q, ml_k, ml_v, ml_i, ml_f, ml_o,
     gd_q, gd_k, gd_v, gd_a, gd_b, gd_z, gate_pre) = jnp.split(proj, SPLIT_POINTS, axis=-1)

    xa = causal_conv(lru_x, lru_conv_w) + lru_conv_b
    ha = rg_lru(xa, lru_wa, lru_ba, lru_wx, lru_bx, lru_lambda) * jax.nn.gelu(lru_y.astype(f32))
    br_a = ha.astype(x.dtype) @ w_br_lru

    hb = mlstm_chunkwise(ml_q.reshape(B, S, ML_HEADS, ML_DQK), ml_k.reshape(B, S, ML_HEADS, ML_DQK),
                         ml_v.reshape(B, S, ML_HEADS, ML_DV), ml_i + ml_i_bias, ml_f + ml_f_bias)
    hb = rms_norm(hb, ml_norm_g) * jax.nn.sigmoid(ml_o.astype(f32)).reshape(B, S, ML_HEADS, ML_DV)
    br_b = hb.reshape(B, S, ML_V_W).astype(x.dtype) @ w_br_ml

    qkv = jax.nn.silu(causal_conv(jnp.concatenate([gd_q, gd_k, gd_v], axis=-1), gd_conv_w))
    q, k, v = jnp.split(qkv, [GD_K_W, 2 * GD_K_W], axis=-1)
    q = l2_norm(q.reshape(B, S, GD_HEADS, GD_DK))
    k = l2_norm(k.reshape(B, S, GD_HEADS, GD_DK))
    g = -jnp.exp(gd_a_log) * jax.nn.softplus(gd_a.astype(f32) + gd_dt_bias)
    beta = jax.nn.sigmoid(gd_b.astype(f32))
    hc = gated_delta_chunked(q, k, v.reshape(B, S, GD_HEADS, GD_DV), g, beta)
    hc = rms_norm(hc, gd_norm_g) * jax.nn.silu(gd_z.astype(f32)).reshape(B, S, GD_HEADS, GD_DV)
    br_c = hc.reshape(B, S, GD_V_W).astype(x.dtype) @ w_br_gd

    gates = jax.nn.sigmoid(gate_pre.astype(f32)).reshape(B, S, N_BRANCH, D_MODEL)
    merged = gates[..., 0, :] * br_a + gates[..., 1, :] * br_b + gates[..., 2, :] * br_c
    return (merged.astype(x.dtype) @ w_out).astype(x.dtype)


def moe_ffn(x, router_w, router_b, w1, b1, w2, b2):
    B, S, D = x.shape
    T = B * S
    A = T * TOP_K
    xf = x.reshape(T, D)
    logits = (xf @ router_w + router_b).astype(jnp.float32)
    top_val, top_idx = lax.top_k(logits, TOP_K)
    gate = jax.nn.softmax(top_val, axis=-1)
    flat_e = top_idx.reshape(A)
    flat_tok = jnp.repeat(jnp.arange(T, dtype=jnp.int32), TOP_K)
    flat_w = gate.reshape(A)
    order = jnp.argsort(flat_e)
    sorted_e = flat_e[order]
    counts = jnp.bincount(flat_e, length=N_EXPERTS)
    padded = ((counts + MOE_BLOCK - 1) // MOE_BLOCK) * MOE_BLOCK
    pad_end = jnp.cumsum(padded)
    pad_start = pad_end - padded
    start = jnp.cumsum(counts) - counts
    dest = pad_start[sorted_e] + (jnp.arange(A, dtype=jnp.int32) - start[sorted_e])
    n_blocks = -(-A // MOE_BLOCK) + N_EXPERTS
    P = n_blocks * MOE_BLOCK
    tok_buf = jnp.full((P,), T, jnp.int32).at[dest].set(flat_tok[order])
    w_buf = jnp.zeros((P,), jnp.float32).at[dest].set(flat_w[order])
    block_e = jnp.clip(jnp.searchsorted(pad_end, jnp.arange(n_blocks) * MOE_BLOCK, side='right'), 0, N_EXPERTS - 1)
    x_pad = jnp.concatenate([xf, jnp.zeros((1, D), xf.dtype)], axis=0)

    def expert_block(args):
        tok, e = args
        hdn = x_pad[tok] @ w1[e] + b1[e]
        glu, lin = hdn[:, :D_FF], hdn[:, D_FF:]
        glu = jnp.minimum(glu, SWIGLU_LIMIT)
        lin = jnp.clip(lin, -SWIGLU_LIMIT, SWIGLU_LIMIT)
        act = glu * jax.nn.sigmoid(SWIGLU_ALPHA * glu) * (lin + 1.0)
        return act @ w2[e] + b2[e]

    y = lax.map(expert_block, (tok_buf.reshape(n_blocks, MOE_BLOCK), block_e)).reshape(P, D)
    y = y * w_buf[:, None].astype(y.dtype)
    out = jnp.zeros((T + 1, D), y.dtype).at[tok_buf].add(y)[:T]
    return out.reshape(B, S, D).astype(x.dtype)


def setup_inputs(seed: int = 0) -> dict:
    key = jax.random.key(seed)
    ks = iter(jax.random.split(key, 40))
    f32 = jnp.float32

    def nrm(shape, scale):
        return scale * jax.random.normal(next(ks), shape, f32)

    def unif(shape, lo, hi):
        return jax.random.uniform(next(ks), shape, f32, lo, hi)

    Lr = DEPTH
    bw = LRU_WIDTH // LRU_BLOCKS
    gd_ch = 2 * GD_K_W + GD_V_W
    a_base = unif((Lr, LRU_WIDTH), 0.9, 0.999) ** (1.0 / LRU_C)
    dt = jnp.exp(unif((Lr, GD_HEADS), math.log(1e-3), math.log(0.1)))
    return {
        'x': nrm((BATCH, SEQ, D_MODEL), 1.0),
        'w_in': nrm((Lr, D_MODEL, D_IN), D_MODEL ** -0.5),
        'lru_conv_w': nrm((Lr, CONV_WIDTH, LRU_WIDTH), CONV_WIDTH ** -0.5),
        'lru_conv_b': nrm((Lr, LRU_WIDTH), 0.02),
        'lru_wa': nrm((Lr, LRU_BLOCKS, bw, bw), bw ** -0.5),
        'lru_ba': nrm((Lr, LRU_BLOCKS, bw), 0.02),
        'lru_wx': nrm((Lr, LRU_BLOCKS, bw, bw), bw ** -0.5),
        'lru_bx': nrm((Lr, LRU_BLOCKS, bw), 0.02),
        'lru_lambda': jnp.log(a_base) - jnp.log1p(-a_base),
        'ml_i_bias': -2.0 + nrm((Lr, ML_HEADS), 0.1),
        'ml_f_bias': jnp.linspace(3.0, 6.0, ML_HEADS)[None, :] + nrm((Lr, ML_HEADS), 0.1),
        'ml_norm_g': 1.0 + nrm((Lr, ML_HEADS, ML_DV), 0.02),
        'gd_conv_w': nrm((Lr, CONV_WIDTH, gd_ch), CONV_WIDTH ** -0.5),
        'gd_a_log': jnp.log(unif((Lr, GD_HEADS), 1.0, 16.0)),
        'gd_dt_bias': dt + jnp.log(-jnp.expm1(-dt)),
        'gd_norm_g': 1.0 + nrm((Lr, GD_DV), 0.02),
        'w_br_lru': nrm((Lr, LRU_WIDTH, D_MODEL), DN_BETA * LRU_WIDTH ** -0.5),
        'w_br_ml': nrm((Lr, ML_V_W, D_MODEL), DN_BETA * ML_V_W ** -0.5),
        'w_br_gd': nrm((Lr, GD_V_W, D_MODEL), DN_BETA * GD_V_W ** -0.5),
        'w_out': nrm((Lr, D_MODEL, D_MODEL), DN_BETA * D_MODEL ** -0.5),
        'ln1_g': 1.0 + nrm((Lr, D_MODEL), 0.02),
        'ln1_b': nrm((Lr, D_MODEL), 0.02),
        'router_w': nrm((Lr, D_MODEL, N_EXPERTS), D_MODEL ** -0.5),
        'router_b': nrm((Lr, N_EXPERTS), 0.01),
        'exp_w1': nrm((Lr, N_EXPERTS, D_MODEL, 2 * D_FF), D_MODEL ** -0.5),
        'exp_b1': nrm((Lr, N_EXPERTS, 2 * D_FF), 0.02),
        'exp_w2': nrm((Lr, N_EXPERTS, D_FF, D_MODEL), DN_BETA * D_FF ** -0.5),
        'exp_b2': nrm((Lr, N_EXPERTS, D_MODEL), 0.02),
        'ln2_g': 1.0 + nrm((Lr, D_MODEL), 0.02),
        'ln2_b': nrm((Lr, D_MODEL), 0.02),
    }


def reference(x, w_in, lru_conv_w, lru_conv_b, lru_wa, lru_ba, lru_wx, lru_bx, lru_lambda,
              ml_i_bias, ml_f_bias, ml_norm_g, gd_conv_w, gd_a_log, gd_dt_bias, gd_norm_g,
              w_br_lru, w_br_ml, w_br_gd, w_out, ln1_g, ln1_b, router_w, router_b,
              exp_w1, exp_b1, exp_w2, exp_b2, ln2_g, ln2_b):
    for l in range(DEPTH):
        y = token_mixers(x, w_in[l], lru_conv_w[l], lru_conv_b[l], lru_wa[l], lru_ba[l], lru_wx[l],
                         lru_bx[l], lru_lambda[l], ml_i_bias[l], ml_f_bias[l], ml_norm_g[l],
                         gd_conv_w[l], gd_a_log[l], gd_dt_bias[l], gd_norm_g[l],
                         w_br_lru[l], w_br_ml[l], w_br_gd[l], w_out[l])
        x = layer_norm(DN_ALPHA * x + y, ln1_g[l], ln1_b[l])
        y = moe_ffn(x, router_w[l], router_b[l], exp_w1[l], exp_b1[l], exp_w2[l], exp_b2[l])
        x = layer_norm(DN_ALPHA * x + y, ln2_g[l], ln2_b[l])
    return x
```

```python
import functools
import math

import jax
import jax.numpy as jnp
from jax import lax
from jax.experimental import pallas as pl
from jax.experimental.pallas import tpu as pltpu

f32 = jnp.float32
bf16 = jnp.bfloat16

D_MODEL = 1024
LRU_WIDTH = 1024
LRU_BLOCKS = 8
LRU_C = 8.0
CONV_WIDTH = 4
ML_HEADS = 4
ML_DQK = 64
ML_DV = 128
GD_HEADS = 8
GD_DK = 64
GD_DV = 64
N_EXPERTS = 32
TOP_K = 4
D_FF = 1024
SWIGLU_LIMIT = 7.0
SWIGLU_ALPHA = 1.702
MOE_BLOCK = 256
N_BRANCH = 3
DEPTH = 2
DN_ALPHA = (2.0 * DEPTH) ** 0.25
LN_EPS = 1e-5
RMS_EPS = 1e-6

ML_QK_W = ML_HEADS * ML_DQK
ML_V_W = ML_HEADS * ML_DV
GD_K_W = GD_HEADS * GD_DK
GD_V_W = GD_HEADS * GD_DV
IN_SPLITS = (LRU_WIDTH, LRU_WIDTH, ML_QK_W, ML_QK_W, ML_V_W, ML_HEADS, ML_HEADS, ML_V_W,
             GD_K_W, GD_K_W, GD_V_W, GD_HEADS, GD_HEADS, GD_V_W, N_BRANCH * D_MODEL)

LANES = 128
SUBLANES = 8
NEG_BIG = -1e30

COL_LRU_X = 0
COL_LRU_Y = 1024
COL_GATE = 2048
COL_ML_Q = 5120
COL_ML_K = 5376
COL_ML_V = 5632
COL_ML_O = 6144
COL_GD_Q = 6656
COL_GD_K = 7168
COL_GD_V = 7680
COL_GD_Z = 8192
COL_SMALL = 8704
PROJ_W = 8832
SM_ML_I = 0
SM_ML_F = 4
SM_GD_A = 8
SM_GD_B = 16

ML_CHUNK = 256
GD_CHUNK = 64
VMEM_LIMIT = 56 * 1024 * 1024


def _cparams(*sem):
    return pltpu.CompilerParams(dimension_semantics=sem, vmem_limit_bytes=VMEM_LIMIT)


def _hdot(a, b):
    return jnp.dot(a, b, precision=lax.Precision.HIGHEST, preferred_element_type=f32)


def _bdot(a, b):
    return jnp.dot(a.astype(bf16), b.astype(bf16), preferred_element_type=f32)


def _bdot_nt(a, b):
    return lax.dot_general(a.astype(bf16), b.astype(bf16), (((1,), (1,)), ((), ())),
                           preferred_element_type=f32)


def _bdot_tn(a, b):
    return lax.dot_general(a.astype(bf16), b.astype(bf16), (((0,), (0,)), ((), ())),
                           preferred_element_type=f32)


def _sigmoid(x):
    return 1.0 / (1.0 + jnp.exp(-x))


def _softplus(x):
    return jnp.maximum(x, 0.0) + jnp.log1p(jnp.exp(-jnp.abs(x)))


def _log_sigmoid(x):
    return jnp.minimum(x, 0.0) - jnp.log1p(jnp.exp(-jnp.abs(x)))


def _layer_norm(z, g, b):
    mu = jnp.mean(z, axis=-1, keepdims=True)
    zc = z - mu
    var = jnp.mean(zc * zc, axis=-1, keepdims=True)
    return zc * lax.rsqrt(var + LN_EPS) * g + b


def _proj_kernel(x_ref, w_ref, o_ref):
    o_ref[...] = jnp.dot(x_ref[...].astype(bf16), w_ref[...], preferred_element_type=f32)


def _proj(x2d, w_arr):
    T = x2d.shape[0]
    tm = min(512, T)
    tn = PROJ_W // 3
    return pl.pallas_call(
        _proj_kernel,
        out_shape=jax.ShapeDtypeStruct((T, PROJ_W), f32),
        grid=(PROJ_W // tn, T // tm),
        in_specs=[pl.BlockSpec((tm, D_MODEL), lambda j, i: (i, 0)),
                  pl.BlockSpec((D_MODEL, tn), lambda j, i: (0, j))],
        out_specs=pl.BlockSpec((tm, tn), lambda j, i: (i, j)),
        compiler_params=_cparams("parallel", "parallel"),
        name="in_proj",
    )(x2d, w_arr)


def _arrange_w_in(w):
    pts = []
    acc = 0
    for s in IN_SPLITS[:-1]:
        acc += s
        pts.append(acc)
    (lru_x, lru_y, ml_q, ml_k, ml_v, ml_i, ml_f, ml_o,
     gd_q, gd_k, gd_v, gd_a, gd_b, gd_z, gate) = jnp.split(w, pts, axis=1)
    pad = jnp.zeros((w.shape[0], LANES - 2 * ML_HEADS - 2 * GD_HEADS), w.dtype)
    small = jnp.concatenate([ml_i, ml_f, gd_a, gd_b, pad], axis=1)
    out = jnp.concatenate([lru_x, lru_y, gate, ml_q, ml_k, ml_v, ml_o, gd_q, gd_k, gd_v, gd_z, small], axis=1)
    return out.astype(bf16)


def _lru_kernel(x_ref, y_ref, cw_ref, cb_ref, wcat_ref, ba_ref, bx_ref, lam_ref, o_ref,
                xext, a_scr, u_scr, carry):
    ts = x_ref.shape[0]
    W = x_ref.shape[1]
    bw = W // LRU_BLOCKS

    @pl.when(pl.program_id(1) == 0)
    def _():
        xext[0:SUBLANES, :] = jnp.zeros((SUBLANES, W), f32)
        carry[...] = jnp.zeros((1, W), f32)

    x = x_ref[...]
    xext[SUBLANES:SUBLANES + ts, :] = x
    xa = (cw_ref[3:4, :] * x
          + cw_ref[2:3, :] * xext[SUBLANES - 1:SUBLANES - 1 + ts, :]
          + cw_ref[1:2, :] * xext[SUBLANES - 2:SUBLANES - 2 + ts, :]
          + cw_ref[0:1, :] * xext[SUBLANES - 3:SUBLANES - 3 + ts, :]) + cb_ref[...]
    xext[0:SUBLANES, :] = x_ref[ts - SUBLANES:ts, :]

    cdec = -LRU_C * _softplus(-lam_ref[...])
    for h in range(LRU_BLOCKS):
        sl = slice(bw * h, bw * (h + 1))
        xh = xa[:, sl]
        g = jnp.dot(xh.astype(bf16), wcat_ref[h], preferred_element_type=f32)
        r = _sigmoid(g[:, :bw] + ba_ref[:, sl])
        ig = _sigmoid(g[:, bw:] + bx_ref[:, sl])
        log_a = r * cdec[:, sl]
        a = jnp.exp(log_a)
        a_scr[:, sl] = a
        u_scr[:, sl] = jnp.sqrt(jnp.tanh(-log_a) * (1.0 + a * a)) * (ig * xh)

    row = lax.broadcasted_iota(jnp.int32, (SUBLANES, W), 0)

    def body(g, cr):
        off = pl.multiple_of(g * SUBLANES, SUBLANES)
        A = a_scr[pl.ds(off, SUBLANES), :]
        U = u_scr[pl.ds(off, SUBLANES), :]
        for s in (1, 2, 4):
            a_sh = pltpu.roll(A, s, 0)
            u_sh = pltpu.roll(U, s, 0)
            m = row >= s
            U = jnp.where(m, A * u_sh + U, U)
            A = jnp.where(m, A * a_sh, A)
        H = A * cr + U
        u_scr[pl.ds(off, SUBLANES), :] = H
        return H[SUBLANES - 1:SUBLANES, :]

    carry[...] = lax.fori_loop(0, ts // SUBLANES, body, carry[...], unroll=2)
    o_ref[...] = (u_scr[...] * jax.nn.gelu(y_ref[...])).astype(o_ref.dtype)


def _lru(proj, B, S, cw, cb, wa, ba, wx, bx, lam):
    T = B * S
    ts = min(512, S)
    nt = S // ts
    W = LRU_WIDTH
    wcat = jnp.concatenate([wa, wx], axis=-1).astype(bf16)
    row = lambda v: v.reshape(1, W)
    full = lambda shp: pl.BlockSpec(shp, lambda b, c: (0,) * len(shp))
    return pl.pallas_call(
        _lru_kernel,
        out_shape=jax.ShapeDtypeStruct((T, W), bf16),
        grid=(B, nt),
        in_specs=[pl.BlockSpec((ts, W), lambda b, c: (b * nt + c, COL_LRU_X // W)),
                  pl.BlockSpec((ts, W), lambda b, c: (b * nt + c, COL_LRU_Y // W)),
                  full((CONV_WIDTH, W)), full((1, W)), full(wcat.shape),
                  full((1, W)), full((1, W)), full((1, W))],
        out_specs=pl.BlockSpec((ts, W), lambda b, c: (b * nt + c, 0)),
        scratch_shapes=[pltpu.VMEM((ts + SUBLANES, W), f32), pltpu.VMEM((ts, W), f32),
                        pltpu.VMEM((ts, W), f32), pltpu.VMEM((1, W), f32)],
        compiler_params=_cparams("parallel", "arbitrary"),
        name="rg_lru",
    )(proj, proj, cw, row(cb), wcat, row(ba), row(bx), row(lam))


def _ml_kernel(q_ref, k_ref, v_ref, og_ref, sm_ref, bias_ref, g_ref, o_ref, c_scr, n_scr, m_scr):
    L = q_ref.shape[0]

    @pl.when(pl.program_id(1) == 0)
    def _():
        c_scr[...] = jnp.zeros(c_scr.shape, f32)
        n_scr[...] = jnp.zeros(n_scr.shape, f32)
        m_scr[...] = jnp.zeros(m_scr.shape, f32)

    sm = sm_ref[...] + bias_ref[...]
    logf = _log_sigmoid(sm)
    ri = lax.broadcasted_iota(jnp.int32, (L, L), 0)
    ci = lax.broadcasted_iota(jnp.int32, (L, L), 1)
    tri = ri >= ci
    b_all = _hdot(tri.astype(f32), logf)
    sm_t = sm.T
    b_t = b_all.T
    q = q_ref[...]
    k = k_ref[...]
    kb = k.astype(bf16)
    lane_head = lax.broadcasted_iota(jnp.int32, q.shape, 1) // ML_DQK
    scale = ML_DQK ** -0.5
    for h in range(ML_HEADS):
        sl = slice(ML_DV * h, ML_DV * (h + 1))
        ig_c = sm[:, SM_ML_I + h:SM_ML_I + h + 1]
        b_c = b_all[:, SM_ML_F + h:SM_ML_F + h + 1]
        ig_r = sm_t[SM_ML_I + h:SM_ML_I + h + 1, :]
        b_r = b_t[SM_ML_F + h:SM_ML_F + h + 1, :]
        b_last = b_c[L - 1:L, :]
        m_prev = m_scr[h:h + 1, 0:1]
        ge_r = b_last - b_r + ig_r
        ge_c = b_last - b_c + ig_c
        m_new = jnp.maximum(b_last + m_prev, jnp.max(ge_r, axis=1, keepdims=True))
        decay = jnp.exp(b_last + m_prev - m_new)
        w_c = jnp.exp(ge_c - m_new)
        dmat = jnp.where(tri, b_c - b_r + ig_r, NEG_BIG)
        inter = b_c + m_prev
        m_t = jnp.maximum(inter, jnp.max(dmat, axis=1, keepdims=True))
        dw = jnp.exp(dmat - m_t)
        iw = jnp.exp(inter - m_t)
        qm = jnp.where(lane_head == h, q, 0.0) * scale
        qmb = qm.astype(bf16)
        s = _bdot_nt(qmb, kb) * dw
        v_h = v_ref[:, sl]
        n_row = n_scr[h:h + 1, :]
        num = _bdot(s, v_h) + iw * _bdot(qmb, c_scr[:, sl])
        den = jnp.sum(s, axis=1, keepdims=True) + iw * jnp.sum(qm * n_row, axis=1, keepdims=True)
        hh = num / jnp.maximum(jnp.abs(den), jnp.exp(-m_t))
        c_scr[:, sl] = decay * c_scr[:, sl] + _bdot_tn(kb, w_c * v_h)
        n_scr[h:h + 1, :] = decay * n_row + jnp.sum(w_c * k, axis=0, keepdims=True)
        m_scr[h:h + 1, :] = jnp.broadcast_to(m_new, (1, LANES))
        ms = jnp.mean(hh * hh, axis=1, keepdims=True)
        hn = hh * lax.rsqrt(ms + RMS_EPS) * g_ref[:, sl]
        o_ref[:, sl] = (hn * _sigmoid(og_ref[:, sl])).astype(o_ref.dtype)


def _mlstm(proj, B, S, i_bias, f_bias, norm_g):
    T = B * S
    L = min(ML_CHUNK, S)
    nc = S // L
    bias = jnp.zeros((1, LANES), f32)
    bias = bias.at[0, SM_ML_I:SM_ML_I + ML_HEADS].set(i_bias).at[0, SM_ML_F:SM_ML_F + ML_HEADS].set(f_bias)
    full = lambda shp: pl.BlockSpec(shp, lambda b, c: (0,) * len(shp))
    blk = lambda w, col: pl.BlockSpec((L, w), lambda b, c: (b * nc + c, col // w))
    return pl.pallas_call(
        _ml_kernel,
        out_shape=jax.ShapeDtypeStruct((T, ML_V_W), bf16),
        grid=(B, nc),
        in_specs=[blk(ML_QK_W, COL_ML_Q), blk(ML_QK_W, COL_ML_K), blk(ML_V_W, COL_ML_V),
                  blk(ML_V_W, COL_ML_O), blk(LANES, COL_SMALL), full((1, LANES)), full((1, ML_V_W))],
        out_specs=pl.BlockSpec((L, ML_V_W), lambda b, c: (b * nc + c, 0)),
        scratch_shapes=[pltpu.VMEM((ML_QK_W, ML_V_W), f32), pltpu.VMEM((SUBLANES, ML_QK_W), f32),
                        pltpu.VMEM((SUBLANES, LANES), f32)],
        compiler_params=_cparams("parallel", "arbitrary"),
        name="mlstm",
    )(proj, proj, proj, proj, proj, bias, norm_g.reshape(1, ML_V_W))


def _unit_lower_inverse(a, ri, ci):
    L = a.shape[0]
    eye = (ri == ci).astype(f32)
    ad = jnp.where((ri >> 3) == (ci >> 3), a, 0.0)
    a2 = _hdot(ad, ad)
    a4 = _hdot(a2, a2)
    x = _hdot(_hdot(eye - ad, eye + a2), eye + a4)
    size = SUBLANES
    shift = 3
    while size < L:
        pair = (ri >> (shift + 1)) == (ci >> (shift + 1))
        lower = (((ri >> shift) & 1) == 1) & (((ci >> shift) & 1) == 0)
        bmat = jnp.where(pair & lower, a, 0.0)
        x = x - _hdot(x, _hdot(bmat, x))
        size *= 2
        shift += 1
    return x


def _gd_kernel(q_ref, k_ref, v_ref, z_ref, sm_ref, cw_ref, alog_ref, dtb_ref, g_ref, o_ref, ext, s_scr):
    L = q_ref.shape[0]
    KW = q_ref.shape[1]

    @pl.when(pl.program_id(1) == 0)
    def _():
        ext[0:SUBLANES, :] = jnp.zeros((SUBLANES, ext.shape[1]), f32)
        s_scr[...] = jnp.zeros(s_scr.shape, f32)

    ext[SUBLANES:SUBLANES + L, 0:KW] = q_ref[...]
    ext[SUBLANES:SUBLANES + L, KW:2 * KW] = k_ref[...]
    ext[SUBLANES:SUBLANES + L, 2 * KW:3 * KW] = v_ref[...]
    xc = (cw_ref[3:4, :] * ext[SUBLANES:SUBLANES + L, :]
          + cw_ref[2:3, :] * ext[SUBLANES - 1:SUBLANES - 1 + L, :]
          + cw_ref[1:2, :] * ext[SUBLANES - 2:SUBLANES - 2 + L, :]
          + cw_ref[0:1, :] * ext[SUBLANES - 3:SUBLANES - 3 + L, :])
    ext[0:SUBLANES, :] = ext[L:L + SUBLANES, :]
    xc = xc * _sigmoid(xc)

    sm = sm_ref[...]
    g_t = -jnp.exp(alog_ref[...]) * _softplus(sm + dtb_ref[...])
    beta_t = _sigmoid(sm)
    ri = lax.broadcasted_iota(jnp.int32, (L, L), 0)
    ci = lax.broadcasted_iota(jnp.int32, (L, L), 1)
    tri = ri >= ci
    strict = ri > ci
    gc_all = _hdot(tri.astype(f32), g_t)
    gc_t = gc_all.T

    lane = lax.broadcasted_iota(jnp.int32, (L, LANES), 1)
    lo = lane < GD_DK
    lane_row = lax.broadcasted_iota(jnp.int32, (1, LANES), 1)
    r2 = lax.broadcasted_iota(jnp.int32, (LANES, LANES), 0)
    c2 = lax.broadcasted_iota(jnp.int32, (LANES, LANES), 1)
    blockdiag = (r2 < GD_DK) == (c2 < GD_DK)

    def half_sums(y):
        s_lo = jnp.sum(jnp.where(lo, y, 0.0), axis=1, keepdims=True)
        s_hi = jnp.sum(jnp.where(lo, 0.0, y), axis=1, keepdims=True)
        return s_lo, s_hi

    def l2n(y):
        s_lo, s_hi = half_sums(y * y)
        return y * jnp.where(lo, lax.rsqrt(s_lo + RMS_EPS), lax.rsqrt(s_hi + RMS_EPS))

    for j in range(KW // LANES):
        sl = slice(LANES * j, LANES * (j + 1))
        qn = l2n(xc[:, sl]) * (GD_DK ** -0.5)
        kn = l2n(xc[:, KW + LANES * j:KW + LANES * (j + 1)])
        vs = xc[:, 2 * KW + LANES * j:2 * KW + LANES * (j + 1)]
        knb = kn.astype(bf16)
        us, ws, ps, egs, eds, gls = [], [], [], [], [], []
        for e in range(2):
            h = 2 * j + e
            me = lo if e == 0 else jnp.logical_not(lo)
            gc_c = gc_all[:, SM_GD_A + h:SM_GD_A + h + 1]
            gc_r = gc_t[SM_GD_A + h:SM_GD_A + h + 1, :]
            beta_c = beta_t[:, SM_GD_B + h:SM_GD_B + h + 1]
            gam = jnp.exp(jnp.where(tri, gc_c - gc_r, NEG_BIG))
            kk = _bdot_nt(jnp.where(me, kn, 0.0), knb)
            amat = jnp.where(strict, beta_c * gam * kk, 0.0)
            tinv = _unit_lower_inverse(amat, ri, ci)
            eg_c = jnp.exp(gc_c)
            us.append(_hdot(tinv, beta_c * vs))
            ws.append(_hdot(tinv, (beta_c * eg_c) * kn))
            ps.append(gam * _bdot_nt(jnp.where(me, qn, 0.0), knb))
            gc_last = gc_c[L - 1:L, :]
            egs.append(eg_c)
            eds.append(jnp.exp(gc_last - gc_c))
            gls.append(jnp.exp(gc_last))
        u = jnp.where(lo, us[0], us[1])
        w = jnp.where(lo, ws[0], ws[1])
        eg = jnp.where(lo, egs[0], egs[1])
        ed = jnp.where(lo, eds[0], eds[1])
        gl = jnp.where(lane_row < GD_DK, gls[0], gls[1])
        s_prev = s_scr[j]
        sb = s_prev.astype(bf16)
        delta = u - _bdot(w, sb)
        db = delta.astype(bf16)
        o = _bdot(qn * eg, sb) + jnp.where(lo, _bdot(ps[0], db), _bdot(ps[1], db))
        upd = _bdot_tn(kn * ed, db)
        s_scr[j] = gl * s_prev + jnp.where(blockdiag, upd, 0.0)
        m_lo, m_hi = half_sums(o * o)
        inv = jnp.where(lo, lax.rsqrt(m_lo * (1.0 / GD_DV) + RMS_EPS), lax.rsqrt(m_hi * (1.0 / GD_DV) + RMS_EPS))
        zs = z_ref[:, sl]
        o_ref[:, sl] = (o * inv * g_ref[:, sl] * (zs * _sigmoid(zs))).astype(o_ref.dtype)


def _gdn(proj, B, S, conv_w, a_log, dt_bias, norm_g):
    T = B * S
    L = min(GD_CHUNK, S)
    nc = S // L
    KW = GD_K_W
    alog = jnp.zeros((1, LANES), f32).at[0, SM_GD_A:SM_GD_A + GD_HEADS].set(a_log)
    dtb = jnp.zeros((1, LANES), f32).at[0, SM_GD_A:SM_GD_A + GD_HEADS].set(dt_bias)
    g_row = jnp.tile(norm_g, GD_HEADS).reshape(1, GD_V_W)
    full = lambda shp: pl.BlockSpec(shp, lambda b, c: (0,) * len(shp))
    blk = lambda w, col: pl.BlockSpec((L, w), lambda b, c: (b * nc + c, col // w))
    return pl.pallas_call(
        _gd_kernel,
        out_shape=jax.ShapeDtypeStruct((T, GD_V_W), bf16),
        grid=(B, nc),
        in_specs=[blk(KW, COL_GD_Q), blk(KW, COL_GD_K), blk(KW, COL_GD_V), blk(KW, COL_GD_Z),
                  blk(LANES, COL_SMALL), full((CONV_WIDTH, 3 * KW)), full((1, LANES)), full((1, LANES)),
                  full((1, GD_V_W))],
        out_specs=pl.BlockSpec((L, GD_V_W), lambda b, c: (b * nc + c, 0)),
        scratch_shapes=[pltpu.VMEM((L + SUBLANES, 3 * KW), f32),
                        pltpu.VMEM((KW // LANES, LANES, LANES), f32)],
        compiler_params=_cparams("parallel", "arbitrary"),
        name="gated_deltanet",
    )(proj, proj, proj, proj, proj, conv_w, alog, dtb, g_row)


def _merge_kernel(ha_ref, hb_ref, hc_ref, g0_ref, g1_ref, g2_ref, x_ref,
                  wa_ref, wb_ref, wc_ref, wo_ref, lg_ref, lb_ref, x1_ref, x1b_ref):
    bra = jnp.dot(ha_ref[...], wa_ref[...], preferred_element_type=f32)
    brb = jnp.dot(hb_ref[...], wb_ref[...], preferred_element_type=f32)
    brc = jnp.dot(hc_ref[...], wc_ref[...], preferred_element_type=f32)
    merged = _sigmoid(g0_ref[...]) * bra + _sigmoid(g1_ref[...]) * brb + _sigmoid(g2_ref[...]) * brc
    y = jnp.dot(merged.astype(bf16), wo_ref[...], preferred_element_type=f32)
    out = _layer_norm(DN_ALPHA * x_ref[...] + y, lg_ref[...], lb_ref[...])
    x1_ref[...] = out
    x1b_ref[...] = out.astype(bf16)


def _merge(ha, hb, hc, proj, x2d, wa, wb, wc, wo, lg, lb):
    T = x2d.shape[0]
    D = D_MODEL
    tm = min(256, T)
    full = lambda shp: pl.BlockSpec(shp, lambda i: (0,) * len(shp))
    rows = lambda w: pl.BlockSpec((tm, w), lambda i: (i, 0))
    gate = lambda n: pl.BlockSpec((tm, D), lambda i: (i, COL_GATE // D + n))
    return pl.pallas_call(
        _merge_kernel,
        out_shape=(jax.ShapeDtypeStruct((T, D), f32), jax.ShapeDtypeStruct((T, D), bf16)),
        grid=(T // tm,),
        in_specs=[rows(LRU_WIDTH), rows(ML_V_W), rows(GD_V_W), gate(0), gate(1), gate(2), rows(D),
                  full((LRU_WIDTH, D)), full((ML_V_W, D)), full((GD_V_W, D)), full((D, D)),
                  full((1, D)), full((1, D))],
        out_specs=(rows(D), rows(D)),
        compiler_params=_cparams("parallel"),
        name="merge_outproj_ln",
    )(ha, hb, hc, proj, proj, proj, x2d, wa.astype(bf16), wb.astype(bf16), wc.astype(bf16),
      wo.astype(bf16), lg.reshape(1, D), lb.reshape(1, D))


PK_IDX = 0
PK_GATE = 4
PK_RANK = 8


def _router_kernel(x_ref, rw_ref, rb_ref, pk_ref, cnt_ref, carry):
    tr = x_ref.shape[0]

    @pl.when(pl.program_id(0) == 0)
    def _():
        carry[...] = jnp.zeros(carry.shape, f32)

    logits = _hdot(x_ref[...], rw_ref[...]) + rb_ref[...]
    lane = lax.broadcasted_iota(jnp.int32, (tr, LANES), 1).astype(f32)
    vals = logits
    idxs, tops = [], []
    for _ in range(TOP_K):
        m = jnp.max(vals, axis=1, keepdims=True)
        idx = jnp.min(jnp.where(vals == m, lane, float(LANES)), axis=1, keepdims=True)
        idxs.append(idx)
        tops.append(m)
        vals = jnp.where(lane == idx, NEG_BIG * 2.0, vals)
    es = [jnp.exp(t - tops[0]) for t in tops]
    tot = es[0] + es[1] + es[2] + es[3]
    onehots = [lane == idx for idx in idxs]
    sel = jnp.zeros((tr, LANES), f32)
    for oh in onehots:
        sel = sel + oh.astype(f32)
    ri = lax.broadcasted_iota(jnp.int32, (tr, tr), 0)
    ci = lax.broadcasted_iota(jnp.int32, (tr, tr), 1)
    before = jnp.dot((ri > ci).astype(bf16), sel.astype(bf16), preferred_element_type=f32) + carry[...]
    carry[...] = carry[...] + jnp.sum(sel, axis=0, keepdims=True)
    cnt_ref[...] = carry[...]
    packed = jnp.zeros((tr, LANES), f32)
    for kk in range(TOP_K):
        rank = jnp.sum(jnp.where(onehots[kk], before, 0.0), axis=1, keepdims=True)
        packed = jnp.where(lane == float(PK_IDX + kk), idxs[kk], packed)
        packed = jnp.where(lane == float(PK_GATE + kk), es[kk] / tot, packed)
        packed = jnp.where(lane == float(PK_RANK + kk), rank, packed)
    pk_ref[...] = packed


def _router(x1, router_w, router_b):
    T = x1.shape[0]
    tr = min(256, T)
    rw = jnp.zeros((D_MODEL, LANES), f32).at[:, :N_EXPERTS].set(router_w)
    rb = jnp.full((1, LANES), NEG_BIG, f32).at[0, :N_EXPERTS].set(router_b)
    full = lambda shp: pl.BlockSpec(shp, lambda i: (0,) * len(shp))
    return pl.pallas_call(
        _router_kernel,
        out_shape=(jax.ShapeDtypeStruct((T, LANES), f32), jax.ShapeDtypeStruct((1, LANES), f32)),
        grid=(T // tr,),
        in_specs=[pl.BlockSpec((tr, D_MODEL), lambda i: (i, 0)), full((D_MODEL, LANES)), full((1, LANES))],
        out_specs=(pl.BlockSpec((tr, LANES), lambda i: (i, 0)), full((1, LANES))),
        scratch_shapes=[pltpu.VMEM((1, LANES), f32)],
        compiler_params=_cparams("arbitrary"),
        name="router",
    )(x1, rw, rb)


def _expert_kernel(be_ref, nu_ref, x_ref, w1_ref, b1_ref, w2_ref, b2_ref, o_ref):
    i = pl.program_id(0)

    @pl.when(i < nu_ref[0])
    def _():
        hdn = jnp.dot(x_ref[...], w1_ref[0], preferred_element_type=f32) + b1_ref[0]
        glu = jnp.minimum(hdn[:, :D_FF], SWIGLU_LIMIT)
        lin = jnp.clip(hdn[:, D_FF:], -SWIGLU_LIMIT, SWIGLU_LIMIT)
        act = glu * _sigmoid(SWIGLU_ALPHA * glu) * (lin + 1.0)
        o_ref[...] = jnp.dot(act.astype(bf16), w2_ref[0], preferred_element_type=f32) + b2_ref[0]

    @pl.when(i >= nu_ref[0])
    def _():
        o_ref[...] = jnp.zeros(o_ref.shape, o_ref.dtype)


def _experts(xs, block_e, n_used, w1, b1, w2, b2):
    P = xs.shape[0]
    nb = P // MOE_BLOCK
    D = D_MODEL
    grid_spec = pltpu.PrefetchScalarGridSpec(
        num_scalar_prefetch=2,
        grid=(nb,),
        in_specs=[pl.BlockSpec((MOE_BLOCK, D), lambda i, be, nu: (i, 0)),
                  pl.BlockSpec((1, D, 2 * D_FF), lambda i, be, nu: (be[i], 0, 0)),
                  pl.BlockSpec((1, 1, 2 * D_FF), lambda i, be, nu: (be[i], 0, 0)),
                  pl.BlockSpec((1, D_FF, D), lambda i, be, nu: (be[i], 0, 0)),
                  pl.BlockSpec((1, 1, D), lambda i, be, nu: (be[i], 0, 0))],
        out_specs=pl.BlockSpec((MOE_BLOCK, D), lambda i, be, nu: (i, 0)),
    )
    return pl.pallas_call(
        _expert_kernel,
        out_shape=jax.ShapeDtypeStruct((P, D), f32),
        grid_spec=grid_spec,
        compiler_params=_cparams("arbitrary"),
        name="experts",
    )(block_e, n_used, xs, w1.astype(bf16), b1.reshape(N_EXPERTS, 1, 2 * D_FF),
      w2.astype(bf16), b2.reshape(N_EXPERTS, 1, D))


def _combine_kernel(yg_ref, pk_ref, x_ref, lg_ref, lb_ref, o_ref):
    D = x_ref.shape[1]
    pk = pk_ref[...]
    y = pk[:, PK_GATE:PK_GATE + 1] * yg_ref[:, 0:D]
    for kk in range(1, TOP_K):
        y = y + pk[:, PK_GATE + kk:PK_GATE + kk + 1] * yg_ref[:, kk * D:(kk + 1) * D]
    o_ref[...] = _layer_norm(DN_ALPHA * x_ref[...] + y, lg_ref[...], lb_ref[...])


def _combine(yg, packed, x1, lg, lb):
    T, D = x1.shape
    tm = min(256, T)
    full = lambda shp: pl.BlockSpec(shp, lambda i: (0,) * len(shp))
    return pl.pallas_call(
        _combine_kernel,
        out_shape=jax.ShapeDtypeStruct((T, D), f32),
        grid=(T // tm,),
        in_specs=[pl.BlockSpec((tm, TOP_K * D), lambda i: (i, 0)), pl.BlockSpec((tm, LANES), lambda i: (i, 0)),
                  pl.BlockSpec((tm, D), lambda i: (i, 0)), full((1, D)), full((1, D))],
        out_specs=pl.BlockSpec((tm, D), lambda i: (i, 0)),
        compiler_params=_cparams("parallel"),
        name="combine_ln",
    )(yg, packed, x1, lg.reshape(1, D), lb.reshape(1, D))


def _moe(x1, x1b, router_w, router_b, w1, b1, w2, b2, lg, lb):
    T, D = x1.shape
    A = T * TOP_K
    packed, cnt = _router(x1, router_w, router_b)
    idx = packed[:, PK_IDX:PK_IDX + TOP_K].astype(jnp.int32)
    rank = packed[:, PK_RANK:PK_RANK + TOP_K].astype(jnp.int32)
    counts = cnt[0, :N_EXPERTS].astype(jnp.int32)
    padded = ((counts + MOE_BLOCK - 1) // MOE_BLOCK) * MOE_BLOCK
    pad_end = jnp.cumsum(padded)
    pad_start = pad_end - padded
    pos = (pad_start[idx] + rank).reshape(A)
    n_blocks = -(-A // MOE_BLOCK) + N_EXPERTS
    P = n_blocks * MOE_BLOCK
    tok = jnp.repeat(jnp.arange(T, dtype=jnp.int32), TOP_K)
    tok_buf = jnp.full((P,), T, jnp.int32).at[pos].set(tok)
    block_e = jnp.clip(jnp.searchsorted(pad_end, jnp.arange(n_blocks, dtype=jnp.int32) * MOE_BLOCK, side='right'),
                       0, N_EXPERTS - 1).astype(jnp.int32)
    n_used = (pad_end[-1:] // MOE_BLOCK).astype(jnp.int32)
    x_pad = jnp.concatenate([x1b, jnp.zeros((1, D), x1b.dtype)], axis=0)
    xs = jnp.take(x_pad, tok_buf, axis=0)
    ys = _experts(xs, block_e, n_used, w1, b1, w2, b2)
    yg = jnp.take(ys, pos, axis=0).reshape(T, TOP_K * D)
    return _combine(yg, packed, x1, lg, lb)


def _layer(x2d, B, S, p):
    proj = _proj(x2d, _arrange_w_in(p['w_in']))
    ha = _lru(proj, B, S, p['lru_conv_w'], p['lru_conv_b'], p['lru_wa'], p['lru_ba'], p['lru_wx'],
              p['lru_bx'], p['lru_lambda'])
    hb = _mlstm(proj, B, S, p['ml_i_bias'], p['ml_f_bias'], p['ml_norm_g'])
    hc = _gdn(proj, B, S, p['gd_conv_w'], p['gd_a_log'], p['gd_dt_bias'], p['gd_norm_g'])
    x1, x1b = _merge(ha, hb, hc, proj, x2d, p['w_br_lru'], p['w_br_ml'], p['w_br_gd'], p['w_out'],
                     p['ln1_g'], p['ln1_b'])
    return _moe(x1, x1b, p['router_w'], p['router_b'], p['exp_w1'], p['exp_b1'], p['exp_w2'], p['exp_b2'],
                p['ln2_g'], p['ln2_b'])


def kernel(x, w_in, lru_conv_w, lru_conv_b, lru_wa, lru_ba, lru_wx, lru_bx, lru_lambda, ml_i_bias, ml_f_bias, ml_norm_g, gd_conv_w, gd_a_log, gd_dt_bias, gd_norm_g, w_br_lru, w_br_ml, w_br_gd, w_out, ln1_g, ln1_b, router_w, router_b, exp_w1, exp_b1, exp_w2, exp_b2, ln2_g, ln2_b):
    B, S, D = x.shape
    params = dict(w_in=w_in, lru_conv_w=lru_conv_w, lru_conv_b=lru_conv_b, lru_wa=lru_wa, lru_ba=lru_ba,
                  lru_wx=lru_wx, lru_bx=lru_bx, lru_lambda=lru_lambda, ml_i_bias=ml_i_bias,
                  ml_f_bias=ml_f_bias, ml_norm_g=ml_norm_g, gd_conv_w=gd_conv_w, gd_a_log=gd_a_log,
                  gd_dt_bias=gd_dt_bias, gd_norm_g=gd_norm_g, w_br_lru=w_br_lru, w_br_ml=w_br_ml,
                  w_br_gd=w_br_gd, w_out=w_out, ln1_g=ln1_g, ln1_b=ln1_b, router_w=router_w,
                  router_b=router_b, exp_w1=exp_w1, exp_b1=exp_b1, exp_w2=exp_w2, exp_b2=exp_b2,
                  ln2_g=ln2_g, ln2_b=ln2_b)
    h = x.reshape(B * S, D)
    for l in range(w_in.shape[0]):
        h = _layer(h, B, S, {k: v[l] for k, v in params.items()})
    return h.reshape(B, S, D)
```

```python
import functools
import math

import jax
import jax.numpy as jnp
from jax import lax
from jax.experimental import pallas as pl
from jax.experimental.pallas import tpu as pltpu

f32 = jnp.float32
bf16 = jnp.bfloat16

D_MODEL = 1024
LRU_WIDTH = 1024
LRU_BLOCKS = 8
LRU_C = 8.0
CONV_WIDTH = 4
ML_HEADS = 4
ML_DQK = 64
ML_DV = 128
GD_HEADS = 8
GD_DK = 64
GD_DV = 64
N_EXPERTS = 32
TOP_K = 4
D_FF = 1024
SWIGLU_LIMIT = 7.0
SWIGLU_ALPHA = 1.702
MOE_BLOCK = 256
N_BRANCH = 3
DEPTH = 2
DN_ALPHA = (2.0 * DEPTH) ** 0.25
LN_EPS = 1e-5
RMS_EPS = 1e-6

ML_QK_W = ML_HEADS * ML_DQK
ML_V_W = ML_HEADS * ML_DV
GD_K_W = GD_HEADS * GD_DK
GD_V_W = GD_HEADS * GD_DV
IN_SPLITS = (LRU_WIDTH, LRU_WIDTH, ML_QK_W, ML_QK_W, ML_V_W, ML_HEADS, ML_HEADS, ML_V_W,
             GD_K_W, GD_K_W, GD_V_W, GD_HEADS, GD_HEADS, GD_V_W, N_BRANCH * D_MODEL)

LANES = 128
SUBLANES = 8
NEG_BIG = -1e30

COL_LRU_X = 0
COL_LRU_Y = 1024
COL_GATE = 2048
COL_ML_Q = 5120
COL_ML_K = 5376
COL_ML_V = 5632
COL_ML_O = 6144
COL_GD_Q = 6656
COL_GD_K = 7168
COL_GD_V = 7680
COL_GD_Z = 8192
COL_SMALL = 8704
PROJ_W = 8832
SM_ML_I = 0
SM_ML_F = 4
SM_GD_A = 8
SM_GD_B = 16

ML_CHUNK = 256
GD_CHUNK = 64
GD_ROWS = 256
GD_SOLVE_PASSES = 1
VMEM_LIMIT = 56 * 1024 * 1024


def _cparams(*sem):
    return pltpu.CompilerParams(dimension_semantics=sem, vmem_limit_bytes=VMEM_LIMIT)


def _hdot(a, b):
    return jnp.dot(a, b, precision=lax.Precision.HIGHEST, preferred_element_type=f32)


def _bdot(a, b):
    return jnp.dot(a.astype(bf16), b.astype(bf16), preferred_element_type=f32)


def _bdot_nt(a, b):
    return lax.dot_general(a.astype(bf16), b.astype(bf16), (((1,), (1,)), ((), ())),
                           preferred_element_type=f32)


def _bdot_tn(a, b):
    return lax.dot_general(a.astype(bf16), b.astype(bf16), (((0,), (0,)), ((), ())),
                           preferred_element_type=f32)


def _sigmoid(x):
    return 1.0 / (1.0 + jnp.exp(-x))


def _softplus(x):
    return jnp.maximum(x, 0.0) + jnp.log1p(jnp.exp(-jnp.abs(x)))


def _log_sigmoid(x):
    return jnp.minimum(x, 0.0) - jnp.log1p(jnp.exp(-jnp.abs(x)))


def _layer_norm(z, g, b):
    mu = jnp.mean(z, axis=-1, keepdims=True)
    zc = z - mu
    var = jnp.mean(zc * zc, axis=-1, keepdims=True)
    return zc * lax.rsqrt(var + LN_EPS) * g + b


def _proj_kernel(x_ref, w_ref, o_ref):
    o_ref[...] = jnp.dot(x_ref[...].astype(bf16), w_ref[...], preferred_element_type=f32)


def _proj(x2d, w_arr):
    T = x2d.shape[0]
    tm = min(512, T)
    tn = PROJ_W // 3
    return pl.pallas_call(
        _proj_kernel,
        out_shape=jax.ShapeDtypeStruct((T, PROJ_W), f32),
        grid=(PROJ_W // tn, T // tm),
        in_specs=[pl.BlockSpec((tm, D_MODEL), lambda j, i: (i, 0)),
                  pl.BlockSpec((D_MODEL, tn), lambda j, i: (0, j))],
        out_specs=pl.BlockSpec((tm, tn), lambda j, i: (i, j)),
        compiler_params=_cparams("parallel", "parallel"),
        name="in_proj",
    )(x2d, w_arr)


def _arrange_w_in(w):
    pts = []
    acc = 0
    for s in IN_SPLITS[:-1]:
        acc += s
        pts.append(acc)
    (lru_x, lru_y, ml_q, ml_k, ml_v, ml_i, ml_f, ml_o,
     gd_q, gd_k, gd_v, gd_a, gd_b, gd_z, gate) = jnp.split(w, pts, axis=1)
    pad = jnp.zeros((w.shape[0], LANES - 2 * ML_HEADS - 2 * GD_HEADS), w.dtype)
    small = jnp.concatenate([ml_i, ml_f, gd_a, gd_b, pad], axis=1)
    out = jnp.concatenate([lru_x, lru_y, gate, ml_q, ml_k, ml_v, ml_o, gd_q, gd_k, gd_v, gd_z, small], axis=1)
    return out.astype(bf16)


def _lru_kernel(x_ref, y_ref, cw_ref, cb_ref, wcat_ref, ba_ref, bx_ref, lam_ref, o_ref,
                xext, a_scr, u_scr, carry):
    ts = x_ref.shape[0]
    W = x_ref.shape[1]
    bw = W // LRU_BLOCKS

    @pl.when(pl.program_id(1) == 0)
    def _():
        xext[0:SUBLANES, :] = jnp.zeros((SUBLANES, W), f32)
        carry[...] = jnp.zeros((1, W), f32)

    x = x_ref[...]
    xext[SUBLANES:SUBLANES + ts, :] = x
    xa = (cw_ref[3:4, :] * x
          + cw_ref[2:3, :] * xext[SUBLANES - 1:SUBLANES - 1 + ts, :]
          + cw_ref[1:2, :] * xext[SUBLANES - 2:SUBLANES - 2 + ts, :]
          + cw_ref[0:1, :] * xext[SUBLANES - 3:SUBLANES - 3 + ts, :]) + cb_ref[...]
    xext[0:SUBLANES, :] = x_ref[ts - SUBLANES:ts, :]

    cdec = -LRU_C * _softplus(-lam_ref[...])
    for h in range(LRU_BLOCKS):
        sl = slice(bw * h, bw * (h + 1))
        xh = xa[:, sl]
        g = jnp.dot(xh.astype(bf16), wcat_ref[h], preferred_element_type=f32)
        r = _sigmoid(g[:, :bw] + ba_ref[:, sl])
        ig = _sigmoid(g[:, bw:] + bx_ref[:, sl])
        log_a = r * cdec[:, sl]
        a = jnp.exp(log_a)
        a_scr[:, sl] = a
        u_scr[:, sl] = jnp.sqrt(jnp.tanh(-log_a) * (1.0 + a * a)) * (ig * xh)

    row = lax.broadcasted_iota(jnp.int32, (SUBLANES, W), 0)

    def body(g, cr):
        off = pl.multiple_of(g * SUBLANES, SUBLANES)
        A = a_scr[pl.ds(off, SUBLANES), :]
        U = u_scr[pl.ds(off, SUBLANES), :]
        for s in (1, 2, 4):
            a_sh = pltpu.roll(A, s, 0)
            u_sh = pltpu.roll(U, s, 0)
            m = row >= s
            U = jnp.where(m, A * u_sh + U, U)
            A = jnp.where(m, A * a_sh, A)
        H = A * cr + U
        u_scr[pl.ds(off, SUBLANES), :] = H
        return H[SUBLANES - 1:SUBLANES, :]

    carry[...] = lax.fori_loop(0, ts // SUBLANES, body, carry[...], unroll=2)
    o_ref[...] = (u_scr[...] * jax.nn.gelu(y_ref[...])).astype(o_ref.dtype)


def _lru(proj, B, S, cw, cb, wa, ba, wx, bx, lam):
    T = B * S
    ts = min(512, S)
    nt = S // ts
    W = LRU_WIDTH
    wcat = jnp.concatenate([wa, wx], axis=-1).astype(bf16)
    row = lambda v: v.reshape(1, W)
    full = lambda shp: pl.BlockSpec(shp, lambda b, c: (0,) * len(shp))
    return pl.pallas_call(
        _lru_kernel,
        out_shape=jax.ShapeDtypeStruct((T, W), bf16),
        grid=(B, nt),
        in_specs=[pl.BlockSpec((ts, W), lambda b, c: (b * nt + c, COL_LRU_X // W)),
                  pl.BlockSpec((ts, W), lambda b, c: (b * nt + c, COL_LRU_Y // W)),
                  full((CONV_WIDTH, W)), full((1, W)), full(wcat.shape),
                  full((1, W)), full((1, W)), full((1, W))],
        out_specs=pl.BlockSpec((ts, W), lambda b, c: (b * nt + c, 0)),
        scratch_shapes=[pltpu.VMEM((ts + SUBLANES, W), f32), pltpu.VMEM((ts, W), f32),
                        pltpu.VMEM((ts, W), f32), pltpu.VMEM((1, W), f32)],
        compiler_params=_cparams("parallel", "arbitrary"),
        name="rg_lru",
    )(proj, proj, cw, row(cb), wcat, row(ba), row(bx), row(lam))


def _ml_kernel(q_ref, k_ref, v_ref, og_ref, sm_ref, bias_ref, g_ref, o_ref, c_scr, n_scr, m_scr):
    L = q_ref.shape[0]

    @pl.when(pl.program_id(1) == 0)
    def _():
        c_scr[...] = jnp.zeros(c_scr.shape, f32)
        n_scr[...] = jnp.zeros(n_scr.shape, f32)
        m_scr[...] = jnp.zeros(m_scr.shape, f32)

    sm = sm_ref[...] + bias_ref[...]
    logf = _log_sigmoid(sm)
    ri = lax.broadcasted_iota(jnp.int32, (L, L), 0)
    ci = lax.broadcasted_iota(jnp.int32, (L, L), 1)
    tri = ri >= ci
    b_all = _hdot(tri.astype(f32), logf)
    sm_t = sm.T
    b_t = b_all.T
    q = q_ref[...]
    k = k_ref[...]
    kb = k.astype(bf16)
    lane_head = lax.broadcasted_iota(jnp.int32, q.shape, 1) // ML_DQK
    scale = ML_DQK ** -0.5
    for h in range(ML_HEADS):
        sl = slice(ML_DV * h, ML_DV * (h + 1))
        ig_c = sm[:, SM_ML_I + h:SM_ML_I + h + 1]
        b_c = b_all[:, SM_ML_F + h:SM_ML_F + h + 1]
        ig_r = sm_t[SM_ML_I + h:SM_ML_I + h + 1, :]
        b_r = b_t[SM_ML_F + h:SM_ML_F + h + 1, :]
        b_last = b_c[L - 1:L, :]
        m_prev = m_scr[h:h + 1, 0:1]
        ge_r = b_last - b_r + ig_r
        ge_c = b_last - b_c + ig_c
        m_new = jnp.maximum(b_last + m_prev, jnp.max(ge_r, axis=1, keepdims=True))
        decay = jnp.exp(b_last + m_prev - m_new)
        w_c = jnp.exp(ge_c - m_new)
        dmat = jnp.where(tri, b_c - b_r + ig_r, NEG_BIG)
        inter = b_c + m_prev
        m_t = jnp.maximum(inter, jnp.max(dmat, axis=1, keepdims=True))
        dw = jnp.exp(dmat - m_t)
        iw = jnp.exp(inter - m_t)
        qm = jnp.where(lane_head == h, q, 0.0) * scale
        qmb = qm.astype(bf16)
        s = _bdot_nt(qmb, kb) * dw
        v_h = v_ref[:, sl]
        n_row = n_scr[h:h + 1, :]
        num = _bdot(s, v_h) + iw * _bdot(qmb, c_scr[:, sl])
        den = jnp.sum(s, axis=1, keepdims=True) + iw * jnp.sum(qm * n_row, axis=1, keepdims=True)
        hh = num / jnp.maximum(jnp.abs(den), jnp.exp(-m_t))
        c_scr[:, sl] = decay * c_scr[:, sl] + _bdot_tn(kb, w_c * v_h)
        n_scr[h:h + 1, :] = decay * n_row + jnp.sum(w_c * k, axis=0, keepdims=True)
        m_scr[h:h + 1, :] = jnp.broadcast_to(m_new, (1, LANES))
        ms = jnp.mean(hh * hh, axis=1, keepdims=True)
        hn = hh * lax.rsqrt(ms + RMS_EPS) * g_ref[:, sl]
        o_ref[:, sl] = (hn * _sigmoid(og_ref[:, sl])).astype(o_ref.dtype)


def _mlstm(proj, B, S, i_bias, f_bias, norm_g):
    T = B * S
    L = min(ML_CHUNK, S)
    nc = S // L
    bias = jnp.zeros((1, LANES), f32)
    bias = bias.at[0, SM_ML_I:SM_ML_I + ML_HEADS].set(i_bias).at[0, SM_ML_F:SM_ML_F + ML_HEADS].set(f_bias)
    full = lambda shp: pl.BlockSpec(shp, lambda b, c: (0,) * len(shp))
    blk = lambda w, col: pl.BlockSpec((L, w), lambda b, c: (b * nc + c, col // w))
    return pl.pallas_call(
        _ml_kernel,
        out_shape=jax.ShapeDtypeStruct((T, ML_V_W), bf16),
        grid=(B, nc),
        in_specs=[blk(ML_QK_W, COL_ML_Q), blk(ML_QK_W, COL_ML_K), blk(ML_V_W, COL_ML_V),
                  blk(ML_V_W, COL_ML_O), blk(LANES, COL_SMALL), full((1, LANES)), full((1, ML_V_W))],
        out_specs=pl.BlockSpec((L, ML_V_W), lambda b, c: (b * nc + c, 0)),
        scratch_shapes=[pltpu.VMEM((ML_QK_W, ML_V_W), f32), pltpu.VMEM((SUBLANES, ML_QK_W), f32),
                        pltpu.VMEM((SUBLANES, LANES), f32)],
        compiler_params=_cparams("parallel", "arbitrary"),
        name="mlstm",
    )(proj, proj, proj, proj, proj, bias, norm_g.reshape(1, ML_V_W))


def _split_bf16(x, passes):
    hi = x.astype(bf16)
    if passes == 1:
        return (hi,)
    return (hi, (x - hi.astype(f32)).astype(bf16))


def _mdot(xs, ys):
    out = jnp.dot(xs[0], ys[0], preferred_element_type=f32)
    if len(xs) > 1:
        out = out + jnp.dot(xs[1], ys[0], preferred_element_type=f32)
        out = out + jnp.dot(xs[0], ys[1], preferred_element_type=f32)
    return out


def _gd_kernel(q_ref, k_ref, v_ref, z_ref, sm_ref, cw_ref, alog_ref, dtb_ref, g_ref, o_ref, ext, s_scr):
    R = q_ref.shape[0]
    KW = q_ref.shape[1]
    L = GD_CHUNK
    nch = R // L
    lsh = L.bit_length() - 1
    PW = nch * L
    nsl = KW // LANES
    passes = GD_SOLVE_PASSES

    @pl.when(pl.program_id(1) == 0)
    def _():
        ext[0:SUBLANES, :] = jnp.zeros((SUBLANES, ext.shape[1]), f32)
        s_scr[...] = jnp.zeros(s_scr.shape, f32)

    ext[SUBLANES:SUBLANES + R, 0:KW] = q_ref[...]
    ext[SUBLANES:SUBLANES + R, KW:2 * KW] = k_ref[...]
    ext[SUBLANES:SUBLANES + R, 2 * KW:3 * KW] = v_ref[...]
    xc = (cw_ref[3:4, :] * ext[SUBLANES:SUBLANES + R, :]
          + cw_ref[2:3, :] * ext[SUBLANES - 1:SUBLANES - 1 + R, :]
          + cw_ref[1:2, :] * ext[SUBLANES - 2:SUBLANES - 2 + R, :]
          + cw_ref[0:1, :] * ext[SUBLANES - 3:SUBLANES - 3 + R, :])
    ext[0:SUBLANES, :] = ext[R:R + SUBLANES, :]
    xc = xc * _sigmoid(xc)

    sm = sm_ref[...]
    g_t = -jnp.exp(alog_ref[...]) * _softplus(sm + dtb_ref[...])
    beta_t = _sigmoid(sm)
    rr = lax.broadcasted_iota(jnp.int32, (R, R), 0)
    cc = lax.broadcasted_iota(jnp.int32, (R, R), 1)
    cum = jnp.where(((rr >> lsh) == (cc >> lsh)) & (rr >= cc), 1.0, 0.0)
    gc_all = _hdot(cum, g_t)
    gc_t = gc_all.T

    pt = lax.broadcasted_iota(jnp.int32, (L, PW), 0)
    plane = lax.broadcasted_iota(jnp.int32, (L, PW), 1)
    ps = plane & (L - 1)
    pc = plane >> lsh
    tri_p = pt >= ps
    strict_p = pt > ps
    eye_p = jnp.where(pt == ps, 1.0, 0.0)
    blk8_p = (pt >> 3) == (ps >> 3)
    cmask_b = [jnp.where(pc == c, 1.0, 0.0).astype(bf16) for c in range(nch)]

    def pick(parts):
        out = parts[nch - 1]
        for c in range(nch - 2, -1, -1):
            out = jnp.where(pc == c, parts[c], out)
        return out

    def chunks(col):
        return [col[c * L:(c + 1) * L] for c in range(nch)]

    def bd(yb):
        return jnp.concatenate([yb * cmask_b[c] for c in range(nch)], axis=0)

    def pdot(xp, yp):
        return _mdot(_split_bf16(xp, passes), [bd(p) for p in _split_bf16(yp, passes)])

    lane_r = lax.broadcasted_iota(jnp.int32, (R, LANES), 1)
    lo_r = lane_r < GD_DK
    lane_l = lax.broadcasted_iota(jnp.int32, (L, LANES), 1)
    lo_l = lane_l < GD_DK
    lane_row = lax.broadcasted_iota(jnp.int32, (1, LANES), 1)
    r2 = lax.broadcasted_iota(jnp.int32, (LANES, LANES), 0)
    c2 = lax.broadcasted_iota(jnp.int32, (LANES, LANES), 1)
    blockdiag = (r2 < GD_DK) == (c2 < GD_DK)

    def half_sums(y, lo):
        s_lo = jnp.sum(jnp.where(lo, y, 0.0), axis=1, keepdims=True)
        s_hi = jnp.sum(jnp.where(lo, 0.0, y), axis=1, keepdims=True)
        return s_lo, s_hi

    def l2n(y):
        s_lo, s_hi = half_sums(y * y, lo_r)
        return y * jnp.where(lo_r, lax.rsqrt(s_lo + RMS_EPS), lax.rsqrt(s_hi + RMS_EPS))

    qns, kns, vss, a_ps, p_bs, gcols, bcols = [], [], [], [], [], [], []
    for j in range(nsl):
        qn = l2n(xc[:, LANES * j:LANES * (j + 1)]) * (GD_DK ** -0.5)
        kn = l2n(xc[:, KW + LANES * j:KW + LANES * (j + 1)])
        qns.append(qn)
        kns.append(kn)
        vss.append(xc[:, 2 * KW + LANES * j:2 * KW + LANES * (j + 1)])
        lhs = jnp.concatenate([jnp.where(lo_r, kn, 0.0), jnp.where(lo_r, 0.0, kn),
                               jnp.where(lo_r, qn, 0.0), jnp.where(lo_r, 0.0, qn)], axis=0)
        gram = _bdot_nt(lhs, kn)
        for e in range(2):
            h = 2 * j + e
            gcol = gc_all[:, SM_GD_A + h:SM_GD_A + h + 1]
            bcol = beta_t[:, SM_GD_B + h:SM_GD_B + h + 1]
            gc_r = gc_t[SM_GD_A + h:SM_GD_A + h + 1, :]
            gam_p = jnp.exp(jnp.where(tri_p, pick(chunks(gcol)) - gc_r, NEG_BIG))
            kk_p = pick(chunks(gram[e * R:(e + 1) * R]))
            qk_p = pick(chunks(gram[(2 + e) * R:(3 + e) * R]))
            a_ps.append(jnp.where(strict_p, pick(chunks(bcol)) * gam_p * kk_p, 0.0))
            p_bs.append((gam_p * qk_p).astype(bf16))
            gcols.append(gcol)
            bcols.append(bcol)

    ads = [jnp.where(blk8_p, a, 0.0) for a in a_ps]
    a2s = [pdot(ad, ad) for ad in ads]
    a4s = [pdot(a2, a2) for a2 in a2s]
    xs = [pdot(eye_p - ad, eye_p + a2) for ad, a2 in zip(ads, a2s)]
    xs = [pdot(x, eye_p + a4) for x, a4 in zip(xs, a4s)]
    for sh in range(3, lsh):
        msk = ((pt >> (sh + 1)) == (ps >> (sh + 1))) & (((pt >> sh) & 1) == 1) & (((ps >> sh) & 1) == 0)
        ts = [pdot(jnp.where(msk, a, 0.0), x) for a, x in zip(a_ps, xs)]
        xs = [x - pdot(x, t) for x, t in zip(xs, ts)]

    us, ws, qgs, kds, gls = [], [], [], [], []
    for j in range(nsl):
        uw, egs, eds, glh = [], [], [], []
        for e in range(2):
            h = 2 * j + e
            gcol, bcol = gcols[h], bcols[h]
            eg = jnp.exp(gcol)
            rhs = jnp.concatenate([bcol * vss[j], (bcol * eg) * kns[j]], axis=1)
            uw.append(_mdot([bd(p) for p in _split_bf16(xs[h], passes)], _split_bf16(rhs, passes)))
            lasts = [gcol[c * L + L - 1:c * L + L] for c in range(nch)]
            gcl = jnp.concatenate([jnp.broadcast_to(v, (L, 1)) for v in lasts], axis=0)
            egs.append(eg)
            eds.append(jnp.exp(gcl - gcol))
            glh.append([jnp.exp(v) for v in lasts])
        us.append(jnp.where(lo_r, uw[0][:, :LANES], uw[1][:, :LANES]))
        ws.append(jnp.where(lo_r, uw[0][:, LANES:], uw[1][:, LANES:]))
        qgs.append(qns[j] * jnp.where(lo_r, egs[0], egs[1]))
        kds.append(kns[j] * jnp.where(lo_r, eds[0], eds[1]))
        gls.append([jnp.where(lane_row < GD_DK, glh[0][c], glh[1][c]) for c in range(nch)])

    states = [s_scr[j] for j in range(nsl)]
    outs = [[] for _ in range(nsl)]
    for c in range(nch):
        rs = slice(c * L, (c + 1) * L)
        for j in range(nsl):
            s_prev = states[j]
            sb = s_prev.astype(bf16)
            delta = us[j][rs] - _bdot(ws[j][rs], sb)
            db = delta.astype(bf16)
            pieces = []
            if c > 0:
                pieces.append(jnp.zeros((c * L, LANES), bf16))
            pieces.append(db)
            if c < nch - 1:
                pieces.append(jnp.zeros(((nch - 1 - c) * L, LANES), bf16))
            dpad = jnp.concatenate(pieces, axis=0) if len(pieces) > 1 else db
            intra = jnp.where(lo_l, jnp.dot(p_bs[2 * j], dpad, preferred_element_type=f32),
                              jnp.dot(p_bs[2 * j + 1], dpad, preferred_element_type=f32))
            outs[j].append(_bdot(qgs[j][rs], sb) + intra)
            upd = _bdot_tn(kds[j][rs], db)
            states[j] = gls[j][c] * s_prev + jnp.where(blockdiag, upd, 0.0)

    for j in range(nsl):
        sl = slice(LANES * j, LANES * (j + 1))
        s_scr[j] = states[j]
        o = jnp.concatenate(outs[j], axis=0) if nch > 1 else outs[j][0]
        m_lo, m_hi = half_sums(o * o, lo_r)
        inv = jnp.where(lo_r, lax.rsqrt(m_lo * (1.0 / GD_DV) + RMS_EPS), lax.rsqrt(m_hi * (1.0 / GD_DV) + RMS_EPS))
        zs = z_ref[:, sl]
        o_ref[:, sl] = (o * inv * g_ref[:, sl] * (zs * _sigmoid(zs))).astype(o_ref.dtype)


def _gdn(proj, B, S, conv_w, a_log, dt_bias, norm_g):
    T = B * S
    L = min(GD_ROWS, S)
    nc = S // L
    KW = GD_K_W
    alog = jnp.zeros((1, LANES), f32).at[0, SM_GD_A:SM_GD_A + GD_HEADS].set(a_log)
    dtb = jnp.zeros((1, LANES), f32).at[0, SM_GD_A:SM_GD_A + GD_HEADS].set(dt_bias)
    g_row = jnp.tile(norm_g, GD_HEADS).reshape(1, GD_V_W)
    full = lambda shp: pl.BlockSpec(shp, lambda b, c: (0,) * len(shp))
    blk = lambda w, col: pl.BlockSpec((L, w), lambda b, c: (b * nc + c, col // w))
    return pl.pallas_call(
        _gd_kernel,
        out_shape=jax.ShapeDtypeStruct((T, GD_V_W), bf16),
        grid=(B, nc),
        in_specs=[blk(KW, COL_GD_Q), blk(KW, COL_GD_K), blk(KW, COL_GD_V), blk(KW, COL_GD_Z),
                  blk(LANES, COL_SMALL), full((CONV_WIDTH, 3 * KW)), full((1, LANES)), full((1, LANES)),
                  full((1, GD_V_W))],
        out_specs=pl.BlockSpec((L, GD_V_W), lambda b, c: (b * nc + c, 0)),
        scratch_shapes=[pltpu.VMEM((L + SUBLANES, 3 * KW), f32),
                        pltpu.VMEM((KW // LANES, LANES, LANES), f32)],
        compiler_params=_cparams("parallel", "arbitrary"),
        name="gated_deltanet",
    )(proj, proj, proj, proj, proj, conv_w, alog, dtb, g_row)


def _merge_kernel(ha_ref, hb_ref, hc_ref, g0_ref, g1_ref, g2_ref, x_ref,
                  wa_ref, wb_ref, wc_ref, wo_ref, lg_ref, lb_ref, x1_ref, x1b_ref):
    bra = jnp.dot(ha_ref[...], wa_ref[...], preferred_element_type=f32)
    brb = jnp.dot(hb_ref[...], wb_ref[...], preferred_element_type=f32)
    brc = jnp.dot(hc_ref[...], wc_ref[...], preferred_element_type=f32)
    merged = _sigmoid(g0_ref[...]) * bra + _sigmoid(g1_ref[...]) * brb + _sigmoid(g2_ref[...]) * brc
    y = jnp.dot(merged.astype(bf16), wo_ref[...], preferred_element_type=f32)
    out = _layer_norm(DN_ALPHA * x_ref[...] + y, lg_ref[...], lb_ref[...])
    x1_ref[...] = out
    x1b_ref[...] = out.astype(bf16)


def _merge(ha, hb, hc, proj, x2d, wa, wb, wc, wo, lg, lb):
    T = x2d.shape[0]
    D = D_MODEL
    tm = min(256, T)
    full = lambda shp: pl.BlockSpec(shp, lambda i: (0,) * len(shp))
    rows = lambda w: pl.BlockSpec((tm, w), lambda i: (i, 0))
    gate = lambda n: pl.BlockSpec((tm, D), lambda i: (i, COL_GATE // D + n))
    return pl.pallas_call(
        _merge_kernel,
        out_shape=(jax.ShapeDtypeStruct((T, D), f32), jax.ShapeDtypeStruct((T, D), bf16)),
        grid=(T // tm,),
        in_specs=[rows(LRU_WIDTH), rows(ML_V_W), rows(GD_V_W), gate(0), gate(1), gate(2), rows(D),
                  full((LRU_WIDTH, D)), full((ML_V_W, D)), full((GD_V_W, D)), full((D, D)),
                  full((1, D)), full((1, D))],
        out_specs=(rows(D), rows(D)),
        compiler_params=_cparams("parallel"),
        name="merge_outproj_ln",
    )(ha, hb, hc, proj, proj, proj, x2d, wa.astype(bf16), wb.astype(bf16), wc.astype(bf16),
      wo.astype(bf16), lg.reshape(1, D), lb.reshape(1, D))


PK_IDX = 0
PK_GATE = 4
PK_RANK = 8


def _router_kernel(x_ref, rw_ref, rb_ref, pk_ref, cnt_ref, carry):
    tr = x_ref.shape[0]

    @pl.when(pl.program_id(0) == 0)
    def _():
        carry[...] = jnp.zeros(carry.shape, f32)

    logits = _hdot(x_ref[...], rw_ref[...]) + rb_ref[...]
    lane = lax.broadcasted_iota(jnp.int32, (tr, LANES), 1).astype(f32)
    vals = logits
    idxs, tops = [], []
    for _ in range(TOP_K):
        m = jnp.max(vals, axis=1, keepdims=True)
        idx = jnp.min(jnp.where(vals == m, lane, float(LANES)), axis=1, keepdims=True)
        idxs.append(idx)
        tops.append(m)
        vals = jnp.where(lane == idx, NEG_BIG * 2.0, vals)
    es = [jnp.exp(t - tops[0]) for t in tops]
    tot = es[0] + es[1] + es[2] + es[3]
    onehots = [lane == idx for idx in idxs]
    sel = jnp.zeros((tr, LANES), f32)
    for oh in onehots:
        sel = sel + oh.astype(f32)
    ri = lax.broadcasted_iota(jnp.int32, (tr, tr), 0)
    ci = lax.broadcasted_iota(jnp.int32, (tr, tr), 1)
    before = jnp.dot((ri > ci).astype(bf16), sel.astype(bf16), preferred_element_type=f32) + carry[...]
    carry[...] = carry[...] + jnp.sum(sel, axis=0, keepdims=True)
    cnt_ref[...] = carry[...]
    packed = jnp.zeros((tr, LANES), f32)
    for kk in range(TOP_K):
        rank = jnp.sum(jnp.where(onehots[kk], before, 0.0), axis=1, keepdims=True)
        packed = jnp.where(lane == float(PK_IDX + kk), idxs[kk], packed)
        packed = jnp.where(lane == float(PK_GATE + kk), es[kk] / tot, packed)
        packed = jnp.where(lane == float(PK_RANK + kk), rank, packed)
    pk_ref[...] = packed


def _router(x1, router_w, router_b):
    T = x1.shape[0]
    tr = min(256, T)
    rw = jnp.zeros((D_MODEL, LANES), f32).at[:, :N_EXPERTS].set(router_w)
    rb = jnp.full((1, LANES), NEG_BIG, f32).at[0, :N_EXPERTS].set(router_b)
    full = lambda shp: pl.BlockSpec(shp, lambda i: (0,) * len(shp))
    return pl.pallas_call(
        _router_kernel,
        out_shape=(jax.ShapeDtypeStruct((T, LANES), f32), jax.ShapeDtypeStruct((1, LANES), f32)),
        grid=(T // tr,),
        in_specs=[pl.BlockSpec((tr, D_MODEL), lambda i: (i, 0)), full((D_MODEL, LANES)), full((1, LANES))],
        out_specs=(pl.BlockSpec((tr, LANES), lambda i: (i, 0)), full((1, LANES))),
        scratch_shapes=[pltpu.VMEM((1, LANES), f32)],
        compiler_params=_cparams("arbitrary"),
        name="router",
    )(x1, rw, rb)


def _expert_kernel(be_ref, nu_ref, x_ref, w1_ref, b1_ref, w2_ref, b2_ref, o_ref, w1b, w2b):
    i = pl.program_id(0)
    used = i < nu_ref[0]
    new_expert = jnp.logical_or(i == 0, be_ref[i] != be_ref[jnp.maximum(i - 1, 0)])

    @pl.when(jnp.logical_and(used, new_expert))
    def _():
        w1b[...] = w1_ref[0].astype(bf16)
        w2b[...] = w2_ref[0].astype(bf16)

    @pl.when(used)
    def _():
        hdn = jnp.dot(x_ref[...], w1b[...], preferred_element_type=f32) + b1_ref[0]
        glu = jnp.minimum(hdn[:, :D_FF], SWIGLU_LIMIT)
        lin = jnp.clip(hdn[:, D_FF:], -SWIGLU_LIMIT, SWIGLU_LIMIT)
        act = glu * _sigmoid(SWIGLU_ALPHA * glu) * (lin + 1.0)
        o_ref[...] = jnp.dot(act.astype(bf16), w2b[...], preferred_element_type=f32) + b2_ref[0]

    @pl.when(i >= nu_ref[0])
    def _():
        o_ref[...] = jnp.zeros(o_ref.shape, o_ref.dtype)


def _experts(xs, block_e, n_used, w1, b1, w2, b2):
    P = xs.shape[0]
    nb = P // MOE_BLOCK
    D = D_MODEL
    grid_spec = pltpu.PrefetchScalarGridSpec(
        num_scalar_prefetch=2,
        grid=(nb,),
        in_specs=[pl.BlockSpec((MOE_BLOCK, D), lambda i, be, nu: (i, 0)),
                  pl.BlockSpec((1, D, 2 * D_FF), lambda i, be, nu: (be[i], 0, 0)),
                  pl.BlockSpec((1, 1, 2 * D_FF), lambda i, be, nu: (be[i], 0, 0)),
                  pl.BlockSpec((1, D_FF, D), lambda i, be, nu: (be[i], 0, 0)),
                  pl.BlockSpec((1, 1, D), lambda i, be, nu: (be[i], 0, 0))],
        out_specs=pl.BlockSpec((MOE_BLOCK, D), lambda i, be, nu: (i, 0)),
        scratch_shapes=[pltpu.VMEM((D, 2 * D_FF), bf16), pltpu.VMEM((D_FF, D), bf16)],
    )
    return pl.pallas_call(
        _expert_kernel,
        out_shape=jax.ShapeDtypeStruct((P, D), f32),
        grid_spec=grid_spec,
        compiler_params=_cparams("arbitrary"),
        name="experts",
    )(block_e, n_used, xs, w1, b1.reshape(N_EXPERTS, 1, 2 * D_FF), w2, b2.reshape(N_EXPERTS, 1, D))


def _combine_kernel(y0_ref, y1_ref, y2_ref, y3_ref, pk_ref, x_ref, lg_ref, lb_ref, o_ref):
    pk = pk_ref[...]
    y = pk[:, PK_GATE:PK_GATE + 1] * y0_ref[...]
    for kk, y_ref in ((1, y1_ref), (2, y2_ref), (3, y3_ref)):
        y = y + pk[:, PK_GATE + kk:PK_GATE + kk + 1] * y_ref[...]
    o_ref[...] = _layer_norm(DN_ALPHA * x_ref[...] + y, lg_ref[...], lb_ref[...])


def _combine(yg, packed, x1, lg, lb):
    T, D = x1.shape
    tm = min(256, T)
    nt = T // tm
    full = lambda shp: pl.BlockSpec(shp, lambda i: (0,) * len(shp))
    choice = lambda kk: pl.BlockSpec((tm, D), lambda i: (kk * nt + i, 0))
    return pl.pallas_call(
        _combine_kernel,
        out_shape=jax.ShapeDtypeStruct((T, D), f32),
        grid=(nt,),
        in_specs=[choice(0), choice(1), choice(2), choice(3), pl.BlockSpec((tm, LANES), lambda i: (i, 0)),
                  pl.BlockSpec((tm, D), lambda i: (i, 0)), full((1, D)), full((1, D))],
        out_specs=pl.BlockSpec((tm, D), lambda i: (i, 0)),
        compiler_params=_cparams("parallel"),
        name="combine_ln",
    )(yg, yg, yg, yg, packed, x1, lg.reshape(1, D), lb.reshape(1, D))


def _moe(x1, x1b, router_w, router_b, w1, b1, w2, b2, lg, lb):
    T, D = x1.shape
    A = T * TOP_K
    packed, cnt = _router(x1, router_w, router_b)
    idx = packed[:, PK_IDX:PK_IDX + TOP_K].astype(jnp.int32)
    rank = packed[:, PK_RANK:PK_RANK + TOP_K].astype(jnp.int32)
    counts = cnt[0, :N_EXPERTS].astype(jnp.int32)
    padded = ((counts + MOE_BLOCK - 1) // MOE_BLOCK) * MOE_BLOCK
    pad_end = jnp.cumsum(padded)
    pad_start = pad_end - padded
    pos = (pad_start[idx] + rank).reshape(A)
    n_blocks = -(-A // MOE_BLOCK) + N_EXPERTS
    P = n_blocks * MOE_BLOCK
    tok = jnp.repeat(jnp.arange(T, dtype=jnp.int32), TOP_K)
    tok_buf = jnp.zeros((P,), jnp.int32).at[pos].set(tok)
    starts = jnp.arange(n_blocks, dtype=jnp.int32) * MOE_BLOCK
    block_e = jnp.minimum(jnp.sum((pad_end[None, :] <= starts[:, None]).astype(jnp.int32), axis=1), N_EXPERTS - 1)
    n_used = (pad_end[-1:] // MOE_BLOCK).astype(jnp.int32)
    xs = jnp.take(x1b, tok_buf, axis=0)
    ys = _experts(xs, block_e, n_used, w1, b1, w2, b2)
    yg = jnp.take(ys, pos.reshape(T, TOP_K).T.reshape(A), axis=0)
    return _combine(yg, packed, x1, lg, lb)


def _layer(x2d, B, S, p):
    proj = _proj(x2d, _arrange_w_in(p['w_in']))
    ha = _lru(proj, B, S, p['lru_conv_w'], p['lru_conv_b'], p['lru_wa'], p['lru_ba'], p['lru_wx'],
              p['lru_bx'], p['lru_lambda'])
    hb = _mlstm(proj, B, S, p['ml_i_bias'], p['ml_f_bias'], p['ml_norm_g'])
    hc = _gdn(proj, B, S, p['gd_conv_w'], p['gd_a_log'], p['gd_dt_bias'], p['gd_norm_g'])
    x1, x1b = _merge(ha, hb, hc, proj, x2d, p['w_br_lru'], p['w_br_ml'], p['w_br_gd'], p['w_out'],
                     p['ln1_g'], p['ln1_b'])
    return _moe(x1, x1b, p['router_w'], p['router_b'], p['exp_w1'], p['exp_b1'], p['exp_w2'], p['exp_b2'],
                p['ln2_g'], p['ln2_b'])


def kernel(x, w_in, lru_conv_w, lru_conv_b, lru_wa, lru_ba, lru_wx, lru_bx, lru_lambda, ml_i_bias, ml_f_bias, ml_norm_g, gd_conv_w, gd_a_log, gd_dt_bias, gd_norm_g, w_br_lru, w_br_ml, w_br_gd, w_out, ln1_g, ln1_b, router_w, router_b, exp_w1, exp_b1, exp_w2, exp_b2, ln2_g, ln2_b):
    B, S, D = x.shape
    params = dict(w_in=w_in, lru_conv_w=lru_conv_w, lru_conv_b=lru_conv_b, lru_wa=lru_wa, lru_ba=lru_ba,
                  lru_wx=lru_wx, lru_bx=lru_bx, lru_lambda=lru_lambda, ml_i_bias=ml_i_bias,
                  ml_f_bias=ml_f_bias, ml_norm_g=ml_norm_g, gd_conv_w=gd_conv_w, gd_a_log=gd_a_log,
                  gd_dt_bias=gd_dt_bias, gd_norm_g=gd_norm_g, w_br_lru=w_br_lru, w_br_ml=w_br_ml,
                  w_br_gd=w_br_gd, w_out=w_out, ln1_g=ln1_g, ln1_b=ln1_b, router_w=router_w,
                  router_b=router_b, exp_w1=exp_w1, exp_b1=exp_b1, exp_w2=exp_w2, exp_b2=exp_b2,
                  ln2_g=ln2_g, ln2_b=ln2_b)
    h = x.reshape(B * S, D)
    for l in range(w_in.shape[0]):
        h = _layer(h, B, S, {k: v[l] for k, v in params.items()})
    return h.reshape(B, S, D)
```

```python
import functools
import math

import jax
import jax.numpy as jnp
from jax import lax
from jax.experimental import pallas as pl
from jax.experimental.pallas import tpu as pltpu

f32 = jnp.float32
bf16 = jnp.bfloat16

D_MODEL = 1024
LRU_WIDTH = 1024
LRU_BLOCKS = 8
LRU_C = 8.0
CONV_WIDTH = 4
ML_HEADS = 4
ML_DQK = 64
ML_DV = 128
GD_HEADS = 8
GD_DK = 64
GD_DV = 64
N_EXPERTS = 32
TOP_K = 4
D_FF = 1024
SWIGLU_LIMIT = 7.0
SWIGLU_ALPHA = 1.702
MOE_BLOCK = 512
N_BRANCH = 3
DEPTH = 2
DN_ALPHA = (2.0 * DEPTH) ** 0.25
LN_EPS = 1e-5
RMS_EPS = 1e-6

ML_QK_W = ML_HEADS * ML_DQK
ML_V_W = ML_HEADS * ML_DV
GD_K_W = GD_HEADS * GD_DK
GD_V_W = GD_HEADS * GD_DV
IN_SPLITS = (LRU_WIDTH, LRU_WIDTH, ML_QK_W, ML_QK_W, ML_V_W, ML_HEADS, ML_HEADS, ML_V_W,
             GD_K_W, GD_K_W, GD_V_W, GD_HEADS, GD_HEADS, GD_V_W, N_BRANCH * D_MODEL)

LANES = 128
SUBLANES = 8
NEG_BIG = -1e30

COL_LRU_X = 0
COL_LRU_Y = 1024
COL_GATE = 2048
COL_ML_Q = 5120
COL_ML_K = 5376
COL_ML_V = 5632
COL_ML_O = 6144
COL_GD_Q = 6656
COL_GD_K = 7168
COL_GD_V = 7680
COL_GD_Z = 8192
COL_SMALL = 8704
PROJ_W = 8832
SM_ML_I = 0
SM_ML_F = 4
SM_GD_A = 8
SM_GD_B = 16

ML_CHUNK = 256
GD_CHUNK = 64
GD_ROWS = 256
GD_SOLVE_PASSES = 1
VMEM_LIMIT = 56 * 1024 * 1024


def _cparams(*sem):
    return pltpu.CompilerParams(dimension_semantics=sem, vmem_limit_bytes=VMEM_LIMIT)


def _hdot(a, b):
    return jnp.dot(a, b, precision=lax.Precision.HIGHEST, preferred_element_type=f32)


def _bdot(a, b):
    return jnp.dot(a.astype(bf16), b.astype(bf16), preferred_element_type=f32)


def _bdot_nt(a, b):
    return lax.dot_general(a.astype(bf16), b.astype(bf16), (((1,), (1,)), ((), ())),
                           preferred_element_type=f32)


def _bdot_tn(a, b):
    return lax.dot_general(a.astype(bf16), b.astype(bf16), (((0,), (0,)), ((), ())),
                           preferred_element_type=f32)


def _sigmoid(x):
    return 1.0 / (1.0 + jnp.exp(-x))


def _softplus(x):
    return jnp.maximum(x, 0.0) + jnp.log1p(jnp.exp(-jnp.abs(x)))


def _log_sigmoid(x):
    return jnp.minimum(x, 0.0) - jnp.log1p(jnp.exp(-jnp.abs(x)))


def _layer_norm(z, g, b):
    mu = jnp.mean(z, axis=-1, keepdims=True)
    zc = z - mu
    var = jnp.mean(zc * zc, axis=-1, keepdims=True)
    return zc * lax.rsqrt(var + LN_EPS) * g + b


def _proj_kernel(x_ref, w_ref, o_ref):
    o_ref[...] = jnp.dot(x_ref[...].astype(bf16), w_ref[...], preferred_element_type=f32)


def _proj(x2d, w_arr):
    T = x2d.shape[0]
    tm = min(512, T)
    tn = PROJ_W // 3
    return pl.pallas_call(
        _proj_kernel,
        out_shape=jax.ShapeDtypeStruct((T, PROJ_W), f32),
        grid=(PROJ_W // tn, T // tm),
        in_specs=[pl.BlockSpec((tm, D_MODEL), lambda j, i: (i, 0)),
                  pl.BlockSpec((D_MODEL, tn), lambda j, i: (0, j))],
        out_specs=pl.BlockSpec((tm, tn), lambda j, i: (i, j)),
        compiler_params=_cparams("parallel", "parallel"),
        name="in_proj",
    )(x2d, w_arr)


def _arrange_w_in(w):
    pts = []
    acc = 0
    for s in IN_SPLITS[:-1]:
        acc += s
        pts.append(acc)
    (lru_x, lru_y, ml_q, ml_k, ml_v, ml_i, ml_f, ml_o,
     gd_q, gd_k, gd_v, gd_a, gd_b, gd_z, gate) = jnp.split(w, pts, axis=1)
    pad = jnp.zeros((w.shape[0], LANES - 2 * ML_HEADS - 2 * GD_HEADS), w.dtype)
    small = jnp.concatenate([ml_i, ml_f, gd_a, gd_b, pad], axis=1)
    out = jnp.concatenate([lru_x, lru_y, gate, ml_q, ml_k, ml_v, ml_o, gd_q, gd_k, gd_v, gd_z, small], axis=1)
    return out.astype(bf16)


def _lru_kernel(x_ref, y_ref, cw_ref, cb_ref, wcat_ref, ba_ref, bx_ref, lam_ref, o_ref,
                xext, a_scr, u_scr, carry):
    ts = x_ref.shape[0]
    W = x_ref.shape[1]
    bw = W // LRU_BLOCKS

    @pl.when(pl.program_id(1) == 0)
    def _():
        xext[0:SUBLANES, :] = jnp.zeros((SUBLANES, W), f32)
        carry[...] = jnp.zeros((1, W), f32)

    x = x_ref[...]
    xext[SUBLANES:SUBLANES + ts, :] = x
    xa = (cw_ref[3:4, :] * x
          + cw_ref[2:3, :] * xext[SUBLANES - 1:SUBLANES - 1 + ts, :]
          + cw_ref[1:2, :] * xext[SUBLANES - 2:SUBLANES - 2 + ts, :]
          + cw_ref[0:1, :] * xext[SUBLANES - 3:SUBLANES - 3 + ts, :]) + cb_ref[...]
    xext[0:SUBLANES, :] = x_ref[ts - SUBLANES:ts, :]

    cdec = -LRU_C * _softplus(-lam_ref[...])
    for h in range(LRU_BLOCKS):
        sl = slice(bw * h, bw * (h + 1))
        xh = xa[:, sl]
        g = jnp.dot(xh.astype(bf16), wcat_ref[h], preferred_element_type=f32)
        r = _sigmoid(g[:, :bw] + ba_ref[:, sl])
        ig = _sigmoid(g[:, bw:] + bx_ref[:, sl])
        log_a = r * cdec[:, sl]
        a = jnp.exp(log_a)
        a_scr[:, sl] = a
        u_scr[:, sl] = jnp.sqrt(jnp.tanh(-log_a) * (1.0 + a * a)) * (ig * xh)

    row = lax.broadcasted_iota(jnp.int32, (SUBLANES, W), 0)

    def body(g, cr):
        off = pl.multiple_of(g * SUBLANES, SUBLANES)
        A = a_scr[pl.ds(off, SUBLANES), :]
        U = u_scr[pl.ds(off, SUBLANES), :]
        for s in (1, 2, 4):
            a_sh = pltpu.roll(A, s, 0)
            u_sh = pltpu.roll(U, s, 0)
            m = row >= s
            U = jnp.where(m, A * u_sh + U, U)
            A = jnp.where(m, A * a_sh, A)
        H = A * cr + U
        u_scr[pl.ds(off, SUBLANES), :] = H
        return H[SUBLANES - 1:SUBLANES, :]

    carry[...] = lax.fori_loop(0, ts // SUBLANES, body, carry[...], unroll=2)
    o_ref[...] = (u_scr[...] * jax.nn.gelu(y_ref[...])).astype(o_ref.dtype)


def _lru(proj, B, S, cw, cb, wa, ba, wx, bx, lam):
    T = B * S
    ts = min(512, S)
    nt = S // ts
    W = LRU_WIDTH
    wcat = jnp.concatenate([wa, wx], axis=-1).astype(bf16)
    row = lambda v: v.reshape(1, W)
    full = lambda shp: pl.BlockSpec(shp, lambda b, c: (0,) * len(shp))
    return pl.pallas_call(
        _lru_kernel,
        out_shape=jax.ShapeDtypeStruct((T, W), bf16),
        grid=(B, nt),
        in_specs=[pl.BlockSpec((ts, W), lambda b, c: (b * nt + c, COL_LRU_X // W)),
                  pl.BlockSpec((ts, W), lambda b, c: (b * nt + c, COL_LRU_Y // W)),
                  full((CONV_WIDTH, W)), full((1, W)), full(wcat.shape),
                  full((1, W)), full((1, W)), full((1, W))],
        out_specs=pl.BlockSpec((ts, W), lambda b, c: (b * nt + c, 0)),
        scratch_shapes=[pltpu.VMEM((ts + SUBLANES, W), f32), pltpu.VMEM((ts, W), f32),
                        pltpu.VMEM((ts, W), f32), pltpu.VMEM((1, W), f32)],
        compiler_params=_cparams("parallel", "arbitrary"),
        name="rg_lru",
    )(proj, proj, cw, row(cb), wcat, row(ba), row(bx), row(lam))


def _ml_kernel(q_ref, k_ref, v_ref, og_ref, sm_ref, bias_ref, g_ref, o_ref, c_scr, n_scr, m_scr):
    L = q_ref.shape[0]

    @pl.when(pl.program_id(1) == 0)
    def _():
        c_scr[...] = jnp.zeros(c_scr.shape, f32)
        n_scr[...] = jnp.zeros(n_scr.shape, f32)
        m_scr[...] = jnp.zeros(m_scr.shape, f32)

    sm = sm_ref[...] + bias_ref[...]
    logf = _log_sigmoid(sm)
    ri = lax.broadcasted_iota(jnp.int32, (L, L), 0)
    ci = lax.broadcasted_iota(jnp.int32, (L, L), 1)
    tri = ri >= ci
    b_all = _hdot(tri.astype(f32), logf)
    sm_t = sm.T
    b_t = b_all.T
    q = q_ref[...]
    k = k_ref[...]
    kb = k.astype(bf16)
    lane_head = lax.broadcasted_iota(jnp.int32, q.shape, 1) // ML_DQK
    scale = ML_DQK ** -0.5
    for h in range(ML_HEADS):
        sl = slice(ML_DV * h, ML_DV * (h + 1))
        ig_c = sm[:, SM_ML_I + h:SM_ML_I + h + 1]
        b_c = b_all[:, SM_ML_F + h:SM_ML_F + h + 1]
        ig_r = sm_t[SM_ML_I + h:SM_ML_I + h + 1, :]
        b_r = b_t[SM_ML_F + h:SM_ML_F + h + 1, :]
        b_last = b_c[L - 1:L, :]
        m_prev = m_scr[h:h + 1, 0:1]
        ge_r = b_last - b_r + ig_r
        ge_c = b_last - b_c + ig_c
        m_new = jnp.maximum(b_last + m_prev, jnp.max(ge_r, axis=1, keepdims=True))
        decay = jnp.exp(b_last + m_prev - m_new)
        w_c = jnp.exp(ge_c - m_new)
        dmat = jnp.where(tri, b_c - b_r + ig_r, NEG_BIG)
        inter = b_c + m_prev
        m_t = jnp.maximum(inter, jnp.max(dmat, axis=1, keepdims=True))
        dw = jnp.exp(dmat - m_t)
        iw = jnp.exp(inter - m_t)
        qm = jnp.where(lane_head == h, q, 0.0) * scale
        qmb = qm.astype(bf16)
        s = _bdot_nt(qmb, kb) * dw
        v_h = v_ref[:, sl]
        n_row = n_scr[h:h + 1, :]
        num = _bdot(s, v_h) + iw * _bdot(qmb, c_scr[:, sl])
        den = jnp.sum(s, axis=1, keepdims=True) + iw * jnp.sum(qm * n_row, axis=1, keepdims=True)
        hh = num / jnp.maximum(jnp.abs(den), jnp.exp(-m_t))
        c_scr[:, sl] = decay * c_scr[:, sl] + _bdot_tn(kb, w_c * v_h)
        n_scr[h:h + 1, :] = decay * n_row + jnp.sum(w_c * k, axis=0, keepdims=True)
        m_scr[h:h + 1, :] = jnp.broadcast_to(m_new, (1, LANES))
        ms = jnp.mean(hh * hh, axis=1, keepdims=True)
        hn = hh * lax.rsqrt(ms + RMS_EPS) * g_ref[:, sl]
        o_ref[:, sl] = (hn * _sigmoid(og_ref[:, sl])).astype(o_ref.dtype)


def _mlstm(proj, B, S, i_bias, f_bias, norm_g):
    T = B * S
    L = min(ML_CHUNK, S)
    nc = S // L
    bias = jnp.zeros((1, LANES), f32)
    bias = bias.at[0, SM_ML_I:SM_ML_I + ML_HEADS].set(i_bias).at[0, SM_ML_F:SM_ML_F + ML_HEADS].set(f_bias)
    full = lambda shp: pl.BlockSpec(shp, lambda b, c: (0,) * len(shp))
    blk = lambda w, col: pl.BlockSpec((L, w), lambda b, c: (b * nc + c, col // w))
    return pl.pallas_call(
        _ml_kernel,
        out_shape=jax.ShapeDtypeStruct((T, ML_V_W), bf16),
        grid=(B, nc),
        in_specs=[blk(ML_QK_W, COL_ML_Q), blk(ML_QK_W, COL_ML_K), blk(ML_V_W, COL_ML_V),
                  blk(ML_V_W, COL_ML_O), blk(LANES, COL_SMALL), full((1, LANES)), full((1, ML_V_W))],
        out_specs=pl.BlockSpec((L, ML_V_W), lambda b, c: (b * nc + c, 0)),
        scratch_shapes=[pltpu.VMEM((ML_QK_W, ML_V_W), f32), pltpu.VMEM((SUBLANES, ML_QK_W), f32),
                        pltpu.VMEM((SUBLANES, LANES), f32)],
        compiler_params=_cparams("parallel", "arbitrary"),
        name="mlstm",
    )(proj, proj, proj, proj, proj, bias, norm_g.reshape(1, ML_V_W))


def _split_bf16(x, passes):
    hi = x.astype(bf16)
    if passes == 1:
        return (hi,)
    return (hi, (x - hi.astype(f32)).astype(bf16))


def _mdot(xs, ys):
    out = jnp.dot(xs[0], ys[0], preferred_element_type=f32)
    if len(xs) > 1:
        out = out + jnp.dot(xs[1], ys[0], preferred_element_type=f32)
        out = out + jnp.dot(xs[0], ys[1], preferred_element_type=f32)
    return out


def _gd_kernel(q_ref, k_ref, v_ref, z_ref, sm_ref, cw_ref, alog_ref, dtb_ref, g_ref, o_ref, ext, s_scr):
    R = q_ref.shape[0]
    KW = q_ref.shape[1]
    L = GD_CHUNK
    nch = R // L
    lsh = L.bit_length() - 1
    PW = nch * L
    nsl = KW // LANES
    passes = GD_SOLVE_PASSES

    @pl.when(pl.program_id(1) == 0)
    def _():
        ext[0:SUBLANES, :] = jnp.zeros((SUBLANES, ext.shape[1]), f32)
        s_scr[...] = jnp.zeros(s_scr.shape, f32)

    ext[SUBLANES:SUBLANES + R, 0:KW] = q_ref[...]
    ext[SUBLANES:SUBLANES + R, KW:2 * KW] = k_ref[...]
    ext[SUBLANES:SUBLANES + R, 2 * KW:3 * KW] = v_ref[...]
    xc = (cw_ref[3:4, :] * ext[SUBLANES:SUBLANES + R, :]
          + cw_ref[2:3, :] * ext[SUBLANES - 1:SUBLANES - 1 + R, :]
          + cw_ref[1:2, :] * ext[SUBLANES - 2:SUBLANES - 2 + R, :]
          + cw_ref[0:1, :] * ext[SUBLANES - 3:SUBLANES - 3 + R, :])
    ext[0:SUBLANES, :] = ext[R:R + SUBLANES, :]
    xc = xc * _sigmoid(xc)

    sm = sm_ref[...]
    g_t = -jnp.exp(alog_ref[...]) * _softplus(sm + dtb_ref[...])
    beta_t = _sigmoid(sm)
    rr = lax.broadcasted_iota(jnp.int32, (R, R), 0)
    cc = lax.broadcasted_iota(jnp.int32, (R, R), 1)
    cum = jnp.where(((rr >> lsh) == (cc >> lsh)) & (rr >= cc), 1.0, 0.0)
    gc_all = _hdot(cum, g_t)
    gc_t = gc_all.T

    pt = lax.broadcasted_iota(jnp.int32, (L, PW), 0)
    plane = lax.broadcasted_iota(jnp.int32, (L, PW), 1)
    ps = plane & (L - 1)
    pc = plane >> lsh
    tri_p = pt >= ps
    strict_p = pt > ps
    eye_p = jnp.where(pt == ps, 1.0, 0.0)
    blk8_p = (pt >> 3) == (ps >> 3)
    cmask_b = [jnp.where(pc == c, 1.0, 0.0).astype(bf16) for c in range(nch)]

    def pick(parts):
        out = parts[nch - 1]
        for c in range(nch - 2, -1, -1):
            out = jnp.where(pc == c, parts[c], out)
        return out

    def chunks(col):
        return [col[c * L:(c + 1) * L] for c in range(nch)]

    def bd(yb):
        return jnp.concatenate([yb * cmask_b[c] for c in range(nch)], axis=0)

    def pdot(xp, yp):
        return _mdot(_split_bf16(xp, passes), [bd(p) for p in _split_bf16(yp, passes)])

    lane_r = lax.broadcasted_iota(jnp.int32, (R, LANES), 1)
    lo_r = lane_r < GD_DK
    lane_l = lax.broadcasted_iota(jnp.int32, (L, LANES), 1)
    lo_l = lane_l < GD_DK
    lane_row = lax.broadcasted_iota(jnp.int32, (1, LANES), 1)
    r2 = lax.broadcasted_iota(jnp.int32, (LANES, LANES), 0)
    c2 = lax.broadcasted_iota(jnp.int32, (LANES, LANES), 1)
    blockdiag = (r2 < GD_DK) == (c2 < GD_DK)

    def half_sums(y, lo):
        s_lo = jnp.sum(jnp.where(lo, y, 0.0), axis=1, keepdims=True)
        s_hi = jnp.sum(jnp.where(lo, 0.0, y), axis=1, keepdims=True)
        return s_lo, s_hi

    def l2n(y):
        s_lo, s_hi = half_sums(y * y, lo_r)
        return y * jnp.where(lo_r, lax.rsqrt(s_lo + RMS_EPS), lax.rsqrt(s_hi + RMS_EPS))

    qns, kns, vss, a_ps, p_bs, gcols, bcols = [], [], [], [], [], [], []
    for j in range(nsl):
        qn = l2n(xc[:, LANES * j:LANES * (j + 1)]) * (GD_DK ** -0.5)
        kn = l2n(xc[:, KW + LANES * j:KW + LANES * (j + 1)])
        qns.append(qn)
        kns.append(kn)
        vss.append(xc[:, 2 * KW + LANES * j:2 * KW + LANES * (j + 1)])
        lhs = jnp.concatenate([jnp.where(lo_r, kn, 0.0), jnp.where(lo_r, 0.0, kn),
                               jnp.where(lo_r, qn, 0.0), jnp.where(lo_r, 0.0, qn)], axis=0)
        gram = _bdot_nt(lhs, kn)
        for e in range(2):
            h = 2 * j + e
            gcol = gc_all[:, SM_GD_A + h:SM_GD_A + h + 1]
            bcol = beta_t[:, SM_GD_B + h:SM_GD_B + h + 1]
            gc_r = gc_t[SM_GD_A + h:SM_GD_A + h + 1, :]
            gam_p = jnp.exp(jnp.where(tri_p, pick(chunks(gcol)) - gc_r, NEG_BIG))
            kk_p = pick(chunks(gram[e * R:(e + 1) * R]))
            qk_p = pick(chunks(gram[(2 + e) * R:(3 + e) * R]))
            a_ps.append(jnp.where(strict_p, pick(chunks(bcol)) * gam_p * kk_p, 0.0))
            p_bs.append((gam_p * qk_p).astype(bf16))
            gcols.append(gcol)
            bcols.append(bcol)

    ads = [jnp.where(blk8_p, a, 0.0) for a in a_ps]
    a2s = [pdot(ad, ad) for ad in ads]
    a4s = [pdot(a2, a2) for a2 in a2s]
    xs = [pdot(eye_p - ad, eye_p + a2) for ad, a2 in zip(ads, a2s)]
    xs = [pdot(x, eye_p + a4) for x, a4 in zip(xs, a4s)]
    for sh in range(3, lsh):
        msk = ((pt >> (sh + 1)) == (ps >> (sh + 1))) & (((pt >> sh) & 1) == 1) & (((ps >> sh) & 1) == 0)
        ts = [pdot(jnp.where(msk, a, 0.0), x) for a, x in zip(a_ps, xs)]
        xs = [x - pdot(x, t) for x, t in zip(xs, ts)]

    us, ws, qgs, kds, gls = [], [], [], [], []
    for j in range(nsl):
        uw, egs, eds, glh = [], [], [], []
        for e in range(2):
            h = 2 * j + e
            gcol, bcol = gcols[h], bcols[h]
            eg = jnp.exp(gcol)
            rhs = jnp.concatenate([bcol * vss[j], (bcol * eg) * kns[j]], axis=1)
            uw.append(_mdot([bd(p) for p in _split_bf16(xs[h], passes)], _split_bf16(rhs, passes)))
            lasts = [gcol[c * L + L - 1:c * L + L] for c in range(nch)]
            gcl = jnp.concatenate([jnp.broadcast_to(v, (L, 1)) for v in lasts], axis=0)
            egs.append(eg)
            eds.append(jnp.exp(gcl - gcol))
            glh.append([jnp.exp(v) for v in lasts])
        us.append(jnp.where(lo_r, uw[0][:, :LANES], uw[1][:, :LANES]))
        ws.append(jnp.where(lo_r, uw[0][:, LANES:], uw[1][:, LANES:]))
        qgs.append(qns[j] * jnp.where(lo_r, egs[0], egs[1]))
        kds.append(kns[j] * jnp.where(lo_r, eds[0], eds[1]))
        gls.append([jnp.where(lane_row < GD_DK, glh[0][c], glh[1][c]) for c in range(nch)])

    states = [s_scr[j] for j in range(nsl)]
    outs = [[] for _ in range(nsl)]
    for c in range(nch):
        rs = slice(c * L, (c + 1) * L)
        for j in range(nsl):
            s_prev = states[j]
            sb = s_prev.astype(bf16)
            delta = us[j][rs] - _bdot(ws[j][rs], sb)
            db = delta.astype(bf16)
            pieces = []
            if c > 0:
                pieces.append(jnp.zeros((c * L, LANES), bf16))
            pieces.append(db)
            if c < nch - 1:
                pieces.append(jnp.zeros(((nch - 1 - c) * L, LANES), bf16))
            dpad = jnp.concatenate(pieces, axis=0) if len(pieces) > 1 else db
            intra = jnp.where(lo_l, jnp.dot(p_bs[2 * j], dpad, preferred_element_type=f32),
                              jnp.dot(p_bs[2 * j + 1], dpad, preferred_element_type=f32))
            outs[j].append(_bdot(qgs[j][rs], sb) + intra)
            upd = _bdot_tn(kds[j][rs], db)
            states[j] = gls[j][c] * s_prev + jnp.where(blockdiag, upd, 0.0)

    for j in range(nsl):
        sl = slice(LANES * j, LANES * (j + 1))
        s_scr[j] = states[j]
        o = jnp.concatenate(outs[j], axis=0) if nch > 1 else outs[j][0]
        m_lo, m_hi = half_sums(o * o, lo_r)
        inv = jnp.where(lo_r, lax.rsqrt(m_lo * (1.0 / GD_DV) + RMS_EPS), lax.rsqrt(m_hi * (1.0 / GD_DV) + RMS_EPS))
        zs = z_ref[:, sl]
        o_ref[:, sl] = (o * inv * g_ref[:, sl] * (zs * _sigmoid(zs))).astype(o_ref.dtype)


def _gdn(proj, B, S, conv_w, a_log, dt_bias, norm_g):
    T = B * S
    L = min(GD_ROWS, S)
    nc = S // L
    KW = GD_K_W
    alog = jnp.zeros((1, LANES), f32).at[0, SM_GD_A:SM_GD_A + GD_HEADS].set(a_log)
    dtb = jnp.zeros((1, LANES), f32).at[0, SM_GD_A:SM_GD_A + GD_HEADS].set(dt_bias)
    g_row = jnp.tile(norm_g, GD_HEADS).reshape(1, GD_V_W)
    full = lambda shp: pl.BlockSpec(shp, lambda b, c: (0,) * len(shp))
    blk = lambda w, col: pl.BlockSpec((L, w), lambda b, c: (b * nc + c, col // w))
    return pl.pallas_call(
        _gd_kernel,
        out_shape=jax.ShapeDtypeStruct((T, GD_V_W), bf16),
        grid=(B, nc),
        in_specs=[blk(KW, COL_GD_Q), blk(KW, COL_GD_K), blk(KW, COL_GD_V), blk(KW, COL_GD_Z),
                  blk(LANES, COL_SMALL), full((CONV_WIDTH, 3 * KW)), full((1, LANES)), full((1, LANES)),
                  full((1, GD_V_W))],
        out_specs=pl.BlockSpec((L, GD_V_W), lambda b, c: (b * nc + c, 0)),
        scratch_shapes=[pltpu.VMEM((L + SUBLANES, 3 * KW), f32),
                        pltpu.VMEM((KW // LANES, LANES, LANES), f32)],
        compiler_params=_cparams("parallel", "arbitrary"),
        name="gated_deltanet",
    )(proj, proj, proj, proj, proj, conv_w, alog, dtb, g_row)


def _merge_kernel(ha_ref, hb_ref, hc_ref, g0_ref, g1_ref, g2_ref, x_ref,
                  wa_ref, wb_ref, wc_ref, wo_ref, lg_ref, lb_ref, x1_ref, x1b_ref):
    bra = jnp.dot(ha_ref[...], wa_ref[...], preferred_element_type=f32)
    brb = jnp.dot(hb_ref[...], wb_ref[...], preferred_element_type=f32)
    brc = jnp.dot(hc_ref[...], wc_ref[...], preferred_element_type=f32)
    merged = _sigmoid(g0_ref[...]) * bra + _sigmoid(g1_ref[...]) * brb + _sigmoid(g2_ref[...]) * brc
    y = jnp.dot(merged.astype(bf16), wo_ref[...], preferred_element_type=f32)
    out = _layer_norm(DN_ALPHA * x_ref[...] + y, lg_ref[...], lb_ref[...])
    x1_ref[...] = out
    x1b_ref[...] = out.astype(bf16)


def _merge(ha, hb, hc, proj, x2d, wa, wb, wc, wo, lg, lb):
    T = x2d.shape[0]
    D = D_MODEL
    tm = min(256, T)
    full = lambda shp: pl.BlockSpec(shp, lambda i: (0,) * len(shp))
    rows = lambda w: pl.BlockSpec((tm, w), lambda i: (i, 0))
    gate = lambda n: pl.BlockSpec((tm, D), lambda i: (i, COL_GATE // D + n))
    return pl.pallas_call(
        _merge_kernel,
        out_shape=(jax.ShapeDtypeStruct((T, D), f32), jax.ShapeDtypeStruct((T, D), bf16)),
        grid=(T // tm,),
        in_specs=[rows(LRU_WIDTH), rows(ML_V_W), rows(GD_V_W), gate(0), gate(1), gate(2), rows(D),
                  full((LRU_WIDTH, D)), full((ML_V_W, D)), full((GD_V_W, D)), full((D, D)),
                  full((1, D)), full((1, D))],
        out_specs=(rows(D), rows(D)),
        compiler_params=_cparams("parallel"),
        name="merge_outproj_ln",
    )(ha, hb, hc, proj, proj, proj, x2d, wa.astype(bf16), wb.astype(bf16), wc.astype(bf16),
      wo.astype(bf16), lg.reshape(1, D), lb.reshape(1, D))


PK_IDX = 0
PK_GATE = 4
PK_RANK = 8


def _router_kernel(x_ref, rw_ref, rb_ref, pk_ref, cnt_ref, carry):
    tr = x_ref.shape[0]

    @pl.when(pl.program_id(0) == 0)
    def _():
        carry[...] = jnp.zeros(carry.shape, f32)

    logits = _mdot(_split_bf16(x_ref[...], 3), _split_bf16(rw_ref[...], 3)) + rb_ref[...]
    lane = lax.broadcasted_iota(jnp.int32, (tr, LANES), 1).astype(f32)
    vals = logits
    idxs, tops = [], []
    for _ in range(TOP_K):
        m = jnp.max(vals, axis=1, keepdims=True)
        idx = jnp.min(jnp.where(vals == m, lane, float(LANES)), axis=1, keepdims=True)
        idxs.append(idx)
        tops.append(m)
        vals = jnp.where(lane == idx, NEG_BIG * 2.0, vals)
    es = [jnp.exp(t - tops[0]) for t in tops]
    tot = es[0] + es[1] + es[2] + es[3]
    onehots = [lane == idx for idx in idxs]
    sel = jnp.zeros((tr, LANES), f32)
    for oh in onehots:
        sel = sel + oh.astype(f32)
    ri = lax.broadcasted_iota(jnp.int32, (tr, tr), 0)
    ci = lax.broadcasted_iota(jnp.int32, (tr, tr), 1)
    before = jnp.dot((ri > ci).astype(bf16), sel.astype(bf16), preferred_element_type=f32) + carry[...]
    carry[...] = carry[...] + jnp.sum(sel, axis=0, keepdims=True)
    cnt_ref[...] = carry[...]
    packed = jnp.zeros((tr, LANES), f32)
    for kk in range(TOP_K):
        rank = jnp.sum(jnp.where(onehots[kk], before, 0.0), axis=1, keepdims=True)
        packed = jnp.where(lane == float(PK_IDX + kk), idxs[kk], packed)
        packed = jnp.where(lane == float(PK_GATE + kk), es[kk] / tot, packed)
        packed = jnp.where(lane == float(PK_RANK + kk), rank, packed)
    pk_ref[...] = packed


def _router(x1, router_w, router_b):
    T = x1.shape[0]
    tr = min(256, T)
    rw = jnp.zeros((D_MODEL, LANES), f32).at[:, :N_EXPERTS].set(router_w)
    rb = jnp.full((1, LANES), NEG_BIG, f32).at[0, :N_EXPERTS].set(router_b)
    full = lambda shp: pl.BlockSpec(shp, lambda i: (0,) * len(shp))
    return pl.pallas_call(
        _router_kernel,
        out_shape=(jax.ShapeDtypeStruct((T, LANES), f32), jax.ShapeDtypeStruct((1, LANES), f32)),
        grid=(T // tr,),
        in_specs=[pl.BlockSpec((tr, D_MODEL), lambda i: (i, 0)), full((D_MODEL, LANES)), full((1, LANES))],
        out_specs=(pl.BlockSpec((tr, LANES), lambda i: (i, 0)), full((1, LANES))),
        scratch_shapes=[pltpu.VMEM((1, LANES), f32)],
        compiler_params=_cparams("arbitrary"),
        name="router",
    )(x1, rw, rb)


def _expert_kernel(be_ref, nu_ref, x_ref, w1_ref, b1_ref, w2_ref, b2_ref, o_ref, w1b, w2b):
    i = pl.program_id(0)
    used = i < nu_ref[0]
    new_expert = jnp.logical_or(i == 0, be_ref[i] != be_ref[jnp.maximum(i - 1, 0)])

    @pl.when(jnp.logical_and(used, new_expert))
    def _():
        w1b[...] = w1_ref[0, 0].astype(bf16)
        w2b[...] = w2_ref[0, 0].astype(bf16)

    @pl.when(used)
    def _():
        hdn = jnp.dot(x_ref[...], w1b[...], preferred_element_type=f32) + b1_ref[0, 0]
        glu = jnp.minimum(hdn[:, :D_FF], SWIGLU_LIMIT)
        lin = jnp.clip(hdn[:, D_FF:], -SWIGLU_LIMIT, SWIGLU_LIMIT)
        act = glu * _sigmoid(SWIGLU_ALPHA * glu) * (lin + 1.0)
        y = jnp.dot(act.astype(bf16), w2b[...], preferred_element_type=f32) + b2_ref[0, 0]
        o_ref[...] = y.astype(o_ref.dtype)

    @pl.when(i >= nu_ref[0])
    def _():
        o_ref[...] = jnp.zeros(o_ref.shape, o_ref.dtype)


def _experts(xs, block_e, n_used, layer, w1, b1, w2, b2):
    P = xs.shape[0]
    nb = P // MOE_BLOCK
    D = D_MODEL
    nl = w1.shape[0]
    grid_spec = pltpu.PrefetchScalarGridSpec(
        num_scalar_prefetch=2,
        grid=(nb,),
        in_specs=[pl.BlockSpec((MOE_BLOCK, D), lambda i, be, nu: (i, 0)),
                  pl.BlockSpec((1, 1, D, 2 * D_FF), lambda i, be, nu: (layer, be[i], 0, 0)),
                  pl.BlockSpec((1, 1, 1, 2 * D_FF), lambda i, be, nu: (layer, be[i], 0, 0)),
                  pl.BlockSpec((1, 1, D_FF, D), lambda i, be, nu: (layer, be[i], 0, 0)),
                  pl.BlockSpec((1, 1, 1, D), lambda i, be, nu: (layer, be[i], 0, 0))],
        out_specs=pl.BlockSpec((MOE_BLOCK, D), lambda i, be, nu: (i, 0)),
        scratch_shapes=[pltpu.VMEM((D, 2 * D_FF), bf16), pltpu.VMEM((D_FF, D), bf16)],
    )
    return pl.pallas_call(
        _expert_kernel,
        out_shape=jax.ShapeDtypeStruct((P, D), bf16),
        grid_spec=grid_spec,
        compiler_params=_cparams("arbitrary"),
        name="experts",
    )(block_e, n_used, xs, w1, b1.reshape(nl, N_EXPERTS, 1, 2 * D_FF), w2, b2.reshape(nl, N_EXPERTS, 1, D))


def _combine_kernel(y0_ref, y1_ref, y2_ref, y3_ref, pk_ref, x_ref, lg_ref, lb_ref, o_ref):
    pk = pk_ref[...]
    y = pk[:, PK_GATE:PK_GATE + 1] * y0_ref[...].astype(f32)
    for kk, y_ref in ((1, y1_ref), (2, y2_ref), (3, y3_ref)):
        y = y + pk[:, PK_GATE + kk:PK_GATE + kk + 1] * y_ref[...].astype(f32)
    o_ref[...] = _layer_norm(DN_ALPHA * x_ref[...] + y, lg_ref[...], lb_ref[...])


def _combine(yg, packed, x1, lg, lb):
    T, D = x1.shape
    tm = min(256, T)
    nt = T // tm
    full = lambda shp: pl.BlockSpec(shp, lambda i: (0,) * len(shp))
    choice = lambda kk: pl.BlockSpec((tm, D), lambda i: (kk * nt + i, 0))
    return pl.pallas_call(
        _combine_kernel,
        out_shape=jax.ShapeDtypeStruct((T, D), f32),
        grid=(nt,),
        in_specs=[choice(0), choice(1), choice(2), choice(3), pl.BlockSpec((tm, LANES), lambda i: (i, 0)),
                  pl.BlockSpec((tm, D), lambda i: (i, 0)), full((1, D)), full((1, D))],
        out_specs=pl.BlockSpec((tm, D), lambda i: (i, 0)),
        compiler_params=_cparams("parallel"),
        name="combine_ln",
    )(yg, yg, yg, yg, packed, x1, lg.reshape(1, D), lb.reshape(1, D))


def _moe(x1, x1b, router_w, router_b, layer, w1, b1, w2, b2, lg, lb):
    T, D = x1.shape
    A = T * TOP_K
    packed, cnt = _router(x1, router_w, router_b)
    idx = packed[:, PK_IDX:PK_IDX + TOP_K].astype(jnp.int32)
    rank = packed[:, PK_RANK:PK_RANK + TOP_K].astype(jnp.int32)
    counts = cnt[0, :N_EXPERTS].astype(jnp.int32)
    padded = ((counts + MOE_BLOCK - 1) // MOE_BLOCK) * MOE_BLOCK
    pad_end = jnp.cumsum(padded)
    pad_start = pad_end - padded
    pos = (pad_start[idx] + rank).reshape(A)
    n_blocks = -(-A // MOE_BLOCK) + N_EXPERTS
    P = n_blocks * MOE_BLOCK
    tok = jnp.repeat(jnp.arange(T, dtype=jnp.int32), TOP_K)
    tok_buf = jnp.zeros((P,), jnp.int32).at[pos].set(tok)
    starts = jnp.arange(n_blocks, dtype=jnp.int32) * MOE_BLOCK
    block_e = jnp.minimum(jnp.sum((pad_end[None, :] <= starts[:, None]).astype(jnp.int32), axis=1), N_EXPERTS - 1)
    n_used = (pad_end[-1:] // MOE_BLOCK).astype(jnp.int32)
    xs = x1b.at[tok_buf].get(mode="promise_in_bounds")
    ys = _experts(xs, block_e, n_used, layer, w1, b1, w2, b2)
    yg = ys.at[pos.reshape(T, TOP_K).T.reshape(A)].get(mode="promise_in_bounds")
    return _combine(yg, packed, x1, lg, lb)


def _layer(x2d, B, S, p, layer, stacked):
    proj = _proj(x2d, _arrange_w_in(p['w_in']))
    ha = _lru(proj, B, S, p['lru_conv_w'], p['lru_conv_b'], p['lru_wa'], p['lru_ba'], p['lru_wx'],
              p['lru_bx'], p['lru_lambda'])
    hb = _mlstm(proj, B, S, p['ml_i_bias'], p['ml_f_bias'], p['ml_norm_g'])
    hc = _gdn(proj, B, S, p['gd_conv_w'], p['gd_a_log'], p['gd_dt_bias'], p['gd_norm_g'])
    x1, x1b = _merge(ha, hb, hc, proj, x2d, p['w_br_lru'], p['w_br_ml'], p['w_br_gd'], p['w_out'],
                     p['ln1_g'], p['ln1_b'])
    return _moe(x1, x1b, p['router_w'], p['router_b'], layer, stacked['exp_w1'], stacked['exp_b1'],
                stacked['exp_w2'], stacked['exp_b2'], p['ln2_g'], p['ln2_b'])


def kernel(x, w_in, lru_conv_w, lru_conv_b, lru_wa, lru_ba, lru_wx, lru_bx, lru_lambda, ml_i_bias, ml_f_bias, ml_norm_g, gd_conv_w, gd_a_log, gd_dt_bias, gd_norm_g, w_br_lru, w_br_ml, w_br_gd, w_out, ln1_g, ln1_b, router_w, router_b, exp_w1, exp_b1, exp_w2, exp_b2, ln2_g, ln2_b):
    B, S, D = x.shape
    params = dict(w_in=w_in, lru_conv_w=lru_conv_w, lru_conv_b=lru_conv_b, lru_wa=lru_wa, lru_ba=lru_ba,
                  lru_wx=lru_wx, lru_bx=lru_bx, lru_lambda=lru_lambda, ml_i_bias=ml_i_bias,
                  ml_f_bias=ml_f_bias, ml_norm_g=ml_norm_g, gd_conv_w=gd_conv_w, gd_a_log=gd_a_log,
                  gd_dt_bias=gd_dt_bias, gd_norm_g=gd_norm_g, w_br_lru=w_br_lru, w_br_ml=w_br_ml,
                  w_br_gd=w_br_gd, w_out=w_out, ln1_g=ln1_g, ln1_b=ln1_b, router_w=router_w,
                  router_b=router_b, ln2_g=ln2_g, ln2_b=ln2_b)
    stacked = dict(exp_w1=exp_w1, exp_b1=exp_b1, exp_w2=exp_w2, exp_b2=exp_b2)
    h = x.reshape(B * S, D)
    for l in range(w_in.shape[0]):
        h = _layer(h, B, S, {k: v[l] for k, v in params.items()}, l, stacked)
    return h.reshape(B, S, D)
```

```python
import functools
import math

import jax
import jax.numpy as jnp
from jax import lax
from jax.experimental import pallas as pl
from jax.experimental.pallas import tpu as pltpu

f32 = jnp.float32
bf16 = jnp.bfloat16

D_MODEL = 1024
LRU_WIDTH = 1024
LRU_BLOCKS = 8
LRU_C = 8.0
CONV_WIDTH = 4
ML_HEADS = 4
ML_DQK = 64
ML_DV = 128
GD_HEADS = 8
GD_DK = 64
GD_DV = 64
N_EXPERTS = 32
TOP_K = 4
D_FF = 1024
SWIGLU_LIMIT = 7.0
SWIGLU_ALPHA = 1.702
MOE_BLOCK = 512
N_BRANCH = 3
DEPTH = 2
DN_ALPHA = (2.0 * DEPTH) ** 0.25
LN_EPS = 1e-5
RMS_EPS = 1e-6

ML_QK_W = ML_HEADS * ML_DQK
ML_V_W = ML_HEADS * ML_DV
GD_K_W = GD_HEADS * GD_DK
GD_V_W = GD_HEADS * GD_DV
IN_SPLITS = (LRU_WIDTH, LRU_WIDTH, ML_QK_W, ML_QK_W, ML_V_W, ML_HEADS, ML_HEADS, ML_V_W,
             GD_K_W, GD_K_W, GD_V_W, GD_HEADS, GD_HEADS, GD_V_W, N_BRANCH * D_MODEL)

LANES = 128
SUBLANES = 8
NEG_BIG = -1e30

COL_LRU_X = 0
COL_LRU_Y = 1024
COL_GATE = 2048
COL_ML_Q = 5120
COL_ML_K = 5376
COL_ML_V = 5632
COL_ML_O = 6144
COL_GD_Q = 6656
COL_GD_K = 7168
COL_GD_V = 7680
COL_GD_Z = 8192
COL_SMALL = 8704
PROJ_W = 8832
SM_ML_I = 0
SM_ML_F = 4
SM_GD_A = 8
SM_GD_B = 16

ML_CHUNK = 256
GD_CHUNK = 64
GD_ROWS = 256
GD_SOLVE_PASSES = 1
VMEM_LIMIT = 56 * 1024 * 1024


def _cparams(*sem):
    return pltpu.CompilerParams(dimension_semantics=sem, vmem_limit_bytes=VMEM_LIMIT)


def _hdot(a, b):
    return jnp.dot(a, b, precision=lax.Precision.HIGHEST, preferred_element_type=f32)


def _bdot(a, b):
    return jnp.dot(a.astype(bf16), b.astype(bf16), preferred_element_type=f32)


def _bdot_nt(a, b):
    return lax.dot_general(a.astype(bf16), b.astype(bf16), (((1,), (1,)), ((), ())),
                           preferred_element_type=f32)


def _bdot_tn(a, b):
    return lax.dot_general(a.astype(bf16), b.astype(bf16), (((0,), (0,)), ((), ())),
                           preferred_element_type=f32)


def _sigmoid(x):
    return 1.0 / (1.0 + jnp.exp(-x))


def _softplus(x):
    return jnp.maximum(x, 0.0) + jnp.log1p(jnp.exp(-jnp.abs(x)))


def _log_sigmoid(x):
    return jnp.minimum(x, 0.0) - jnp.log1p(jnp.exp(-jnp.abs(x)))


def _layer_norm(z, g, b):
    mu = jnp.mean(z, axis=-1, keepdims=True)
    zc = z - mu
    var = jnp.mean(zc * zc, axis=-1, keepdims=True)
    return zc * lax.rsqrt(var + LN_EPS) * g + b


def _proj_kernel(x_ref, w_ref, o_ref):
    o_ref[...] = jnp.dot(x_ref[...].astype(bf16), w_ref[...], preferred_element_type=f32)


def _proj(x2d, w_arr):
    T = x2d.shape[0]
    tm = min(512, T)
    tn = PROJ_W // 3
    return pl.pallas_call(
        _proj_kernel,
        out_shape=jax.ShapeDtypeStruct((T, PROJ_W), f32),
        grid=(PROJ_W // tn, T // tm),
        in_specs=[pl.BlockSpec((tm, D_MODEL), lambda j, i: (i, 0)),
                  pl.BlockSpec((D_MODEL, tn), lambda j, i: (0, j))],
        out_specs=pl.BlockSpec((tm, tn), lambda j, i: (i, j)),
        compiler_params=_cparams("parallel", "parallel"),
        name="in_proj",
    )(x2d, w_arr)


def _arrange_w_in(w):
    pts = []
    acc = 0
    for s in IN_SPLITS[:-1]:
        acc += s
        pts.append(acc)
    (lru_x, lru_y, ml_q, ml_k, ml_v, ml_i, ml_f, ml_o,
     gd_q, gd_k, gd_v, gd_a, gd_b, gd_z, gate) = jnp.split(w, pts, axis=1)
    pad = jnp.zeros((w.shape[0], LANES - 2 * ML_HEADS - 2 * GD_HEADS), w.dtype)
    small = jnp.concatenate([ml_i, ml_f, gd_a, gd_b, pad], axis=1)
    out = jnp.concatenate([lru_x, lru_y, gate, ml_q, ml_k, ml_v, ml_o, gd_q, gd_k, gd_v, gd_z, small], axis=1)
    return out.astype(bf16)


def _lru_kernel(x_ref, y_ref, cw_ref, cb_ref, wcat_ref, ba_ref, bx_ref, lam_ref, o_ref,
                xext, a_scr, u_scr, carry):
    ts = x_ref.shape[0]
    W = x_ref.shape[1]
    bw = W // LRU_BLOCKS

    @pl.when(pl.program_id(1) == 0)
    def _():
        xext[0:SUBLANES, :] = jnp.zeros((SUBLANES, W), f32)
        carry[...] = jnp.zeros((1, W), f32)

    x = x_ref[...]
    xext[SUBLANES:SUBLANES + ts, :] = x
    xa = (cw_ref[3:4, :] * x
          + cw_ref[2:3, :] * xext[SUBLANES - 1:SUBLANES - 1 + ts, :]
          + cw_ref[1:2, :] * xext[SUBLANES - 2:SUBLANES - 2 + ts, :]
          + cw_ref[0:1, :] * xext[SUBLANES - 3:SUBLANES - 3 + ts, :]) + cb_ref[...]
    xext[0:SUBLANES, :] = x_ref[ts - SUBLANES:ts, :]

    cdec = -LRU_C * _softplus(-lam_ref[...])
    for h in range(LRU_BLOCKS):
        sl = slice(bw * h, bw * (h + 1))
        xh = xa[:, sl]
        g = jnp.dot(xh.astype(bf16), wcat_ref[h], preferred_element_type=f32)
        r = _sigmoid(g[:, :bw] + ba_ref[:, sl])
        ig = _sigmoid(g[:, bw:] + bx_ref[:, sl])
        log_a = r * cdec[:, sl]
        a = jnp.exp(log_a)
        a_scr[:, sl] = a
        u_scr[:, sl] = jnp.sqrt(jnp.tanh(-log_a) * (1.0 + a * a)) * (ig * xh)

    row = lax.broadcasted_iota(jnp.int32, (SUBLANES, W), 0)

    def body(g, cr):
        off = pl.multiple_of(g * SUBLANES, SUBLANES)
        A = a_scr[pl.ds(off, SUBLANES), :]
        U = u_scr[pl.ds(off, SUBLANES), :]
        for s in (1, 2, 4):
            a_sh = pltpu.roll(A, s, 0)
            u_sh = pltpu.roll(U, s, 0)
            m = row >= s
            U = jnp.where(m, A * u_sh + U, U)
            A = jnp.where(m, A * a_sh, A)
        H = A * cr + U
        u_scr[pl.ds(off, SUBLANES), :] = H
        return H[SUBLANES - 1:SUBLANES, :]

    carry[...] = lax.fori_loop(0, ts // SUBLANES, body, carry[...], unroll=2)
    o_ref[...] = (u_scr[...] * jax.nn.gelu(y_ref[...])).astype(o_ref.dtype)


def _lru(proj, B, S, cw, cb, wa, ba, wx, bx, lam):
    T = B * S
    ts = min(512, S)
    nt = S // ts
    W = LRU_WIDTH
    wcat = jnp.concatenate([wa, wx], axis=-1).astype(bf16)
    row = lambda v: v.reshape(1, W)
    full = lambda shp: pl.BlockSpec(shp, lambda b, c: (0,) * len(shp))
    return pl.pallas_call(
        _lru_kernel,
        out_shape=jax.ShapeDtypeStruct((T, W), bf16),
        grid=(B, nt),
        in_specs=[pl.BlockSpec((ts, W), lambda b, c: (b * nt + c, COL_LRU_X // W)),
                  pl.BlockSpec((ts, W), lambda b, c: (b * nt + c, COL_LRU_Y // W)),
                  full((CONV_WIDTH, W)), full((1, W)), full(wcat.shape),
                  full((1, W)), full((1, W)), full((1, W))],
        out_specs=pl.BlockSpec((ts, W), lambda b, c: (b * nt + c, 0)),
        scratch_shapes=[pltpu.VMEM((ts + SUBLANES, W), f32), pltpu.VMEM((ts, W), f32),
                        pltpu.VMEM((ts, W), f32), pltpu.VMEM((1, W), f32)],
        compiler_params=_cparams("parallel", "arbitrary"),
        name="rg_lru",
    )(proj, proj, cw, row(cb), wcat, row(ba), row(bx), row(lam))


def _ml_kernel(q_ref, k_ref, v_ref, og_ref, sm_ref, bias_ref, g_ref, o_ref, c_scr, n_scr, m_scr):
    L = q_ref.shape[0]

    @pl.when(pl.program_id(1) == 0)
    def _():
        c_scr[...] = jnp.zeros(c_scr.shape, f32)
        n_scr[...] = jnp.zeros(n_scr.shape, f32)
        m_scr[...] = jnp.zeros(m_scr.shape, f32)

    sm = sm_ref[...] + bias_ref[...]
    logf = _log_sigmoid(sm)
    ri = lax.broadcasted_iota(jnp.int32, (L, L), 0)
    ci = lax.broadcasted_iota(jnp.int32, (L, L), 1)
    tri = ri >= ci
    b_all = _hdot(tri.astype(f32), logf)
    sm_t = sm.T
    b_t = b_all.T
    q = q_ref[...]
    k = k_ref[...]
    kb = k.astype(bf16)
    lane_head = lax.broadcasted_iota(jnp.int32, q.shape, 1) // ML_DQK
    scale = ML_DQK ** -0.5
    for h in range(ML_HEADS):
        sl = slice(ML_DV * h, ML_DV * (h + 1))
        ig_c = sm[:, SM_ML_I + h:SM_ML_I + h + 1]
        b_c = b_all[:, SM_ML_F + h:SM_ML_F + h + 1]
        ig_r = sm_t[SM_ML_I + h:SM_ML_I + h + 1, :]
        b_r = b_t[SM_ML_F + h:SM_ML_F + h + 1, :]
        b_last = b_c[L - 1:L, :]
        m_prev = m_scr[h:h + 1, 0:1]
        ge_r = b_last - b_r + ig_r
        ge_c = b_last - b_c + ig_c
        m_new = jnp.maximum(b_last + m_prev, jnp.max(ge_r, axis=1, keepdims=True))
        decay = jnp.exp(b_last + m_prev - m_new)
        w_c = jnp.exp(ge_c - m_new)
        dmat = jnp.where(tri, b_c - b_r + ig_r, NEG_BIG)
        inter = b_c + m_prev
        m_t = jnp.maximum(inter, jnp.max(dmat, axis=1, keepdims=True))
        dw = jnp.exp(dmat - m_t)
        iw = jnp.exp(inter - m_t)
        qm = jnp.where(lane_head == h, q, 0.0) * scale
        qmb = qm.astype(bf16)
        s = _bdot_nt(qmb, kb) * dw
        v_h = v_ref[:, sl]
        n_row = n_scr[h:h + 1, :]
        num = _bdot(s, v_h) + iw * _bdot(qmb, c_scr[:, sl])
        den = jnp.sum(s, axis=1, keepdims=True) + iw * jnp.sum(qm * n_row, axis=1, keepdims=True)
        hh = num / jnp.maximum(jnp.abs(den), jnp.exp(-m_t))
        c_scr[:, sl] = decay * c_scr[:, sl] + _bdot_tn(kb, w_c * v_h)
        n_scr[h:h + 1, :] = decay * n_row + jnp.sum(w_c * k, axis=0, keepdims=True)
        m_scr[h:h + 1, :] = jnp.broadcast_to(m_new, (1, LANES))
        ms = jnp.mean(hh * hh, axis=1, keepdims=True)
        hn = hh * lax.rsqrt(ms + RMS_EPS) * g_ref[:, sl]
        o_ref[:, sl] = (hn * _sigmoid(og_ref[:, sl])).astype(o_ref.dtype)


def _mlstm(proj, B, S, i_bias, f_bias, norm_g):
    T = B * S
    L = min(ML_CHUNK, S)
    nc = S // L
    bias = jnp.zeros((1, LANES), f32)
    bias = bias.at[0, SM_ML_I:SM_ML_I + ML_HEADS].set(i_bias).at[0, SM_ML_F:SM_ML_F + ML_HEADS].set(f_bias)
    full = lambda shp: pl.BlockSpec(shp, lambda b, c: (0,) * len(shp))
    blk = lambda w, col: pl.BlockSpec((L, w), lambda b, c: (b * nc + c, col // w))
    return pl.pallas_call(
        _ml_kernel,
        out_shape=jax.ShapeDtypeStruct((T, ML_V_W), bf16),
        grid=(B, nc),
        in_specs=[blk(ML_QK_W, COL_ML_Q), blk(ML_QK_W, COL_ML_K), blk(ML_V_W, COL_ML_V),
                  blk(ML_V_W, COL_ML_O), blk(LANES, COL_SMALL), full((1, LANES)), full((1, ML_V_W))],
        out_specs=pl.BlockSpec((L, ML_V_W), lambda b, c: (b * nc + c, 0)),
        scratch_shapes=[pltpu.VMEM((ML_QK_W, ML_V_W), f32), pltpu.VMEM((SUBLANES, ML_QK_W), f32),
                        pltpu.VMEM((SUBLANES, LANES), f32)],
        compiler_params=_cparams("parallel", "arbitrary"),
        name="mlstm",
    )(proj, proj, proj, proj, proj, bias, norm_g.reshape(1, ML_V_W))


def _split_bf16(x, passes):
    hi = x.astype(bf16)
    if passes == 1:
        return (hi,)
    return (hi, (x - hi.astype(f32)).astype(bf16))


def _mdot(xs, ys):
    out = jnp.dot(xs[0], ys[0], preferred_element_type=f32)
    if len(xs) > 1:
        out = out + jnp.dot(xs[1], ys[0], preferred_element_type=f32)
        out = out + jnp.dot(xs[0], ys[1], preferred_element_type=f32)
    return out


def _gd_kernel(q_ref, k_ref, v_ref, z_ref, sm_ref, cw_ref, alog_ref, dtb_ref, g_ref, o_ref, ext, s_scr):
    R = q_ref.shape[0]
    KW = q_ref.shape[1]
    L = GD_CHUNK
    nch = R // L
    lsh = L.bit_length() - 1
    PW = nch * L
    nsl = KW // LANES
    passes = GD_SOLVE_PASSES

    @pl.when(pl.program_id(1) == 0)
    def _():
        ext[0:SUBLANES, :] = jnp.zeros((SUBLANES, ext.shape[1]), f32)
        s_scr[...] = jnp.zeros(s_scr.shape, f32)

    ext[SUBLANES:SUBLANES + R, 0:KW] = q_ref[...]
    ext[SUBLANES:SUBLANES + R, KW:2 * KW] = k_ref[...]
    ext[SUBLANES:SUBLANES + R, 2 * KW:3 * KW] = v_ref[...]
    xc = (cw_ref[3:4, :] * ext[SUBLANES:SUBLANES + R, :]
          + cw_ref[2:3, :] * ext[SUBLANES - 1:SUBLANES - 1 + R, :]
          + cw_ref[1:2, :] * ext[SUBLANES - 2:SUBLANES - 2 + R, :]
          + cw_ref[0:1, :] * ext[SUBLANES - 3:SUBLANES - 3 + R, :])
    ext[0:SUBLANES, :] = ext[R:R + SUBLANES, :]
    xc = xc * _sigmoid(xc)

    sm = sm_ref[...]
    g_t = -jnp.exp(alog_ref[...]) * _softplus(sm + dtb_ref[...])
    beta_t = _sigmoid(sm)
    rr = lax.broadcasted_iota(jnp.int32, (R, R), 0)
    cc = lax.broadcasted_iota(jnp.int32, (R, R), 1)
    cum = jnp.where(((rr >> lsh) == (cc >> lsh)) & (rr >= cc), 1.0, 0.0)
    gc_all = _hdot(cum, g_t)
    gc_t = gc_all.T

    pt = lax.broadcasted_iota(jnp.int32, (L, PW), 0)
    plane = lax.broadcasted_iota(jnp.int32, (L, PW), 1)
    ps = plane & (L - 1)
    pc = plane >> lsh
    tri_p = pt >= ps
    strict_p = pt > ps
    eye_p = jnp.where(pt == ps, 1.0, 0.0)
    blk8_p = (pt >> 3) == (ps >> 3)
    cmask_b = [jnp.where(pc == c, 1.0, 0.0).astype(bf16) for c in range(nch)]

    def pick(parts):
        out = parts[nch - 1]
        for c in range(nch - 2, -1, -1):
            out = jnp.where(pc == c, parts[c], out)
        return out

    def chunks(col):
        return [col[c * L:(c + 1) * L] for c in range(nch)]

    def bd(yb):
        return jnp.concatenate([yb * cmask_b[c] for c in range(nch)], axis=0)

    def pdot(xp, yp):
        return _mdot(_split_bf16(xp, passes), [bd(p) for p in _split_bf16(yp, passes)])

    lane_r = lax.broadcasted_iota(jnp.int32, (R, LANES), 1)
    lo_r = lane_r < GD_DK
    lane_l = lax.broadcasted_iota(jnp.int32, (L, LANES), 1)
    lo_l = lane_l < GD_DK
    lane_row = lax.broadcasted_iota(jnp.int32, (1, LANES), 1)
    r2 = lax.broadcasted_iota(jnp.int32, (LANES, LANES), 0)
    c2 = lax.broadcasted_iota(jnp.int32, (LANES, LANES), 1)
    blockdiag = (r2 < GD_DK) == (c2 < GD_DK)

    def half_sums(y, lo):
        s_lo = jnp.sum(jnp.where(lo, y, 0.0), axis=1, keepdims=True)
        s_hi = jnp.sum(jnp.where(lo, 0.0, y), axis=1, keepdims=True)
        return s_lo, s_hi

    def l2n(y):
        s_lo, s_hi = half_sums(y * y, lo_r)
        return y * jnp.where(lo_r, lax.rsqrt(s_lo + RMS_EPS), lax.rsqrt(s_hi + RMS_EPS))

    qns, kns, vss, a_ps, p_bs, gcols, bcols = [], [], [], [], [], [], []
    for j in range(nsl):
        qn = l2n(xc[:, LANES * j:LANES * (j + 1)]) * (GD_DK ** -0.5)
        kn = l2n(xc[:, KW + LANES * j:KW + LANES * (j + 1)])
        qns.append(qn)
        kns.append(kn)
        vss.append(xc[:, 2 * KW + LANES * j:2 * KW + LANES * (j + 1)])
        lhs = jnp.concatenate([jnp.where(lo_r, kn, 0.0), jnp.where(lo_r, 0.0, kn),
                               jnp.where(lo_r, qn, 0.0), jnp.where(lo_r, 0.0, qn)], axis=0)
        gram = _bdot_nt(lhs, kn)
        for e in range(2):
            h = 2 * j + e
            gcol = gc_all[:, SM_GD_A + h:SM_GD_A + h + 1]
            bcol = beta_t[:, SM_GD_B + h:SM_GD_B + h + 1]
            gc_r = gc_t[SM_GD_A + h:SM_GD_A + h + 1, :]
            gam_p = jnp.exp(jnp.where(tri_p, pick(chunks(gcol)) - gc_r, NEG_BIG))
            kk_p = pick(chunks(gram[e * R:(e + 1) * R]))
            qk_p = pick(chunks(gram[(2 + e) * R:(3 + e) * R]))
            a_ps.append(jnp.where(strict_p, pick(chunks(bcol)) * gam_p * kk_p, 0.0))
            p_bs.append((gam_p * qk_p).astype(bf16))
            gcols.append(gcol)
            bcols.append(bcol)

    ads = [jnp.where(blk8_p, a, 0.0) for a in a_ps]
    a2s = [pdot(ad, ad) for ad in ads]
    a4s = [pdot(a2, a2) for a2 in a2s]
    xs = [pdot(eye_p - ad, eye_p + a2) for ad, a2 in zip(ads, a2s)]
    xs = [pdot(x, eye_p + a4) for x, a4 in zip(xs, a4s)]
    for sh in range(3, lsh):
        msk = ((pt >> (sh + 1)) == (ps >> (sh + 1))) & (((pt >> sh) & 1) == 1) & (((ps >> sh) & 1) == 0)
        ts = [pdot(jnp.where(msk, a, 0.0), x) for a, x in zip(a_ps, xs)]
        xs = [x - pdot(x, t) for x, t in zip(xs, ts)]

    us, ws, qgs, kds, gls = [], [], [], [], []
    for j in range(nsl):
        uw, egs, eds, glh = [], [], [], []
        for e in range(2):
            h = 2 * j + e
            gcol, bcol = gcols[h], bcols[h]
            eg = jnp.exp(gcol)
            rhs = jnp.concatenate([bcol * vss[j], (bcol * eg) * kns[j]], axis=1)
            uw.append(_mdot([bd(p) for p in _split_bf16(xs[h], passes)], _split_bf16(rhs, passes)))
            lasts = [gcol[c * L + L - 1:c * L + L] for c in range(nch)]
            gcl = jnp.concatenate([jnp.broadcast_to(v, (L, 1)) for v in lasts], axis=0)
            egs.append(eg)
            eds.append(jnp.exp(gcl - gcol))
            glh.append([jnp.exp(v) for v in lasts])
        us.append(jnp.where(lo_r, uw[0][:, :LANES], uw[1][:, :LANES]))
        ws.append(jnp.where(lo_r, uw[0][:, LANES:], uw[1][:, LANES:]))
        qgs.append(qns[j] * jnp.where(lo_r, egs[0], egs[1]))
        kds.append(kns[j] * jnp.where(lo_r, eds[0], eds[1]))
        gls.append([jnp.where(lane_row < GD_DK, glh[0][c], glh[1][c]) for c in range(nch)])

    states = [s_scr[j] for j in range(nsl)]
    outs = [[] for _ in range(nsl)]
    for c in range(nch):
        rs = slice(c * L, (c + 1) * L)
        for j in range(nsl):
            s_prev = states[j]
            sb = s_prev.astype(bf16)
            delta = us[j][rs] - _bdot(ws[j][rs], sb)
            db = delta.astype(bf16)
            pieces = []
            if c > 0:
                pieces.append(jnp.zeros((c * L, LANES), bf16))
            pieces.append(db)
            if c < nch - 1:
                pieces.append(jnp.zeros(((nch - 1 - c) * L, LANES), bf16))
            dpad = jnp.concatenate(pieces, axis=0) if len(pieces) > 1 else db
            intra = jnp.where(lo_l, jnp.dot(p_bs[2 * j], dpad, preferred_element_type=f32),
                              jnp.dot(p_bs[2 * j + 1], dpad, preferred_element_type=f32))
            outs[j].append(_bdot(qgs[j][rs], sb) + intra)
            upd = _bdot_tn(kds[j][rs], db)
            states[j] = gls[j][c] * s_prev + jnp.where(blockdiag, upd, 0.0)

    for j in range(nsl):
        sl = slice(LANES * j, LANES * (j + 1))
        s_scr[j] = states[j]
        o = jnp.concatenate(outs[j], axis=0) if nch > 1 else outs[j][0]
        m_lo, m_hi = half_sums(o * o, lo_r)
        inv = jnp.where(lo_r, lax.rsqrt(m_lo * (1.0 / GD_DV) + RMS_EPS), lax.rsqrt(m_hi * (1.0 / GD_DV) + RMS_EPS))
        zs = z_ref[:, sl]
        o_ref[:, sl] = (o * inv * g_ref[:, sl] * (zs * _sigmoid(zs))).astype(o_ref.dtype)


def _gdn(proj, B, S, conv_w, a_log, dt_bias, norm_g):
    T = B * S
    L = min(GD_ROWS, S)
    nc = S // L
    KW = GD_K_W
    alog = jnp.zeros((1, LANES), f32).at[0, SM_GD_A:SM_GD_A + GD_HEADS].set(a_log)
    dtb = jnp.zeros((1, LANES), f32).at[0, SM_GD_A:SM_GD_A + GD_HEADS].set(dt_bias)
    g_row = jnp.tile(norm_g, GD_HEADS).reshape(1, GD_V_W)
    full = lambda shp: pl.BlockSpec(shp, lambda b, c: (0,) * len(shp))
    blk = lambda w, col: pl.BlockSpec((L, w), lambda b, c: (b * nc + c, col // w))
    return pl.pallas_call(
        _gd_kernel,
        out_shape=jax.ShapeDtypeStruct((T, GD_V_W), bf16),
        grid=(B, nc),
        in_specs=[blk(KW, COL_GD_Q), blk(KW, COL_GD_K), blk(KW, COL_GD_V), blk(KW, COL_GD_Z),
                  blk(LANES, COL_SMALL), full((CONV_WIDTH, 3 * KW)), full((1, LANES)), full((1, LANES)),
                  full((1, GD_V_W))],
        out_specs=pl.BlockSpec((L, GD_V_W), lambda b, c: (b * nc + c, 0)),
        scratch_shapes=[pltpu.VMEM((L + SUBLANES, 3 * KW), f32),
                        pltpu.VMEM((KW // LANES, LANES, LANES), f32)],
        compiler_params=_cparams("parallel", "arbitrary"),
        name="gated_deltanet",
    )(proj, proj, proj, proj, proj, conv_w, alog, dtb, g_row)


def _merge_kernel(ha_ref, hb_ref, hc_ref, g0_ref, g1_ref, g2_ref, x_ref,
                  wa_ref, wb_ref, wc_ref, wo_ref, lg_ref, lb_ref, x1_ref, x1b_ref):
    bra = jnp.dot(ha_ref[...], wa_ref[...], preferred_element_type=f32)
    brb = jnp.dot(hb_ref[...], wb_ref[...], preferred_element_type=f32)
    brc = jnp.dot(hc_ref[...], wc_ref[...], preferred_element_type=f32)
    merged = _sigmoid(g0_ref[...]) * bra + _sigmoid(g1_ref[...]) * brb + _sigmoid(g2_ref[...]) * brc
    y = jnp.dot(merged.astype(bf16), wo_ref[...], preferred_element_type=f32)
    out = _layer_norm(DN_ALPHA * x_ref[...] + y, lg_ref[...], lb_ref[...])
    x1_ref[...] = out
    x1b_ref[...] = out.astype(bf16)


def _merge(ha, hb, hc, proj, x2d, wa, wb, wc, wo, lg, lb):
    T = x2d.shape[0]
    D = D_MODEL
    tm = min(256, T)
    full = lambda shp: pl.BlockSpec(shp, lambda i: (0,) * len(shp))
    rows = lambda w: pl.BlockSpec((tm, w), lambda i: (i, 0))
    gate = lambda n: pl.BlockSpec((tm, D), lambda i: (i, COL_GATE // D + n))
    return pl.pallas_call(
        _merge_kernel,
        out_shape=(jax.ShapeDtypeStruct((T, D), f32), jax.ShapeDtypeStruct((T, D), bf16)),
        grid=(T // tm,),
        in_specs=[rows(LRU_WIDTH), rows(ML_V_W), rows(GD_V_W), gate(0), gate(1), gate(2), rows(D),
                  full((LRU_WIDTH, D)), full((ML_V_W, D)), full((GD_V_W, D)), full((D, D)),
                  full((1, D)), full((1, D))],
        out_specs=(rows(D), rows(D)),
        compiler_params=_cparams("parallel"),
        name="merge_outproj_ln",
    )(ha, hb, hc, proj, proj, proj, x2d, wa.astype(bf16), wb.astype(bf16), wc.astype(bf16),
      wo.astype(bf16), lg.reshape(1, D), lb.reshape(1, D))


PK_IDX = 0
PK_GATE = 4
PK_RANK = 8


def _router_kernel(x_ref, rw_ref, rb_ref, pk_ref, cnt_ref, carry):
    tr = x_ref.shape[0]

    @pl.when(pl.program_id(0) == 0)
    def _():
        carry[...] = jnp.zeros(carry.shape, f32)

    logits = _mdot(_split_bf16(x_ref[...], 3), _split_bf16(rw_ref[...], 3)) + rb_ref[...]
    lane = lax.broadcasted_iota(jnp.int32, (tr, LANES), 1).astype(f32)
    vals = logits
    idxs, tops = [], []
    for _ in range(TOP_K):
        m = jnp.max(vals, axis=1, keepdims=True)
        idx = jnp.min(jnp.where(vals == m, lane, float(LANES)), axis=1, keepdims=True)
        idxs.append(idx)
        tops.append(m)
        vals = jnp.where(lane == idx, NEG_BIG * 2.0, vals)
    es = [jnp.exp(t - tops[0]) for t in tops]
    tot = es[0] + es[1] + es[2] + es[3]
    onehots = [lane == idx for idx in idxs]
    sel = jnp.zeros((tr, LANES), f32)
    for oh in onehots:
        sel = sel + oh.astype(f32)
    ri = lax.broadcasted_iota(jnp.int32, (tr, tr), 0)
    ci = lax.broadcasted_iota(jnp.int32, (tr, tr), 1)
    before = jnp.dot((ri > ci).astype(bf16), sel.astype(bf16), preferred_element_type=f32) + carry[...]
    carry[...] = carry[...] + jnp.sum(sel, axis=0, keepdims=True)
    cnt_ref[...] = carry[...]
    packed = jnp.zeros((tr, LANES), f32)
    for kk in range(TOP_K):
        rank = jnp.sum(jnp.where(onehots[kk], before, 0.0), axis=1, keepdims=True)
        packed = jnp.where(lane == float(PK_IDX + kk), idxs[kk], packed)
        packed = jnp.where(lane == float(PK_GATE + kk), es[kk] / tot, packed)
        packed = jnp.where(lane == float(PK_RANK + kk), rank, packed)
    pk_ref[...] = packed


def _router(x1, router_w, router_b):
    T = x1.shape[0]
    tr = min(256, T)
    rw = jnp.zeros((D_MODEL, LANES), f32).at[:, :N_EXPERTS].set(router_w)
    rb = jnp.full((1, LANES), NEG_BIG, f32).at[0, :N_EXPERTS].set(router_b)
    full = lambda shp: pl.BlockSpec(shp, lambda i: (0,) * len(shp))
    return pl.pallas_call(
        _router_kernel,
        out_shape=(jax.ShapeDtypeStruct((T, LANES), f32), jax.ShapeDtypeStruct((1, LANES), f32)),
        grid=(T // tr,),
        in_specs=[pl.BlockSpec((tr, D_MODEL), lambda i: (i, 0)), full((D_MODEL, LANES)), full((1, LANES))],
        out_specs=(pl.BlockSpec((tr, LANES), lambda i: (i, 0)), full((1, LANES))),
        scratch_shapes=[pltpu.VMEM((1, LANES), f32)],
        compiler_params=_cparams("arbitrary"),
        name="router",
    )(x1, rw, rb)


def _expert_kernel(be_ref, nu_ref, x_ref, w1_ref, b1_ref, w2_ref, b2_ref, o_ref, w1b, w2b):
    i = pl.program_id(0)
    used = i < nu_ref[0]
    new_expert = jnp.logical_or(i == 0, be_ref[i] != be_ref[jnp.maximum(i - 1, 0)])

    @pl.when(jnp.logical_and(used, new_expert))
    def _():
        w1b[...] = w1_ref[0, 0].astype(bf16)
        w2b[...] = w2_ref[0, 0].astype(bf16)

    @pl.when(used)
    def _():
        hdn = jnp.dot(x_ref[...], w1b[...], preferred_element_type=f32) + b1_ref[0, 0]
        glu = jnp.minimum(hdn[:, :D_FF], SWIGLU_LIMIT)
        lin = jnp.clip(hdn[:, D_FF:], -SWIGLU_LIMIT, SWIGLU_LIMIT)
        act = glu * _sigmoid(SWIGLU_ALPHA * glu) * (lin + 1.0)
        y = jnp.dot(act.astype(bf16), w2b[...], preferred_element_type=f32) + b2_ref[0, 0]
        o_ref[...] = y.astype(o_ref.dtype)

    @pl.when(i >= nu_ref[0])
    def _():
        o_ref[...] = jnp.zeros(o_ref.shape, o_ref.dtype)


def _experts(xs, block_e, n_used, layer, w1, b1, w2, b2):
    P = xs.shape[0]
    nb = P // MOE_BLOCK
    D = D_MODEL
    nl = w1.shape[0]
    grid_spec = pltpu.PrefetchScalarGridSpec(
        num_scalar_prefetch=2,
        grid=(nb,),
        in_specs=[pl.BlockSpec((MOE_BLOCK, D), lambda i, be, nu: (i, 0)),
                  pl.BlockSpec((1, 1, D, 2 * D_FF), lambda i, be, nu: (layer, be[i], 0, 0)),
                  pl.BlockSpec((1, 1, 1, 2 * D_FF), lambda i, be, nu: (layer, be[i], 0, 0)),
                  pl.BlockSpec((1, 1, D_FF, D), lambda i, be, nu: (layer, be[i], 0, 0)),
                  pl.BlockSpec((1, 1, 1, D), lambda i, be, nu: (layer, be[i], 0, 0))],
        out_specs=pl.BlockSpec((MOE_BLOCK, D), lambda i, be, nu: (i, 0)),
        scratch_shapes=[pltpu.VMEM((D, 2 * D_FF), bf16), pltpu.VMEM((D_FF, D), bf16)],
    )
    return pl.pallas_call(
        _expert_kernel,
        out_shape=jax.ShapeDtypeStruct((P, D), f32),
        grid_spec=grid_spec,
        compiler_params=_cparams("arbitrary"),
        name="experts",
    )(block_e, n_used, xs, w1, b1.reshape(nl, N_EXPERTS, 1, 2 * D_FF), w2, b2.reshape(nl, N_EXPERTS, 1, D))


def _combine_kernel(pos_cur, pos_nxt, ys_hbm, pk_ref, x_ref, lg_ref, lb_ref, o_ref, ybuf, sem):
    i = pl.program_id(0)
    n = pl.num_programs(0)
    tm = x_ref.shape[0]
    slot = lax.rem(i, 2)

    def row_copy(pos_ref, s, t, kk):
        p = pos_ref[0, 0, t * TOP_K + kk]
        return pltpu.make_async_copy(ys_hbm.at[pl.ds(p, 1)], ybuf.at[s, pl.ds(kk * tm + t, 1)], sem.at[s])

    def issue(pos_ref, s):
        def body(t, c):
            for kk in range(TOP_K):
                row_copy(pos_ref, s, t, kk).start()
            return c
        lax.fori_loop(0, tm, body, 0, unroll=4)

    @pl.when(i == 0)
    def _():
        issue(pos_cur, 0)

    @pl.when(i + 1 < n)
    def _():
        issue(pos_nxt, 1 - slot)

    pltpu.make_async_copy(ys_hbm.at[pl.ds(0, TOP_K * tm)], ybuf.at[slot], sem.at[slot]).wait()
    pk = pk_ref[...]
    y = pk[:, PK_GATE:PK_GATE + 1] * ybuf[slot, pl.ds(0, tm), :]
    for kk in range(1, TOP_K):
        y = y + pk[:, PK_GATE + kk:PK_GATE + kk + 1] * ybuf[slot, pl.ds(kk * tm, tm), :]
    o_ref[...] = _layer_norm(DN_ALPHA * x_ref[...] + y, lg_ref[...], lb_ref[...])


def _combine(ys, pos, packed, x1, lg, lb):
    T, D = x1.shape
    tm = min(256, T)
    nt = T // tm
    pos3 = pos.reshape(nt, 1, TOP_K * tm)
    full = lambda shp: pl.BlockSpec(shp, lambda i: (0,) * len(shp))
    smem_blk = lambda fn: pl.BlockSpec((1, 1, TOP_K * tm), fn, memory_space=pltpu.SMEM)
    return pl.pallas_call(
        _combine_kernel,
        out_shape=jax.ShapeDtypeStruct((T, D), f32),
        grid=(nt,),
        in_specs=[smem_blk(lambda i: (i, 0, 0)), smem_blk(lambda i: (jnp.minimum(i + 1, nt - 1), 0, 0)),
                  pl.BlockSpec(memory_space=pl.ANY), pl.BlockSpec((tm, LANES), lambda i: (i, 0)),
                  pl.BlockSpec((tm, D), lambda i: (i, 0)), full((1, D)), full((1, D))],
        out_specs=pl.BlockSpec((tm, D), lambda i: (i, 0)),
        scratch_shapes=[pltpu.VMEM((2, TOP_K * tm, D), f32), pltpu.SemaphoreType.DMA((2,))],
        compiler_params=_cparams("arbitrary"),
        name="combine_ln",
    )(pos3, pos3, ys, packed, x1, lg.reshape(1, D), lb.reshape(1, D))


def _moe(x1, x1b, router_w, router_b, layer, w1, b1, w2, b2, lg, lb):
    T, D = x1.shape
    A = T * TOP_K
    packed, cnt = _router(x1, router_w, router_b)
    idx = packed[:, PK_IDX:PK_IDX + TOP_K].astype(jnp.int32)
    rank = packed[:, PK_RANK:PK_RANK + TOP_K].astype(jnp.int32)
    counts = cnt[0, :N_EXPERTS].astype(jnp.int32)
    padded = ((counts + MOE_BLOCK - 1) // MOE_BLOCK) * MOE_BLOCK
    pad_end = jnp.cumsum(padded)
    pad_start = pad_end - padded
    pos = (pad_start[idx] + rank).reshape(A)
    n_blocks = -(-A // MOE_BLOCK) + N_EXPERTS
    P = n_blocks * MOE_BLOCK
    tok = jnp.repeat(jnp.arange(T, dtype=jnp.int32), TOP_K)
    tok_buf = jnp.zeros((P,), jnp.int32).at[pos].set(tok)
    starts = jnp.arange(n_blocks, dtype=jnp.int32) * MOE_BLOCK
    block_e = jnp.minimum(jnp.sum((pad_end[None, :] <= starts[:, None]).astype(jnp.int32), axis=1), N_EXPERTS - 1)
    n_used = (pad_end[-1:] // MOE_BLOCK).astype(jnp.int32)
    xs = x1b.at[tok_buf].get(mode="promise_in_bounds")
    ys = _experts(xs, block_e, n_used, layer, w1, b1, w2, b2)
    return _combine(ys, pos, packed, x1, lg, lb)


def _layer(x2d, B, S, p, layer, stacked):
    proj = _proj(x2d, _arrange_w_in(p['w_in']))
    ha = _lru(proj, B, S, p['lru_conv_w'], p['lru_conv_b'], p['lru_wa'], p['lru_ba'], p['lru_wx'],
              p['lru_bx'], p['lru_lambda'])
    hb = _mlstm(proj, B, S, p['ml_i_bias'], p['ml_f_bias'], p['ml_norm_g'])
    hc = _gdn(proj, B, S, p['gd_conv_w'], p['gd_a_log'], p['gd_dt_bias'], p['gd_norm_g'])
    x1, x1b = _merge(ha, hb, hc, proj, x2d, p['w_br_lru'], p['w_br_ml'], p['w_br_gd'], p['w_out'],
                     p['ln1_g'], p['ln1_b'])
    return _moe(x1, x1b, p['router_w'], p['router_b'], layer, stacked['exp_w1'], stacked['exp_b1'],
                stacked['exp_w2'], stacked['exp_b2'], p['ln2_g'], p['ln2_b'])


def kernel(x, w_in, lru_conv_w, lru_conv_b, lru_wa, lru_ba, lru_wx, lru_bx, lru_lambda, ml_i_bias, ml_f_bias, ml_norm_g, gd_conv_w, gd_a_log, gd_dt_bias, gd_norm_g, w_br_lru, w_br_ml, w_br_gd, w_out, ln1_g, ln1_b, router_w, router_b, exp_w1, exp_b1, exp_w2, exp_b2, ln2_g, ln2_b):
    B, S, D = x.shape
    params = dict(w_in=w_in, lru_conv_w=lru_conv_w, lru_conv_b=lru_conv_b, lru_wa=lru_wa, lru_ba=lru_ba,
                  lru_wx=lru_wx, lru_bx=lru_bx, lru_lambda=lru_lambda, ml_i_bias=ml_i_bias,
                  ml_f_bias=ml_f_bias, ml_norm_g=ml_norm_g, gd_conv_w=gd_conv_w, gd_a_log=gd_a_log,
                  gd_dt_bias=gd_dt_bias, gd_norm_g=gd_norm_g, w_br_lru=w_br_lru, w_br_ml=w_br_ml,
                  w_br_gd=w_br_gd, w_out=w_out, ln1_g=ln1_g, ln1_b=ln1_b, router_w=router_w,
                  router_b=router_b, ln2_g=ln2_g, ln2_b=ln2_b)
    stacked = dict(exp_w1=exp_w1, exp_b1=exp_b1, exp_w2=exp_w2, exp_b2=exp_b2)
    h = x.reshape(B * S, D)
    for l in range(w_in.shape[0]):
        h = _layer(h, B, S, {k: v[l] for k, v in params.items()}, l, stacked)
    return h.reshape(B, S, D)
```

```python
import functools
import math

import jax
import jax.numpy as jnp
from jax import lax
from jax.experimental import pallas as pl
from jax.experimental.pallas import tpu as pltpu

f32 = jnp.float32
bf16 = jnp.bfloat16

D_MODEL = 1024
LRU_WIDTH = 1024
LRU_BLOCKS = 8
LRU_C = 8.0
CONV_WIDTH = 4
ML_HEADS = 4
ML_DQK = 64
ML_DV = 128
GD_HEADS = 8
GD_DK = 64
GD_DV = 64
N_EXPERTS = 32
TOP_K = 4
D_FF = 1024
SWIGLU_LIMIT = 7.0
SWIGLU_ALPHA = 1.702
MOE_BLOCK = 512
N_BRANCH = 3
DEPTH = 2
DN_ALPHA = (2.0 * DEPTH) ** 0.25
LN_EPS = 1e-5
RMS_EPS = 1e-6

ML_QK_W = ML_HEADS * ML_DQK
ML_V_W = ML_HEADS * ML_DV
GD_K_W = GD_HEADS * GD_DK
GD_V_W = GD_HEADS * GD_DV
IN_SPLITS = (LRU_WIDTH, LRU_WIDTH, ML_QK_W, ML_QK_W, ML_V_W, ML_HEADS, ML_HEADS, ML_V_W,
             GD_K_W, GD_K_W, GD_V_W, GD_HEADS, GD_HEADS, GD_V_W, N_BRANCH * D_MODEL)

LANES = 128
SUBLANES = 8
NEG_BIG = -1e30

COL_LRU_X = 0
COL_LRU_Y = 1024
COL_GATE = 2048
COL_ML_Q = 5120
COL_ML_K = 5376
COL_ML_V = 5632
COL_ML_O = 6144
COL_GD_Q = 6656
COL_GD_K = 7168
COL_GD_V = 7680
COL_GD_Z = 8192
COL_SMALL = 8704
PROJ_W = 8832
SM_ML_I = 0
SM_ML_F = 4
SM_GD_A = 8
SM_GD_B = 16

ML_CHUNK = 256
GD_CHUNK = 64
GD_ROWS = 256
GD_SOLVE_PASSES = 1
VMEM_LIMIT = 56 * 1024 * 1024


def _cparams(*sem):
    return pltpu.CompilerParams(dimension_semantics=sem, vmem_limit_bytes=VMEM_LIMIT)


def _hdot(a, b):
    return jnp.dot(a, b, precision=lax.Precision.HIGHEST, preferred_element_type=f32)


def _bdot(a, b):
    return jnp.dot(a.astype(bf16), b.astype(bf16), preferred_element_type=f32)


def _bdot_nt(a, b):
    return lax.dot_general(a.astype(bf16), b.astype(bf16), (((1,), (1,)), ((), ())),
                           preferred_element_type=f32)


def _bdot_tn(a, b):
    return lax.dot_general(a.astype(bf16), b.astype(bf16), (((0,), (0,)), ((), ())),
                           preferred_element_type=f32)


def _sigmoid(x):
    return 1.0 / (1.0 + jnp.exp(-x))


def _softplus(x):
    return jnp.maximum(x, 0.0) + jnp.log1p(jnp.exp(-jnp.abs(x)))


def _log_sigmoid(x):
    return jnp.minimum(x, 0.0) - jnp.log1p(jnp.exp(-jnp.abs(x)))


def _layer_norm(z, g, b):
    mu = jnp.mean(z, axis=-1, keepdims=True)
    zc = z - mu
    var = jnp.mean(zc * zc, axis=-1, keepdims=True)
    return zc * lax.rsqrt(var + LN_EPS) * g + b


def _proj_kernel(x_ref, w_ref, o_ref):
    o_ref[...] = jnp.dot(x_ref[...].astype(bf16), w_ref[...], preferred_element_type=f32)


def _proj(x2d, w_arr):
    T = x2d.shape[0]
    tm = min(512, T)
    tn = PROJ_W // 3
    return pl.pallas_call(
        _proj_kernel,
        out_shape=jax.ShapeDtypeStruct((T, PROJ_W), f32),
        grid=(PROJ_W // tn, T // tm),
        in_specs=[pl.BlockSpec((tm, D_MODEL), lambda j, i: (i, 0)),
                  pl.BlockSpec((D_MODEL, tn), lambda j, i: (0, j))],
        out_specs=pl.BlockSpec((tm, tn), lambda j, i: (i, j)),
        compiler_params=_cparams("parallel", "parallel"),
        name="in_proj",
    )(x2d, w_arr)


def _arrange_w_in(w):
    pts = []
    acc = 0
    for s in IN_SPLITS[:-1]:
        acc += s
        pts.append(acc)
    (lru_x, lru_y, ml_q, ml_k, ml_v, ml_i, ml_f, ml_o,
     gd_q, gd_k, gd_v, gd_a, gd_b, gd_z, gate) = jnp.split(w, pts, axis=1)
    pad = jnp.zeros((w.shape[0], LANES - 2 * ML_HEADS - 2 * GD_HEADS), w.dtype)
    small = jnp.concatenate([ml_i, ml_f, gd_a, gd_b, pad], axis=1)
    out = jnp.concatenate([lru_x, lru_y, gate, ml_q, ml_k, ml_v, ml_o, gd_q, gd_k, gd_v, gd_z, small], axis=1)
    return out.astype(bf16)


def _lru_kernel(x_ref, y_ref, cw_ref, cb_ref, wcat_ref, ba_ref, bx_ref, lam_ref, o_ref,
                xext, a_scr, u_scr, carry):
    ts = x_ref.shape[0]
    W = x_ref.shape[1]
    bw = W // LRU_BLOCKS

    @pl.when(pl.program_id(1) == 0)
    def _():
        xext[0:SUBLANES, :] = jnp.zeros((SUBLANES, W), f32)
        carry[...] = jnp.zeros((1, W), f32)

    x = x_ref[...]
    xext[SUBLANES:SUBLANES + ts, :] = x
    xa = (cw_ref[3:4, :] * x
          + cw_ref[2:3, :] * xext[SUBLANES - 1:SUBLANES - 1 + ts, :]
          + cw_ref[1:2, :] * xext[SUBLANES - 2:SUBLANES - 2 + ts, :]
          + cw_ref[0:1, :] * xext[SUBLANES - 3:SUBLANES - 3 + ts, :]) + cb_ref[...]
    xext[0:SUBLANES, :] = x_ref[ts - SUBLANES:ts, :]

    cdec = -LRU_C * _softplus(-lam_ref[...])
    for h in range(LRU_BLOCKS):
        sl = slice(bw * h, bw * (h + 1))
        xh = xa[:, sl]
        g = jnp.dot(xh.astype(bf16), wcat_ref[h], preferred_element_type=f32)
        r = _sigmoid(g[:, :bw] + ba_ref[:, sl])
        ig = _sigmoid(g[:, bw:] + bx_ref[:, sl])
        log_a = r * cdec[:, sl]
        a = jnp.exp(log_a)
        a_scr[:, sl] = a
        u_scr[:, sl] = jnp.sqrt(jnp.tanh(-log_a) * (1.0 + a * a)) * (ig * xh)

    row = lax.broadcasted_iota(jnp.int32, (SUBLANES, W), 0)

    def body(g, cr):
        off = pl.multiple_of(g * SUBLANES, SUBLANES)
        A = a_scr[pl.ds(off, SUBLANES), :]
        U = u_scr[pl.ds(off, SUBLANES), :]
        for s in (1, 2, 4):
            a_sh = pltpu.roll(A, s, 0)
            u_sh = pltpu.roll(U, s, 0)
            m = row >= s
            U = jnp.where(m, A * u_sh + U, U)
            A = jnp.where(m, A * a_sh, A)
        H = A * cr + U
        u_scr[pl.ds(off, SUBLANES), :] = H
        return H[SUBLANES - 1:SUBLANES, :]

    carry[...] = lax.fori_loop(0, ts // SUBLANES, body, carry[...], unroll=2)
    o_ref[...] = (u_scr[...] * jax.nn.gelu(y_ref[...])).astype(o_ref.dtype)


def _lru(proj, B, S, cw, cb, wa, ba, wx, bx, lam):
    T = B * S
    ts = min(512, S)
    nt = S // ts
    W = LRU_WIDTH
    wcat = jnp.concatenate([wa, wx], axis=-1).astype(bf16)
    row = lambda v: v.reshape(1, W)
    full = lambda shp: pl.BlockSpec(shp, lambda b, c: (0,) * len(shp))
    return pl.pallas_call(
        _lru_kernel,
        out_shape=jax.ShapeDtypeStruct((T, W), bf16),
        grid=(B, nt),
        in_specs=[pl.BlockSpec((ts, W), lambda b, c: (b * nt + c, COL_LRU_X // W)),
                  pl.BlockSpec((ts, W), lambda b, c: (b * nt + c, COL_LRU_Y // W)),
                  full((CONV_WIDTH, W)), full((1, W)), full(wcat.shape),
                  full((1, W)), full((1, W)), full((1, W))],
        out_specs=pl.BlockSpec((ts, W), lambda b, c: (b * nt + c, 0)),
        scratch_shapes=[pltpu.VMEM((ts + SUBLANES, W), f32), pltpu.VMEM((ts, W), f32),
                        pltpu.VMEM((ts, W), f32), pltpu.VMEM((1, W), f32)],
        compiler_params=_cparams("parallel", "arbitrary"),
        name="rg_lru",
    )(proj, proj, cw, row(cb), wcat, row(ba), row(bx), row(lam))


def _ml_kernel(q_ref, k_ref, v_ref, og_ref, sm_ref, bias_ref, g_ref, o_ref, c_scr, n_scr, m_scr):
    L = q_ref.shape[0]

    @pl.when(pl.program_id(1) == 0)
    def _():
        c_scr[...] = jnp.zeros(c_scr.shape, f32)
        n_scr[...] = jnp.zeros(n_scr.shape, f32)
        m_scr[...] = jnp.zeros(m_scr.shape, f32)

    sm = sm_ref[...] + bias_ref[...]
    logf = _log_sigmoid(sm)
    ri = lax.broadcasted_iota(jnp.int32, (L, L), 0)
    ci = lax.broadcasted_iota(jnp.int32, (L, L), 1)
    tri = ri >= ci
    b_all = _hdot(tri.astype(f32), logf)
    sm_t = sm.T
    b_t = b_all.T
    q = q_ref[...]
    k = k_ref[...]
    kb = k.astype(bf16)
    lane_head = lax.broadcasted_iota(jnp.int32, q.shape, 1) // ML_DQK
    scale = ML_DQK ** -0.5
    for h in range(ML_HEADS):
        sl = slice(ML_DV * h, ML_DV * (h + 1))
        ig_c = sm[:, SM_ML_I + h:SM_ML_I + h + 1]
        b_c = b_all[:, SM_ML_F + h:SM_ML_F + h + 1]
        ig_r = sm_t[SM_ML_I + h:SM_ML_I + h + 1, :]
        b_r = b_t[SM_ML_F + h:SM_ML_F + h + 1, :]
        b_last = b_c[L - 1:L, :]
        m_prev = m_scr[h:h + 1, 0:1]
        ge_r = b_last - b_r + ig_r
        ge_c = b_last - b_c + ig_c
        m_new = jnp.maximum(b_last + m_prev, jnp.max(ge_r, axis=1, keepdims=True))
        decay = jnp.exp(b_last + m_prev - m_new)
        w_c = jnp.exp(ge_c - m_new)
        dmat = jnp.where(tri, b_c - b_r + ig_r, NEG_BIG)
        inter = b_c + m_prev
        m_t = jnp.maximum(inter, jnp.max(dmat, axis=1, keepdims=True))
        dw = jnp.exp(dmat - m_t)
        iw = jnp.exp(inter - m_t)
        qm = jnp.where(lane_head == h, q, 0.0) * scale
        qmb = qm.astype(bf16)
        s = _bdot_nt(qmb, kb) * dw
        v_h = v_ref[:, sl]
        n_row = n_scr[h:h + 1, :]
        num = _bdot(s, v_h) + iw * _bdot(qmb, c_scr[:, sl])
        den = jnp.sum(s, axis=1, keepdims=True) + iw * jnp.sum(qm * n_row, axis=1, keepdims=True)
        hh = num / jnp.maximum(jnp.abs(den), jnp.exp(-m_t))
        c_scr[:, sl] = decay * c_scr[:, sl] + _bdot_tn(kb, w_c * v_h)
        n_scr[h:h + 1, :] = decay * n_row + jnp.sum(w_c * k, axis=0, keepdims=True)
        m_scr[h:h + 1, :] = jnp.broadcast_to(m_new, (1, LANES))
        ms = jnp.mean(hh * hh, axis=1, keepdims=True)
        hn = hh * lax.rsqrt(ms + RMS_EPS) * g_ref[:, sl]
        o_ref[:, sl] = (hn * _sigmoid(og_ref[:, sl])).astype(o_ref.dtype)


def _mlstm(proj, B, S, i_bias, f_bias, norm_g):
    T = B * S
    L = min(ML_CHUNK, S)
    nc = S // L
    bias = jnp.zeros((1, LANES), f32)
    bias = bias.at[0, SM_ML_I:SM_ML_I + ML_HEADS].set(i_bias).at[0, SM_ML_F:SM_ML_F + ML_HEADS].set(f_bias)
    full = lambda shp: pl.BlockSpec(shp, lambda b, c: (0,) * len(shp))
    blk = lambda w, col: pl.BlockSpec((L, w), lambda b, c: (b * nc + c, col // w))
    return pl.pallas_call(
        _ml_kernel,
        out_shape=jax.ShapeDtypeStruct((T, ML_V_W), bf16),
        grid=(B, nc),
        in_specs=[blk(ML_QK_W, COL_ML_Q), blk(ML_QK_W, COL_ML_K), blk(ML_V_W, COL_ML_V),
                  blk(ML_V_W, COL_ML_O), blk(LANES, COL_SMALL), full((1, LANES)), full((1, ML_V_W))],
        out_specs=pl.BlockSpec((L, ML_V_W), lambda b, c: (b * nc + c, 0)),
        scratch_shapes=[pltpu.VMEM((ML_QK_W, ML_V_W), f32), pltpu.VMEM((SUBLANES, ML_QK_W), f32),
                        pltpu.VMEM((SUBLANES, LANES), f32)],
        compiler_params=_cparams("parallel", "arbitrary"),
        name="mlstm",
    )(proj, proj, proj, proj, proj, bias, norm_g.reshape(1, ML_V_W))


def _split_bf16(x, passes):
    hi = x.astype(bf16)
    if passes == 1:
        return (hi,)
    return (hi, (x - hi.astype(f32)).astype(bf16))


def _mdot(xs, ys):
    out = jnp.dot(xs[0], ys[0], preferred_element_type=f32)
    if len(xs) > 1:
        out = out + jnp.dot(xs[1], ys[0], preferred_element_type=f32)
        out = out + jnp.dot(xs[0], ys[1], preferred_element_type=f32)
    return out


def _gd_kernel(q_ref, k_ref, v_ref, z_ref, sm_ref, cw_ref, alog_ref, dtb_ref, g_ref, o_ref, ext, s_scr):
    R = q_ref.shape[0]
    KW = q_ref.shape[1]
    L = GD_CHUNK
    nch = R // L
    lsh = L.bit_length() - 1
    PW = nch * L
    nsl = KW // LANES
    passes = GD_SOLVE_PASSES

    @pl.when(pl.program_id(1) == 0)
    def _():
        ext[0:SUBLANES, :] = jnp.zeros((SUBLANES, ext.shape[1]), f32)
        s_scr[...] = jnp.zeros(s_scr.shape, f32)

    ext[SUBLANES:SUBLANES + R, 0:KW] = q_ref[...]
    ext[SUBLANES:SUBLANES + R, KW:2 * KW] = k_ref[...]
    ext[SUBLANES:SUBLANES + R, 2 * KW:3 * KW] = v_ref[...]
    xc = (cw_ref[3:4, :] * ext[SUBLANES:SUBLANES + R, :]
          + cw_ref[2:3, :] * ext[SUBLANES - 1:SUBLANES - 1 + R, :]
          + cw_ref[1:2, :] * ext[SUBLANES - 2:SUBLANES - 2 + R, :]
          + cw_ref[0:1, :] * ext[SUBLANES - 3:SUBLANES - 3 + R, :])
    ext[0:SUBLANES, :] = ext[R:R + SUBLANES, :]
    xc = xc * _sigmoid(xc)

    sm = sm_ref[...]
    g_t = -jnp.exp(alog_ref[...]) * _softplus(sm + dtb_ref[...])
    beta_t = _sigmoid(sm)
    rr = lax.broadcasted_iota(jnp.int32, (R, R), 0)
    cc = lax.broadcasted_iota(jnp.int32, (R, R), 1)
    cum = jnp.where(((rr >> lsh) == (cc >> lsh)) & (rr >= cc), 1.0, 0.0)
    gc_all = _hdot(cum, g_t)
    gc_t = gc_all.T

    pt = lax.broadcasted_iota(jnp.int32, (L, PW), 0)
    plane = lax.broadcasted_iota(jnp.int32, (L, PW), 1)
    ps = plane & (L - 1)
    pc = plane >> lsh
    tri_p = pt >= ps
    strict_p = pt > ps
    eye_p = jnp.where(pt == ps, 1.0, 0.0)
    blk8_p = (pt >> 3) == (ps >> 3)
    cmask_b = [jnp.where(pc == c, 1.0, 0.0).astype(bf16) for c in range(nch)]

    def pick(parts):
        out = parts[nch - 1]
        for c in range(nch - 2, -1, -1):
            out = jnp.where(pc == c, parts[c], out)
        return out

    def chunks(col):
        return [col[c * L:(c + 1) * L] for c in range(nch)]

    def bd(yb):
        return jnp.concatenate([yb * cmask_b[c] for c in range(nch)], axis=0)

    def pdot(xp, yp):
        return _mdot(_split_bf16(xp, passes), [bd(p) for p in _split_bf16(yp, passes)])

    lane_r = lax.broadcasted_iota(jnp.int32, (R, LANES), 1)
    lo_r = lane_r < GD_DK
    lane_l = lax.broadcasted_iota(jnp.int32, (L, LANES), 1)
    lo_l = lane_l < GD_DK
    lane_row = lax.broadcasted_iota(jnp.int32, (1, LANES), 1)
    r2 = lax.broadcasted_iota(jnp.int32, (LANES, LANES), 0)
    c2 = lax.broadcasted_iota(jnp.int32, (LANES, LANES), 1)
    blockdiag = (r2 < GD_DK) == (c2 < GD_DK)

    def half_sums(y, lo):
        s_lo = jnp.sum(jnp.where(lo, y, 0.0), axis=1, keepdims=True)
        s_hi = jnp.sum(jnp.where(lo, 0.0, y), axis=1, keepdims=True)
        return s_lo, s_hi

    def l2n(y):
        s_lo, s_hi = half_sums(y * y, lo_r)
        return y * jnp.where(lo_r, lax.rsqrt(s_lo + RMS_EPS), lax.rsqrt(s_hi + RMS_EPS))

    qns, kns, vss, a_ps, p_bs, gcols, bcols = [], [], [], [], [], [], []
    for j in range(nsl):
        qn = l2n(xc[:, LANES * j:LANES * (j + 1)]) * (GD_DK ** -0.5)
        kn = l2n(xc[:, KW + LANES * j:KW + LANES * (j + 1)])
        qns.append(qn)
        kns.append(kn)
        vss.append(xc[:, 2 * KW + LANES * j:2 * KW + LANES * (j + 1)])
        lhs = jnp.concatenate([jnp.where(lo_r, kn, 0.0), jnp.where(lo_r, 0.0, kn),
                               jnp.where(lo_r, qn, 0.0), jnp.where(lo_r, 0.0, qn)], axis=0)
        gram = _bdot_nt(lhs, kn)
        for e in range(2):
            h = 2 * j + e
            gcol = gc_all[:, SM_GD_A + h:SM_GD_A + h + 1]
            bcol = beta_t[:, SM_GD_B + h:SM_GD_B + h + 1]
            gc_r = gc_t[SM_GD_A + h:SM_GD_A + h + 1, :]
            gam_p = jnp.exp(jnp.where(tri_p, pick(chunks(gcol)) - gc_r, NEG_BIG))
            kk_p = pick(chunks(gram[e * R:(e + 1) * R]))
            qk_p = pick(chunks(gram[(2 + e) * R:(3 + e) * R]))
            a_ps.append(jnp.where(strict_p, pick(chunks(bcol)) * gam_p * kk_p, 0.0))
            p_bs.append((gam_p * qk_p).astype(bf16))
            gcols.append(gcol)
            bcols.append(bcol)

    ads = [jnp.where(blk8_p, a, 0.0) for a in a_ps]
    a2s = [pdot(ad, ad) for ad in ads]
    a4s = [pdot(a2, a2) for a2 in a2s]
    xs = [pdot(eye_p - ad, eye_p + a2) for ad, a2 in zip(ads, a2s)]
    xs = [pdot(x, eye_p + a4) for x, a4 in zip(xs, a4s)]
    for sh in range(3, lsh):
        msk = ((pt >> (sh + 1)) == (ps >> (sh + 1))) & (((pt >> sh) & 1) == 1) & (((ps >> sh) & 1) == 0)
        ts = [pdot(jnp.where(msk, a, 0.0), x) for a, x in zip(a_ps, xs)]
        xs = [x - pdot(x, t) for x, t in zip(xs, ts)]

    us, ws, qgs, kds, gls = [], [], [], [], []
    for j in range(nsl):
        uw, egs, eds, glh = [], [], [], []
        for e in range(2):
            h = 2 * j + e
            gcol, bcol = gcols[h], bcols[h]
            eg = jnp.exp(gcol)
            rhs = jnp.concatenate([bcol * vss[j], (bcol * eg) * kns[j]], axis=1)
            uw.append(_mdot([bd(p) for p in _split_bf16(xs[h], passes)], _split_bf16(rhs, passes)))
            lasts = [gcol[c * L + L - 1:c * L + L] for c in range(nch)]
            gcl = jnp.concatenate([jnp.broadcast_to(v, (L, 1)) for v in lasts], axis=0)
            egs.append(eg)
            eds.append(jnp.exp(gcl - gcol))
            glh.append([jnp.exp(v) for v in lasts])
        us.append(jnp.where(lo_r, uw[0][:, :LANES], uw[1][:, :LANES]))
        ws.append(jnp.where(lo_r, uw[0][:, LANES:], uw[1][:, LANES:]))
        qgs.append(qns[j] * jnp.where(lo_r, egs[0], egs[1]))
        kds.append(kns[j] * jnp.where(lo_r, eds[0], eds[1]))
        gls.append([jnp.where(lane_row < GD_DK, glh[0][c], glh[1][c]) for c in range(nch)])

    states = [s_scr[j] for j in range(nsl)]
    outs = [[] for _ in range(nsl)]
    for c in range(nch):
        rs = slice(c * L, (c + 1) * L)
        for j in range(nsl):
            s_prev = states[j]
            sb = s_prev.astype(bf16)
            delta = us[j][rs] - _bdot(ws[j][rs], sb)
            db = delta.astype(bf16)
            pieces = []
            if c > 0:
                pieces.append(jnp.zeros((c * L, LANES), bf16))
            pieces.append(db)
            if c < nch - 1:
                pieces.append(jnp.zeros(((nch - 1 - c) * L, LANES), bf16))
            dpad = jnp.concatenate(pieces, axis=0) if len(pieces) > 1 else db
            intra = jnp.where(lo_l, jnp.dot(p_bs[2 * j], dpad, preferred_element_type=f32),
                              jnp.dot(p_bs[2 * j + 1], dpad, preferred_element_type=f32))
            outs[j].append(_bdot(qgs[j][rs], sb) + intra)
            upd = _bdot_tn(kds[j][rs], db)
            states[j] = gls[j][c] * s_prev + jnp.where(blockdiag, upd, 0.0)

    for j in range(nsl):
        sl = slice(LANES * j, LANES * (j + 1))
        s_scr[j] = states[j]
        o = jnp.concatenate(outs[j], axis=0) if nch > 1 else outs[j][0]
        m_lo, m_hi = half_sums(o * o, lo_r)
        inv = jnp.where(lo_r, lax.rsqrt(m_lo * (1.0 / GD_DV) + RMS_EPS), lax.rsqrt(m_hi * (1.0 / GD_DV) + RMS_EPS))
        zs = z_ref[:, sl]
        o_ref[:, sl] = (o * inv * g_ref[:, sl] * (zs * _sigmoid(zs))).astype(o_ref.dtype)


def _gdn(proj, B, S, conv_w, a_log, dt_bias, norm_g):
    T = B * S
    L = min(GD_ROWS, S)
    nc = S // L
    KW = GD_K_W
    alog = jnp.zeros((1, LANES), f32).at[0, SM_GD_A:SM_GD_A + GD_HEADS].set(a_log)
    dtb = jnp.zeros((1, LANES), f32).at[0, SM_GD_A:SM_GD_A + GD_HEADS].set(dt_bias)
    g_row = jnp.tile(norm_g, GD_HEADS).reshape(1, GD_V_W)
    full = lambda shp: pl.BlockSpec(shp, lambda b, c: (0,) * len(shp))
    blk = lambda w, col: pl.BlockSpec((L, w), lambda b, c: (b * nc + c, col // w))
    return pl.pallas_call(
        _gd_kernel,
        out_shape=jax.ShapeDtypeStruct((T, GD_V_W), bf16),
        grid=(B, nc),
        in_specs=[blk(KW, COL_GD_Q), blk(KW, COL_GD_K), blk(KW, COL_GD_V), blk(KW, COL_GD_Z),
                  blk(LANES, COL_SMALL), full((CONV_WIDTH, 3 * KW)), full((1, LANES)), full((1, LANES)),
                  full((1, GD_V_W))],
        out_specs=pl.BlockSpec((L, GD_V_W), lambda b, c: (b * nc + c, 0)),
        scratch_shapes=[pltpu.VMEM((L + SUBLANES, 3 * KW), f32),
                        pltpu.VMEM((KW // LANES, LANES, LANES), f32)],
        compiler_params=_cparams("parallel", "arbitrary"),
        name="gated_deltanet",
    )(proj, proj, proj, proj, proj, conv_w, alog, dtb, g_row)


def _merge_kernel(ha_ref, hb_ref, hc_ref, g0_ref, g1_ref, g2_ref, x_ref,
                  wa_ref, wb_ref, wc_ref, wo_ref, lg_ref, lb_ref, x1_ref, x1b_ref):
    bra = jnp.dot(ha_ref[...], wa_ref[...], preferred_element_type=f32)
    brb = jnp.dot(hb_ref[...], wb_ref[...], preferred_element_type=f32)
    brc = jnp.dot(hc_ref[...], wc_ref[...], preferred_element_type=f32)
    merged = _sigmoid(g0_ref[...]) * bra + _sigmoid(g1_ref[...]) * brb + _sigmoid(g2_ref[...]) * brc
    y = jnp.dot(merged.astype(bf16), wo_ref[...], preferred_element_type=f32)
    out = _layer_norm(DN_ALPHA * x_ref[...] + y, lg_ref[...], lb_ref[...])
    x1_ref[...] = out
    x1b_ref[...] = out.astype(bf16)


def _merge(ha, hb, hc, proj, x2d, wa, wb, wc, wo, lg, lb):
    T = x2d.shape[0]
    D = D_MODEL
    tm = min(256, T)
    full = lambda shp: pl.BlockSpec(shp, lambda i: (0,) * len(shp))
    rows = lambda w: pl.BlockSpec((tm, w), lambda i: (i, 0))
    gate = lambda n: pl.BlockSpec((tm, D), lambda i: (i, COL_GATE // D + n))
    return pl.pallas_call(
        _merge_kernel,
        out_shape=(jax.ShapeDtypeStruct((T, D), f32), jax.ShapeDtypeStruct((T, D), bf16)),
        grid=(T // tm,),
        in_specs=[rows(LRU_WIDTH), rows(ML_V_W), rows(GD_V_W), gate(0), gate(1), gate(2), rows(D),
                  full((LRU_WIDTH, D)), full((ML_V_W, D)), full((GD_V_W, D)), full((D, D)),
                  full((1, D)), full((1, D))],
        out_specs=(rows(D), rows(D)),
        compiler_params=_cparams("parallel"),
        name="merge_outproj_ln",
    )(ha, hb, hc, proj, proj, proj, x2d, wa.astype(bf16), wb.astype(bf16), wc.astype(bf16),
      wo.astype(bf16), lg.reshape(1, D), lb.reshape(1, D))


PK_IDX = 0
PK_GATE = 4
PK_RANK = 8


def _router_kernel(x_ref, rw_ref, rb_ref, pk_ref, cnt_ref, carry):
    tr = x_ref.shape[0]

    @pl.when(pl.program_id(0) == 0)
    def _():
        carry[...] = jnp.zeros(carry.shape, f32)

    logits = _mdot(_split_bf16(x_ref[...], 3), _split_bf16(rw_ref[...], 3)) + rb_ref[...]
    lane = lax.broadcasted_iota(jnp.int32, (tr, LANES), 1).astype(f32)
    vals = logits
    idxs, tops = [], []
    for _ in range(TOP_K):
        m = jnp.max(vals, axis=1, keepdims=True)
        idx = jnp.min(jnp.where(vals == m, lane, float(LANES)), axis=1, keepdims=True)
        idxs.append(idx)
        tops.append(m)
        vals = jnp.where(lane == idx, NEG_BIG * 2.0, vals)
    es = [jnp.exp(t - tops[0]) for t in tops]
    tot = es[0] + es[1] + es[2] + es[3]
    onehots = [lane == idx for idx in idxs]
    sel = jnp.zeros((tr, LANES), f32)
    for oh in onehots:
        sel = sel + oh.astype(f32)
    ri = lax.broadcasted_iota(jnp.int32, (tr, tr), 0)
    ci = lax.broadcasted_iota(jnp.int32, (tr, tr), 1)
    before = jnp.dot((ri > ci).astype(bf16), sel.astype(bf16), preferred_element_type=f32) + carry[...]
    carry[...] = carry[...] + jnp.sum(sel, axis=0, keepdims=True)
    cnt_ref[...] = carry[...]
    packed = jnp.zeros((tr, LANES), f32)
    for kk in range(TOP_K):
        rank = jnp.sum(jnp.where(onehots[kk], before, 0.0), axis=1, keepdims=True)
        packed = jnp.where(lane == float(PK_IDX + kk), idxs[kk], packed)
        packed = jnp.where(lane == float(PK_GATE + kk), es[kk] / tot, packed)
        packed = jnp.where(lane == float(PK_RANK + kk), rank, packed)
    pk_ref[...] = packed


def _router(x1, router_w, router_b):
    T = x1.shape[0]
    tr = min(256, T)
    rw = jnp.zeros((D_MODEL, LANES), f32).at[:, :N_EXPERTS].set(router_w)
    rb = jnp.full((1, LANES), NEG_BIG, f32).at[0, :N_EXPERTS].set(router_b)
    full = lambda shp: pl.BlockSpec(shp, lambda i: (0,) * len(shp))
    return pl.pallas_call(
        _router_kernel,
        out_shape=(jax.ShapeDtypeStruct((T, LANES), f32), jax.ShapeDtypeStruct((1, LANES), f32)),
        grid=(T // tr,),
        in_specs=[pl.BlockSpec((tr, D_MODEL), lambda i: (i, 0)), full((D_MODEL, LANES)), full((1, LANES))],
        out_specs=(pl.BlockSpec((tr, LANES), lambda i: (i, 0)), full((1, LANES))),
        scratch_shapes=[pltpu.VMEM((1, LANES), f32)],
        compiler_params=_cparams("arbitrary"),
        name="router",
    )(x1, rw, rb)


def _expert_kernel(be_ref, nu_ref, x_ref, w1_ref, b1_ref, w2_ref, b2_ref, o_ref, w1b, w2b):
    i = pl.program_id(0)
    used = i < nu_ref[0]
    new_expert = jnp.logical_or(i == 0, be_ref[i] != be_ref[jnp.maximum(i - 1, 0)])

    @pl.when(jnp.logical_and(used, new_expert))
    def _():
        w1b[...] = w1_ref[0, 0].astype(bf16)
        w2b[...] = w2_ref[0, 0].astype(bf16)

    @pl.when(used)
    def _():
        hdn = jnp.dot(x_ref[...], w1b[...], preferred_element_type=f32) + b1_ref[0, 0]
        glu = jnp.minimum(hdn[:, :D_FF], SWIGLU_LIMIT)
        lin = jnp.clip(hdn[:, D_FF:], -SWIGLU_LIMIT, SWIGLU_LIMIT)
        act = glu * _sigmoid(SWIGLU_ALPHA * glu) * (lin + 1.0)
        y = jnp.dot(act.astype(bf16), w2b[...], preferred_element_type=f32) + b2_ref[0, 0]
        o_ref[...] = y.astype(o_ref.dtype)

    @pl.when(i >= nu_ref[0])
    def _():
        o_ref[...] = jnp.zeros(o_ref.shape, o_ref.dtype)


def _experts(xs, block_e, n_used, layer, w1, b1, w2, b2):
    P = xs.shape[0]
    nb = P // MOE_BLOCK
    D = D_MODEL
    nl = w1.shape[0]
    grid_spec = pltpu.PrefetchScalarGridSpec(
        num_scalar_prefetch=2,
        grid=(nb,),
        in_specs=[pl.BlockSpec((MOE_BLOCK, D), lambda i, be, nu: (i, 0)),
                  pl.BlockSpec((1, 1, D, 2 * D_FF), lambda i, be, nu: (layer, be[i], 0, 0)),
                  pl.BlockSpec((1, 1, 1, 2 * D_FF), lambda i, be, nu: (layer, be[i], 0, 0)),
                  pl.BlockSpec((1, 1, D_FF, D), lambda i, be, nu: (layer, be[i], 0, 0)),
                  pl.BlockSpec((1, 1, 1, D), lambda i, be, nu: (layer, be[i], 0, 0))],
        out_specs=pl.BlockSpec((MOE_BLOCK, D), lambda i, be, nu: (i, 0)),
        scratch_shapes=[pltpu.VMEM((D, 2 * D_FF), bf16), pltpu.VMEM((D_FF, D), bf16)],
    )
    return pl.pallas_call(
        _expert_kernel,
        out_shape=jax.ShapeDtypeStruct((P, D), bf16),
        grid_spec=grid_spec,
        compiler_params=_cparams("arbitrary"),
        name="experts",
    )(block_e, n_used, xs, w1, b1.reshape(nl, N_EXPERTS, 1, 2 * D_FF), w2, b2.reshape(nl, N_EXPERTS, 1, D))


def _combine_kernel(y0_ref, y1_ref, y2_ref, y3_ref, pk_ref, x_ref, lg_ref, lb_ref, o_ref):
    pk = pk_ref[...]
    y = pk[:, PK_GATE:PK_GATE + 1] * y0_ref[...].astype(f32)
    for kk, y_ref in ((1, y1_ref), (2, y2_ref), (3, y3_ref)):
        y = y + pk[:, PK_GATE + kk:PK_GATE + kk + 1] * y_ref[...].astype(f32)
    o_ref[...] = _layer_norm(DN_ALPHA * x_ref[...] + y, lg_ref[...], lb_ref[...])


def _combine(yg, packed, x1, lg, lb):
    T, D = x1.shape
    tm = min(256, T)
    nt = T // tm
    full = lambda shp: pl.BlockSpec(shp, lambda i: (0,) * len(shp))
    choice = lambda kk: pl.BlockSpec((tm, D), lambda i: (kk * nt + i, 0))
    return pl.pallas_call(
        _combine_kernel,
        out_shape=jax.ShapeDtypeStruct((T, D), f32),
        grid=(nt,),
        in_specs=[choice(0), choice(1), choice(2), choice(3), pl.BlockSpec((tm, LANES), lambda i: (i, 0)),
                  pl.BlockSpec((tm, D), lambda i: (i, 0)), full((1, D)), full((1, D))],
        out_specs=pl.BlockSpec((tm, D), lambda i: (i, 0)),
        compiler_params=_cparams("parallel"),
        name="combine_ln",
    )(yg, yg, yg, yg, packed, x1, lg.reshape(1, D), lb.reshape(1, D))


def _moe(x1, x1b, router_w, router_b, layer, w1, b1, w2, b2, lg, lb):
    T, D = x1.shape
    A = T * TOP_K
    packed, cnt = _router(x1, router_w, router_b)
    idx = packed[:, PK_IDX:PK_IDX + TOP_K].astype(jnp.int32)
    rank = packed[:, PK_RANK:PK_RANK + TOP_K].astype(jnp.int32)
    counts = cnt[0, :N_EXPERTS].astype(jnp.int32)
    padded = ((counts + MOE_BLOCK - 1) // MOE_BLOCK) * MOE_BLOCK
    pad_end = jnp.cumsum(padded)
    pad_start = pad_end - padded
    pos = (pad_start[idx] + rank).reshape(A)
    n_blocks = -(-A // MOE_BLOCK) + N_EXPERTS
    P = n_blocks * MOE_BLOCK
    tok = jnp.repeat(jnp.arange(T, dtype=jnp.int32), TOP_K)
    tok_buf = (jnp.arange(P, dtype=jnp.int32) % T).at[pos].set(tok)
    starts = jnp.arange(n_blocks, dtype=jnp.int32) * MOE_BLOCK
    block_e = jnp.minimum(jnp.sum((pad_end[None, :] <= starts[:, None]).astype(jnp.int32), axis=1), N_EXPERTS - 1)
    n_used = (pad_end[-1:] // MOE_BLOCK).astype(jnp.int32)
    xs = x1b.at[tok_buf].get(mode="promise_in_bounds")
    ys = _experts(xs, block_e, n_used, layer, w1, b1, w2, b2)
    yg = ys.at[pos.reshape(T, TOP_K).T.reshape(A)].get(mode="promise_in_bounds")
    return _combine(yg, packed, x1, lg, lb)


def _layer(x2d, B, S, p, layer, stacked):
    proj = _proj(x2d, _arrange_w_in(p['w_in']))
    ha = _lru(proj, B, S, p['lru_conv_w'], p['lru_conv_b'], p['lru_wa'], p['lru_ba'], p['lru_wx'],
              p['lru_bx'], p['lru_lambda'])
    hb = _mlstm(proj, B, S, p['ml_i_bias'], p['ml_f_bias'], p['ml_norm_g'])
    hc = _gdn(proj, B, S, p['gd_conv_w'], p['gd_a_log'], p['gd_dt_bias'], p['gd_norm_g'])
    x1, x1b = _merge(ha, hb, hc, proj, x2d, p['w_br_lru'], p['w_br_ml'], p['w_br_gd'], p['w_out'],
                     p['ln1_g'], p['ln1_b'])
    return _moe(x1, x1b, p['router_w'], p['router_b'], layer, stacked['exp_w1'], stacked['exp_b1'],
                stacked['exp_w2'], stacked['exp_b2'], p['ln2_g'], p['ln2_b'])


def kernel(x, w_in, lru_conv_w, lru_conv_b, lru_wa, lru_ba, lru_wx, lru_bx, lru_lambda, ml_i_bias, ml_f_bias, ml_norm_g, gd_conv_w, gd_a_log, gd_dt_bias, gd_norm_g, w_br_lru, w_br_ml, w_br_gd, w_out, ln1_g, ln1_b, router_w, router_b, exp_w1, exp_b1, exp_w2, exp_b2, ln2_g, ln2_b):
    B, S, D = x.shape
    params = dict(w_in=w_in, lru_conv_w=lru_conv_w, lru_conv_b=lru_conv_b, lru_wa=lru_wa, lru_ba=lru_ba,
                  lru_wx=lru_wx, lru_bx=lru_bx, lru_lambda=lru_lambda, ml_i_bias=ml_i_bias,
                  ml_f_bias=ml_f_bias, ml_norm_g=ml_norm_g, gd_conv_w=gd_conv_w, gd_a_log=gd_a_log,
                  gd_dt_bias=gd_dt_bias, gd_norm_g=gd_norm_g, w_br_lru=w_br_lru, w_br_ml=w_br_ml,
                  w_br_gd=w_br_gd, w_out=w_out, ln1_g=ln1_g, ln1_b=ln1_b, router_w=router_w,
                  router_b=router_b, ln2_g=ln2_g, ln2_b=ln2_b)
    stacked = dict(exp_w1=exp_w1, exp_b1=exp_b1, exp_w2=exp_w2, exp_b2=exp_b2)
    h = x.reshape(B * S, D)
    for l in range(w_in.shape[0]):
        h = _layer(h, B, S, {k: v[l] for k, v in params.items()}, l, stacked)
    return h.reshape(B, S, D)
```

```python
import functools
import math

import jax
import jax.numpy as jnp
from jax import lax
from jax.experimental import pallas as pl
from jax.experimental.pallas import tpu as pltpu
from jax.experimental.pallas import tpu_sc as plsc

f32 = jnp.float32
bf16 = jnp.bfloat16

D_MODEL = 1024
LRU_WIDTH = 1024
LRU_BLOCKS = 8
LRU_C = 8.0
CONV_WIDTH = 4
ML_HEADS = 4
ML_DQK = 64
ML_DV = 128
GD_HEADS = 8
GD_DK = 64
GD_DV = 64
N_EXPERTS = 32
TOP_K = 4
D_FF = 1024
SWIGLU_LIMIT = 7.0
SWIGLU_ALPHA = 1.702
MOE_BLOCK = 512
N_BRANCH = 3
DEPTH = 2
DN_ALPHA = (2.0 * DEPTH) ** 0.25
LN_EPS = 1e-5
RMS_EPS = 1e-6

ML_QK_W = ML_HEADS * ML_DQK
ML_V_W = ML_HEADS * ML_DV
GD_K_W = GD_HEADS * GD_DK
GD_V_W = GD_HEADS * GD_DV
IN_SPLITS = (LRU_WIDTH, LRU_WIDTH, ML_QK_W, ML_QK_W, ML_V_W, ML_HEADS, ML_HEADS, ML_V_W,
             GD_K_W, GD_K_W, GD_V_W, GD_HEADS, GD_HEADS, GD_V_W, N_BRANCH * D_MODEL)

LANES = 128
SUBLANES = 8
NEG_BIG = -1e30

COL_LRU_X = 0
COL_LRU_Y = 1024
COL_GATE = 2048
COL_ML_Q = 5120
COL_ML_K = 5376
COL_ML_V = 5632
COL_ML_O = 6144
COL_GD_Q = 6656
COL_GD_K = 7168
COL_GD_V = 7680
COL_GD_Z = 8192
COL_SMALL = 8704
PROJ_W = 8832
SM_ML_I = 0
SM_ML_F = 4
SM_GD_A = 8
SM_GD_B = 16

ML_CHUNK = 256
GD_CHUNK = 64
GD_ROWS = 256
GD_SOLVE_PASSES = 1
VMEM_LIMIT = 56 * 1024 * 1024
SC_ROWS = 128


def _cparams(*sem):
    return pltpu.CompilerParams(dimension_semantics=sem, vmem_limit_bytes=VMEM_LIMIT)


def _hdot(a, b):
    return jnp.dot(a, b, precision=lax.Precision.HIGHEST, preferred_element_type=f32)


def _bdot(a, b):
    return jnp.dot(a.astype(bf16), b.astype(bf16), preferred_element_type=f32)


def _bdot_nt(a, b):
    return lax.dot_general(a.astype(bf16), b.astype(bf16), (((1,), (1,)), ((), ())),
                           preferred_element_type=f32)


def _bdot_tn(a, b):
    return lax.dot_general(a.astype(bf16), b.astype(bf16), (((0,), (0,)), ((), ())),
                           preferred_element_type=f32)


def _sigmoid(x):
    return 1.0 / (1.0 + jnp.exp(-x))


def _softplus(x):
    return jnp.maximum(x, 0.0) + jnp.log1p(jnp.exp(-jnp.abs(x)))


def _log_sigmoid(x):
    return jnp.minimum(x, 0.0) - jnp.log1p(jnp.exp(-jnp.abs(x)))


def _pack_halves(v):
    n = v.shape[1] // 2
    hi = lax.bitcast_convert_type(v[:, :n].astype(bf16).astype(f32), jnp.uint32)
    lo = lax.bitcast_convert_type(v[:, n:].astype(bf16).astype(f32), jnp.uint32)
    return hi | (lo >> 16)


def _unpack_halves(p):
    hi = lax.bitcast_convert_type(p & jnp.uint32(0xFFFF0000), f32)
    lo = lax.bitcast_convert_type(p << 16, f32)
    return jnp.concatenate([hi, lo], axis=1)


def _layer_norm(z, g, b):
    mu = jnp.mean(z, axis=-1, keepdims=True)
    zc = z - mu
    var = jnp.mean(zc * zc, axis=-1, keepdims=True)
    return zc * lax.rsqrt(var + LN_EPS) * g + b


def _proj_kernel(x_ref, w_ref, o_ref):
    o_ref[...] = jnp.dot(x_ref[...].astype(bf16), w_ref[...], preferred_element_type=f32)


def _proj(x2d, w_arr):
    T = x2d.shape[0]
    tm = min(512, T)
    tn = PROJ_W // 3
    return pl.pallas_call(
        _proj_kernel,
        out_shape=jax.ShapeDtypeStruct((T, PROJ_W), f32),
        grid=(PROJ_W // tn, T // tm),
        in_specs=[pl.BlockSpec((tm, D_MODEL), lambda j, i: (i, 0)),
                  pl.BlockSpec((D_MODEL, tn), lambda j, i: (0, j))],
        out_specs=pl.BlockSpec((tm, tn), lambda j, i: (i, j)),
        compiler_params=_cparams("parallel", "parallel"),
        name="in_proj",
    )(x2d, w_arr)


def _arrange_w_in(w):
    pts = []
    acc = 0
    for s in IN_SPLITS[:-1]:
        acc += s
        pts.append(acc)
    (lru_x, lru_y, ml_q, ml_k, ml_v, ml_i, ml_f, ml_o,
     gd_q, gd_k, gd_v, gd_a, gd_b, gd_z, gate) = jnp.split(w, pts, axis=1)
    pad = jnp.zeros((w.shape[0], LANES - 2 * ML_HEADS - 2 * GD_HEADS), w.dtype)
    small = jnp.concatenate([ml_i, ml_f, gd_a, gd_b, pad], axis=1)
    out = jnp.concatenate([lru_x, lru_y, gate, ml_q, ml_k, ml_v, ml_o, gd_q, gd_k, gd_v, gd_z, small], axis=1)
    return out.astype(bf16)


def _lru_kernel(x_ref, y_ref, cw_ref, cb_ref, wcat_ref, ba_ref, bx_ref, lam_ref, o_ref,
                xext, a_scr, u_scr, carry):
    ts = x_ref.shape[0]
    W = x_ref.shape[1]
    bw = W // LRU_BLOCKS

    @pl.when(pl.program_id(1) == 0)
    def _():
        xext[0:SUBLANES, :] = jnp.zeros((SUBLANES, W), f32)
        carry[...] = jnp.zeros((1, W), f32)

    x = x_ref[...]
    xext[SUBLANES:SUBLANES + ts, :] = x
    xa = (cw_ref[3:4, :] * x
          + cw_ref[2:3, :] * xext[SUBLANES - 1:SUBLANES - 1 + ts, :]
          + cw_ref[1:2, :] * xext[SUBLANES - 2:SUBLANES - 2 + ts, :]
          + cw_ref[0:1, :] * xext[SUBLANES - 3:SUBLANES - 3 + ts, :]) + cb_ref[...]
    xext[0:SUBLANES, :] = x_ref[ts - SUBLANES:ts, :]

    cdec = -LRU_C * _softplus(-lam_ref[...])
    for h in range(LRU_BLOCKS):
        sl = slice(bw * h, bw * (h + 1))
        xh = xa[:, sl]
        g = jnp.dot(xh.astype(bf16), wcat_ref[h], preferred_element_type=f32)
        r = _sigmoid(g[:, :bw] + ba_ref[:, sl])
        ig = _sigmoid(g[:, bw:] + bx_ref[:, sl])
        log_a = r * cdec[:, sl]
        a = jnp.exp(log_a)
        a_scr[:, sl] = a
        u_scr[:, sl] = jnp.sqrt(jnp.tanh(-log_a) * (1.0 + a * a)) * (ig * xh)

    row = lax.broadcasted_iota(jnp.int32, (SUBLANES, W), 0)

    def body(g, cr):
        off = pl.multiple_of(g * SUBLANES, SUBLANES)
        A = a_scr[pl.ds(off, SUBLANES), :]
        U = u_scr[pl.ds(off, SUBLANES), :]
        for s in (1, 2, 4):
            a_sh = pltpu.roll(A, s, 0)
            u_sh = pltpu.roll(U, s, 0)
            m = row >= s
            U = jnp.where(m, A * u_sh + U, U)
            A = jnp.where(m, A * a_sh, A)
        H = A * cr + U
        u_scr[pl.ds(off, SUBLANES), :] = H
        return H[SUBLANES - 1:SUBLANES, :]

    carry[...] = lax.fori_loop(0, ts // SUBLANES, body, carry[...], unroll=2)
    o_ref[...] = (u_scr[...] * jax.nn.gelu(y_ref[...])).astype(o_ref.dtype)


def _lru(proj, B, S, cw, cb, wa, ba, wx, bx, lam):
    T = B * S
    ts = min(512, S)
    nt = S // ts
    W = LRU_WIDTH
    wcat = jnp.concatenate([wa, wx], axis=-1).astype(bf16)
    row = lambda v: v.reshape(1, W)
    full = lambda shp: pl.BlockSpec(shp, lambda b, c: (0,) * len(shp))
    return pl.pallas_call(
        _lru_kernel,
        out_shape=jax.ShapeDtypeStruct((T, W), bf16),
        grid=(B, nt),
        in_specs=[pl.BlockSpec((ts, W), lambda b, c: (b * nt + c, COL_LRU_X // W)),
                  pl.BlockSpec((ts, W), lambda b, c: (b * nt + c, COL_LRU_Y // W)),
                  full((CONV_WIDTH, W)), full((1, W)), full(wcat.shape),
                  full((1, W)), full((1, W)), full((1, W))],
        out_specs=pl.BlockSpec((ts, W), lambda b, c: (b * nt + c, 0)),
        scratch_shapes=[pltpu.VMEM((ts + SUBLANES, W), f32), pltpu.VMEM((ts, W), f32),
                        pltpu.VMEM((ts, W), f32), pltpu.VMEM((1, W), f32)],
        compiler_params=_cparams("parallel", "arbitrary"),
        name="rg_lru",
    )(proj, proj, cw, row(cb), wcat, row(ba), row(bx), row(lam))


def _ml_kernel(q_ref, k_ref, v_ref, og_ref, sm_ref, bias_ref, g_ref, o_ref, c_scr, n_scr, m_scr):
    L = q_ref.shape[0]

    @pl.when(pl.program_id(1) == 0)
    def _():
        c_scr[...] = jnp.zeros(c_scr.shape, f32)
        n_scr[...] = jnp.zeros(n_scr.shape, f32)
        m_scr[...] = jnp.zeros(m_scr.shape, f32)

    sm = sm_ref[...] + bias_ref[...]
    logf = _log_sigmoid(sm)
    ri = lax.broadcasted_iota(jnp.int32, (L, L), 0)
    ci = lax.broadcasted_iota(jnp.int32, (L, L), 1)
    tri = ri >= ci
    b_all = _hdot(tri.astype(f32), logf)
    sm_t = sm.T
    b_t = b_all.T
    q = q_ref[...]
    k = k_ref[...]
    kb = k.astype(bf16)
    lane_head = lax.broadcasted_iota(jnp.int32, q.shape, 1) // ML_DQK
    scale = ML_DQK ** -0.5
    for h in range(ML_HEADS):
        sl = slice(ML_DV * h, ML_DV * (h + 1))
        ig_c = sm[:, SM_ML_I + h:SM_ML_I + h + 1]
        b_c = b_all[:, SM_ML_F + h:SM_ML_F + h + 1]
        ig_r = sm_t[SM_ML_I + h:SM_ML_I + h + 1, :]
        b_r = b_t[SM_ML_F + h:SM_ML_F + h + 1, :]
        b_last = b_c[L - 1:L, :]
        m_prev = m_scr[h:h + 1, 0:1]
        ge_r = b_last - b_r + ig_r
        ge_c = b_last - b_c + ig_c
        m_new = jnp.maximum(b_last + m_prev, jnp.max(ge_r, axis=1, keepdims=True))
        decay = jnp.exp(b_last + m_prev - m_new)
        w_c = jnp.exp(ge_c - m_new)
        dmat = jnp.where(tri, b_c - b_r + ig_r, NEG_BIG)
        inter = b_c + m_prev
        m_t = jnp.maximum(inter, jnp.max(dmat, axis=1, keepdims=True))
        dw = jnp.exp(dmat - m_t)
        iw = jnp.exp(inter - m_t)
        qm = jnp.where(lane_head == h, q, 0.0) * scale
        qmb = qm.astype(bf16)
        s = _bdot_nt(qmb, kb) * dw
        v_h = v_ref[:, sl]
        n_row = n_scr[h:h + 1, :]
        num = _bdot(s, v_h) + iw * _bdot(qmb, c_scr[:, sl])
        den = jnp.sum(s, axis=1, keepdims=True) + iw * jnp.sum(qm * n_row, axis=1, keepdims=True)
        hh = num / jnp.maximum(jnp.abs(den), jnp.exp(-m_t))
        c_scr[:, sl] = decay * c_scr[:, sl] + _bdot_tn(kb, w_c * v_h)
        n_scr[h:h + 1, :] = decay * n_row + jnp.sum(w_c * k, axis=0, keepdims=True)
        m_scr[h:h + 1, :] = jnp.broadcast_to(m_new, (1, LANES))
        ms = jnp.mean(hh * hh, axis=1, keepdims=True)
        hn = hh * lax.rsqrt(ms + RMS_EPS) * g_ref[:, sl]
        o_ref[:, sl] = (hn * _sigmoid(og_ref[:, sl])).astype(o_ref.dtype)


def _mlstm(proj, B, S, i_bias, f_bias, norm_g):
    T = B * S
    L = min(ML_CHUNK, S)
    nc = S // L
    bias = jnp.zeros((1, LANES), f32)
    bias = bias.at[0, SM_ML_I:SM_ML_I + ML_HEADS].set(i_bias).at[0, SM_ML_F:SM_ML_F + ML_HEADS].set(f_bias)
    full = lambda shp: pl.BlockSpec(shp, lambda b, c: (0,) * len(shp))
    blk = lambda w, col: pl.BlockSpec((L, w), lambda b, c: (b * nc + c, col // w))
    return pl.pallas_call(
        _ml_kernel,
        out_shape=jax.ShapeDtypeStruct((T, ML_V_W), bf16),
        grid=(B, nc),
        in_specs=[blk(ML_QK_W, COL_ML_Q), blk(ML_QK_W, COL_ML_K), blk(ML_V_W, COL_ML_V),
                  blk(ML_V_W, COL_ML_O), blk(LANES, COL_SMALL), full((1, LANES)), full((1, ML_V_W))],
        out_specs=pl.BlockSpec((L, ML_V_W), lambda b, c: (b * nc + c, 0)),
        scratch_shapes=[pltpu.VMEM((ML_QK_W, ML_V_W), f32), pltpu.VMEM((SUBLANES, ML_QK_W), f32),
                        pltpu.VMEM((SUBLANES, LANES), f32)],
        compiler_params=_cparams("parallel", "arbitrary"),
        name="mlstm",
    )(proj, proj, proj, proj, proj, bias, norm_g.reshape(1, ML_V_W))


def _split_bf16(x, passes):
    hi = x.astype(bf16)
    if passes == 1:
        return (hi,)
    return (hi, (x - hi.astype(f32)).astype(bf16))


def _mdot(xs, ys):
    out = jnp.dot(xs[0], ys[0], preferred_element_type=f32)
    if len(xs) > 1:
        out = out + jnp.dot(xs[1], ys[0], preferred_element_type=f32)
        out = out + jnp.dot(xs[0], ys[1], preferred_element_type=f32)
    return out


def _gd_kernel(q_ref, k_ref, v_ref, z_ref, sm_ref, cw_ref, alog_ref, dtb_ref, g_ref, o_ref, ext, s_scr):
    R = q_ref.shape[0]
    KW = q_ref.shape[1]
    L = GD_CHUNK
    nch = R // L
    lsh = L.bit_length() - 1
    PW = nch * L
    nsl = KW // LANES
    passes = GD_SOLVE_PASSES

    @pl.when(pl.program_id(1) == 0)
    def _():
        ext[0:SUBLANES, :] = jnp.zeros((SUBLANES, ext.shape[1]), f32)
        s_scr[...] = jnp.zeros(s_scr.shape, f32)

    ext[SUBLANES:SUBLANES + R, 0:KW] = q_ref[...]
    ext[SUBLANES:SUBLANES + R, KW:2 * KW] = k_ref[...]
    ext[SUBLANES:SUBLANES + R, 2 * KW:3 * KW] = v_ref[...]
    xc = (cw_ref[3:4, :] * ext[SUBLANES:SUBLANES + R, :]
          + cw_ref[2:3, :] * ext[SUBLANES - 1:SUBLANES - 1 + R, :]
          + cw_ref[1:2, :] * ext[SUBLANES - 2:SUBLANES - 2 + R, :]
          + cw_ref[0:1, :] * ext[SUBLANES - 3:SUBLANES - 3 + R, :])
    ext[0:SUBLANES, :] = ext[R:R + SUBLANES, :]
    xc = xc * _sigmoid(xc)

    sm = sm_ref[...]
    g_t = -jnp.exp(alog_ref[...]) * _softplus(sm + dtb_ref[...])
    beta_t = _sigmoid(sm)
    rr = lax.broadcasted_iota(jnp.int32, (R, R), 0)
    cc = lax.broadcasted_iota(jnp.int32, (R, R), 1)
    cum = jnp.where(((rr >> lsh) == (cc >> lsh)) & (rr >= cc), 1.0, 0.0)
    gc_all = _hdot(cum, g_t)
    gc_t = gc_all.T

    pt = lax.broadcasted_iota(jnp.int32, (L, PW), 0)
    plane = lax.broadcasted_iota(jnp.int32, (L, PW), 1)
    ps = plane & (L - 1)
    pc = plane >> lsh
    tri_p = pt >= ps
    strict_p = pt > ps
    eye_p = jnp.where(pt == ps, 1.0, 0.0)
    blk8_p = (pt >> 3) == (ps >> 3)
    cmask_b = [jnp.where(pc == c, 1.0, 0.0).astype(bf16) for c in range(nch)]

    def pick(parts):
        out = parts[nch - 1]
        for c in range(nch - 2, -1, -1):
            out = jnp.where(pc == c, parts[c], out)
        return out

    def chunks(col):
        return [col[c * L:(c + 1) * L] for c in range(nch)]

    def bd(yb):
        return jnp.concatenate([yb * cmask_b[c] for c in range(nch)], axis=0)

    def pdot(xp, yp):
        return _mdot(_split_bf16(xp, passes), [bd(p) for p in _split_bf16(yp, passes)])

    lane_r = lax.broadcasted_iota(jnp.int32, (R, LANES), 1)
    lo_r = lane_r < GD_DK
    lane_l = lax.broadcasted_iota(jnp.int32, (L, LANES), 1)
    lo_l = lane_l < GD_DK
    lane_row = lax.broadcasted_iota(jnp.int32, (1, LANES), 1)
    r2 = lax.broadcasted_iota(jnp.int32, (LANES, LANES), 0)
    c2 = lax.broadcasted_iota(jnp.int32, (LANES, LANES), 1)
    blockdiag = (r2 < GD_DK) == (c2 < GD_DK)

    def half_sums(y, lo):
        s_lo = jnp.sum(jnp.where(lo, y, 0.0), axis=1, keepdims=True)
        s_hi = jnp.sum(jnp.where(lo, 0.0, y), axis=1, keepdims=True)
        return s_lo, s_hi

    def l2n(y):
        s_lo, s_hi = half_sums(y * y, lo_r)
        return y * jnp.where(lo_r, lax.rsqrt(s_lo + RMS_EPS), lax.rsqrt(s_hi + RMS_EPS))

    qns, kns, vss, a_ps, p_bs, gcols, bcols = [], [], [], [], [], [], []
    for j in range(nsl):
        qn = l2n(xc[:, LANES * j:LANES * (j + 1)]) * (GD_DK ** -0.5)
        kn = l2n(xc[:, KW + LANES * j:KW + LANES * (j + 1)])
        qns.append(qn)
        kns.append(kn)
        vss.append(xc[:, 2 * KW + LANES * j:2 * KW + LANES * (j + 1)])
        lhs = jnp.concatenate([jnp.where(lo_r, kn, 0.0), jnp.where(lo_r, 0.0, kn),
                               jnp.where(lo_r, qn, 0.0), jnp.where(lo_r, 0.0, qn)], axis=0)
        gram = _bdot_nt(lhs, kn)
        for e in range(2):
            h = 2 * j + e
            gcol = gc_all[:, SM_GD_A + h:SM_GD_A + h + 1]
            bcol = beta_t[:, SM_GD_B + h:SM_GD_B + h + 1]
            gc_r = gc_t[SM_GD_A + h:SM_GD_A + h + 1, :]
            gam_p = jnp.exp(jnp.where(tri_p, pick(chunks(gcol)) - gc_r, NEG_BIG))
            kk_p = pick(chunks(gram[e * R:(e + 1) * R]))
            qk_p = pick(chunks(gram[(2 + e) * R:(3 + e) * R]))
            a_ps.append(jnp.where(strict_p, pick(chunks(bcol)) * gam_p * kk_p, 0.0))
            p_bs.append((gam_p * qk_p).astype(bf16))
            gcols.append(gcol)
            bcols.append(bcol)

    ads = [jnp.where(blk8_p, a, 0.0) for a in a_ps]
    a2s = [pdot(ad, ad) for ad in ads]
    a4s = [pdot(a2, a2) for a2 in a2s]
    xs = [pdot(eye_p - ad, eye_p + a2) for ad, a2 in zip(ads, a2s)]
    xs = [pdot(x, eye_p + a4) for x, a4 in zip(xs, a4s)]
    for sh in range(3, lsh):
        msk = ((pt >> (sh + 1)) == (ps >> (sh + 1))) & (((pt >> sh) & 1) == 1) & (((ps >> sh) & 1) == 0)
        ts = [pdot(jnp.where(msk, a, 0.0), x) for a, x in zip(a_ps, xs)]
        xs = [x - pdot(x, t) for x, t in zip(xs, ts)]

    us, ws, qgs, kds, gls = [], [], [], [], []
    for j in range(nsl):
        uw, egs, eds, glh = [], [], [], []
        for e in range(2):
            h = 2 * j + e
            gcol, bcol = gcols[h], bcols[h]
            eg = jnp.exp(gcol)
            rhs = jnp.concatenate([bcol * vss[j], (bcol * eg) * kns[j]], axis=1)
            uw.append(_mdot([bd(p) for p in _split_bf16(xs[h], passes)], _split_bf16(rhs, passes)))
            lasts = [gcol[c * L + L - 1:c * L + L] for c in range(nch)]
            gcl = jnp.concatenate([jnp.broadcast_to(v, (L, 1)) for v in lasts], axis=0)
            egs.append(eg)
            eds.append(jnp.exp(gcl - gcol))
            glh.append([jnp.exp(v) for v in lasts])
        us.append(jnp.where(lo_r, uw[0][:, :LANES], uw[1][:, :LANES]))
        ws.append(jnp.where(lo_r, uw[0][:, LANES:], uw[1][:, LANES:]))
        qgs.append(qns[j] * jnp.where(lo_r, egs[0], egs[1]))
        kds.append(kns[j] * jnp.where(lo_r, eds[0], eds[1]))
        gls.append([jnp.where(lane_row < GD_DK, glh[0][c], glh[1][c]) for c in range(nch)])

    states = [s_scr[j] for j in range(nsl)]
    outs = [[] for _ in range(nsl)]
    for c in range(nch):
        rs = slice(c * L, (c + 1) * L)
        for j in range(nsl):
            s_prev = states[j]
            sb = s_prev.astype(bf16)
            delta = us[j][rs] - _bdot(ws[j][rs], sb)
            db = delta.astype(bf16)
            pieces = []
            if c > 0:
                pieces.append(jnp.zeros((c * L, LANES), bf16))
            pieces.append(db)
            if c < nch - 1:
                pieces.append(jnp.zeros(((nch - 1 - c) * L, LANES), bf16))
            dpad = jnp.concatenate(pieces, axis=0) if len(pieces) > 1 else db
            intra = jnp.where(lo_l, jnp.dot(p_bs[2 * j], dpad, preferred_element_type=f32),
                              jnp.dot(p_bs[2 * j + 1], dpad, preferred_element_type=f32))
            outs[j].append(_bdot(qgs[j][rs], sb) + intra)
            upd = _bdot_tn(kds[j][rs], db)
            states[j] = gls[j][c] * s_prev + jnp.where(blockdiag, upd, 0.0)

    for j in range(nsl):
        sl = slice(LANES * j, LANES * (j + 1))
        s_scr[j] = states[j]
        o = jnp.concatenate(outs[j], axis=0) if nch > 1 else outs[j][0]
        m_lo, m_hi = half_sums(o * o, lo_r)
        inv = jnp.where(lo_r, lax.rsqrt(m_lo * (1.0 / GD_DV) + RMS_EPS), lax.rsqrt(m_hi * (1.0 / GD_DV) + RMS_EPS))
        zs = z_ref[:, sl]
        o_ref[:, sl] = (o * inv * g_ref[:, sl] * (zs * _sigmoid(zs))).astype(o_ref.dtype)


def _gdn(proj, B, S, conv_w, a_log, dt_bias, norm_g):
    T = B * S
    L = min(GD_ROWS, S)
    nc = S // L
    KW = GD_K_W
    alog = jnp.zeros((1, LANES), f32).at[0, SM_GD_A:SM_GD_A + GD_HEADS].set(a_log)
    dtb = jnp.zeros((1, LANES), f32).at[0, SM_GD_A:SM_GD_A + GD_HEADS].set(dt_bias)
    g_row = jnp.tile(norm_g, GD_HEADS).reshape(1, GD_V_W)
    full = lambda shp: pl.BlockSpec(shp, lambda b, c: (0,) * len(shp))
    blk = lambda w, col: pl.BlockSpec((L, w), lambda b, c: (b * nc + c, col // w))
    return pl.pallas_call(
        _gd_kernel,
        out_shape=jax.ShapeDtypeStruct((T, GD_V_W), bf16),
        grid=(B, nc),
        in_specs=[blk(KW, COL_GD_Q), blk(KW, COL_GD_K), blk(KW, COL_GD_V), blk(KW, COL_GD_Z),
                  blk(LANES, COL_SMALL), full((CONV_WIDTH, 3 * KW)), full((1, LANES)), full((1, LANES)),
                  full((1, GD_V_W))],
        out_specs=pl.BlockSpec((L, GD_V_W), lambda b, c: (b * nc + c, 0)),
        scratch_shapes=[pltpu.VMEM((L + SUBLANES, 3 * KW), f32),
                        pltpu.VMEM((KW // LANES, LANES, LANES), f32)],
        compiler_params=_cparams("parallel", "arbitrary"),
        name="gated_deltanet",
    )(proj, proj, proj, proj, proj, conv_w, alog, dtb, g_row)


def _merge_kernel(ha_ref, hb_ref, hc_ref, g0_ref, g1_ref, g2_ref, x_ref,
                  wa_ref, wb_ref, wc_ref, wo_ref, lg_ref, lb_ref, x1_ref, x1p_ref):
    bra = jnp.dot(ha_ref[...], wa_ref[...], preferred_element_type=f32)
    brb = jnp.dot(hb_ref[...], wb_ref[...], preferred_element_type=f32)
    brc = jnp.dot(hc_ref[...], wc_ref[...], preferred_element_type=f32)
    merged = _sigmoid(g0_ref[...]) * bra + _sigmoid(g1_ref[...]) * brb + _sigmoid(g2_ref[...]) * brc
    y = jnp.dot(merged.astype(bf16), wo_ref[...], preferred_element_type=f32)
    out = _layer_norm(DN_ALPHA * x_ref[...] + y, lg_ref[...], lb_ref[...])
    x1_ref[...] = out
    x1p_ref[...] = _pack_halves(out)


def _merge(ha, hb, hc, proj, x2d, wa, wb, wc, wo, lg, lb):
    T = x2d.shape[0]
    D = D_MODEL
    tm = min(256, T)
    full = lambda shp: pl.BlockSpec(shp, lambda i: (0,) * len(shp))
    rows = lambda w: pl.BlockSpec((tm, w), lambda i: (i, 0))
    gate = lambda n: pl.BlockSpec((tm, D), lambda i: (i, COL_GATE // D + n))
    return pl.pallas_call(
        _merge_kernel,
        out_shape=(jax.ShapeDtypeStruct((T, D), f32), jax.ShapeDtypeStruct((T, D // 2), jnp.uint32)),
        grid=(T // tm,),
        in_specs=[rows(LRU_WIDTH), rows(ML_V_W), rows(GD_V_W), gate(0), gate(1), gate(2), rows(D),
                  full((LRU_WIDTH, D)), full((ML_V_W, D)), full((GD_V_W, D)), full((D, D)),
                  full((1, D)), full((1, D))],
        out_specs=(rows(D), rows(D // 2)),
        compiler_params=_cparams("parallel"),
        name="merge_outproj_ln",
    )(ha, hb, hc, proj, proj, proj, x2d, wa.astype(bf16), wb.astype(bf16), wc.astype(bf16),
      wo.astype(bf16), lg.reshape(1, D), lb.reshape(1, D))


PK_IDX = 0
PK_GATE = 4
PK_RANK = 8


def _router_kernel(x_ref, rw_ref, rb_ref, pk_ref, cnt_ref, carry):
    tr = x_ref.shape[0]

    @pl.when(pl.program_id(0) == 0)
    def _():
        carry[...] = jnp.zeros(carry.shape, f32)

    logits = _mdot(_split_bf16(x_ref[...], 3), _split_bf16(rw_ref[...], 3)) + rb_ref[...]
    lane = lax.broadcasted_iota(jnp.int32, (tr, LANES), 1).astype(f32)
    vals = logits
    idxs, tops = [], []
    for _ in range(TOP_K):
        m = jnp.max(vals, axis=1, keepdims=True)
        idx = jnp.min(jnp.where(vals == m, lane, float(LANES)), axis=1, keepdims=True)
        idxs.append(idx)
        tops.append(m)
        vals = jnp.where(lane == idx, NEG_BIG * 2.0, vals)
    es = [jnp.exp(t - tops[0]) for t in tops]
    tot = es[0] + es[1] + es[2] + es[3]
    onehots = [lane == idx for idx in idxs]
    sel = jnp.zeros((tr, LANES), f32)
    for oh in onehots:
        sel = sel + oh.astype(f32)
    ri = lax.broadcasted_iota(jnp.int32, (tr, tr), 0)
    ci = lax.broadcasted_iota(jnp.int32, (tr, tr), 1)
    before = jnp.dot((ri > ci).astype(bf16), sel.astype(bf16), preferred_element_type=f32) + carry[...]
    carry[...] = carry[...] + jnp.sum(sel, axis=0, keepdims=True)
    cnt_ref[...] = carry[...]
    packed = jnp.zeros((tr, LANES), f32)
    for kk in range(TOP_K):
        rank = jnp.sum(jnp.where(onehots[kk], before, 0.0), axis=1, keepdims=True)
        packed = jnp.where(lane == float(PK_IDX + kk), idxs[kk], packed)
        packed = jnp.where(lane == float(PK_GATE + kk), es[kk] / tot, packed)
        packed = jnp.where(lane == float(PK_RANK + kk), rank, packed)
    pk_ref[...] = packed


def _router(x1, router_w, router_b):
    T = x1.shape[0]
    tr = min(256, T)
    rw = jnp.zeros((D_MODEL, LANES), f32).at[:, :N_EXPERTS].set(router_w)
    rb = jnp.full((1, LANES), NEG_BIG, f32).at[0, :N_EXPERTS].set(router_b)
    full = lambda shp: pl.BlockSpec(shp, lambda i: (0,) * len(shp))
    return pl.pallas_call(
        _router_kernel,
        out_shape=(jax.ShapeDtypeStruct((T, LANES), f32), jax.ShapeDtypeStruct((1, LANES), f32)),
        grid=(T // tr,),
        in_specs=[pl.BlockSpec((tr, D_MODEL), lambda i: (i, 0)), full((D_MODEL, LANES)), full((1, LANES))],
        out_specs=(pl.BlockSpec((tr, LANES), lambda i: (i, 0)), full((1, LANES))),
        scratch_shapes=[pltpu.VMEM((1, LANES), f32)],
        compiler_params=_cparams("arbitrary"),
        name="router",
    )(x1, rw, rb)


def _expert_kernel(be_ref, nu_ref, x_ref, w1_ref, b1_ref, w2_ref, b2_ref, o_ref, w1b, w2b):
    i = pl.program_id(0)
    used = i < nu_ref[0]
    new_expert = jnp.logical_or(i == 0, be_ref[i] != be_ref[jnp.maximum(i - 1, 0)])

    @pl.when(jnp.logical_and(used, new_expert))
    def _():
        w1b[...] = w1_ref[0, 0].astype(bf16)
        w2b[...] = w2_ref[0, 0].astype(bf16)

    @pl.when(used)
    def _():
        x = _unpack_halves(x_ref[...]).astype(bf16)
        hdn = jnp.dot(x, w1b[...], preferred_element_type=f32) + b1_ref[0, 0]
        glu = jnp.minimum(hdn[:, :D_FF], SWIGLU_LIMIT)
        lin = jnp.clip(hdn[:, D_FF:], -SWIGLU_LIMIT, SWIGLU_LIMIT)
        act = glu * _sigmoid(SWIGLU_ALPHA * glu) * (lin + 1.0)
        y = jnp.dot(act.astype(bf16), w2b[...], preferred_element_type=f32) + b2_ref[0, 0]
        o_ref[...] = _pack_halves(y)

    @pl.when(i >= nu_ref[0])
    def _():
        o_ref[...] = jnp.zeros(o_ref.shape, o_ref.dtype)


def _experts(xs, block_e, n_used, layer, w1, b1, w2, b2):
    P = xs.shape[0]
    nb = P // MOE_BLOCK
    D = D_MODEL
    nl = w1.shape[0]
    grid_spec = pltpu.PrefetchScalarGridSpec(
        num_scalar_prefetch=2,
        grid=(nb,),
        in_specs=[pl.BlockSpec((MOE_BLOCK, D // 2), lambda i, be, nu: (i, 0)),
                  pl.BlockSpec((1, 1, D, 2 * D_FF), lambda i, be, nu: (layer, be[i], 0, 0)),
                  pl.BlockSpec((1, 1, 1, 2 * D_FF), lambda i, be, nu: (layer, be[i], 0, 0)),
                  pl.BlockSpec((1, 1, D_FF, D), lambda i, be, nu: (layer, be[i], 0, 0)),
                  pl.BlockSpec((1, 1, 1, D), lambda i, be, nu: (layer, be[i], 0, 0))],
        out_specs=pl.BlockSpec((MOE_BLOCK, D // 2), lambda i, be, nu: (i, 0)),
        scratch_shapes=[pltpu.VMEM((D, 2 * D_FF), bf16), pltpu.VMEM((D_FF, D), bf16)],
    )
    return pl.pallas_call(
        _expert_kernel,
        out_shape=jax.ShapeDtypeStruct((P, D // 2), jnp.uint32),
        grid_spec=grid_spec,
        compiler_params=_cparams("arbitrary"),
        name="experts",
    )(block_e, n_used, xs, w1, b1.reshape(nl, N_EXPERTS, 1, 2 * D_FF), w2, b2.reshape(nl, N_EXPERTS, 1, D))


def _combine_kernel(y0_ref, y1_ref, y2_ref, y3_ref, pk_ref, x_ref, lg_ref, lb_ref, o_ref):
    pk = pk_ref[...]
    y = pk[:, PK_GATE:PK_GATE + 1] * _unpack_halves(y0_ref[...])
    for kk, y_ref in ((1, y1_ref), (2, y2_ref), (3, y3_ref)):
        y = y + pk[:, PK_GATE + kk:PK_GATE + kk + 1] * _unpack_halves(y_ref[...])
    o_ref[...] = _layer_norm(DN_ALPHA * x_ref[...] + y, lg_ref[...], lb_ref[...])


def _combine(yg, packed, x1, lg, lb):
    T, D = x1.shape
    tm = min(256, T)
    nt = T // tm
    full = lambda shp: pl.BlockSpec(shp, lambda i: (0,) * len(shp))
    choice = lambda kk: pl.BlockSpec((tm, D // 2), lambda i: (kk * nt + i, 0))
    return pl.pallas_call(
        _combine_kernel,
        out_shape=jax.ShapeDtypeStruct((T, D), f32),
        grid=(nt,),
        in_specs=[choice(0), choice(1), choice(2), choice(3), pl.BlockSpec((tm, LANES), lambda i: (i, 0)),
                  pl.BlockSpec((tm, D), lambda i: (i, 0)), full((1, D)), full((1, D))],
        out_specs=pl.BlockSpec((tm, D), lambda i: (i, 0)),
        compiler_params=_cparams("parallel"),
        name="combine_ln",
    )(yg, yg, yg, yg, packed, x1, lg.reshape(1, D), lb.reshape(1, D))


def _sc_mesh():
    return plsc.VectorSubcoreMesh(core_axis_name="c", subcore_axis_name="s")


def _sc_dispatch(x, posk, n_slots):
    t_tokens, w = x.shape

    @functools.partial(pl.kernel, out_type=jax.ShapeDtypeStruct((n_slots, w), x.dtype), mesh=_sc_mesh(),
                       name="sc_dispatch")
    def k(x_hbm, p_hbm, o_hbm):
        def body(x_vmem, p_vmem):
            for kk in range(TOP_K):
                pltpu.sync_copy(x_vmem, o_hbm.at[p_vmem.at[kk]])

        pltpu.emit_pipeline(
            body,
            grid=(t_tokens // SC_ROWS,),
            in_specs=[pl.BlockSpec((SC_ROWS, w), lambda i: (i, 0), pipeline_mode=pl.Buffered(1)),
                      pl.BlockSpec((TOP_K, SC_ROWS), lambda i: (0, i))],
            out_specs=[],
            core_axis_name=("c", "s"),
            dimension_semantics=(pltpu.PARALLEL,),
        )(x_hbm, p_hbm)

    return k(x, posk)


def _sc_gather(table, idx):
    n = idx.shape[0]
    w = table.shape[1]

    @functools.partial(pl.kernel, out_type=jax.ShapeDtypeStruct((n, w), table.dtype), mesh=_sc_mesh(),
                       name="sc_gather")
    def k(t_hbm, i_hbm, o_hbm):
        def body(i_vmem, o_vmem):
            pltpu.sync_copy(t_hbm.at[i_vmem.at[0]], o_vmem)

        pltpu.emit_pipeline(
            body,
            grid=(n // SC_ROWS,),
            in_specs=[pl.BlockSpec((1, SC_ROWS), lambda i: (0, i))],
            out_specs=[pl.BlockSpec((SC_ROWS, w), lambda i: (i, 0), pipeline_mode=pl.Buffered(1))],
            core_axis_name=("c", "s"),
            dimension_semantics=(pltpu.PARALLEL,),
        )(i_hbm, o_hbm)

    return k(table, idx.reshape(1, n))


def _moe(x1, x1p, router_w, router_b, layer, w1, b1, w2, b2, lg, lb):
    T, D = x1.shape
    A = T * TOP_K
    packed, cnt = _router(x1, router_w, router_b)
    idx = packed[:, PK_IDX:PK_IDX + TOP_K].astype(jnp.int32)
    rank = packed[:, PK_RANK:PK_RANK + TOP_K].astype(jnp.int32)
    counts = cnt[0, :N_EXPERTS].astype(jnp.int32)
    padded = ((counts + MOE_BLOCK - 1) // MOE_BLOCK) * MOE_BLOCK
    pad_end = jnp.cumsum(padded)
    pad_start = pad_end - padded
    posk = (pad_start[idx] + rank).T
    n_blocks = -(-A // MOE_BLOCK) + N_EXPERTS
    P = n_blocks * MOE_BLOCK
    starts = jnp.arange(n_blocks, dtype=jnp.int32) * MOE_BLOCK
    block_e = jnp.minimum(jnp.sum((pad_end[None, :] <= starts[:, None]).astype(jnp.int32), axis=1), N_EXPERTS - 1)
    n_used = (pad_end[-1:] // MOE_BLOCK).astype(jnp.int32)
    xs = _sc_dispatch(x1p, posk, P)
    ys = _experts(xs, block_e, n_used, layer, w1, b1, w2, b2)
    yg = _sc_gather(ys, posk.reshape(A))
    return _combine(yg, packed, x1, lg, lb)


def _layer(x2d, B, S, p, layer, stacked):
    proj = _proj(x2d, _arrange_w_in(p['w_in']))
    ha = _lru(proj, B, S, p['lru_conv_w'], p['lru_conv_b'], p['lru_wa'], p['lru_ba'], p['lru_wx'],
              p['lru_bx'], p['lru_lambda'])
    hb = _mlstm(proj, B, S, p['ml_i_bias'], p['ml_f_bias'], p['ml_norm_g'])
    hc = _gdn(proj, B, S, p['gd_conv_w'], p['gd_a_log'], p['gd_dt_bias'], p['gd_norm_g'])
    x1, x1p = _merge(ha, hb, hc, proj, x2d, p['w_br_lru'], p['w_br_ml'], p['w_br_gd'], p['w_out'],
                     p['ln1_g'], p['ln1_b'])
    return _moe(x1, x1p, p['router_w'], p['router_b'], layer, stacked['exp_w1'], stacked['exp_b1'],
                stacked['exp_w2'], stacked['exp_b2'], p['ln2_g'], p['ln2_b'])


def kernel(x, w_in, lru_conv_w, lru_conv_b, lru_wa, lru_ba, lru_wx, lru_bx, lru_lambda, ml_i_bias, ml_f_bias, ml_norm_g, gd_conv_w, gd_a_log, gd_dt_bias, gd_norm_g, w_br_lru, w_br_ml, w_br_gd, w_out, ln1_g, ln1_b, router_w, router_b, exp_w1, exp_b1, exp_w2, exp_b2, ln2_g, ln2_b):
    B, S, D = x.shape
    params = dict(w_in=w_in, lru_conv_w=lru_conv_w, lru_conv_b=lru_conv_b, lru_wa=lru_wa, lru_ba=lru_ba,
                  lru_wx=lru_wx, lru_bx=lru_bx, lru_lambda=lru_lambda, ml_i_bias=ml_i_bias,
                  ml_f_bias=ml_f_bias, ml_norm_g=ml_norm_g, gd_conv_w=gd_conv_w, gd_a_log=gd_a_log,
                  gd_dt_bias=gd_dt_bias, gd_norm_g=gd_norm_g, w_br_lru=w_br_lru, w_br_ml=w_br_ml,
                  w_br_gd=w_br_gd, w_out=w_out, ln1_g=ln1_g, ln1_b=ln1_b, router_w=router_w,
                  router_b=router_b, ln2_g=ln2_g, ln2_b=ln2_b)
    stacked = dict(exp_w1=exp_w1, exp_b1=exp_b1, exp_w2=exp_w2, exp_b2=exp_b2)
    h = x.reshape(B * S, D)
    for l in range(w_in.shape[0]):
        h = _layer(h, B, S, {k: v[l] for k, v in params.items()}, l, stacked)
    return h.reshape(B, S, D)
```

```python
import functools
import math

import jax
import jax.numpy as jnp
from jax import lax
from jax.experimental import pallas as pl
from jax.experimental.pallas import tpu as pltpu
from jax.experimental.pallas import tpu_sc as plsc

f32 = jnp.float32
bf16 = jnp.bfloat16

D_MODEL = 1024
LRU_WIDTH = 1024
LRU_BLOCKS = 8
LRU_C = 8.0
CONV_WIDTH = 4
ML_HEADS = 4
ML_DQK = 64
ML_DV = 128
GD_HEADS = 8
GD_DK = 64
GD_DV = 64
N_EXPERTS = 32
TOP_K = 4
D_FF = 1024
SWIGLU_LIMIT = 7.0
SWIGLU_ALPHA = 1.702
MOE_BLOCK = 512
N_BRANCH = 3
DEPTH = 2
DN_ALPHA = (2.0 * DEPTH) ** 0.25
LN_EPS = 1e-5
RMS_EPS = 1e-6

ML_QK_W = ML_HEADS * ML_DQK
ML_V_W = ML_HEADS * ML_DV
GD_K_W = GD_HEADS * GD_DK
GD_V_W = GD_HEADS * GD_DV
IN_SPLITS = (LRU_WIDTH, LRU_WIDTH, ML_QK_W, ML_QK_W, ML_V_W, ML_HEADS, ML_HEADS, ML_V_W,
             GD_K_W, GD_K_W, GD_V_W, GD_HEADS, GD_HEADS, GD_V_W, N_BRANCH * D_MODEL)

LANES = 128
SUBLANES = 8
NEG_BIG = -1e30

SM_ML_I = 0
SM_ML_F = 4
SM_GD_A = 8
SM_GD_B = 16

ML_CHUNK = 256
GD_CHUNK = 64
GD_ROWS = 256
GD_SOLVE_PASSES = 1
VMEM_LIMIT = 56 * 1024 * 1024
SC_ROWS = 128


def _cparams(*sem):
    return pltpu.CompilerParams(dimension_semantics=sem, vmem_limit_bytes=VMEM_LIMIT)


def _hdot(a, b):
    return jnp.dot(a, b, precision=lax.Precision.HIGHEST, preferred_element_type=f32)


def _bdot(a, b):
    return jnp.dot(a.astype(bf16), b.astype(bf16), preferred_element_type=f32)


def _bdot_nt(a, b):
    return lax.dot_general(a.astype(bf16), b.astype(bf16), (((1,), (1,)), ((), ())),
                           preferred_element_type=f32)


def _bdot_tn(a, b):
    return lax.dot_general(a.astype(bf16), b.astype(bf16), (((0,), (0,)), ((), ())),
                           preferred_element_type=f32)


def _sigmoid(x):
    return 1.0 / (1.0 + jnp.exp(-x))


def _softplus(x):
    return jnp.maximum(x, 0.0) + jnp.log1p(jnp.exp(-jnp.abs(x)))


def _log_sigmoid(x):
    return jnp.minimum(x, 0.0) - jnp.log1p(jnp.exp(-jnp.abs(x)))


def _pack_halves(v):
    n = v.shape[1] // 2
    hi = lax.bitcast_convert_type(v[:, :n].astype(bf16).astype(f32), jnp.uint32)
    lo = lax.bitcast_convert_type(v[:, n:].astype(bf16).astype(f32), jnp.uint32)
    return hi | (lo >> 16)


def _unpack_halves(p):
    hi = lax.bitcast_convert_type(p & jnp.uint32(0xFFFF0000), f32)
    lo = lax.bitcast_convert_type(p << 16, f32)
    return jnp.concatenate([hi, lo], axis=1)


def _layer_norm(z, g, b):
    mu = jnp.mean(z, axis=-1, keepdims=True)
    zc = z - mu
    var = jnp.mean(zc * zc, axis=-1, keepdims=True)
    return zc * lax.rsqrt(var + LN_EPS) * g + b


def _split_w_in(w):
    pts = []
    acc = 0
    for s in IN_SPLITS[:-1]:
        acc += s
        pts.append(acc)
    (lru_x, lru_y, ml_q, ml_k, ml_v, ml_i, ml_f, ml_o,
     gd_q, gd_k, gd_v, gd_a, gd_b, gd_z, gate) = jnp.split(w, pts, axis=1)
    pad = jnp.zeros((w.shape[0], LANES - 2 * ML_HEADS - 2 * GD_HEADS), w.dtype)
    small = jnp.concatenate([ml_i, ml_f, gd_a, gd_b, pad], axis=1)
    cat = lambda *cols: jnp.concatenate(cols, axis=1).astype(bf16)
    return dict(lru=cat(lru_x, lru_y), ml=cat(ml_q, ml_k, ml_v, ml_o, small),
                gd=cat(gd_q, gd_k, gd_v, gd_z, small), gate=gate.astype(bf16))


def _project(x_ref, w_ref):
    return jnp.dot(x_ref[...].astype(bf16), w_ref[...], preferred_element_type=f32)


def _lru_kernel(xin_ref, xnext_ref, w_ref, cw_ref, cb_ref, wcat_ref, ba_ref, bx_ref, lam_ref, o_ref,
                xext, pr_even, pr_odd, a_scr, u_scr, carry):
    ts = xin_ref.shape[0]
    W = a_scr.shape[1]
    bw = W // LRU_BLOCKS
    step = pl.program_id(0) * pl.num_programs(1) + pl.program_id(1)

    @pl.when(step == 0)
    def _():
        pr_even[...] = _project(xin_ref, w_ref)

    @pl.when(pl.program_id(1) == 0)
    def _():
        xext[0:SUBLANES, :] = jnp.zeros((SUBLANES, W), f32)
        carry[...] = jnp.zeros((1, W), f32)

    def tile(cur, nxt):
        x = cur[:, 0:W]
        xext[SUBLANES:SUBLANES + ts, :] = x
        xa = (cw_ref[3:4, :] * x
              + cw_ref[2:3, :] * xext[SUBLANES - 1:SUBLANES - 1 + ts, :]
              + cw_ref[1:2, :] * xext[SUBLANES - 2:SUBLANES - 2 + ts, :]
              + cw_ref[0:1, :] * xext[SUBLANES - 3:SUBLANES - 3 + ts, :]) + cb_ref[...]
        xext[0:SUBLANES, :] = xext[ts:ts + SUBLANES, :]

        xnb = xnext_ref[...].astype(bf16)
        pw = 2 * W // LRU_BLOCKS
        cdec = -LRU_C * _softplus(-lam_ref[...])
        for h in range(LRU_BLOCKS):
            sl = slice(bw * h, bw * (h + 1))
            psl = slice(pw * h, pw * (h + 1))
            nxt[:, psl] = jnp.dot(xnb, w_ref[:, psl], preferred_element_type=f32)
            xh = xa[:, sl]
            g = jnp.dot(xh.astype(bf16), wcat_ref[h], preferred_element_type=f32)
            r = _sigmoid(g[:, :bw] + ba_ref[:, sl])
            ig = _sigmoid(g[:, bw:] + bx_ref[:, sl])
            log_a = r * cdec[:, sl]
            a = jnp.exp(log_a)
            a_scr[:, sl] = a
            u_scr[:, sl] = jnp.sqrt(jnp.tanh(-log_a) * (1.0 + a * a)) * (ig * xh)

        row = lax.broadcasted_iota(jnp.int32, (SUBLANES, W), 0)

        def body(g, cr):
            off = pl.multiple_of(g * SUBLANES, SUBLANES)
            A = a_scr[pl.ds(off, SUBLANES), :]
            U = u_scr[pl.ds(off, SUBLANES), :]
            for s in (1, 2, 4):
                a_sh = pltpu.roll(A, s, 0)
                u_sh = pltpu.roll(U, s, 0)
                m = row >= s
                U = jnp.where(m, A * u_sh + U, U)
                A = jnp.where(m, A * a_sh, A)
            H = A * cr + U
            u_scr[pl.ds(off, SUBLANES), :] = H
            return H[SUBLANES - 1:SUBLANES, :]

        carry[...] = lax.fori_loop(0, ts // SUBLANES, body, carry[...], unroll=2)
        o_ref[...] = (u_scr[...] * jax.nn.gelu(cur[:, W:2 * W])).astype(o_ref.dtype)

    @pl.when(lax.rem(step, 2) == 0)
    def _():
        tile(pr_even, pr_odd)

    @pl.when(lax.rem(step, 2) == 1)
    def _():
        tile(pr_odd, pr_even)


def _lru(x2d, w_lru, B, S, cw, cb, wa, ba, wx, bx, lam):
    T = B * S
    ts = min(512, S)
    nt = S // ts
    W = LRU_WIDTH
    wcat = jnp.concatenate([wa, wx], axis=-1).astype(bf16)
    row = lambda v: v.reshape(1, W)
    full = lambda shp: pl.BlockSpec(shp, lambda b, c: (0,) * len(shp))
    last = B * nt - 1
    return pl.pallas_call(
        _lru_kernel,
        out_shape=jax.ShapeDtypeStruct((T, W), bf16),
        grid=(B, nt),
        in_specs=[pl.BlockSpec((ts, D_MODEL), lambda b, c: (b * nt + c, 0)),
                  pl.BlockSpec((ts, D_MODEL), lambda b, c: (jnp.minimum(b * nt + c + 1, last), 0)),
                  full(w_lru.shape), full((CONV_WIDTH, W)), full((1, W)), full(wcat.shape),
                  full((1, W)), full((1, W)), full((1, W))],
        out_specs=pl.BlockSpec((ts, W), lambda b, c: (b * nt + c, 0)),
        scratch_shapes=[pltpu.VMEM((ts + SUBLANES, W), f32), pltpu.VMEM((ts, 2 * W), f32),
                        pltpu.VMEM((ts, 2 * W), f32), pltpu.VMEM((ts, W), f32), pltpu.VMEM((ts, W), f32),
                        pltpu.VMEM((1, W), f32)],
        compiler_params=_cparams("arbitrary", "arbitrary"),
        name="rg_lru",
    )(x2d, x2d, w_lru, cw, row(cb), wcat, row(ba), row(bx), row(lam))


def _ml_kernel(xin_ref, w_ref, bias_ref, g_ref, o_ref, c_scr, n_scr, m_scr):
    L = xin_ref.shape[0]

    @pl.when(pl.program_id(1) == 0)
    def _():
        c_scr[...] = jnp.zeros(c_scr.shape, f32)
        n_scr[...] = jnp.zeros(n_scr.shape, f32)
        m_scr[...] = jnp.zeros(m_scr.shape, f32)

    pr = _project(xin_ref, w_ref)
    q = pr[:, 0:ML_QK_W]
    k = pr[:, ML_QK_W:2 * ML_QK_W]
    v = pr[:, 2 * ML_QK_W:2 * ML_QK_W + ML_V_W]
    og = pr[:, 2 * ML_QK_W + ML_V_W:2 * ML_QK_W + 2 * ML_V_W]
    sm = pr[:, 2 * ML_QK_W + 2 * ML_V_W:] + bias_ref[...]
    logf = _log_sigmoid(sm)
    ri = lax.broadcasted_iota(jnp.int32, (L, L), 0)
    ci = lax.broadcasted_iota(jnp.int32, (L, L), 1)
    tri = ri >= ci
    b_all = _hdot(tri.astype(f32), logf)
    sm_t = sm.T
    b_t = b_all.T
    kb = k.astype(bf16)
    lane_head = lax.broadcasted_iota(jnp.int32, q.shape, 1) // ML_DQK
    scale = ML_DQK ** -0.5
    for h in range(ML_HEADS):
        sl = slice(ML_DV * h, ML_DV * (h + 1))
        ig_c = sm[:, SM_ML_I + h:SM_ML_I + h + 1]
        b_c = b_all[:, SM_ML_F + h:SM_ML_F + h + 1]
        ig_r = sm_t[SM_ML_I + h:SM_ML_I + h + 1, :]
        b_r = b_t[SM_ML_F + h:SM_ML_F + h + 1, :]
        b_last = b_c[L - 1:L, :]
        m_prev = m_scr[h:h + 1, 0:1]
        ge_r = b_last - b_r + ig_r
        ge_c = b_last - b_c + ig_c
        m_new = jnp.maximum(b_last + m_prev, jnp.max(ge_r, axis=1, keepdims=True))
        decay = jnp.exp(b_last + m_prev - m_new)
        w_c = jnp.exp(ge_c - m_new)
        dmat = jnp.where(tri, b_c - b_r + ig_r, NEG_BIG)
        inter = b_c + m_prev
        m_t = jnp.maximum(inter, jnp.max(dmat, axis=1, keepdims=True))
        dw = jnp.exp(dmat - m_t)
        iw = jnp.exp(inter - m_t)
        qm = jnp.where(lane_head == h, q, 0.0) * scale
        qmb = qm.astype(bf16)
        s = _bdot_nt(qmb, kb) * dw
        v_h = v[:, sl]
        n_row = n_scr[h:h + 1, :]
        num = _bdot(s, v_h) + iw * _bdot(qmb, c_scr[:, sl])
        den = jnp.sum(s, axis=1, keepdims=True) + iw * jnp.sum(qm * n_row, axis=1, keepdims=True)
        hh = num / jnp.maximum(jnp.abs(den), jnp.exp(-m_t))
        c_scr[:, sl] = decay * c_scr[:, sl] + _bdot_tn(kb, w_c * v_h)
        n_scr[h:h + 1, :] = decay * n_row + jnp.sum(w_c * k, axis=0, keepdims=True)
        m_scr[h:h + 1, :] = jnp.broadcast_to(m_new, (1, LANES))
        ms = jnp.mean(hh * hh, axis=1, keepdims=True)
        hn = hh * lax.rsqrt(ms + RMS_EPS) * g_ref[:, sl]
        o_ref[:, sl] = (hn * _sigmoid(og[:, sl])).astype(o_ref.dtype)


def _mlstm(x2d, w_ml, B, S, i_bias, f_bias, norm_g):
    T = B * S
    L = min(ML_CHUNK, S)
    nc = S // L
    bias = jnp.zeros((1, LANES), f32)
    bias = bias.at[0, SM_ML_I:SM_ML_I + ML_HEADS].set(i_bias).at[0, SM_ML_F:SM_ML_F + ML_HEADS].set(f_bias)
    full = lambda shp: pl.BlockSpec(shp, lambda b, c: (0,) * len(shp))
    return pl.pallas_call(
        _ml_kernel,
        out_shape=jax.ShapeDtypeStruct((T, ML_V_W), bf16),
        grid=(B, nc),
        in_specs=[pl.BlockSpec((L, D_MODEL), lambda b, c: (b * nc + c, 0)), full(w_ml.shape),
                  full((1, LANES)), full((1, ML_V_W))],
        out_specs=pl.BlockSpec((L, ML_V_W), lambda b, c: (b * nc + c, 0)),
        scratch_shapes=[pltpu.VMEM((ML_QK_W, ML_V_W), f32), pltpu.VMEM((SUBLANES, ML_QK_W), f32),
                        pltpu.VMEM((SUBLANES, LANES), f32)],
        compiler_params=_cparams("parallel", "arbitrary"),
        name="mlstm",
    )(x2d, w_ml, bias, norm_g.reshape(1, ML_V_W))


def _split_bf16(x, passes):
    hi = x.astype(bf16)
    if passes == 1:
        return (hi,)
    return (hi, (x - hi.astype(f32)).astype(bf16))


def _mdot(xs, ys):
    out = jnp.dot(xs[0], ys[0], preferred_element_type=f32)
    if len(xs) > 1:
        out = out + jnp.dot(xs[1], ys[0], preferred_element_type=f32)
        out = out + jnp.dot(xs[0], ys[1], preferred_element_type=f32)
    return out


def _gd_kernel(xin_ref, w_ref, cw_ref, alog_ref, dtb_ref, g_ref, o_ref, ext, s_scr):
    R = xin_ref.shape[0]
    KW = GD_K_W
    L = GD_CHUNK
    nch = R // L
    lsh = L.bit_length() - 1
    PW = nch * L
    nsl = KW // LANES
    passes = GD_SOLVE_PASSES

    @pl.when(pl.program_id(1) == 0)
    def _():
        ext[0:SUBLANES, :] = jnp.zeros((SUBLANES, ext.shape[1]), f32)
        s_scr[...] = jnp.zeros(s_scr.shape, f32)

    pr = _project(xin_ref, w_ref)
    z = pr[:, 3 * KW:4 * KW]
    sm = pr[:, 4 * KW:]
    ext[SUBLANES:SUBLANES + R, :] = pr[:, 0:3 * KW]
    xc = (cw_ref[3:4, :] * ext[SUBLANES:SUBLANES + R, :]
          + cw_ref[2:3, :] * ext[SUBLANES - 1:SUBLANES - 1 + R, :]
          + cw_ref[1:2, :] * ext[SUBLANES - 2:SUBLANES - 2 + R, :]
          + cw_ref[0:1, :] * ext[SUBLANES - 3:SUBLANES - 3 + R, :])
    ext[0:SUBLANES, :] = ext[R:R + SUBLANES, :]
    xc = xc * _sigmoid(xc)

    g_t = -jnp.exp(alog_ref[...]) * _softplus(sm + dtb_ref[...])
    beta_t = _sigmoid(sm)
    rr = lax.broadcasted_iota(jnp.int32, (R, R), 0)
    cc = lax.broadcasted_iota(jnp.int32, (R, R), 1)
    cum = jnp.where(((rr >> lsh) == (cc >> lsh)) & (rr >= cc), 1.0, 0.0)
    gc_all = _hdot(cum, g_t)
    gc_t = gc_all.T

    pt = lax.broadcasted_iota(jnp.int32, (L, PW), 0)
    plane = lax.broadcasted_iota(jnp.int32, (L, PW), 1)
    ps = plane & (L - 1)
    pc = plane >> lsh
    tri_p = pt >= ps
    strict_p = pt > ps
    eye_p = jnp.where(pt == ps, 1.0, 0.0)
    blk8_p = (pt >> 3) == (ps >> 3)
    cmask_b = [jnp.where(pc == c, 1.0, 0.0).astype(bf16) for c in range(nch)]

    def pick(parts):
        out = parts[nch - 1]
        for c in range(nch - 2, -1, -1):
            out = jnp.where(pc == c, parts[c], out)
        return out

    def chunks(col):
        return [col[c * L:(c + 1) * L] for c in range(nch)]

    def bd(yb):
        return jnp.concatenate([yb * cmask_b[c] for c in range(nch)], axis=0)

    def pdot(xp, yp):
        return _mdot(_split_bf16(xp, passes), [bd(p) for p in _split_bf16(yp, passes)])

    lane_r = lax.broadcasted_iota(jnp.int32, (R, LANES), 1)
    lo_r = lane_r < GD_DK
    lane_l = lax.broadcasted_iota(jnp.int32, (L, LANES), 1)
    lo_l = lane_l < GD_DK
    lane_row = lax.broadcasted_iota(jnp.int32, (1, LANES), 1)
    r2 = lax.broadcasted_iota(jnp.int32, (LANES, LANES), 0)
    c2 = lax.broadcasted_iota(jnp.int32, (LANES, LANES), 1)
    blockdiag = (r2 < GD_DK) == (c2 < GD_DK)

    def half_sums(y, lo):
        s_lo = jnp.sum(jnp.where(lo, y, 0.0), axis=1, keepdims=True)
        s_hi = jnp.sum(jnp.where(lo, 0.0, y), axis=1, keepdims=True)
        return s_lo, s_hi

    def l2n(y):
        s_lo, s_hi = half_sums(y * y, lo_r)
        return y * jnp.where(lo_r, lax.rsqrt(s_lo + RMS_EPS), lax.rsqrt(s_hi + RMS_EPS))

    qns, kns, vss, a_ps, p_bs, gcols, bcols = [], [], [], [], [], [], []
    for j in range(nsl):
        qn = l2n(xc[:, LANES * j:LANES * (j + 1)]) * (GD_DK ** -0.5)
        kn = l2n(xc[:, KW + LANES * j:KW + LANES * (j + 1)])
        qns.append(qn)
        kns.append(kn)
        vss.append(xc[:, 2 * KW + LANES * j:2 * KW + LANES * (j + 1)])
        lhs = jnp.concatenate([jnp.where(lo_r, kn, 0.0), jnp.where(lo_r, 0.0, kn),
                               jnp.where(lo_r, qn, 0.0), jnp.where(lo_r, 0.0, qn)], axis=0)
        gram = _bdot_nt(lhs, kn)
        for e in range(2):
            h = 2 * j + e
            gcol = gc_all[:, SM_GD_A + h:SM_GD_A + h + 1]
            bcol = beta_t[:, SM_GD_B + h:SM_GD_B + h + 1]
            gc_r = gc_t[SM_GD_A + h:SM_GD_A + h + 1, :]
            gam_p = jnp.exp(jnp.where(tri_p, pick(chunks(gcol)) - gc_r, NEG_BIG))
            kk_p = pick(chunks(gram[e * R:(e + 1) * R]))
            qk_p = pick(chunks(gram[(2 + e) * R:(3 + e) * R]))
            a_ps.append(jnp.where(strict_p, pick(chunks(bcol)) * gam_p * kk_p, 0.0))
            p_bs.append((gam_p * qk_p).astype(bf16))
            gcols.append(gcol)
            bcols.append(bcol)

    ads = [jnp.where(blk8_p, a, 0.0) for a in a_ps]
    a2s = [pdot(ad, ad) for ad in ads]
    a4s = [pdot(a2, a2) for a2 in a2s]
    xs = [pdot(eye_p - ad, eye_p + a2) for ad, a2 in zip(ads, a2s)]
    xs = [pdot(x, eye_p + a4) for x, a4 in zip(xs, a4s)]
    for sh in range(3, lsh):
        msk = ((pt >> (sh + 1)) == (ps >> (sh + 1))) & (((pt >> sh) & 1) == 1) & (((ps >> sh) & 1) == 0)
        ts = [pdot(jnp.where(msk, a, 0.0), x) for a, x in zip(a_ps, xs)]
        xs = [x - pdot(x, t) for x, t in zip(xs, ts)]

    us, ws, qgs, kds, gls = [], [], [], [], []
    for j in range(nsl):
        uw, egs, eds, glh = [], [], [], []
        for e in range(2):
            h = 2 * j + e
            gcol, bcol = gcols[h], bcols[h]
            eg = jnp.exp(gcol)
            rhs = jnp.concatenate([bcol * vss[j], (bcol * eg) * kns[j]], axis=1)
            uw.append(_mdot([bd(p) for p in _split_bf16(xs[h], passes)], _split_bf16(rhs, passes)))
            lasts = [gcol[c * L + L - 1:c * L + L] for c in range(nch)]
            gcl = jnp.concatenate([jnp.broadcast_to(v, (L, 1)) for v in lasts], axis=0)
            egs.append(eg)
            eds.append(jnp.exp(gcl - gcol))
            glh.append([jnp.exp(v) for v in lasts])
        us.append(jnp.where(lo_r, uw[0][:, :LANES], uw[1][:, :LANES]))
        ws.append(jnp.where(lo_r, uw[0][:, LANES:], uw[1][:, LANES:]))
        qgs.append(qns[j] * jnp.where(lo_r, egs[0], egs[1]))
        kds.append(kns[j] * jnp.where(lo_r, eds[0], eds[1]))
        gls.append([jnp.where(lane_row < GD_DK, glh[0][c], glh[1][c]) for c in range(nch)])

    states = [s_scr[j] for j in range(nsl)]
    outs = [[] for _ in range(nsl)]
    for c in range(nch):
        rs = slice(c * L, (c + 1) * L)
        for j in range(nsl):
            s_prev = states[j]
            sb = s_prev.astype(bf16)
            delta = us[j][rs] - _bdot(ws[j][rs], sb)
            db = delta.astype(bf16)
            pieces = []
            if c > 0:
                pieces.append(jnp.zeros((c * L, LANES), bf16))
            pieces.append(db)
            if c < nch - 1:
                pieces.append(jnp.zeros(((nch - 1 - c) * L, LANES), bf16))
            dpad = jnp.concatenate(pieces, axis=0) if len(pieces) > 1 else db
            intra = jnp.where(lo_l, jnp.dot(p_bs[2 * j], dpad, preferred_element_type=f32),
                              jnp.dot(p_bs[2 * j + 1], dpad, preferred_element_type=f32))
            outs[j].append(_bdot(qgs[j][rs], sb) + intra)
            upd = _bdot_tn(kds[j][rs], db)
            states[j] = gls[j][c] * s_prev + jnp.where(blockdiag, upd, 0.0)

    for j in range(nsl):
        sl = slice(LANES * j, LANES * (j + 1))
        s_scr[j] = states[j]
        o = jnp.concatenate(outs[j], axis=0) if nch > 1 else outs[j][0]
        m_lo, m_hi = half_sums(o * o, lo_r)
        inv = jnp.where(lo_r, lax.rsqrt(m_lo * (1.0 / GD_DV) + RMS_EPS), lax.rsqrt(m_hi * (1.0 / GD_DV) + RMS_EPS))
        zs = z[:, sl]
        o_ref[:, sl] = (o * inv * g_ref[:, sl] * (zs * _sigmoid(zs))).astype(o_ref.dtype)


def _gdn(x2d, w_gd, B, S, conv_w, a_log, dt_bias, norm_g):
    T = B * S
    L = min(GD_ROWS, S)
    nc = S // L
    KW = GD_K_W
    alog = jnp.zeros((1, LANES), f32).at[0, SM_GD_A:SM_GD_A + GD_HEADS].set(a_log)
    dtb = jnp.zeros((1, LANES), f32).at[0, SM_GD_A:SM_GD_A + GD_HEADS].set(dt_bias)
    g_row = jnp.tile(norm_g, GD_HEADS).reshape(1, GD_V_W)
    full = lambda shp: pl.BlockSpec(shp, lambda b, c: (0,) * len(shp))
    return pl.pallas_call(
        _gd_kernel,
        out_shape=jax.ShapeDtypeStruct((T, GD_V_W), bf16),
        grid=(B, nc),
        in_specs=[pl.BlockSpec((L, D_MODEL), lambda b, c: (b * nc + c, 0)), full(w_gd.shape),
                  full((CONV_WIDTH, 3 * KW)), full((1, LANES)), full((1, LANES)), full((1, GD_V_W))],
        out_specs=pl.BlockSpec((L, GD_V_W), lambda b, c: (b * nc + c, 0)),
        scratch_shapes=[pltpu.VMEM((L + SUBLANES, 3 * KW), f32),
                        pltpu.VMEM((KW // LANES, LANES, LANES), f32)],
        compiler_params=_cparams("parallel", "arbitrary"),
        name="gated_deltanet",
    )(x2d, w_gd, conv_w, alog, dtb, g_row)


def _merge_kernel(ha_ref, hb_ref, hc_ref, x_ref, wg_ref,
                  wa_ref, wb_ref, wc_ref, wo_ref, lg_ref, lb_ref, x1_ref, x1p_ref):
    D = x_ref.shape[1]
    gate_pre = _project(x_ref, wg_ref)
    bra = jnp.dot(ha_ref[...], wa_ref[...], preferred_element_type=f32)
    brb = jnp.dot(hb_ref[...], wb_ref[...], preferred_element_type=f32)
    brc = jnp.dot(hc_ref[...], wc_ref[...], preferred_element_type=f32)
    merged = (_sigmoid(gate_pre[:, 0:D]) * bra + _sigmoid(gate_pre[:, D:2 * D]) * brb
              + _sigmoid(gate_pre[:, 2 * D:3 * D]) * brc)
    y = jnp.dot(merged.astype(bf16), wo_ref[...], preferred_element_type=f32)
    out = _layer_norm(DN_ALPHA * x_ref[...] + y, lg_ref[...], lb_ref[...])
    x1_ref[...] = out
    x1p_ref[...] = _pack_halves(out)


def _merge(ha, hb, hc, x2d, w_gate, wa, wb, wc, wo, lg, lb):
    T = x2d.shape[0]
    D = D_MODEL
    tm = min(256, T)
    full = lambda shp: pl.BlockSpec(shp, lambda i: (0,) * len(shp))
    rows = lambda w: pl.BlockSpec((tm, w), lambda i: (i, 0))
    return pl.pallas_call(
        _merge_kernel,
        out_shape=(jax.ShapeDtypeStruct((T, D), f32), jax.ShapeDtypeStruct((T, D // 2), jnp.uint32)),
        grid=(T // tm,),
        in_specs=[rows(LRU_WIDTH), rows(ML_V_W), rows(GD_V_W), rows(D), full(w_gate.shape),
                  full((LRU_WIDTH, D)), full((ML_V_W, D)), full((GD_V_W, D)), full((D, D)),
                  full((1, D)), full((1, D))],
        out_specs=(rows(D), rows(D // 2)),
        compiler_params=_cparams("parallel"),
        name="merge_outproj_ln",
    )(ha, hb, hc, x2d, w_gate, wa.astype(bf16), wb.astype(bf16), wc.astype(bf16),
      wo.astype(bf16), lg.reshape(1, D), lb.reshape(1, D))


PK_IDX = 0
PK_GATE = 4
PK_RANK = 8


def _router_kernel(x_ref, rw_ref, rb_ref, pk_ref, cnt_ref, carry):
    tr = x_ref.shape[0]

    @pl.when(pl.program_id(0) == 0)
    def _():
        carry[...] = jnp.zeros(carry.shape, f32)

    logits = _mdot(_split_bf16(x_ref[...], 3), _split_bf16(rw_ref[...], 3)) + rb_ref[...]
    lane = lax.broadcasted_iota(jnp.int32, (tr, LANES), 1).astype(f32)
    vals = logits
    idxs, tops = [], []
    for _ in range(TOP_K):
        m = jnp.max(vals, axis=1, keepdims=True)
        idx = jnp.min(jnp.where(vals == m, lane, float(LANES)), axis=1, keepdims=True)
        idxs.append(idx)
        tops.append(m)
        vals = jnp.where(lane == idx, NEG_BIG * 2.0, vals)
    es = [jnp.exp(t - tops[0]) for t in tops]
    tot = es[0] + es[1] + es[2] + es[3]
    onehots = [lane == idx for idx in idxs]
    sel = jnp.zeros((tr, LANES), f32)
    for oh in onehots:
        sel = sel + oh.astype(f32)
    ri = lax.broadcasted_iota(jnp.int32, (tr, tr), 0)
    ci = lax.broadcasted_iota(jnp.int32, (tr, tr), 1)
    before = jnp.dot((ri > ci).astype(bf16), sel.astype(bf16), preferred_element_type=f32) + carry[...]
    carry[...] = carry[...] + jnp.sum(sel, axis=0, keepdims=True)
    cnt_ref[...] = carry[...]
    packed = jnp.zeros((tr, LANES), f32)
    for kk in range(TOP_K):
        rank = jnp.sum(jnp.where(onehots[kk], before, 0.0), axis=1, keepdims=True)
        packed = jnp.where(lane == float(PK_IDX + kk), idxs[kk], packed)
        packed = jnp.where(lane == float(PK_GATE + kk), es[kk] / tot, packed)
        packed = jnp.where(lane == float(PK_RANK + kk), rank, packed)
    pk_ref[...] = packed


def _router(x1, router_w, router_b):
    T = x1.shape[0]
    tr = min(256, T)
    rw = jnp.zeros((D_MODEL, LANES), f32).at[:, :N_EXPERTS].set(router_w)
    rb = jnp.full((1, LANES), NEG_BIG, f32).at[0, :N_EXPERTS].set(router_b)
    full = lambda shp: pl.BlockSpec(shp, lambda i: (0,) * len(shp))
    return pl.pallas_call(
        _router_kernel,
        out_shape=(jax.ShapeDtypeStruct((T, LANES), f32), jax.ShapeDtypeStruct((1, LANES), f32)),
        grid=(T // tr,),
        in_specs=[pl.BlockSpec((tr, D_MODEL), lambda i: (i, 0)), full((D_MODEL, LANES)), full((1, LANES))],
        out_specs=(pl.BlockSpec((tr, LANES), lambda i: (i, 0)), full((1, LANES))),
        scratch_shapes=[pltpu.VMEM((1, LANES), f32)],
        compiler_params=_cparams("arbitrary"),
        name="router",
    )(x1, rw, rb)


def _expert_kernel(be_ref, nu_ref, x_ref, w1_ref, b1_ref, w2_ref, b2_ref, o_ref, w1b, w2b):
    i = pl.program_id(0)
    used = i < nu_ref[0]
    new_expert = jnp.logical_or(i == 0, be_ref[i] != be_ref[jnp.maximum(i - 1, 0)])

    @pl.when(jnp.logical_and(used, new_expert))
    def _():
        w1b[...] = w1_ref[0, 0].astype(bf16)
        w2b[...] = w2_ref[0, 0].astype(bf16)

    @pl.when(used)
    def _():
        x = _unpack_halves(x_ref[...]).astype(bf16)
        hdn = jnp.dot(x, w1b[...], preferred_element_type=f32) + b1_ref[0, 0]
        glu = jnp.minimum(hdn[:, :D_FF], SWIGLU_LIMIT)
        lin = jnp.clip(hdn[:, D_FF:], -SWIGLU_LIMIT, SWIGLU_LIMIT)
        act = glu * _sigmoid(SWIGLU_ALPHA * glu) * (lin + 1.0)
        y = jnp.dot(act.astype(bf16), w2b[...], preferred_element_type=f32) + b2_ref[0, 0]
        o_ref[...] = _pack_halves(y)

    @pl.when(i >= nu_ref[0])
    def _():
        o_ref[...] = jnp.zeros(o_ref.shape, o_ref.dtype)


def _experts(xs, block_e, n_used, layer, w1, b1, w2, b2):
    P = xs.shape[0]
    nb = P // MOE_BLOCK
    D = D_MODEL
    nl = w1.shape[0]
    grid_spec = pltpu.PrefetchScalarGridSpec(
        num_scalar_prefetch=2,
        grid=(nb,),
        in_specs=[pl.BlockSpec((MOE_BLOCK, D // 2), lambda i, be, nu: (i, 0)),
                  pl.BlockSpec((1, 1, D, 2 * D_FF), lambda i, be, nu: (layer, be[i], 0, 0)),
                  pl.BlockSpec((1, 1, 1, 2 * D_FF), lambda i, be, nu: (layer, be[i], 0, 0)),
                  pl.BlockSpec((1, 1, D_FF, D), lambda i, be, nu: (layer, be[i], 0, 0)),
                  pl.BlockSpec((1, 1, 1, D), lambda i, be, nu: (layer, be[i], 0, 0))],
        out_specs=pl.BlockSpec((MOE_BLOCK, D // 2), lambda i, be, nu: (i, 0)),
        scratch_shapes=[pltpu.VMEM((D, 2 * D_FF), bf16), pltpu.VMEM((D_FF, D), bf16)],
    )
    return pl.pallas_call(
        _expert_kernel,
        out_shape=jax.ShapeDtypeStruct((P, D // 2), jnp.uint32),
        grid_spec=grid_spec,
        compiler_params=_cparams("arbitrary"),
        name="experts",
    )(block_e, n_used, xs, w1, b1.reshape(nl, N_EXPERTS, 1, 2 * D_FF), w2, b2.reshape(nl, N_EXPERTS, 1, D))


def _combine_kernel(y0_ref, y1_ref, y2_ref, y3_ref, pk_ref, x_ref, lg_ref, lb_ref, o_ref):
    pk = pk_ref[...]
    y = pk[:, PK_GATE:PK_GATE + 1] * _unpack_halves(y0_ref[...])
    for kk, y_ref in ((1, y1_ref), (2, y2_ref), (3, y3_ref)):
        y = y + pk[:, PK_GATE + kk:PK_GATE + kk + 1] * _unpack_halves(y_ref[...])
    o_ref[...] = _layer_norm(DN_ALPHA * x_ref[...] + y, lg_ref[...], lb_ref[...])


def _combine(yg, packed, x1, lg, lb):
    T, D = x1.shape
    tm = min(256, T)
    nt = T // tm
    full = lambda shp: pl.BlockSpec(shp, lambda i: (0,) * len(shp))
    choice = lambda kk: pl.BlockSpec((tm, D // 2), lambda i: (kk * nt + i, 0))
    return pl.pallas_call(
        _combine_kernel,
        out_shape=jax.ShapeDtypeStruct((T, D), f32),
        grid=(nt,),
        in_specs=[choice(0), choice(1), choice(2), choice(3), pl.BlockSpec((tm, LANES), lambda i: (i, 0)),
                  pl.BlockSpec((tm, D), lambda i: (i, 0)), full((1, D)), full((1, D))],
        out_specs=pl.BlockSpec((tm, D), lambda i: (i, 0)),
        compiler_params=_cparams("parallel"),
        name="combine_ln",
    )(yg, yg, yg, yg, packed, x1, lg.reshape(1, D), lb.reshape(1, D))


def _sc_mesh():
    return plsc.VectorSubcoreMesh(core_axis_name="c", subcore_axis_name="s")


def _sc_dispatch(x, posk, n_slots):
    t_tokens, w = x.shape

    @functools.partial(pl.kernel, out_type=jax.ShapeDtypeStruct((n_slots, w), x.dtype), mesh=_sc_mesh(),
                       name="sc_dispatch")
    def k(x_hbm, p_hbm, o_hbm):
        def body(x_vmem, p_vmem):
            for kk in range(TOP_K):
                pltpu.sync_copy(x_vmem, o_hbm.at[p_vmem.at[kk]])

        pltpu.emit_pipeline(
            body,
            grid=(t_tokens // SC_ROWS,),
            in_specs=[pl.BlockSpec((SC_ROWS, w), lambda i: (i, 0), pipeline_mode=pl.Buffered(1)),
                      pl.BlockSpec((TOP_K, SC_ROWS), lambda i: (0, i))],
            out_specs=[],
            core_axis_name=("c", "s"),
            dimension_semantics=(pltpu.PARALLEL,),
        )(x_hbm, p_hbm)

    return k(x, posk)


def _sc_gather(table, idx):
    n = idx.shape[0]
    w = table.shape[1]

    @functools.partial(pl.kernel, out_type=jax.ShapeDtypeStruct((n, w), table.dtype), mesh=_sc_mesh(),
                       name="sc_gather")
    def k(t_hbm, i_hbm, o_hbm):
        def body(i_vmem, o_vmem):
            pltpu.sync_copy(t_hbm.at[i_vmem.at[0]], o_vmem)

        pltpu.emit_pipeline(
            body,
            grid=(n // SC_ROWS,),
            in_specs=[pl.BlockSpec((1, SC_ROWS), lambda i: (0, i))],
            out_specs=[pl.BlockSpec((SC_ROWS, w), lambda i: (i, 0), pipeline_mode=pl.Buffered(1))],
            core_axis_name=("c", "s"),
            dimension_semantics=(pltpu.PARALLEL,),
        )(i_hbm, o_hbm)

    return k(table, idx.reshape(1, n))


def _moe(x1, x1p, router_w, router_b, layer, w1, b1, w2, b2, lg, lb):
    T, D = x1.shape
    A = T * TOP_K
    packed, cnt = _router(x1, router_w, router_b)
    idx = packed[:, PK_IDX:PK_IDX + TOP_K].astype(jnp.int32)
    rank = packed[:, PK_RANK:PK_RANK + TOP_K].astype(jnp.int32)
    counts = cnt[0, :N_EXPERTS].astype(jnp.int32)
    padded = ((counts + MOE_BLOCK - 1) // MOE_BLOCK) * MOE_BLOCK
    pad_end = jnp.cumsum(padded)
    pad_start = pad_end - padded
    posk = (pad_start[idx] + rank).T
    n_blocks = -(-A // MOE_BLOCK) + N_EXPERTS
    P = n_blocks * MOE_BLOCK
    starts = jnp.arange(n_blocks, dtype=jnp.int32) * MOE_BLOCK
    block_e = jnp.minimum(jnp.sum((pad_end[None, :] <= starts[:, None]).astype(jnp.int32), axis=1), N_EXPERTS - 1)
    n_used = (pad_end[-1:] // MOE_BLOCK).astype(jnp.int32)
    xs = _sc_dispatch(x1p, posk, P)
    ys = _experts(xs, block_e, n_used, layer, w1, b1, w2, b2)
    yg = _sc_gather(ys, posk.reshape(A))
    return _combine(yg, packed, x1, lg, lb)


def _layer(x2d, B, S, p, layer, stacked):
    w = _split_w_in(p['w_in'])
    ha = _lru(x2d, w['lru'], B, S, p['lru_conv_w'], p['lru_conv_b'], p['lru_wa'], p['lru_ba'], p['lru_wx'],
              p['lru_bx'], p['lru_lambda'])
    hb = _mlstm(x2d, w['ml'], B, S, p['ml_i_bias'], p['ml_f_bias'], p['ml_norm_g'])
    hc = _gdn(x2d, w['gd'], B, S, p['gd_conv_w'], p['gd_a_log'], p['gd_dt_bias'], p['gd_norm_g'])
    x1, x1p = _merge(ha, hb, hc, x2d, w['gate'], p['w_br_lru'], p['w_br_ml'], p['w_br_gd'], p['w_out'],
                     p['ln1_g'], p['ln1_b'])
    return _moe(x1, x1p, p['router_w'], p['router_b'], layer, stacked['exp_w1'], stacked['exp_b1'],
                stacked['exp_w2'], stacked['exp_b2'], p['ln2_g'], p['ln2_b'])


def kernel(x, w_in, lru_conv_w, lru_conv_b, lru_wa, lru_ba, lru_wx, lru_bx, lru_lambda, ml_i_bias, ml_f_bias, ml_norm_g, gd_conv_w, gd_a_log, gd_dt_bias, gd_norm_g, w_br_lru, w_br_ml, w_br_gd, w_out, ln1_g, ln1_b, router_w, router_b, exp_w1, exp_b1, exp_w2, exp_b2, ln2_g, ln2_b):
    B, S, D = x.shape
    params = dict(w_in=w_in, lru_conv_w=lru_conv_w, lru_conv_b=lru_conv_b, lru_wa=lru_wa, lru_ba=lru_ba,
                  lru_wx=lru_wx, lru_bx=lru_bx, lru_lambda=lru_lambda, ml_i_bias=ml_i_bias,
                  ml_f_bias=ml_f_bias, ml_norm_g=ml_norm_g, gd_conv_w=gd_conv_w, gd_a_log=gd_a_log,
                  gd_dt_bias=gd_dt_bias, gd_norm_g=gd_norm_g, w_br_lru=w_br_lru, w_br_ml=w_br_ml,
                  w_br_gd=w_br_gd, w_out=w_out, ln1_g=ln1_g, ln1_b=ln1_b, router_w=router_w,
                  router_b=router_b, ln2_g=ln2_g, ln2_b=ln2_b)
    stacked = dict(exp_w1=exp_w1, exp_b1=exp_b1, exp_w2=exp_w2, exp_b2=exp_b2)
    h = x.reshape(B * S, D)
    for l in range(w_in.shape[0]):
        h = _layer(h, B, S, {k: v[l] for k, v in params.items()}, l, stacked)
    return h.reshape(B, S, D)
```

```python
import functools
import math

import jax
import jax.numpy as jnp
from jax import lax
from jax.experimental import pallas as pl
from jax.experimental.pallas import tpu as pltpu
from jax.experimental.pallas import tpu_sc as plsc

f32 = jnp.float32
bf16 = jnp.bfloat16

D_MODEL = 1024
LRU_WIDTH = 1024
LRU_BLOCKS = 8
LRU_C = 8.0
CONV_WIDTH = 4
ML_HEADS = 4
ML_DQK = 64
ML_DV = 128
GD_HEADS = 8
GD_DK = 64
GD_DV = 64
N_EXPERTS = 32
TOP_K = 4
D_FF = 1024
SWIGLU_LIMIT = 7.0
SWIGLU_ALPHA = 1.702
MOE_BLOCK = 512
N_BRANCH = 3
DEPTH = 2
DN_ALPHA = (2.0 * DEPTH) ** 0.25
LN_EPS = 1e-5
RMS_EPS = 1e-6

ML_QK_W = ML_HEADS * ML_DQK
ML_V_W = ML_HEADS * ML_DV
GD_K_W = GD_HEADS * GD_DK
GD_V_W = GD_HEADS * GD_DV
IN_SPLITS = (LRU_WIDTH, LRU_WIDTH, ML_QK_W, ML_QK_W, ML_V_W, ML_HEADS, ML_HEADS, ML_V_W,
             GD_K_W, GD_K_W, GD_V_W, GD_HEADS, GD_HEADS, GD_V_W, N_BRANCH * D_MODEL)

LANES = 128
SUBLANES = 8
NEG_BIG = -1e30

SM_ML_I = 0
SM_ML_F = 4
SM_GD_A = 8
SM_GD_B = 16

ML_CHUNK = 256
GD_CHUNK = 64
GD_ROWS = 256
GD_SOLVE_PASSES = 1
VMEM_LIMIT = 56 * 1024 * 1024
SC_ROWS = 128


def _cparams(*sem):
    return pltpu.CompilerParams(dimension_semantics=sem, vmem_limit_bytes=VMEM_LIMIT)


def _hdot(a, b):
    return jnp.dot(a, b, precision=lax.Precision.HIGHEST, preferred_element_type=f32)


def _bdot(a, b):
    return jnp.dot(a.astype(bf16), b.astype(bf16), preferred_element_type=f32)


def _bdot_nt(a, b):
    return lax.dot_general(a.astype(bf16), b.astype(bf16), (((1,), (1,)), ((), ())),
                           preferred_element_type=f32)


def _bdot_tn(a, b):
    return lax.dot_general(a.astype(bf16), b.astype(bf16), (((0,), (0,)), ((), ())),
                           preferred_element_type=f32)


def _sigmoid(x):
    return 1.0 / (1.0 + jnp.exp(-x))


def _softplus(x):
    return jnp.maximum(x, 0.0) + jnp.log1p(jnp.exp(-jnp.abs(x)))


def _log_sigmoid(x):
    return jnp.minimum(x, 0.0) - jnp.log1p(jnp.exp(-jnp.abs(x)))


def _pack_halves(v):
    n = v.shape[1] // 2
    hi = lax.bitcast_convert_type(v[:, :n].astype(bf16).astype(f32), jnp.uint32)
    lo = lax.bitcast_convert_type(v[:, n:].astype(bf16).astype(f32), jnp.uint32)
    return hi | (lo >> 16)


def _unpack_halves(p):
    hi = lax.bitcast_convert_type(p & jnp.uint32(0xFFFF0000), f32)
    lo = lax.bitcast_convert_type(p << 16, f32)
    return jnp.concatenate([hi, lo], axis=1)


def _layer_norm(z, g, b):
    mu = jnp.mean(z, axis=-1, keepdims=True)
    zc = z - mu
    var = jnp.mean(zc * zc, axis=-1, keepdims=True)
    return zc * lax.rsqrt(var + LN_EPS) * g + b


def _split_w_in(w):
    pts = []
    acc = 0
    for s in IN_SPLITS[:-1]:
        acc += s
        pts.append(acc)
    (lru_x, lru_y, ml_q, ml_k, ml_v, ml_i, ml_f, ml_o,
     gd_q, gd_k, gd_v, gd_a, gd_b, gd_z, gate) = jnp.split(w, pts, axis=1)
    pad = jnp.zeros((w.shape[0], LANES - 2 * ML_HEADS - 2 * GD_HEADS), w.dtype)
    small = jnp.concatenate([ml_i, ml_f, gd_a, gd_b, pad], axis=1)
    cat = lambda *cols: jnp.concatenate(cols, axis=1).astype(bf16)
    return dict(lru=cat(lru_x, lru_y), ml=cat(ml_q, ml_k, ml_v, ml_o, small),
                gd=cat(gd_q, gd_k, gd_v, gd_z, small), gate=gate.astype(bf16))


def _project(x_ref, w_ref):
    return jnp.dot(x_ref[...].astype(bf16), w_ref[...], preferred_element_type=f32)


def _lru_kernel(xin_ref, xnext_ref, w_ref, cw_ref, cb_ref, wcat_ref, ba_ref, bx_ref, lam_ref, o_ref,
                xext, pr_even, pr_odd, a_scr, u_scr, carry):
    ts = xin_ref.shape[0]
    W = a_scr.shape[1]
    bw = W // LRU_BLOCKS
    step = pl.program_id(0) * pl.num_programs(1) + pl.program_id(1)

    @pl.when(step == 0)
    def _():
        pr_even[...] = _project(xin_ref, w_ref)

    @pl.when(pl.program_id(1) == 0)
    def _():
        xext[0:SUBLANES, :] = jnp.zeros((SUBLANES, W), f32)
        carry[...] = jnp.zeros((1, W), f32)

    def tile(cur, nxt):
        x = cur[:, 0:W]
        xext[SUBLANES:SUBLANES + ts, :] = x
        xa = (cw_ref[3:4, :] * x
              + cw_ref[2:3, :] * xext[SUBLANES - 1:SUBLANES - 1 + ts, :]
              + cw_ref[1:2, :] * xext[SUBLANES - 2:SUBLANES - 2 + ts, :]
              + cw_ref[0:1, :] * xext[SUBLANES - 3:SUBLANES - 3 + ts, :]) + cb_ref[...]
        xext[0:SUBLANES, :] = xext[ts:ts + SUBLANES, :]

        xnb = xnext_ref[...].astype(bf16)
        pw = 2 * W // LRU_BLOCKS
        cdec = -LRU_C * _softplus(-lam_ref[...])
        for h in range(LRU_BLOCKS):
            sl = slice(bw * h, bw * (h + 1))
            psl = slice(pw * h, pw * (h + 1))
            nxt[:, psl] = jnp.dot(xnb, w_ref[:, psl], preferred_element_type=f32)
            xh = xa[:, sl]
            g = jnp.dot(xh.astype(bf16), wcat_ref[h], preferred_element_type=f32)
            r = _sigmoid(g[:, :bw] + ba_ref[:, sl])
            ig = _sigmoid(g[:, bw:] + bx_ref[:, sl])
            log_a = r * cdec[:, sl]
            a = jnp.exp(log_a)
            a_scr[:, sl] = a
            u_scr[:, sl] = jnp.sqrt(jnp.tanh(-log_a) * (1.0 + a * a)) * (ig * xh)

        row = lax.broadcasted_iota(jnp.int32, (SUBLANES, W), 0)

        def body(g, cr):
            off = pl.multiple_of(g * SUBLANES, SUBLANES)
            A = a_scr[pl.ds(off, SUBLANES), :]
            U = u_scr[pl.ds(off, SUBLANES), :]
            for s in (1, 2, 4):
                a_sh = pltpu.roll(A, s, 0)
                u_sh = pltpu.roll(U, s, 0)
                m = row >= s
                U = jnp.where(m, A * u_sh + U, U)
                A = jnp.where(m, A * a_sh, A)
            H = A * cr + U
            u_scr[pl.ds(off, SUBLANES), :] = H
            return H[SUBLANES - 1:SUBLANES, :]

        carry[...] = lax.fori_loop(0, ts // SUBLANES, body, carry[...], unroll=2)
        o_ref[...] = (u_scr[...] * jax.nn.gelu(cur[:, W:2 * W])).astype(o_ref.dtype)

    @pl.when(lax.rem(step, 2) == 0)
    def _():
        tile(pr_even, pr_odd)

    @pl.when(lax.rem(step, 2) == 1)
    def _():
        tile(pr_odd, pr_even)


def _lru(x2d, w_lru, B, S, cw, cb, wa, ba, wx, bx, lam):
    T = B * S
    ts = min(512, S)
    nt = S // ts
    W = LRU_WIDTH
    wcat = jnp.concatenate([wa, wx], axis=-1).astype(bf16)
    row = lambda v: v.reshape(1, W)
    full = lambda shp: pl.BlockSpec(shp, lambda b, c: (0,) * len(shp))
    last = B * nt - 1
    return pl.pallas_call(
        _lru_kernel,
        out_shape=jax.ShapeDtypeStruct((T, W), bf16),
        grid=(B, nt),
        in_specs=[pl.BlockSpec((ts, D_MODEL), lambda b, c: (b * nt + c, 0)),
                  pl.BlockSpec((ts, D_MODEL), lambda b, c: (jnp.minimum(b * nt + c + 1, last), 0)),
                  full(w_lru.shape), full((CONV_WIDTH, W)), full((1, W)), full(wcat.shape),
                  full((1, W)), full((1, W)), full((1, W))],
        out_specs=pl.BlockSpec((ts, W), lambda b, c: (b * nt + c, 0)),
        scratch_shapes=[pltpu.VMEM((ts + SUBLANES, W), f32), pltpu.VMEM((ts, 2 * W), f32),
                        pltpu.VMEM((ts, 2 * W), f32), pltpu.VMEM((ts, W), f32), pltpu.VMEM((ts, W), f32),
                        pltpu.VMEM((1, W), f32)],
        compiler_params=_cparams("arbitrary", "arbitrary"),
        name="rg_lru",
    )(x2d, x2d, w_lru, cw, row(cb), wcat, row(ba), row(bx), row(lam))


def _ml_kernel(xin_ref, w_ref, bias_ref, g_ref, o_ref, c_scr, n_scr, m_scr):
    L = xin_ref.shape[0]

    @pl.when(pl.program_id(1) == 0)
    def _():
        c_scr[...] = jnp.zeros(c_scr.shape, f32)
        n_scr[...] = jnp.zeros(n_scr.shape, f32)
        m_scr[...] = jnp.zeros(m_scr.shape, f32)

    pr = _project(xin_ref, w_ref)
    q = pr[:, 0:ML_QK_W]
    k = pr[:, ML_QK_W:2 * ML_QK_W]
    v = pr[:, 2 * ML_QK_W:2 * ML_QK_W + ML_V_W]
    og = pr[:, 2 * ML_QK_W + ML_V_W:2 * ML_QK_W + 2 * ML_V_W]
    sm = pr[:, 2 * ML_QK_W + 2 * ML_V_W:] + bias_ref[...]
    logf = _log_sigmoid(sm)
    ri = lax.broadcasted_iota(jnp.int32, (L, L), 0)
    ci = lax.broadcasted_iota(jnp.int32, (L, L), 1)
    tri = ri >= ci
    b_all = _hdot(tri.astype(f32), logf)
    sm_t = sm.T
    b_t = b_all.T
    kb = k.astype(bf16)
    lane_head = lax.broadcasted_iota(jnp.int32, q.shape, 1) // ML_DQK
    scale = ML_DQK ** -0.5
    for h in range(ML_HEADS):
        sl = slice(ML_DV * h, ML_DV * (h + 1))
        ig_c = sm[:, SM_ML_I + h:SM_ML_I + h + 1]
        b_c = b_all[:, SM_ML_F + h:SM_ML_F + h + 1]
        ig_r = sm_t[SM_ML_I + h:SM_ML_I + h + 1, :]
        b_r = b_t[SM_ML_F + h:SM_ML_F + h + 1, :]
        b_last = b_c[L - 1:L, :]
        m_prev = m_scr[h:h + 1, 0:1]
        ge_r = b_last - b_r + ig_r
        ge_c = b_last - b_c + ig_c
        m_new = jnp.maximum(b_last + m_prev, jnp.max(ge_r, axis=1, keepdims=True))
        decay = jnp.exp(b_last + m_prev - m_new)
        w_c = jnp.exp(ge_c - m_new)
        dmat = jnp.where(tri, b_c - b_r + ig_r, NEG_BIG)
        inter = b_c + m_prev
        m_t = jnp.maximum(inter, jnp.max(dmat, axis=1, keepdims=True))
        dw = jnp.exp(dmat - m_t)
        iw = jnp.exp(inter - m_t)
        qm = jnp.where(lane_head == h, q, 0.0) * scale
        qmb = qm.astype(bf16)
        s = _bdot_nt(qmb, kb) * dw
        v_h = v[:, sl]
        n_row = n_scr[h:h + 1, :]
        num = _bdot(s, v_h) + iw * _bdot(qmb, c_scr[:, sl])
        den = jnp.sum(s, axis=1, keepdims=True) + iw * jnp.sum(qm * n_row, axis=1, keepdims=True)
        hh = num / jnp.maximum(jnp.abs(den), jnp.exp(-m_t))
        c_scr[:, sl] = decay * c_scr[:, sl] + _bdot_tn(kb, w_c * v_h)
        n_scr[h:h + 1, :] = decay * n_row + jnp.sum(w_c * k, axis=0, keepdims=True)
        m_scr[h:h + 1, :] = jnp.broadcast_to(m_new, (1, LANES))
        ms = jnp.mean(hh * hh, axis=1, keepdims=True)
        hn = hh * lax.rsqrt(ms + RMS_EPS) * g_ref[:, sl]
        o_ref[:, sl] = (hn * _sigmoid(og[:, sl])).astype(o_ref.dtype)


def _mlstm(x2d, w_ml, B, S, i_bias, f_bias, norm_g):
    T = B * S
    L = min(ML_CHUNK, S)
    nc = S // L
    bias = jnp.zeros((1, LANES), f32)
    bias = bias.at[0, SM_ML_I:SM_ML_I + ML_HEADS].set(i_bias).at[0, SM_ML_F:SM_ML_F + ML_HEADS].set(f_bias)
    full = lambda shp: pl.BlockSpec(shp, lambda b, c: (0,) * len(shp))
    return pl.pallas_call(
        _ml_kernel,
        out_shape=jax.ShapeDtypeStruct((T, ML_V_W), bf16),
        grid=(B, nc),
        in_specs=[pl.BlockSpec((L, D_MODEL), lambda b, c: (b * nc + c, 0)), full(w_ml.shape),
                  full((1, LANES)), full((1, ML_V_W))],
        out_specs=pl.BlockSpec((L, ML_V_W), lambda b, c: (b * nc + c, 0)),
        scratch_shapes=[pltpu.VMEM((ML_QK_W, ML_V_W), f32), pltpu.VMEM((SUBLANES, ML_QK_W), f32),
                        pltpu.VMEM((SUBLANES, LANES), f32)],
        compiler_params=_cparams("parallel", "arbitrary"),
        name="mlstm",
    )(x2d, w_ml, bias, norm_g.reshape(1, ML_V_W))


def _split_bf16(x, passes):
    hi = x.astype(bf16)
    if passes == 1:
        return (hi,)
    return (hi, (x - hi.astype(f32)).astype(bf16))


def _mdot(xs, ys):
    out = jnp.dot(xs[0], ys[0], preferred_element_type=f32)
    if len(xs) > 1:
        out = out + jnp.dot(xs[1], ys[0], preferred_element_type=f32)
        out = out + jnp.dot(xs[0], ys[1], preferred_element_type=f32)
    return out


def _gd_kernel(xin_ref, w_ref, cw_ref, alog_ref, dtb_ref, g_ref, o_ref, ext, s_scr):
    R = xin_ref.shape[0]
    KW = GD_K_W
    L = GD_CHUNK
    nch = R // L
    lsh = L.bit_length() - 1
    PW = nch * L
    nsl = KW // LANES
    passes = GD_SOLVE_PASSES

    @pl.when(pl.program_id(1) == 0)
    def _():
        ext[0:SUBLANES, :] = jnp.zeros((SUBLANES, ext.shape[1]), f32)
        s_scr[...] = jnp.zeros(s_scr.shape, f32)

    pr = _project(xin_ref, w_ref)
    z = pr[:, 3 * KW:4 * KW]
    sm = pr[:, 4 * KW:]
    ext[SUBLANES:SUBLANES + R, :] = pr[:, 0:3 * KW]
    xc = (cw_ref[3:4, :] * ext[SUBLANES:SUBLANES + R, :]
          + cw_ref[2:3, :] * ext[SUBLANES - 1:SUBLANES - 1 + R, :]
          + cw_ref[1:2, :] * ext[SUBLANES - 2:SUBLANES - 2 + R, :]
          + cw_ref[0:1, :] * ext[SUBLANES - 3:SUBLANES - 3 + R, :])
    ext[0:SUBLANES, :] = ext[R:R + SUBLANES, :]
    xc = xc * _sigmoid(xc)

    g_t = -jnp.exp(alog_ref[...]) * _softplus(sm + dtb_ref[...])
    beta_t = _sigmoid(sm)
    rr = lax.broadcasted_iota(jnp.int32, (R, R), 0)
    cc = lax.broadcasted_iota(jnp.int32, (R, R), 1)
    cum = jnp.where(((rr >> lsh) == (cc >> lsh)) & (rr >= cc), 1.0, 0.0)
    gc_all = _hdot(cum, g_t)
    gc_t = gc_all.T

    pt = lax.broadcasted_iota(jnp.int32, (L, PW), 0)
    plane = lax.broadcasted_iota(jnp.int32, (L, PW), 1)
    ps = plane & (L - 1)
    pc = plane >> lsh
    tri_p = pt >= ps
    strict_p = pt > ps
    eye_p = jnp.where(pt == ps, 1.0, 0.0)
    blk8_p = (pt >> 3) == (ps >> 3)
    cmask_b = [jnp.where(pc == c, 1.0, 0.0).astype(bf16) for c in range(nch)]

    def pick(parts):
        out = parts[nch - 1]
        for c in range(nch - 2, -1, -1):
            out = jnp.where(pc == c, parts[c], out)
        return out

    def chunks(col):
        return [col[c * L:(c + 1) * L] for c in range(nch)]

    def bd(yb):
        return jnp.concatenate([yb * cmask_b[c] for c in range(nch)], axis=0)

    def pdot(xp, yp):
        return _mdot(_split_bf16(xp, passes), [bd(p) for p in _split_bf16(yp, passes)])

    lane_r = lax.broadcasted_iota(jnp.int32, (R, LANES), 1)
    lo_r = lane_r < GD_DK
    lane_l = lax.broadcasted_iota(jnp.int32, (L, LANES), 1)
    lo_l = lane_l < GD_DK
    lane_row = lax.broadcasted_iota(jnp.int32, (1, LANES), 1)
    r2 = lax.broadcasted_iota(jnp.int32, (LANES, LANES), 0)
    c2 = lax.broadcasted_iota(jnp.int32, (LANES, LANES), 1)
    blockdiag = (r2 < GD_DK) == (c2 < GD_DK)

    def half_sums(y, lo):
        s_lo = jnp.sum(jnp.where(lo, y, 0.0), axis=1, keepdims=True)
        s_hi = jnp.sum(jnp.where(lo, 0.0, y), axis=1, keepdims=True)
        return s_lo, s_hi

    def l2n(y):
        s_lo, s_hi = half_sums(y * y, lo_r)
        return y * jnp.where(lo_r, lax.rsqrt(s_lo + RMS_EPS), lax.rsqrt(s_hi + RMS_EPS))

    qns, kns, vss, a_ps, p_bs, gcols, bcols = [], [], [], [], [], [], []
    for j in range(nsl):
        qn = l2n(xc[:, LANES * j:LANES * (j + 1)]) * (GD_DK ** -0.5)
        kn = l2n(xc[:, KW + LANES * j:KW + LANES * (j + 1)])
        qns.append(qn)
        kns.append(kn)
        vss.append(xc[:, 2 * KW + LANES * j:2 * KW + LANES * (j + 1)])
        lhs = jnp.concatenate([jnp.where(lo_r, kn, 0.0), jnp.where(lo_r, 0.0, kn),
                               jnp.where(lo_r, qn, 0.0), jnp.where(lo_r, 0.0, qn)], axis=0)
        gram = _bdot_nt(lhs, kn)
        for e in range(2):
            h = 2 * j + e
            gcol = gc_all[:, SM_GD_A + h:SM_GD_A + h + 1]
            bcol = beta_t[:, SM_GD_B + h:SM_GD_B + h + 1]
            gc_r = gc_t[SM_GD_A + h:SM_GD_A + h + 1, :]
            gam_p = jnp.exp(jnp.where(tri_p, pick(chunks(gcol)) - gc_r, NEG_BIG))
            kk_p = pick(chunks(gram[e * R:(e + 1) * R]))
            qk_p = pick(chunks(gram[(2 + e) * R:(3 + e) * R]))
            a_ps.append(jnp.where(strict_p, pick(chunks(bcol)) * gam_p * kk_p, 0.0))
            p_bs.append((gam_p * qk_p).astype(bf16))
            gcols.append(gcol)
            bcols.append(bcol)

    ads = [jnp.where(blk8_p, a, 0.0) for a in a_ps]
    a2s = [pdot(ad, ad) for ad in ads]
    a4s = [pdot(a2, a2) for a2 in a2s]
    xs = [pdot(eye_p - ad, eye_p + a2) for ad, a2 in zip(ads, a2s)]
    xs = [pdot(x, eye_p + a4) for x, a4 in zip(xs, a4s)]
    for sh in range(3, lsh):
        msk = ((pt >> (sh + 1)) == (ps >> (sh + 1))) & (((pt >> sh) & 1) == 1) & (((ps >> sh) & 1) == 0)
        ts = [pdot(jnp.where(msk, a, 0.0), x) for a, x in zip(a_ps, xs)]
        xs = [x - pdot(x, t) for x, t in zip(xs, ts)]

    us, ws, qgs, kds, gls = [], [], [], [], []
    for j in range(nsl):
        uw, egs, eds, glh = [], [], [], []
        for e in range(2):
            h = 2 * j + e
            gcol, bcol = gcols[h], bcols[h]
            eg = jnp.exp(gcol)
            rhs = jnp.concatenate([bcol * vss[j], (bcol * eg) * kns[j]], axis=1)
            uw.append(_mdot([bd(p) for p in _split_bf16(xs[h], passes)], _split_bf16(rhs, passes)))
            lasts = [gcol[c * L + L - 1:c * L + L] for c in range(nch)]
            gcl = jnp.concatenate([jnp.broadcast_to(v, (L, 1)) for v in lasts], axis=0)
            egs.append(eg)
            eds.append(jnp.exp(gcl - gcol))
            glh.append([jnp.exp(v) for v in lasts])
        us.append(jnp.where(lo_r, uw[0][:, :LANES], uw[1][:, :LANES]))
        ws.append(jnp.where(lo_r, uw[0][:, LANES:], uw[1][:, LANES:]))
        qgs.append(qns[j] * jnp.where(lo_r, egs[0], egs[1]))
        kds.append(kns[j] * jnp.where(lo_r, eds[0], eds[1]))
        gls.append([jnp.where(lane_row < GD_DK, glh[0][c], glh[1][c]) for c in range(nch)])

    states = [s_scr[j] for j in range(nsl)]
    outs = [[] for _ in range(nsl)]
    for c in range(nch):
        rs = slice(c * L, (c + 1) * L)
        for j in range(nsl):
            s_prev = states[j]
            sb = s_prev.astype(bf16)
            delta = us[j][rs] - _bdot(ws[j][rs], sb)
            db = delta.astype(bf16)
            pieces = []
            if c > 0:
                pieces.append(jnp.zeros((c * L, LANES), bf16))
            pieces.append(db)
            if c < nch - 1:
                pieces.append(jnp.zeros(((nch - 1 - c) * L, LANES), bf16))
            dpad = jnp.concatenate(pieces, axis=0) if len(pieces) > 1 else db
            intra = jnp.where(lo_l, jnp.dot(p_bs[2 * j], dpad, preferred_element_type=f32),
                              jnp.dot(p_bs[2 * j + 1], dpad, preferred_element_type=f32))
            outs[j].append(_bdot(qgs[j][rs], sb) + intra)
            upd = _bdot_tn(kds[j][rs], db)
            states[j] = gls[j][c] * s_prev + jnp.where(blockdiag, upd, 0.0)

    for j in range(nsl):
        sl = slice(LANES * j, LANES * (j + 1))
        s_scr[j] = states[j]
        o = jnp.concatenate(outs[j], axis=0) if nch > 1 else outs[j][0]
        m_lo, m_hi = half_sums(o * o, lo_r)
        inv = jnp.where(lo_r, lax.rsqrt(m_lo * (1.0 / GD_DV) + RMS_EPS), lax.rsqrt(m_hi * (1.0 / GD_DV) + RMS_EPS))
        zs = z[:, sl]
        o_ref[:, sl] = (o * inv * g_ref[:, sl] * (zs * _sigmoid(zs))).astype(o_ref.dtype)


def _gdn(x2d, w_gd, B, S, conv_w, a_log, dt_bias, norm_g):
    T = B * S
    L = min(GD_ROWS, S)
    nc = S // L
    KW = GD_K_W
    alog = jnp.zeros((1, LANES), f32).at[0, SM_GD_A:SM_GD_A + GD_HEADS].set(a_log)
    dtb = jnp.zeros((1, LANES), f32).at[0, SM_GD_A:SM_GD_A + GD_HEADS].set(dt_bias)
    g_row = jnp.tile(norm_g, GD_HEADS).reshape(1, GD_V_W)
    full = lambda shp: pl.BlockSpec(shp, lambda b, c: (0,) * len(shp))
    return pl.pallas_call(
        _gd_kernel,
        out_shape=jax.ShapeDtypeStruct((T, GD_V_W), bf16),
        grid=(B, nc),
        in_specs=[pl.BlockSpec((L, D_MODEL), lambda b, c: (b * nc + c, 0)), full(w_gd.shape),
                  full((CONV_WIDTH, 3 * KW)), full((1, LANES)), full((1, LANES)), full((1, GD_V_W))],
        out_specs=pl.BlockSpec((L, GD_V_W), lambda b, c: (b * nc + c, 0)),
        scratch_shapes=[pltpu.VMEM((L + SUBLANES, 3 * KW), f32),
                        pltpu.VMEM((KW // LANES, LANES, LANES), f32)],
        compiler_params=_cparams("parallel", "arbitrary"),
        name="gated_deltanet",
    )(x2d, w_gd, conv_w, alog, dtb, g_row)


def _merge_kernel(ha_ref, hb_ref, hc_ref, x_ref, wg_ref, wa_ref, wb_ref, wc_ref, wo_ref, lg_ref, lb_ref,
                  rw_ref, rb_ref, x1_ref, x1p_ref, pk_ref, cnt_ref, prev, carry):
    D = x_ref.shape[1]
    i = pl.program_id(0)

    @pl.when(i == 0)
    def _():
        prev[...] = jnp.zeros(prev.shape, f32)
        carry[...] = jnp.zeros(carry.shape, f32)

    route = _route_stages(prev[...], jnp.where(i > 0, 1.0, 0.0), rw_ref, rb_ref, pk_ref, cnt_ref, carry)
    xb = x_ref[...].astype(bf16)

    next(route)
    g0 = jnp.dot(xb, wg_ref[:, 0:D], preferred_element_type=f32)
    next(route)
    g1 = jnp.dot(xb, wg_ref[:, D:2 * D], preferred_element_type=f32)
    next(route)
    g2 = jnp.dot(xb, wg_ref[:, 2 * D:3 * D], preferred_element_type=f32)
    next(route)
    bra = jnp.dot(ha_ref[...], wa_ref[...], preferred_element_type=f32)
    next(route)
    brb = jnp.dot(hb_ref[...], wb_ref[...], preferred_element_type=f32)
    brc = jnp.dot(hc_ref[...], wc_ref[...], preferred_element_type=f32)
    next(route)
    merged = _sigmoid(g0) * bra + _sigmoid(g1) * brb + _sigmoid(g2) * brc
    y = jnp.dot(merged.astype(bf16), wo_ref[...], preferred_element_type=f32)
    for _ in route:
        pass
    out = _layer_norm(DN_ALPHA * x_ref[...] + y, lg_ref[...], lb_ref[...])
    x1_ref[...] = out
    x1p_ref[...] = _pack_halves(out)
    prev[...] = out


def _merge(ha, hb, hc, x2d, w_gate, wa, wb, wc, wo, lg, lb, router_w, router_b):
    T = x2d.shape[0]
    D = D_MODEL
    tm = min(256, T)
    nt = T // tm
    rw, rb = _router_params(router_w, router_b)
    full = lambda shp: pl.BlockSpec(shp, lambda i: (0,) * len(shp))
    rows = lambda w: pl.BlockSpec((tm, w), lambda i: (jnp.minimum(i, nt - 1), 0))
    routed = pl.BlockSpec((tm, LANES), lambda i: (jnp.maximum(i - 1, 0), 0))
    return pl.pallas_call(
        _merge_kernel,
        out_shape=(jax.ShapeDtypeStruct((T, D), f32), jax.ShapeDtypeStruct((T, D // 2), jnp.uint32),
                   jax.ShapeDtypeStruct((T, LANES), f32), jax.ShapeDtypeStruct((1, LANES), f32)),
        grid=(nt + 1,),
        in_specs=[rows(LRU_WIDTH), rows(ML_V_W), rows(GD_V_W), rows(D), full(w_gate.shape),
                  full((LRU_WIDTH, D)), full((ML_V_W, D)), full((GD_V_W, D)), full((D, D)),
                  full((1, D)), full((1, D)), full((D, LANES)), full((1, LANES))],
        out_specs=(rows(D), rows(D // 2), routed, full((1, LANES))),
        scratch_shapes=[pltpu.VMEM((tm, D), f32), pltpu.VMEM((1, LANES), f32)],
        compiler_params=_cparams("arbitrary"),
        name="merge_outproj_ln_route",
    )(ha, hb, hc, x2d, w_gate, wa.astype(bf16), wb.astype(bf16), wc.astype(bf16),
      wo.astype(bf16), lg.reshape(1, D), lb.reshape(1, D), rw, rb)


PK_IDX = 0
PK_GATE = 4
PK_RANK = 8


def _route_stages(x, weight, rw_ref, rb_ref, pk_ref, cnt_ref, carry):
    tr = x.shape[0]
    logits = _mdot(_split_bf16(x, 3), _split_bf16(rw_ref[...], 3)) + rb_ref[...]
    lane = lax.broadcasted_iota(jnp.int32, (tr, LANES), 1).astype(f32)
    vals = logits
    idxs, tops = [], []
    yield
    for _ in range(TOP_K):
        m = jnp.max(vals, axis=1, keepdims=True)
        idx = jnp.min(jnp.where(vals == m, lane, float(LANES)), axis=1, keepdims=True)
        idxs.append(idx)
        tops.append(m)
        vals = jnp.where(lane == idx, NEG_BIG * 2.0, vals)
        yield
    es = [jnp.exp(t - tops[0]) for t in tops]
    tot = es[0] + es[1] + es[2] + es[3]
    onehots = [lane == idx for idx in idxs]
    sel = jnp.zeros((tr, LANES), f32)
    for oh in onehots:
        sel = sel + oh.astype(f32)
    ri = lax.broadcasted_iota(jnp.int32, (tr, tr), 0)
    ci = lax.broadcasted_iota(jnp.int32, (tr, tr), 1)
    before = jnp.dot((ri > ci).astype(bf16), sel.astype(bf16), preferred_element_type=f32) + carry[...]
    carry[...] = carry[...] + weight * jnp.sum(sel, axis=0, keepdims=True)
    cnt_ref[...] = carry[...]
    yield
    packed = jnp.zeros((tr, LANES), f32)
    for kk in range(TOP_K):
        rank = jnp.sum(jnp.where(onehots[kk], before, 0.0), axis=1, keepdims=True)
        packed = jnp.where(lane == float(PK_IDX + kk), idxs[kk], packed)
        packed = jnp.where(lane == float(PK_GATE + kk), es[kk] / tot, packed)
        packed = jnp.where(lane == float(PK_RANK + kk), rank, packed)
    pk_ref[...] = packed


def _router_params(router_w, router_b):
    rw = jnp.zeros((D_MODEL, LANES), f32).at[:, :N_EXPERTS].set(router_w)
    rb = jnp.full((1, LANES), NEG_BIG, f32).at[0, :N_EXPERTS].set(router_b)
    return rw, rb


def _expert_kernel(be_ref, nu_ref, x_ref, w1_ref, b1_ref, w2_ref, b2_ref, o_ref, w1b, w2b):
    i = pl.program_id(0)
    used = i < nu_ref[0]
    new_expert = jnp.logical_or(i == 0, be_ref[i] != be_ref[jnp.maximum(i - 1, 0)])

    @pl.when(jnp.logical_and(used, new_expert))
    def _():
        w1b[...] = w1_ref[0, 0].astype(bf16)
        w2b[...] = w2_ref[0, 0].astype(bf16)

    @pl.when(used)
    def _():
        x = _unpack_halves(x_ref[...]).astype(bf16)
        hdn = jnp.dot(x, w1b[...], preferred_element_type=f32) + b1_ref[0, 0]
        glu = jnp.minimum(hdn[:, :D_FF], SWIGLU_LIMIT)
        lin = jnp.clip(hdn[:, D_FF:], -SWIGLU_LIMIT, SWIGLU_LIMIT)
        act = glu * _sigmoid(SWIGLU_ALPHA * glu) * (lin + 1.0)
        y = jnp.dot(act.astype(bf16), w2b[...], preferred_element_type=f32) + b2_ref[0, 0]
        o_ref[...] = _pack_halves(y)

    @pl.when(i >= nu_ref[0])
    def _():
        o_ref[...] = jnp.zeros(o_ref.shape, o_ref.dtype)


def _experts(xs, block_e, n_used, layer, w1, b1, w2, b2):
    P = xs.shape[0]
    nb = P // MOE_BLOCK
    D = D_MODEL
    nl = w1.shape[0]
    grid_spec = pltpu.PrefetchScalarGridSpec(
        num_scalar_prefetch=2,
        grid=(nb,),
        in_specs=[pl.BlockSpec((MOE_BLOCK, D // 2), lambda i, be, nu: (i, 0)),
                  pl.BlockSpec((1, 1, D, 2 * D_FF), lambda i, be, nu: (layer, be[i], 0, 0)),
                  pl.BlockSpec((1, 1, 1, 2 * D_FF), lambda i, be, nu: (layer, be[i], 0, 0)),
                  pl.BlockSpec((1, 1, D_FF, D), lambda i, be, nu: (layer, be[i], 0, 0)),
                  pl.BlockSpec((1, 1, 1, D), lambda i, be, nu: (layer, be[i], 0, 0))],
        out_specs=pl.BlockSpec((MOE_BLOCK, D // 2), lambda i, be, nu: (i, 0)),
        scratch_shapes=[pltpu.VMEM((D, 2 * D_FF), bf16), pltpu.VMEM((D_FF, D), bf16)],
    )
    return pl.pallas_call(
        _expert_kernel,
        out_shape=jax.ShapeDtypeStruct((P, D // 2), jnp.uint32),
        grid_spec=grid_spec,
        compiler_params=_cparams("arbitrary"),
        name="experts",
    )(block_e, n_used, xs, w1, b1.reshape(nl, N_EXPERTS, 1, 2 * D_FF), w2, b2.reshape(nl, N_EXPERTS, 1, D))


def _combine_kernel(y0_ref, y1_ref, y2_ref, y3_ref, pk_ref, x_ref, lg_ref, lb_ref, o_ref):
    pk = pk_ref[...]
    y = pk[:, PK_GATE:PK_GATE + 1] * _unpack_halves(y0_ref[...])
    for kk, y_ref in ((1, y1_ref), (2, y2_ref), (3, y3_ref)):
        y = y + pk[:, PK_GATE + kk:PK_GATE + kk + 1] * _unpack_halves(y_ref[...])
    o_ref[...] = _layer_norm(DN_ALPHA * x_ref[...] + y, lg_ref[...], lb_ref[...])


def _combine(yg, packed, x1, lg, lb):
    T, D = x1.shape
    tm = min(256, T)
    nt = T // tm
    full = lambda shp: pl.BlockSpec(shp, lambda i: (0,) * len(shp))
    choice = lambda kk: pl.BlockSpec((tm, D // 2), lambda i: (kk * nt + i, 0))
    return pl.pallas_call(
        _combine_kernel,
        out_shape=jax.ShapeDtypeStruct((T, D), f32),
        grid=(nt,),
        in_specs=[choice(0), choice(1), choice(2), choice(3), pl.BlockSpec((tm, LANES), lambda i: (i, 0)),
                  pl.BlockSpec((tm, D), lambda i: (i, 0)), full((1, D)), full((1, D))],
        out_specs=pl.BlockSpec((tm, D), lambda i: (i, 0)),
        compiler_params=_cparams("parallel"),
        name="combine_ln",
    )(yg, yg, yg, yg, packed, x1, lg.reshape(1, D), lb.reshape(1, D))


def _sc_mesh():
    return plsc.VectorSubcoreMesh(core_axis_name="c", subcore_axis_name="s")


def _sc_dispatch(x, posk, n_slots):
    t_tokens, w = x.shape

    @functools.partial(pl.kernel, out_type=jax.ShapeDtypeStruct((n_slots, w), x.dtype), mesh=_sc_mesh(),
                       name="sc_dispatch")
    def k(x_hbm, p_hbm, o_hbm):
        def body(x_vmem, p_vmem):
            for kk in range(TOP_K):
                pltpu.sync_copy(x_vmem, o_hbm.at[p_vmem.at[kk]])

        pltpu.emit_pipeline(
            body,
            grid=(t_tokens // SC_ROWS,),
            in_specs=[pl.BlockSpec((SC_ROWS, w), lambda i: (i, 0), pipeline_mode=pl.Buffered(1)),
                      pl.BlockSpec((TOP_K, SC_ROWS), lambda i: (0, i))],
            out_specs=[],
            core_axis_name=("c", "s"),
            dimension_semantics=(pltpu.PARALLEL,),
        )(x_hbm, p_hbm)

    return k(x, posk)


def _sc_gather(table, idx):
    n = idx.shape[0]
    w = table.shape[1]

    @functools.partial(pl.kernel, out_type=jax.ShapeDtypeStruct((n, w), table.dtype), mesh=_sc_mesh(),
                       name="sc_gather")
    def k(t_hbm, i_hbm, o_hbm):
        def body(i_vmem, o_vmem):
            pltpu.sync_copy(t_hbm.at[i_vmem.at[0]], o_vmem)

        pltpu.emit_pipeline(
            body,
            grid=(n // SC_ROWS,),
            in_specs=[pl.BlockSpec((1, SC_ROWS), lambda i: (0, i))],
            out_specs=[pl.BlockSpec((SC_ROWS, w), lambda i: (i, 0), pipeline_mode=pl.Buffered(1))],
            core_axis_name=("c", "s"),
            dimension_semantics=(pltpu.PARALLEL,),
        )(i_hbm, o_hbm)

    return k(table, idx.reshape(1, n))


def _moe(x1, x1p, packed, cnt, layer, w1, b1, w2, b2, lg, lb):
    T, D = x1.shape
    A = T * TOP_K
    idx = packed[:, PK_IDX:PK_IDX + TOP_K].astype(jnp.int32)
    rank = packed[:, PK_RANK:PK_RANK + TOP_K].astype(jnp.int32)
    counts = cnt[0, :N_EXPERTS].astype(jnp.int32)
    padded = ((counts + MOE_BLOCK - 1) // MOE_BLOCK) * MOE_BLOCK
    pad_end = jnp.cumsum(padded)
    pad_start = pad_end - padded
    posk = (pad_start[idx] + rank).T
    n_blocks = -(-A // MOE_BLOCK) + N_EXPERTS
    P = n_blocks * MOE_BLOCK
    starts = jnp.arange(n_blocks, dtype=jnp.int32) * MOE_BLOCK
    block_e = jnp.minimum(jnp.sum((pad_end[None, :] <= starts[:, None]).astype(jnp.int32), axis=1), N_EXPERTS - 1)
    n_used = (pad_end[-1:] // MOE_BLOCK).astype(jnp.int32)
    xs = _sc_dispatch(x1p, posk, P)
    ys = _experts(xs, block_e, n_used, layer, w1, b1, w2, b2)
    yg = _sc_gather(ys, posk.reshape(A))
    return _combine(yg, packed, x1, lg, lb)


def _layer(x2d, B, S, p, layer, stacked):
    w = _split_w_in(p['w_in'])
    ha = _lru(x2d, w['lru'], B, S, p['lru_conv_w'], p['lru_conv_b'], p['lru_wa'], p['lru_ba'], p['lru_wx'],
              p['lru_bx'], p['lru_lambda'])
    hb = _mlstm(x2d, w['ml'], B, S, p['ml_i_bias'], p['ml_f_bias'], p['ml_norm_g'])
    hc = _gdn(x2d, w['gd'], B, S, p['gd_conv_w'], p['gd_a_log'], p['gd_dt_bias'], p['gd_norm_g'])
    x1, x1p, packed, cnt = _merge(ha, hb, hc, x2d, w['gate'], p['w_br_lru'], p['w_br_ml'], p['w_br_gd'],
                                  p['w_out'], p['ln1_g'], p['ln1_b'], p['router_w'], p['router_b'])
    return _moe(x1, x1p, packed, cnt, layer, stacked['exp_w1'], stacked['exp_b1'],
                stacked['exp_w2'], stacked['exp_b2'], p['ln2_g'], p['ln2_b'])


def kernel(x, w_in, lru_conv_w, lru_conv_b, lru_wa, lru_ba, lru_wx, lru_bx, lru_lambda, ml_i_bias, ml_f_bias, ml_norm_g, gd_conv_w, gd_a_log, gd_dt_bias, gd_norm_g, w_br_lru, w_br_ml, w_br_gd, w_out, ln1_g, ln1_b, router_w, router_b, exp_w1, exp_b1, exp_w2, exp_b2, ln2_g, ln2_b):
    B, S, D = x.shape
    params = dict(w_in=w_in, lru_conv_w=lru_conv_w, lru_conv_b=lru_conv_b, lru_wa=lru_wa, lru_ba=lru_ba,
                  lru_wx=lru_wx, lru_bx=lru_bx, lru_lambda=lru_lambda, ml_i_bias=ml_i_bias,
                  ml_f_bias=ml_f_bias, ml_norm_g=ml_norm_g, gd_conv_w=gd_conv_w, gd_a_log=gd_a_log,
                  gd_dt_bias=gd_dt_bias, gd_norm_g=gd_norm_g, w_br_lru=w_br_lru, w_br_ml=w_br_ml,
                  w_br_gd=w_br_gd, w_out=w_out, ln1_g=ln1_g, ln1_b=ln1_b, router_w=router_w,
                  router_b=router_b, ln2_g=ln2_g, ln2_b=ln2_b)
    stacked = dict(exp_w1=exp_w1, exp_b1=exp_b1, exp_w2=exp_w2, exp_b2=exp_b2)
    h = x.reshape(B * S, D)
    for l in range(w_in.shape[0]):
        h = _layer(h, B, S, {k: v[l] for k, v in params.items()}, l, stacked)
    return h.reshape(B, S, D)
```

```python
import functools
import math

import jax
import jax.numpy as jnp
from jax import lax
from jax.experimental import pallas as pl
from jax.experimental.pallas import tpu as pltpu
from jax.experimental.pallas import tpu_sc as plsc

f32 = jnp.float32
bf16 = jnp.bfloat16

D_MODEL = 1024
LRU_WIDTH = 1024
LRU_BLOCKS = 8
LRU_C = 8.0
CONV_WIDTH = 4
ML_HEADS = 4
ML_DQK = 64
ML_DV = 128
GD_HEADS = 8
GD_DK = 64
GD_DV = 64
N_EXPERTS = 32
TOP_K = 4
D_FF = 1024
SWIGLU_LIMIT = 7.0
SWIGLU_ALPHA = 1.702
MOE_BLOCK = 512
N_BRANCH = 3
DEPTH = 2
DN_ALPHA = (2.0 * DEPTH) ** 0.25
LN_EPS = 1e-5
RMS_EPS = 1e-6

ML_QK_W = ML_HEADS * ML_DQK
ML_V_W = ML_HEADS * ML_DV
GD_K_W = GD_HEADS * GD_DK
GD_V_W = GD_HEADS * GD_DV
IN_SPLITS = (LRU_WIDTH, LRU_WIDTH, ML_QK_W, ML_QK_W, ML_V_W, ML_HEADS, ML_HEADS, ML_V_W,
             GD_K_W, GD_K_W, GD_V_W, GD_HEADS, GD_HEADS, GD_V_W, N_BRANCH * D_MODEL)

LANES = 128
SUBLANES = 8
NEG_BIG = -1e30

SM_ML_I = 0
SM_ML_F = 4
SM_GD_A = 8
SM_GD_B = 16

ML_CHUNK = 256
GD_CHUNK = 64
GD_ROWS = 256
GD_SOLVE_PASSES = 1
VMEM_LIMIT = 56 * 1024 * 1024
SC_ROWS = 128


def _cparams(*sem):
    return pltpu.CompilerParams(dimension_semantics=sem, vmem_limit_bytes=VMEM_LIMIT)


def _hdot(a, b):
    return jnp.dot(a, b, precision=lax.Precision.HIGHEST, preferred_element_type=f32)


def _bdot(a, b):
    return jnp.dot(a.astype(bf16), b.astype(bf16), preferred_element_type=f32)


def _bdot_nt(a, b):
    return lax.dot_general(a.astype(bf16), b.astype(bf16), (((1,), (1,)), ((), ())),
                           preferred_element_type=f32)


def _bdot_tn(a, b):
    return lax.dot_general(a.astype(bf16), b.astype(bf16), (((0,), (0,)), ((), ())),
                           preferred_element_type=f32)


def _sigmoid(x):
    return 1.0 / (1.0 + jnp.exp(-x))


def _softplus(x):
    return jnp.maximum(x, 0.0) + jnp.log1p(jnp.exp(-jnp.abs(x)))


def _log_sigmoid(x):
    return jnp.minimum(x, 0.0) - jnp.log1p(jnp.exp(-jnp.abs(x)))


def _pack_halves(v):
    n = v.shape[1] // 2
    hi = lax.bitcast_convert_type(v[:, :n].astype(bf16).astype(f32), jnp.uint32)
    lo = lax.bitcast_convert_type(v[:, n:].astype(bf16).astype(f32), jnp.uint32)
    return hi | (lo >> 16)


def _unpack_halves(p):
    hi = lax.bitcast_convert_type(p & jnp.uint32(0xFFFF0000), f32)
    lo = lax.bitcast_convert_type(p << 16, f32)
    return jnp.concatenate([hi, lo], axis=1)


def _layer_norm(z, g, b):
    mu = jnp.mean(z, axis=-1, keepdims=True)
    zc = z - mu
    var = jnp.mean(zc * zc, axis=-1, keepdims=True)
    return zc * lax.rsqrt(var + LN_EPS) * g + b


def _split_w_in(w):
    pts = []
    acc = 0
    for s in IN_SPLITS[:-1]:
        acc += s
        pts.append(acc)
    (lru_x, lru_y, ml_q, ml_k, ml_v, ml_i, ml_f, ml_o,
     gd_q, gd_k, gd_v, gd_a, gd_b, gd_z, gate) = jnp.split(w, pts, axis=1)
    pad = jnp.zeros((w.shape[0], LANES - 2 * ML_HEADS - 2 * GD_HEADS), w.dtype)
    small = jnp.concatenate([ml_i, ml_f, gd_a, gd_b, pad], axis=1)
    cat = lambda *cols: jnp.concatenate(cols, axis=1).astype(bf16)
    return dict(lru=cat(lru_x, lru_y), ml=cat(ml_q, ml_k, ml_v, ml_o, small),
                gd=cat(gd_q, gd_k, gd_v, gd_z, small), gate=gate.astype(bf16))


def _project(x_ref, w_ref):
    return jnp.dot(x_ref[...].astype(bf16), w_ref[...], preferred_element_type=f32)


def _lru_kernel(xin_ref, xnext_ref, w_ref, cw_ref, cb_ref, wcat_ref, ba_ref, bx_ref, lam_ref, o_ref,
                xext, pr_even, pr_odd, a_scr, u_scr, carry):
    ts = xin_ref.shape[0]
    W = a_scr.shape[1]
    bw = W // LRU_BLOCKS
    step = pl.program_id(0) * pl.num_programs(1) + pl.program_id(1)

    @pl.when(step == 0)
    def _():
        pr_even[...] = _project(xin_ref, w_ref)

    @pl.when(pl.program_id(1) == 0)
    def _():
        xext[0:SUBLANES, :] = jnp.zeros((SUBLANES, W), f32)
        carry[...] = jnp.zeros((1, W), f32)

    def tile(cur, nxt):
        x = cur[:, 0:W]
        xext[SUBLANES:SUBLANES + ts, :] = x
        xa = (cw_ref[3:4, :] * x
              + cw_ref[2:3, :] * xext[SUBLANES - 1:SUBLANES - 1 + ts, :]
              + cw_ref[1:2, :] * xext[SUBLANES - 2:SUBLANES - 2 + ts, :]
              + cw_ref[0:1, :] * xext[SUBLANES - 3:SUBLANES - 3 + ts, :]) + cb_ref[...]
        xext[0:SUBLANES, :] = xext[ts:ts + SUBLANES, :]

        xnb = xnext_ref[...].astype(bf16)
        pw = 2 * W // LRU_BLOCKS
        cdec = -LRU_C * _softplus(-lam_ref[...])
        for h in range(LRU_BLOCKS):
            sl = slice(bw * h, bw * (h + 1))
            psl = slice(pw * h, pw * (h + 1))
            nxt[:, psl] = jnp.dot(xnb, w_ref[:, psl], preferred_element_type=f32)
            xh = xa[:, sl]
            g = jnp.dot(xh.astype(bf16), wcat_ref[h], preferred_element_type=f32)
            r = _sigmoid(g[:, :bw] + ba_ref[:, sl])
            ig = _sigmoid(g[:, bw:] + bx_ref[:, sl])
            log_a = r * cdec[:, sl]
            a = jnp.exp(log_a)
            a_scr[:, sl] = a
            u_scr[:, sl] = jnp.sqrt(jnp.tanh(-log_a) * (1.0 + a * a)) * (ig * xh)

        row = lax.broadcasted_iota(jnp.int32, (SUBLANES, W), 0)

        def body(g, cr):
            off = pl.multiple_of(g * SUBLANES, SUBLANES)
            A = a_scr[pl.ds(off, SUBLANES), :]
            U = u_scr[pl.ds(off, SUBLANES), :]
            for s in (1, 2, 4):
                a_sh = pltpu.roll(A, s, 0)
                u_sh = pltpu.roll(U, s, 0)
                m = row >= s
                U = jnp.where(m, A * u_sh + U, U)
                A = jnp.where(m, A * a_sh, A)
            H = A * cr + U
            u_scr[pl.ds(off, SUBLANES), :] = H
            return H[SUBLANES - 1:SUBLANES, :]

        carry[...] = lax.fori_loop(0, ts // SUBLANES, body, carry[...], unroll=4)
        o_ref[...] = (u_scr[...] * jax.nn.gelu(cur[:, W:2 * W])).astype(o_ref.dtype)

    @pl.when(lax.rem(step, 2) == 0)
    def _():
        tile(pr_even, pr_odd)

    @pl.when(lax.rem(step, 2) == 1)
    def _():
        tile(pr_odd, pr_even)


def _lru(x2d, w_lru, B, S, cw, cb, wa, ba, wx, bx, lam):
    T = B * S
    ts = min(512, S)
    nt = S // ts
    W = LRU_WIDTH
    wcat = jnp.concatenate([wa, wx], axis=-1).astype(bf16)
    row = lambda v: v.reshape(1, W)
    full = lambda shp: pl.BlockSpec(shp, lambda b, c: (0,) * len(shp))
    last = B * nt - 1
    return pl.pallas_call(
        _lru_kernel,
        out_shape=jax.ShapeDtypeStruct((T, W), bf16),
        grid=(B, nt),
        in_specs=[pl.BlockSpec((ts, D_MODEL), lambda b, c: (b * nt + c, 0)),
                  pl.BlockSpec((ts, D_MODEL), lambda b, c: (jnp.minimum(b * nt + c + 1, last), 0)),
                  full(w_lru.shape), full((CONV_WIDTH, W)), full((1, W)), full(wcat.shape),
                  full((1, W)), full((1, W)), full((1, W))],
        out_specs=pl.BlockSpec((ts, W), lambda b, c: (b * nt + c, 0)),
        scratch_shapes=[pltpu.VMEM((ts + SUBLANES, W), f32), pltpu.VMEM((ts, 2 * W), f32),
                        pltpu.VMEM((ts, 2 * W), f32), pltpu.VMEM((ts, W), f32), pltpu.VMEM((ts, W), f32),
                        pltpu.VMEM((1, W), f32)],
        compiler_params=_cparams("arbitrary", "arbitrary"),
        name="rg_lru",
    )(x2d, x2d, w_lru, cw, row(cb), wcat, row(ba), row(bx), row(lam))


def _ml_kernel(xin_ref, w_ref, bias_ref, g_ref, o_ref, c_scr, n_scr, m_scr):
    L = xin_ref.shape[0]

    @pl.when(pl.program_id(1) == 0)
    def _():
        c_scr[...] = jnp.zeros(c_scr.shape, f32)
        n_scr[...] = jnp.zeros(n_scr.shape, f32)
        m_scr[...] = jnp.zeros(m_scr.shape, f32)

    pr = _project(xin_ref, w_ref)
    q = pr[:, 0:ML_QK_W]
    k = pr[:, ML_QK_W:2 * ML_QK_W]
    v = pr[:, 2 * ML_QK_W:2 * ML_QK_W + ML_V_W]
    og = pr[:, 2 * ML_QK_W + ML_V_W:2 * ML_QK_W + 2 * ML_V_W]
    sm = pr[:, 2 * ML_QK_W + 2 * ML_V_W:] + bias_ref[...]
    logf = _log_sigmoid(sm)
    ri = lax.broadcasted_iota(jnp.int32, (L, L), 0)
    ci = lax.broadcasted_iota(jnp.int32, (L, L), 1)
    tri = ri >= ci
    b_all = _hdot(tri.astype(f32), logf)
    sm_t = sm.T
    b_t = b_all.T
    kb = k.astype(bf16)
    lane_head = lax.broadcasted_iota(jnp.int32, q.shape, 1) // ML_DQK
    scale = ML_DQK ** -0.5
    heads = range(ML_HEADS)
    sls = [slice(ML_DV * h, ML_DV * (h + 1)) for h in heads]
    b_cs, m_news, decays, w_cs, m_ts, dws, iws = [], [], [], [], [], [], []
    for h in heads:
        ig_c = sm[:, SM_ML_I + h:SM_ML_I + h + 1]
        b_c = b_all[:, SM_ML_F + h:SM_ML_F + h + 1]
        ig_r = sm_t[SM_ML_I + h:SM_ML_I + h + 1, :]
        b_r = b_t[SM_ML_F + h:SM_ML_F + h + 1, :]
        b_last = b_c[L - 1:L, :]
        m_prev = m_scr[h:h + 1, 0:1]
        ge_r = b_last - b_r + ig_r
        ge_c = b_last - b_c + ig_c
        m_new = jnp.maximum(b_last + m_prev, jnp.max(ge_r, axis=1, keepdims=True))
        dmat = jnp.where(tri, b_c - b_r + ig_r, NEG_BIG)
        inter = b_c + m_prev
        m_t = jnp.maximum(inter, jnp.max(dmat, axis=1, keepdims=True))
        m_news.append(m_new)
        decays.append(jnp.exp(b_last + m_prev - m_new))
        w_cs.append(jnp.exp(ge_c - m_new))
        m_ts.append(m_t)
        dws.append(jnp.exp(dmat - m_t))
        iws.append(jnp.exp(inter - m_t))
    qms = [jnp.where(lane_head == h, q, 0.0) * scale for h in heads]
    qmbs = [qm.astype(bf16) for qm in qms]
    ss = [_bdot_nt(qmbs[h], kb) * dws[h] for h in heads]
    n_rows = [n_scr[h:h + 1, :] for h in heads]
    nums = [_bdot(ss[h], v[:, sls[h]]) + iws[h] * _bdot(qmbs[h], c_scr[:, sls[h]]) for h in heads]
    dens = [jnp.sum(ss[h], axis=1, keepdims=True) + iws[h] * jnp.sum(qms[h] * n_rows[h], axis=1, keepdims=True)
            for h in heads]
    hhs = [nums[h] / jnp.maximum(jnp.abs(dens[h]), jnp.exp(-m_ts[h])) for h in heads]
    for h in heads:
        c_scr[:, sls[h]] = decays[h] * c_scr[:, sls[h]] + _bdot_tn(kb, w_cs[h] * v[:, sls[h]])
        n_scr[h:h + 1, :] = decays[h] * n_rows[h] + jnp.sum(w_cs[h] * k, axis=0, keepdims=True)
        m_scr[h:h + 1, :] = jnp.broadcast_to(m_news[h], (1, LANES))
    for h in heads:
        ms = jnp.mean(hhs[h] * hhs[h], axis=1, keepdims=True)
        hn = hhs[h] * lax.rsqrt(ms + RMS_EPS) * g_ref[:, sls[h]]
        o_ref[:, sls[h]] = (hn * _sigmoid(og[:, sls[h]])).astype(o_ref.dtype)


def _mlstm(x2d, w_ml, B, S, i_bias, f_bias, norm_g):
    T = B * S
    L = min(ML_CHUNK, S)
    nc = S // L
    bias = jnp.zeros((1, LANES), f32)
    bias = bias.at[0, SM_ML_I:SM_ML_I + ML_HEADS].set(i_bias).at[0, SM_ML_F:SM_ML_F + ML_HEADS].set(f_bias)
    full = lambda shp: pl.BlockSpec(shp, lambda b, c: (0,) * len(shp))
    return pl.pallas_call(
        _ml_kernel,
        out_shape=jax.ShapeDtypeStruct((T, ML_V_W), bf16),
        grid=(B, nc),
        in_specs=[pl.BlockSpec((L, D_MODEL), lambda b, c: (b * nc + c, 0)), full(w_ml.shape),
                  full((1, LANES)), full((1, ML_V_W))],
        out_specs=pl.BlockSpec((L, ML_V_W), lambda b, c: (b * nc + c, 0)),
        scratch_shapes=[pltpu.VMEM((ML_QK_W, ML_V_W), f32), pltpu.VMEM((SUBLANES, ML_QK_W), f32),
                        pltpu.VMEM((SUBLANES, LANES), f32)],
        compiler_params=_cparams("parallel", "arbitrary"),
        name="mlstm",
    )(x2d, w_ml, bias, norm_g.reshape(1, ML_V_W))


def _split_bf16(x, passes):
    hi = x.astype(bf16)
    if passes == 1:
        return (hi,)
    return (hi, (x - hi.astype(f32)).astype(bf16))


def _mdot(xs, ys):
    out = jnp.dot(xs[0], ys[0], preferred_element_type=f32)
    if len(xs) > 1:
        out = out + jnp.dot(xs[1], ys[0], preferred_element_type=f32)
        out = out + jnp.dot(xs[0], ys[1], preferred_element_type=f32)
    return out


def _gd_kernel(xin_ref, w_ref, cw_ref, alog_ref, dtb_ref, g_ref, o_ref, ext, s_scr):
    R = xin_ref.shape[0]
    KW = GD_K_W
    L = GD_CHUNK
    nch = R // L
    lsh = L.bit_length() - 1
    PW = nch * L
    nsl = KW // LANES
    passes = GD_SOLVE_PASSES

    @pl.when(pl.program_id(1) == 0)
    def _():
        ext[0:SUBLANES, :] = jnp.zeros((SUBLANES, ext.shape[1]), f32)
        s_scr[...] = jnp.zeros(s_scr.shape, f32)

    pr = _project(xin_ref, w_ref)
    z = pr[:, 3 * KW:4 * KW]
    sm = pr[:, 4 * KW:]
    ext[SUBLANES:SUBLANES + R, :] = pr[:, 0:3 * KW]
    xc = (cw_ref[3:4, :] * ext[SUBLANES:SUBLANES + R, :]
          + cw_ref[2:3, :] * ext[SUBLANES - 1:SUBLANES - 1 + R, :]
          + cw_ref[1:2, :] * ext[SUBLANES - 2:SUBLANES - 2 + R, :]
          + cw_ref[0:1, :] * ext[SUBLANES - 3:SUBLANES - 3 + R, :])
    ext[0:SUBLANES, :] = ext[R:R + SUBLANES, :]
    xc = xc * _sigmoid(xc)

    g_t = -jnp.exp(alog_ref[...]) * _softplus(sm + dtb_ref[...])
    beta_t = _sigmoid(sm)
    rr = lax.broadcasted_iota(jnp.int32, (R, R), 0)
    cc = lax.broadcasted_iota(jnp.int32, (R, R), 1)
    cum = jnp.where(((rr >> lsh) == (cc >> lsh)) & (rr >= cc), 1.0, 0.0)
    gc_all = _hdot(cum, g_t)
    gc_t = gc_all.T

    pt = lax.broadcasted_iota(jnp.int32, (L, PW), 0)
    plane = lax.broadcasted_iota(jnp.int32, (L, PW), 1)
    ps = plane & (L - 1)
    pc = plane >> lsh
    tri_p = pt >= ps
    strict_p = pt > ps
    eye_p = jnp.where(pt == ps, 1.0, 0.0)
    blk8_p = (pt >> 3) == (ps >> 3)
    cmask_b = [jnp.where(pc == c, 1.0, 0.0).astype(bf16) for c in range(nch)]

    def pick(parts):
        out = parts[nch - 1]
        for c in range(nch - 2, -1, -1):
            out = jnp.where(pc == c, parts[c], out)
        return out

    def chunks(col):
        return [col[c * L:(c + 1) * L] for c in range(nch)]

    def bd(yb):
        return jnp.concatenate([yb * cmask_b[c] for c in range(nch)], axis=0)

    def pdot(xp, yp):
        return _mdot(_split_bf16(xp, passes), [bd(p) for p in _split_bf16(yp, passes)])

    lane_r = lax.broadcasted_iota(jnp.int32, (R, LANES), 1)
    lo_r = lane_r < GD_DK
    lane_l = lax.broadcasted_iota(jnp.int32, (L, LANES), 1)
    lo_l = lane_l < GD_DK
    lane_row = lax.broadcasted_iota(jnp.int32, (1, LANES), 1)
    r2 = lax.broadcasted_iota(jnp.int32, (LANES, LANES), 0)
    c2 = lax.broadcasted_iota(jnp.int32, (LANES, LANES), 1)
    blockdiag = (r2 < GD_DK) == (c2 < GD_DK)

    def half_sums(y, lo):
        s_lo = jnp.sum(jnp.where(lo, y, 0.0), axis=1, keepdims=True)
        s_hi = jnp.sum(jnp.where(lo, 0.0, y), axis=1, keepdims=True)
        return s_lo, s_hi

    def l2n(y):
        s_lo, s_hi = half_sums(y * y, lo_r)
        return y * jnp.where(lo_r, lax.rsqrt(s_lo + RMS_EPS), lax.rsqrt(s_hi + RMS_EPS))

    qns, kns, vss, a_ps, p_bs, gcols, bcols = [], [], [], [], [], [], []
    for j in range(nsl):
        qn = l2n(xc[:, LANES * j:LANES * (j + 1)]) * (GD_DK ** -0.5)
        kn = l2n(xc[:, KW + LANES * j:KW + LANES * (j + 1)])
        qns.append(qn)
        kns.append(kn)
        vss.append(xc[:, 2 * KW + LANES * j:2 * KW + LANES * (j + 1)])
        lhs = jnp.concatenate([jnp.where(lo_r, kn, 0.0), jnp.where(lo_r, 0.0, kn),
                               jnp.where(lo_r, qn, 0.0), jnp.where(lo_r, 0.0, qn)], axis=0)
        gram = _bdot_nt(lhs, kn)
        for e in range(2):
            h = 2 * j + e
            gcol = gc_all[:, SM_GD_A + h:SM_GD_A + h + 1]
            bcol = beta_t[:, SM_GD_B + h:SM_GD_B + h + 1]
            gc_r = gc_t[SM_GD_A + h:SM_GD_A + h + 1, :]
            gam_p = jnp.exp(jnp.where(tri_p, pick(chunks(gcol)) - gc_r, NEG_BIG))
            kk_p = pick(chunks(gram[e * R:(e + 1) * R]))
            qk_p = pick(chunks(gram[(2 + e) * R:(3 + e) * R]))
            a_ps.append(jnp.where(strict_p, pick(chunks(bcol)) * gam_p * kk_p, 0.0))
            p_bs.append((gam_p * qk_p).astype(bf16))
            gcols.append(gcol)
            bcols.append(bcol)

    ads = [jnp.where(blk8_p, a, 0.0) for a in a_ps]
    a2s = [pdot(ad, ad) for ad in ads]
    a4s = [pdot(a2, a2) for a2 in a2s]
    xs = [pdot(eye_p - ad, eye_p + a2) for ad, a2 in zip(ads, a2s)]
    xs = [pdot(x, eye_p + a4) for x, a4 in zip(xs, a4s)]
    for sh in range(3, lsh):
        msk = ((pt >> (sh + 1)) == (ps >> (sh + 1))) & (((pt >> sh) & 1) == 1) & (((ps >> sh) & 1) == 0)
        ts = [pdot(jnp.where(msk, a, 0.0), x) for a, x in zip(a_ps, xs)]
        xs = [x - pdot(x, t) for x, t in zip(xs, ts)]

    us, ws, qgs, kds, gls = [], [], [], [], []
    for j in range(nsl):
        uw, egs, eds, glh = [], [], [], []
        for e in range(2):
            h = 2 * j + e
            gcol, bcol = gcols[h], bcols[h]
            eg = jnp.exp(gcol)
            rhs = jnp.concatenate([bcol * vss[j], (bcol * eg) * kns[j]], axis=1)
            uw.append(_mdot([bd(p) for p in _split_bf16(xs[h], passes)], _split_bf16(rhs, passes)))
            lasts = [gcol[c * L + L - 1:c * L + L] for c in range(nch)]
            gcl = jnp.concatenate([jnp.broadcast_to(v, (L, 1)) for v in lasts], axis=0)
            egs.append(eg)
            eds.append(jnp.exp(gcl - gcol))
            glh.append([jnp.exp(v) for v in lasts])
        us.append(jnp.where(lo_r, uw[0][:, :LANES], uw[1][:, :LANES]))
        ws.append(jnp.where(lo_r, uw[0][:, LANES:], uw[1][:, LANES:]))
        qgs.append(qns[j] * jnp.where(lo_r, egs[0], egs[1]))
        kds.append(kns[j] * jnp.where(lo_r, eds[0], eds[1]))
        gls.append([jnp.where(lane_row < GD_DK, glh[0][c], glh[1][c]) for c in range(nch)])

    states = [s_scr[j] for j in range(nsl)]
    outs = [[] for _ in range(nsl)]
    for c in range(nch):
        rs = slice(c * L, (c + 1) * L)
        for j in range(nsl):
            s_prev = states[j]
            sb = s_prev.astype(bf16)
            delta = us[j][rs] - _bdot(ws[j][rs], sb)
            db = delta.astype(bf16)
            pieces = []
            if c > 0:
                pieces.append(jnp.zeros((c * L, LANES), bf16))
            pieces.append(db)
            if c < nch - 1:
                pieces.append(jnp.zeros(((nch - 1 - c) * L, LANES), bf16))
            dpad = jnp.concatenate(pieces, axis=0) if len(pieces) > 1 else db
            intra = jnp.where(lo_l, jnp.dot(p_bs[2 * j], dpad, preferred_element_type=f32),
                              jnp.dot(p_bs[2 * j + 1], dpad, preferred_element_type=f32))
            outs[j].append(_bdot(qgs[j][rs], sb) + intra)
            upd = _bdot_tn(kds[j][rs], db)
            states[j] = gls[j][c] * s_prev + jnp.where(blockdiag, upd, 0.0)

    for j in range(nsl):
        sl = slice(LANES * j, LANES * (j + 1))
        s_scr[j] = states[j]
        o = jnp.concatenate(outs[j], axis=0) if nch > 1 else outs[j][0]
        m_lo, m_hi = half_sums(o * o, lo_r)
        inv = jnp.where(lo_r, lax.rsqrt(m_lo * (1.0 / GD_DV) + RMS_EPS), lax.rsqrt(m_hi * (1.0 / GD_DV) + RMS_EPS))
        zs = z[:, sl]
        o_ref[:, sl] = (o * inv * g_ref[:, sl] * (zs * _sigmoid(zs))).astype(o_ref.dtype)


def _gdn(x2d, w_gd, B, S, conv_w, a_log, dt_bias, norm_g):
    T = B * S
    L = min(GD_ROWS, S)
    nc = S // L
    KW = GD_K_W
    alog = jnp.zeros((1, LANES), f32).at[0, SM_GD_A:SM_GD_A + GD_HEADS].set(a_log)
    dtb = jnp.zeros((1, LANES), f32).at[0, SM_GD_A:SM_GD_A + GD_HEADS].set(dt_bias)
    g_row = jnp.tile(norm_g, GD_HEADS).reshape(1, GD_V_W)
    full = lambda shp: pl.BlockSpec(shp, lambda b, c: (0,) * len(shp))
    return pl.pallas_call(
        _gd_kernel,
        out_shape=jax.ShapeDtypeStruct((T, GD_V_W), bf16),
        grid=(B, nc),
        in_specs=[pl.BlockSpec((L, D_MODEL), lambda b, c: (b * nc + c, 0)), full(w_gd.shape),
                  full((CONV_WIDTH, 3 * KW)), full((1, LANES)), full((1, LANES)), full((1, GD_V_W))],
        out_specs=pl.BlockSpec((L, GD_V_W), lambda b, c: (b * nc + c, 0)),
        scratch_shapes=[pltpu.VMEM((L + SUBLANES, 3 * KW), f32),
                        pltpu.VMEM((KW // LANES, LANES, LANES), f32)],
        compiler_params=_cparams("parallel", "arbitrary"),
        name="gated_deltanet",
    )(x2d, w_gd, conv_w, alog, dtb, g_row)


def _merge_kernel(ha_ref, hb_ref, hc_ref, x_ref, wg_ref, wa_ref, wb_ref, wc_ref, wo_ref, lg_ref, lb_ref,
                  rw_ref, rb_ref, x1_ref, x1p_ref, pk_ref, cnt_ref, prev, carry):
    D = x_ref.shape[1]
    i = pl.program_id(0)

    @pl.when(i == 0)
    def _():
        prev[...] = jnp.zeros(prev.shape, f32)
        carry[...] = jnp.zeros(carry.shape, f32)

    route = _route_stages(prev[...], jnp.where(i > 0, 1.0, 0.0), rw_ref, rb_ref, pk_ref, cnt_ref, carry)
    xb = x_ref[...].astype(bf16)

    next(route)
    g0 = jnp.dot(xb, wg_ref[:, 0:D], preferred_element_type=f32)
    next(route)
    g1 = jnp.dot(xb, wg_ref[:, D:2 * D], preferred_element_type=f32)
    next(route)
    g2 = jnp.dot(xb, wg_ref[:, 2 * D:3 * D], preferred_element_type=f32)
    next(route)
    bra = jnp.dot(ha_ref[...], wa_ref[...], preferred_element_type=f32)
    next(route)
    brb = jnp.dot(hb_ref[...], wb_ref[...], preferred_element_type=f32)
    brc = jnp.dot(hc_ref[...], wc_ref[...], preferred_element_type=f32)
    next(route)
    merged = _sigmoid(g0) * bra + _sigmoid(g1) * brb + _sigmoid(g2) * brc
    y = jnp.dot(merged.astype(bf16), wo_ref[...], preferred_element_type=f32)
    for _ in route:
        pass
    out = _layer_norm(DN_ALPHA * x_ref[...] + y, lg_ref[...], lb_ref[...])
    x1_ref[...] = out
    x1p_ref[...] = _pack_halves(out)
    prev[...] = out


def _merge(ha, hb, hc, x2d, w_gate, wa, wb, wc, wo, lg, lb, router_w, router_b):
    T = x2d.shape[0]
    D = D_MODEL
    tm = min(256, T)
    nt = T // tm
    rw, rb = _router_params(router_w, router_b)
    full = lambda shp: pl.BlockSpec(shp, lambda i: (0,) * len(shp))
    rows = lambda w: pl.BlockSpec((tm, w), lambda i: (jnp.minimum(i, nt - 1), 0))
    routed = pl.BlockSpec((tm, LANES), lambda i: (jnp.maximum(i - 1, 0), 0))
    return pl.pallas_call(
        _merge_kernel,
        out_shape=(jax.ShapeDtypeStruct((T, D), f32), jax.ShapeDtypeStruct((T, D // 2), jnp.uint32),
                   jax.ShapeDtypeStruct((T, LANES), f32), jax.ShapeDtypeStruct((1, LANES), f32)),
        grid=(nt + 1,),
        in_specs=[rows(LRU_WIDTH), rows(ML_V_W), rows(GD_V_W), rows(D), full(w_gate.shape),
                  full((LRU_WIDTH, D)), full((ML_V_W, D)), full((GD_V_W, D)), full((D, D)),
                  full((1, D)), full((1, D)), full((D, LANES)), full((1, LANES))],
        out_specs=(rows(D), rows(D // 2), routed, full((1, LANES))),
        scratch_shapes=[pltpu.VMEM((tm, D), f32), pltpu.VMEM((1, LANES), f32)],
        compiler_params=_cparams("arbitrary"),
        name="merge_outproj_ln_route",
    )(ha, hb, hc, x2d, w_gate, wa.astype(bf16), wb.astype(bf16), wc.astype(bf16),
      wo.astype(bf16), lg.reshape(1, D), lb.reshape(1, D), rw, rb)


PK_IDX = 0
PK_GATE = 4
PK_RANK = 8


def _route_stages(x, weight, rw_ref, rb_ref, pk_ref, cnt_ref, carry):
    tr = x.shape[0]
    logits = _mdot(_split_bf16(x, 3), _split_bf16(rw_ref[...], 3)) + rb_ref[...]
    lane = lax.broadcasted_iota(jnp.int32, (tr, LANES), 1).astype(f32)
    vals = logits
    idxs, tops = [], []
    yield
    for _ in range(TOP_K):
        m = jnp.max(vals, axis=1, keepdims=True)
        idx = jnp.min(jnp.where(vals == m, lane, float(LANES)), axis=1, keepdims=True)
        idxs.append(idx)
        tops.append(m)
        vals = jnp.where(lane == idx, NEG_BIG * 2.0, vals)
        yield
    es = [jnp.exp(t - tops[0]) for t in tops]
    tot = es[0] + es[1] + es[2] + es[3]
    onehots = [lane == idx for idx in idxs]
    sel = jnp.zeros((tr, LANES), f32)
    for oh in onehots:
        sel = sel + oh.astype(f32)
    ri = lax.broadcasted_iota(jnp.int32, (tr, tr), 0)
    ci = lax.broadcasted_iota(jnp.int32, (tr, tr), 1)
    before = jnp.dot((ri > ci).astype(bf16), sel.astype(bf16), preferred_element_type=f32) + carry[...]
    carry[...] = carry[...] + weight * jnp.sum(sel, axis=0, keepdims=True)
    cnt_ref[...] = carry[...]
    yield
    packed = jnp.zeros((tr, LANES), f32)
    for kk in range(TOP_K):
        rank = jnp.sum(jnp.where(onehots[kk], before, 0.0), axis=1, keepdims=True)
        packed = jnp.where(lane == float(PK_IDX + kk), idxs[kk], packed)
        packed = jnp.where(lane == float(PK_GATE + kk), es[kk] / tot, packed)
        packed = jnp.where(lane == float(PK_RANK + kk), rank, packed)
    pk_ref[...] = packed


def _router_params(router_w, router_b):
    rw = jnp.zeros((D_MODEL, LANES), f32).at[:, :N_EXPERTS].set(router_w)
    rb = jnp.full((1, LANES), NEG_BIG, f32).at[0, :N_EXPERTS].set(router_b)
    return rw, rb


def _expert_kernel(be_ref, nu_ref, x_ref, w1_ref, b1_ref, w2_ref, b2_ref, o_ref, w1b, w2b):
    i = pl.program_id(0)
    used = i < nu_ref[0]
    new_expert = jnp.logical_or(i == 0, be_ref[i] != be_ref[jnp.maximum(i - 1, 0)])

    @pl.when(jnp.logical_and(used, new_expert))
    def _():
        w1b[...] = w1_ref[0, 0].astype(bf16)
        w2b[...] = w2_ref[0, 0].astype(bf16)

    @pl.when(used)
    def _():
        x = _unpack_halves(x_ref[...]).astype(bf16)
        hdn = jnp.dot(x, w1b[...], preferred_element_type=f32) + b1_ref[0, 0]
        glu = jnp.minimum(hdn[:, :D_FF], SWIGLU_LIMIT)
        lin = jnp.clip(hdn[:, D_FF:], -SWIGLU_LIMIT, SWIGLU_LIMIT)
        act = glu * _sigmoid(SWIGLU_ALPHA * glu) * (lin + 1.0)
        y = jnp.dot(act.astype(bf16), w2b[...], preferred_element_type=f32) + b2_ref[0, 0]
        o_ref[...] = _pack_halves(y)

    @pl.when(i >= nu_ref[0])
    def _():
        o_ref[...] = jnp.zeros(o_ref.shape, o_ref.dtype)


def _experts(xs, block_e, n_used, layer, w1, b1, w2, b2):
    P = xs.shape[0]
    nb = P // MOE_BLOCK
    D = D_MODEL
    nl = w1.shape[0]
    grid_spec = pltpu.PrefetchScalarGridSpec(
        num_scalar_prefetch=2,
        grid=(nb,),
        in_specs=[pl.BlockSpec((MOE_BLOCK, D // 2), lambda i, be, nu: (i, 0)),
                  pl.BlockSpec((1, 1, D, 2 * D_FF), lambda i, be, nu: (layer, be[i], 0, 0)),
                  pl.BlockSpec((1, 1, 1, 2 * D_FF), lambda i, be, nu: (layer, be[i], 0, 0)),
                  pl.BlockSpec((1, 1, D_FF, D), lambda i, be, nu: (layer, be[i], 0, 0)),
                  pl.BlockSpec((1, 1, 1, D), lambda i, be, nu: (layer, be[i], 0, 0))],
        out_specs=pl.BlockSpec((MOE_BLOCK, D // 2), lambda i, be, nu: (i, 0)),
        scratch_shapes=[pltpu.VMEM((D, 2 * D_FF), bf16), pltpu.VMEM((D_FF, D), bf16)],
    )
    return pl.pallas_call(
        _expert_kernel,
        out_shape=jax.ShapeDtypeStruct((P, D // 2), jnp.uint32),
        grid_spec=grid_spec,
        compiler_params=_cparams("arbitrary"),
        name="experts",
    )(block_e, n_used, xs, w1, b1.reshape(nl, N_EXPERTS, 1, 2 * D_FF), w2, b2.reshape(nl, N_EXPERTS, 1, D))


def _combine_kernel(y0_ref, y1_ref, y2_ref, y3_ref, pk_ref, x_ref, lg_ref, lb_ref, o_ref):
    pk = pk_ref[...]
    y = pk[:, PK_GATE:PK_GATE + 1] * _unpack_halves(y0_ref[...])
    for kk, y_ref in ((1, y1_ref), (2, y2_ref), (3, y3_ref)):
        y = y + pk[:, PK_GATE + kk:PK_GATE + kk + 1] * _unpack_halves(y_ref[...])
    o_ref[...] = _layer_norm(DN_ALPHA * x_ref[...] + y, lg_ref[...], lb_ref[...])


def _combine(yg, packed, x1, lg, lb):
    T, D = x1.shape
    tm = min(512, T)
    nt = T // tm
    full = lambda shp: pl.BlockSpec(shp, lambda i: (0,) * len(shp))
    choice = lambda kk: pl.BlockSpec((tm, D // 2), lambda i: (kk * nt + i, 0))
    return pl.pallas_call(
        _combine_kernel,
        out_shape=jax.ShapeDtypeStruct((T, D), f32),
        grid=(nt,),
        in_specs=[choice(0), choice(1), choice(2), choice(3), pl.BlockSpec((tm, LANES), lambda i: (i, 0)),
                  pl.BlockSpec((tm, D), lambda i: (i, 0)), full((1, D)), full((1, D))],
        out_specs=pl.BlockSpec((tm, D), lambda i: (i, 0)),
        compiler_params=_cparams("parallel"),
        name="combine_ln",
    )(yg, yg, yg, yg, packed, x1, lg.reshape(1, D), lb.reshape(1, D))


def _sc_mesh():
    return plsc.VectorSubcoreMesh(core_axis_name="c", subcore_axis_name="s")


def _sc_dispatch(x, posk, n_slots):
    t_tokens, w = x.shape

    @functools.partial(pl.kernel, out_type=jax.ShapeDtypeStruct((n_slots, w), x.dtype), mesh=_sc_mesh(),
                       name="sc_dispatch")
    def k(x_hbm, p_hbm, o_hbm):
        def body(x_vmem, p_vmem):
            for kk in range(TOP_K):
                pltpu.sync_copy(x_vmem, o_hbm.at[p_vmem.at[kk]])

        pltpu.emit_pipeline(
            body,
            grid=(t_tokens // SC_ROWS,),
            in_specs=[pl.BlockSpec((SC_ROWS, w), lambda i: (i, 0), pipeline_mode=pl.Buffered(1)),
                      pl.BlockSpec((TOP_K, SC_ROWS), lambda i: (0, i))],
            out_specs=[],
            core_axis_name=("c", "s"),
            dimension_semantics=(pltpu.PARALLEL,),
        )(x_hbm, p_hbm)

    return k(x, posk)


def _sc_gather(table, idx):
    n = idx.shape[0]
    w = table.shape[1]

    @functools.partial(pl.kernel, out_type=jax.ShapeDtypeStruct((n, w), table.dtype), mesh=_sc_mesh(),
                       name="sc_gather")
    def k(t_hbm, i_hbm, o_hbm):
        def body(i_vmem, o_vmem):
            pltpu.sync_copy(t_hbm.at[i_vmem.at[0]], o_vmem)

        pltpu.emit_pipeline(
            body,
            grid=(n // SC_ROWS,),
            in_specs=[pl.BlockSpec((1, SC_ROWS), lambda i: (0, i))],
            out_specs=[pl.BlockSpec((SC_ROWS, w), lambda i: (i, 0), pipeline_mode=pl.Buffered(1))],
            core_axis_name=("c", "s"),
            dimension_semantics=(pltpu.PARALLEL,),
        )(i_hbm, o_hbm)

    return k(table, idx.reshape(1, n))


def _moe(x1, x1p, packed, cnt, layer, w1, b1, w2, b2, lg, lb):
    T, D = x1.shape
    A = T * TOP_K
    idx = packed[:, PK_IDX:PK_IDX + TOP_K].astype(jnp.int32)
    rank = packed[:, PK_RANK:PK_RANK + TOP_K].astype(jnp.int32)
    counts = cnt[0, :N_EXPERTS].astype(jnp.int32)
    padded = ((counts + MOE_BLOCK - 1) // MOE_BLOCK) * MOE_BLOCK
    pad_end = jnp.cumsum(padded)
    pad_start = pad_end - padded
    posk = (pad_start[idx] + rank).T
    n_blocks = -(-A // MOE_BLOCK) + N_EXPERTS
    P = n_blocks * MOE_BLOCK
    starts = jnp.arange(n_blocks, dtype=jnp.int32) * MOE_BLOCK
    block_e = jnp.minimum(jnp.sum((pad_end[None, :] <= starts[:, None]).astype(jnp.int32), axis=1), N_EXPERTS - 1)
    n_used = (pad_end[-1:] // MOE_BLOCK).astype(jnp.int32)
    xs = _sc_dispatch(x1p, posk, P)
    ys = _experts(xs, block_e, n_used, layer, w1, b1, w2, b2)
    yg = _sc_gather(ys, posk.reshape(A))
    return _combine(yg, packed, x1, lg, lb)


def _layer(x2d, B, S, p, layer, stacked):
    w = _split_w_in(p['w_in'])
    ha = _lru(x2d, w['lru'], B, S, p['lru_conv_w'], p['lru_conv_b'], p['lru_wa'], p['lru_ba'], p['lru_wx'],
              p['lru_bx'], p['lru_lambda'])
    hb = _mlstm(x2d, w['ml'], B, S, p['ml_i_bias'], p['ml_f_bias'], p['ml_norm_g'])
    hc = _gdn(x2d, w['gd'], B, S, p['gd_conv_w'], p['gd_a_log'], p['gd_dt_bias'], p['gd_norm_g'])
    x1, x1p, packed, cnt = _merge(ha, hb, hc, x2d, w['gate'], p['w_br_lru'], p['w_br_ml'], p['w_br_gd'],
                                  p['w_out'], p['ln1_g'], p['ln1_b'], p['router_w'], p['router_b'])
    return _moe(x1, x1p, packed, cnt, layer, stacked['exp_w1'], stacked['exp_b1'],
                stacked['exp_w2'], stacked['exp_b2'], p['ln2_g'], p['ln2_b'])


def kernel(x, w_in, lru_conv_w, lru_conv_b, lru_wa, lru_ba, lru_wx, lru_bx, lru_lambda, ml_i_bias, ml_f_bias, ml_norm_g, gd_conv_w, gd_a_log, gd_dt_bias, gd_norm_g, w_br_lru, w_br_ml, w_br_gd, w_out, ln1_g, ln1_b, router_w, router_b, exp_w1, exp_b1, exp_w2, exp_b2, ln2_g, ln2_b):
    B, S, D = x.shape
    params = dict(w_in=w_in, lru_conv_w=lru_conv_w, lru_conv_b=lru_conv_b, lru_wa=lru_wa, lru_ba=lru_ba,
                  lru_wx=lru_wx, lru_bx=lru_bx, lru_lambda=lru_lambda, ml_i_bias=ml_i_bias,
                  ml_f_bias=ml_f_bias, ml_norm_g=ml_norm_g, gd_conv_w=gd_conv_w, gd_a_log=gd_a_log,
                  gd_dt_bias=gd_dt_bias, gd_norm_g=gd_norm_g, w_br_lru=w_br_lru, w_br_ml=w_br_ml,
                  w_br_gd=w_br_gd, w_out=w_out, ln1_g=ln1_g, ln1_b=ln1_b, router_w=router_w,
                  router_b=router_b, ln2_g=ln2_g, ln2_b=ln2_b)
    stacked = dict(exp_w1=exp_w1, exp_b1=exp_b1, exp_w2=exp_w2, exp_b2=exp_b2)
    h = x.reshape(B * S, D)
    for l in range(w_in.shape[0]):
        h = _layer(h, B, S, {k: v[l] for k, v in params.items()}, l, stacked)
    return h.reshape(B, S, D)
```

```python
import functools

import jax
import jax.numpy as jnp
from jax import lax
from jax.experimental import pallas as pl
from jax.experimental.pallas import tpu as pltpu
from jax.experimental.pallas import tpu_sc as plsc

f32 = jnp.float32
bf16 = jnp.bfloat16

D_MODEL = 1024
LRU_WIDTH = 1024
LRU_BLOCKS = 8
LRU_C = 8.0
CONV_WIDTH = 4
ML_HEADS = 4
ML_DQK = 64
ML_DV = 128
GD_HEADS = 8
GD_DK = 64
GD_DV = 64
N_EXPERTS = 32
TOP_K = 4
D_FF = 1024
SWIGLU_LIMIT = 7.0
SWIGLU_ALPHA = 1.702
MOE_BLOCK = 512
N_BRANCH = 3
DEPTH = 2
DN_ALPHA = (2.0 * DEPTH) ** 0.25
LN_EPS = 1e-5
RMS_EPS = 1e-6

ML_QK_W = ML_HEADS * ML_DQK
ML_V_W = ML_HEADS * ML_DV
GD_K_W = GD_HEADS * GD_DK
GD_V_W = GD_HEADS * GD_DV
IN_SPLITS = (LRU_WIDTH, LRU_WIDTH, ML_QK_W, ML_QK_W, ML_V_W, ML_HEADS, ML_HEADS, ML_V_W,
             GD_K_W, GD_K_W, GD_V_W, GD_HEADS, GD_HEADS, GD_V_W, N_BRANCH * D_MODEL)

LANES = 128
SUBLANES = 8
NEG_BIG = -1e30

SM_ML_I = 0
SM_ML_F = 4
SM_GD_A = 8
SM_GD_B = 16

ML_CHUNK = 256
GD_CHUNK = 64
GD_ROWS = 256
GD_SOLVE_PASSES = 1
VMEM_LIMIT = 56 * 1024 * 1024
SC_ROWS = 128


def _cparams(*sem):
    return pltpu.CompilerParams(dimension_semantics=sem, vmem_limit_bytes=VMEM_LIMIT)


def _hdot(a, b):
    return jnp.dot(a, b, precision=lax.Precision.HIGHEST, preferred_element_type=f32)


def _bdot(a, b):
    return jnp.dot(a.astype(bf16), b.astype(bf16), preferred_element_type=f32)


def _bdot_nt(a, b):
    return lax.dot_general(a.astype(bf16), b.astype(bf16), (((1,), (1,)), ((), ())),
                           preferred_element_type=f32)


def _bdot_tn(a, b):
    return lax.dot_general(a.astype(bf16), b.astype(bf16), (((0,), (0,)), ((), ())),
                           preferred_element_type=f32)


def _split_bf16(x, passes):
    hi = x.astype(bf16)
    if passes == 1:
        return (hi,)
    return (hi, (x - hi.astype(f32)).astype(bf16))


def _mdot(xs, ys):
    out = jnp.dot(xs[0], ys[0], preferred_element_type=f32)
    if len(xs) > 1:
        out = out + jnp.dot(xs[1], ys[0], preferred_element_type=f32)
        out = out + jnp.dot(xs[0], ys[1], preferred_element_type=f32)
    return out


def _sigmoid(x):
    return 1.0 / (1.0 + jnp.exp(-x))


def _softplus(x):
    return jnp.maximum(x, 0.0) + jnp.log1p(jnp.exp(-jnp.abs(x)))


def _log_sigmoid(x):
    return jnp.minimum(x, 0.0) - jnp.log1p(jnp.exp(-jnp.abs(x)))


def _pack_halves(v):
    n = v.shape[1] // 2
    hi = lax.bitcast_convert_type(v[:, :n].astype(bf16).astype(f32), jnp.uint32)
    lo = lax.bitcast_convert_type(v[:, n:].astype(bf16).astype(f32), jnp.uint32)
    return hi | (lo >> 16)


def _unpack_halves(p):
    hi = lax.bitcast_convert_type(p & jnp.uint32(0xFFFF0000), f32)
    lo = lax.bitcast_convert_type(p << 16, f32)
    return jnp.concatenate([hi, lo], axis=1)


def _layer_norm(z, g, b):
    mu = jnp.mean(z, axis=-1, keepdims=True)
    zc = z - mu
    var = jnp.mean(zc * zc, axis=-1, keepdims=True)
    return zc * lax.rsqrt(var + LN_EPS) * g + b


def _split_w_in(w):
    pts = []
    acc = 0
    for s in IN_SPLITS[:-1]:
        acc += s
        pts.append(acc)
    (lru_x, lru_y, ml_q, ml_k, ml_v, ml_i, ml_f, ml_o,
     gd_q, gd_k, gd_v, gd_a, gd_b, gd_z, gate) = jnp.split(w, pts, axis=1)
    pad = jnp.zeros((w.shape[0], LANES - 2 * ML_HEADS - 2 * GD_HEADS), w.dtype)
    small = jnp.concatenate([ml_i, ml_f, gd_a, gd_b, pad], axis=1)
    cat = lambda *cols: jnp.concatenate(cols, axis=1).astype(bf16)
    return dict(lru=cat(lru_x, lru_y), ml=cat(ml_q, ml_k, ml_v, ml_o, small),
                gd=cat(gd_q, gd_k, gd_v, gd_z, small), gate=gate.astype(bf16))


def _project(x_ref, w_ref):
    return jnp.dot(x_ref[...].astype(bf16), w_ref[...], preferred_element_type=f32)


def _lru_kernel(xin_ref, xnext_ref, w_ref, cw_ref, cb_ref, wcat_ref, ba_ref, bx_ref, lam_ref, o_ref,
                xext, pr_even, pr_odd, a_scr, u_scr, carry):
    ts = xin_ref.shape[0]
    W = a_scr.shape[1]
    bw = W // LRU_BLOCKS
    step = pl.program_id(0) * pl.num_programs(1) + pl.program_id(1)

    @pl.when(step == 0)
    def _():
        pr_even[...] = _project(xin_ref, w_ref)

    @pl.when(pl.program_id(1) == 0)
    def _():
        xext[0:SUBLANES, :] = jnp.zeros((SUBLANES, W), f32)
        carry[...] = jnp.zeros((1, W), f32)

    def tile(cur, nxt):
        x = cur[:, 0:W]
        xext[SUBLANES:SUBLANES + ts, :] = x
        xa = (cw_ref[3:4, :] * x
              + cw_ref[2:3, :] * xext[SUBLANES - 1:SUBLANES - 1 + ts, :]
              + cw_ref[1:2, :] * xext[SUBLANES - 2:SUBLANES - 2 + ts, :]
              + cw_ref[0:1, :] * xext[SUBLANES - 3:SUBLANES - 3 + ts, :]) + cb_ref[...]
        xext[0:SUBLANES, :] = xext[ts:ts + SUBLANES, :]

        xnb = xnext_ref[...].astype(bf16)
        pw = 2 * W // LRU_BLOCKS
        cdec = -LRU_C * _softplus(-lam_ref[...])
        for h in range(LRU_BLOCKS):
            sl = slice(bw * h, bw * (h + 1))
            psl = slice(pw * h, pw * (h + 1))
            nxt[:, psl] = jnp.dot(xnb, w_ref[:, psl], preferred_element_type=f32)
            xh = xa[:, sl]
            g = jnp.dot(xh.astype(bf16), wcat_ref[h], preferred_element_type=f32)
            r = _sigmoid(g[:, :bw] + ba_ref[:, sl])
            ig = _sigmoid(g[:, bw:] + bx_ref[:, sl])
            log_a = r * cdec[:, sl]
            a = jnp.exp(log_a)
            a_scr[:, sl] = a
            u_scr[:, sl] = jnp.sqrt(jnp.tanh(-log_a) * (1.0 + a * a)) * (ig * xh)

        row = lax.broadcasted_iota(jnp.int32, (SUBLANES, W), 0)

        def body(g, cr):
            off = pl.multiple_of(g * SUBLANES, SUBLANES)
            A = a_scr[pl.ds(off, SUBLANES), :]
            U = u_scr[pl.ds(off, SUBLANES), :]
            for s in (1, 2, 4):
                a_sh = pltpu.roll(A, s, 0)
                u_sh = pltpu.roll(U, s, 0)
                m = row >= s
                U = jnp.where(m, A * u_sh + U, U)
                A = jnp.where(m, A * a_sh, A)
            H = A * cr + U
            u_scr[pl.ds(off, SUBLANES), :] = H
            return H[SUBLANES - 1:SUBLANES, :]

        carry[...] = lax.fori_loop(0, ts // SUBLANES, body, carry[...], unroll=4)
        o_ref[...] = (u_scr[...] * jax.nn.gelu(cur[:, W:2 * W])).astype(o_ref.dtype)

    @pl.when(lax.rem(step, 2) == 0)
    def _():
        tile(pr_even, pr_odd)

    @pl.when(lax.rem(step, 2) == 1)
    def _():
        tile(pr_odd, pr_even)


def _lru(x2d, w_lru, B, S, cw, cb, wa, ba, wx, bx, lam):
    T = B * S
    ts = min(512, S)
    nt = S // ts
    W = LRU_WIDTH
    wcat = jnp.concatenate([wa, wx], axis=-1).astype(bf16)
    row = lambda v: v.reshape(1, W)
    full = lambda shp: pl.BlockSpec(shp, lambda b, c: (0,) * len(shp))
    last = B * nt - 1
    return pl.pallas_call(
        _lru_kernel,
        out_shape=jax.ShapeDtypeStruct((T, W), bf16),
        grid=(B, nt),
        in_specs=[pl.BlockSpec((ts, D_MODEL), lambda b, c: (b * nt + c, 0)),
                  pl.BlockSpec((ts, D_MODEL), lambda b, c: (jnp.minimum(b * nt + c + 1, last), 0)),
                  full(w_lru.shape), full((CONV_WIDTH, W)), full((1, W)), full(wcat.shape),
                  full((1, W)), full((1, W)), full((1, W))],
        out_specs=pl.BlockSpec((ts, W), lambda b, c: (b * nt + c, 0)),
        scratch_shapes=[pltpu.VMEM((ts + SUBLANES, W), f32), pltpu.VMEM((ts, 2 * W), f32),
                        pltpu.VMEM((ts, 2 * W), f32), pltpu.VMEM((ts, W), f32), pltpu.VMEM((ts, W), f32),
                        pltpu.VMEM((1, W), f32)],
        compiler_params=_cparams("arbitrary", "arbitrary"),
        name="rg_lru",
    )(x2d, x2d, w_lru, cw, row(cb), wcat, row(ba), row(bx), row(lam))


def _ml_kernel(xin_ref, w_ref, bias_ref, g_ref, o_ref, c_scr, n_scr, m_scr):
    L = xin_ref.shape[0]

    @pl.when(pl.program_id(1) == 0)
    def _():
        c_scr[...] = jnp.zeros(c_scr.shape, f32)
        n_scr[...] = jnp.zeros(n_scr.shape, f32)
        m_scr[...] = jnp.zeros(m_scr.shape, f32)

    pr = _project(xin_ref, w_ref)
    q = pr[:, 0:ML_QK_W]
    k = pr[:, ML_QK_W:2 * ML_QK_W]
    v = pr[:, 2 * ML_QK_W:2 * ML_QK_W + ML_V_W]
    og = pr[:, 2 * ML_QK_W + ML_V_W:2 * ML_QK_W + 2 * ML_V_W]
    sm = pr[:, 2 * ML_QK_W + 2 * ML_V_W:] + bias_ref[...]
    logf = _log_sigmoid(sm)
    ri = lax.broadcasted_iota(jnp.int32, (L, L), 0)
    ci = lax.broadcasted_iota(jnp.int32, (L, L), 1)
    tri = ri >= ci
    b_all = _hdot(tri.astype(f32), logf)
    sm_t = sm.T
    b_t = b_all.T
    kb = k.astype(bf16)
    lane_head = lax.broadcasted_iota(jnp.int32, q.shape, 1) // ML_DQK
    scale = ML_DQK ** -0.5
    heads = range(ML_HEADS)
    sls = [slice(ML_DV * h, ML_DV * (h + 1)) for h in heads]
    m_news, decays, w_cs, m_ts, dws, iws = [], [], [], [], [], []
    for h in heads:
        ig_c = sm[:, SM_ML_I + h:SM_ML_I + h + 1]
        b_c = b_all[:, SM_ML_F + h:SM_ML_F + h + 1]
        ig_r = sm_t[SM_ML_I + h:SM_ML_I + h + 1, :]
        b_r = b_t[SM_ML_F + h:SM_ML_F + h + 1, :]
        b_last = b_c[L - 1:L, :]
        m_prev = m_scr[h:h + 1, 0:1]
        ge_r = b_last - b_r + ig_r
        ge_c = b_last - b_c + ig_c
        m_new = jnp.maximum(b_last + m_prev, jnp.max(ge_r, axis=1, keepdims=True))
        dmat = jnp.where(tri, b_c - b_r + ig_r, NEG_BIG)
        inter = b_c + m_prev
        m_t = jnp.maximum(inter, jnp.max(dmat, axis=1, keepdims=True))
        m_news.append(m_new)
        decays.append(jnp.exp(b_last + m_prev - m_new))
        w_cs.append(jnp.exp(ge_c - m_new))
        m_ts.append(m_t)
        dws.append(jnp.exp(dmat - m_t))
        iws.append(jnp.exp(inter - m_t))
    qms = [jnp.where(lane_head == h, q, 0.0) * scale for h in heads]
    qmbs = [qm.astype(bf16) for qm in qms]
    ss = [_bdot_nt(qmbs[h], kb) * dws[h] for h in heads]
    n_rows = [n_scr[h:h + 1, :] for h in heads]
    nums = [_bdot(ss[h], v[:, sls[h]]) + iws[h] * _bdot(qmbs[h], c_scr[:, sls[h]]) for h in heads]
    dens = [jnp.sum(ss[h], axis=1, keepdims=True) + iws[h] * jnp.sum(qms[h] * n_rows[h], axis=1, keepdims=True)
            for h in heads]
    hhs = [nums[h] / jnp.maximum(jnp.abs(dens[h]), jnp.exp(-m_ts[h])) for h in heads]
    for h in heads:
        c_scr[:, sls[h]] = decays[h] * c_scr[:, sls[h]] + _bdot_tn(kb, w_cs[h] * v[:, sls[h]])
        n_scr[h:h + 1, :] = decays[h] * n_rows[h] + jnp.sum(w_cs[h] * k, axis=0, keepdims=True)
        m_scr[h:h + 1, :] = jnp.broadcast_to(m_news[h], (1, LANES))
    for h in heads:
        ms = jnp.mean(hhs[h] * hhs[h], axis=1, keepdims=True)
        hn = hhs[h] * lax.rsqrt(ms + RMS_EPS) * g_ref[:, sls[h]]
        o_ref[:, sls[h]] = (hn * _sigmoid(og[:, sls[h]])).astype(o_ref.dtype)


def _mlstm(x2d, w_ml, B, S, i_bias, f_bias, norm_g):
    T = B * S
    L = min(ML_CHUNK, S)
    nc = S // L
    bias = jnp.zeros((1, LANES), f32)
    bias = bias.at[0, SM_ML_I:SM_ML_I + ML_HEADS].set(i_bias).at[0, SM_ML_F:SM_ML_F + ML_HEADS].set(f_bias)
    full = lambda shp: pl.BlockSpec(shp, lambda b, c: (0,) * len(shp))
    return pl.pallas_call(
        _ml_kernel,
        out_shape=jax.ShapeDtypeStruct((T, ML_V_W), bf16),
        grid=(B, nc),
        in_specs=[pl.BlockSpec((L, D_MODEL), lambda b, c: (b * nc + c, 0)), full(w_ml.shape),
                  full((1, LANES)), full((1, ML_V_W))],
        out_specs=pl.BlockSpec((L, ML_V_W), lambda b, c: (b * nc + c, 0)),
        scratch_shapes=[pltpu.VMEM((ML_QK_W, ML_V_W), f32), pltpu.VMEM((SUBLANES, ML_QK_W), f32),
                        pltpu.VMEM((SUBLANES, LANES), f32)],
        compiler_params=_cparams("parallel", "arbitrary"),
        name="mlstm",
    )(x2d, w_ml, bias, norm_g.reshape(1, ML_V_W))


def _gd_kernel(xin_ref, w_ref, cw_ref, alog_ref, dtb_ref, g_ref, o_ref, ext, s_scr):
    R = xin_ref.shape[0]
    KW = GD_K_W
    L = GD_CHUNK
    nch = R // L
    lsh = L.bit_length() - 1
    PW = nch * L
    nsl = KW // LANES
    passes = GD_SOLVE_PASSES

    @pl.when(pl.program_id(1) == 0)
    def _():
        ext[0:SUBLANES, :] = jnp.zeros((SUBLANES, ext.shape[1]), f32)
        s_scr[...] = jnp.zeros(s_scr.shape, f32)

    pr = _project(xin_ref, w_ref)
    z = pr[:, 3 * KW:4 * KW]
    sm = pr[:, 4 * KW:]
    ext[SUBLANES:SUBLANES + R, :] = pr[:, 0:3 * KW]
    xc = (cw_ref[3:4, :] * ext[SUBLANES:SUBLANES + R, :]
          + cw_ref[2:3, :] * ext[SUBLANES - 1:SUBLANES - 1 + R, :]
          + cw_ref[1:2, :] * ext[SUBLANES - 2:SUBLANES - 2 + R, :]
          + cw_ref[0:1, :] * ext[SUBLANES - 3:SUBLANES - 3 + R, :])
    ext[0:SUBLANES, :] = ext[R:R + SUBLANES, :]
    xc = xc * _sigmoid(xc)

    g_t = -jnp.exp(alog_ref[...]) * _softplus(sm + dtb_ref[...])
    beta_t = _sigmoid(sm)
    rr = lax.broadcasted_iota(jnp.int32, (R, R), 0)
    cc = lax.broadcasted_iota(jnp.int32, (R, R), 1)
    cum = jnp.where(((rr >> lsh) == (cc >> lsh)) & (rr >= cc), 1.0, 0.0)
    gc_all = _hdot(cum, g_t)
    gc_t = gc_all.T

    pt = lax.broadcasted_iota(jnp.int32, (L, PW), 0)
    plane = lax.broadcasted_iota(jnp.int32, (L, PW), 1)
    ps = plane & (L - 1)
    pc = plane >> lsh
    tri_p = pt >= ps
    strict_p = pt > ps
    eye_p = jnp.where(pt == ps, 1.0, 0.0)
    blk8_p = (pt >> 3) == (ps >> 3)
    cmask_b = [jnp.where(pc == c, 1.0, 0.0).astype(bf16) for c in range(nch)]

    def pick(parts):
        out = parts[nch - 1]
        for c in range(nch - 2, -1, -1):
            out = jnp.where(pc == c, parts[c], out)
        return out

    def chunks(col):
        return [col[c * L:(c + 1) * L] for c in range(nch)]

    def bd(yb):
        return jnp.concatenate([yb * cmask_b[c] for c in range(nch)], axis=0)

    def pdot(xp, yp):
        return _mdot(_split_bf16(xp, passes), [bd(p) for p in _split_bf16(yp, passes)])

    lane_r = lax.broadcasted_iota(jnp.int32, (R, LANES), 1)
    lo_r = lane_r < GD_DK
    lane_l = lax.broadcasted_iota(jnp.int32, (L, LANES), 1)
    lo_l = lane_l < GD_DK
    lane_row = lax.broadcasted_iota(jnp.int32, (1, LANES), 1)
    r2 = lax.broadcasted_iota(jnp.int32, (LANES, LANES), 0)
    c2 = lax.broadcasted_iota(jnp.int32, (LANES, LANES), 1)
    blockdiag = (r2 < GD_DK) == (c2 < GD_DK)

    def half_sums(y, lo):
        s_lo = jnp.sum(jnp.where(lo, y, 0.0), axis=1, keepdims=True)
        s_hi = jnp.sum(jnp.where(lo, 0.0, y), axis=1, keepdims=True)
        return s_lo, s_hi

    def l2n(y):
        s_lo, s_hi = half_sums(y * y, lo_r)
        return y * jnp.where(lo_r, lax.rsqrt(s_lo + RMS_EPS), lax.rsqrt(s_hi + RMS_EPS))

    qns, kns, vss, a_ps, p_bs, gcols, bcols = [], [], [], [], [], [], []
    for j in range(nsl):
        qn = l2n(xc[:, LANES * j:LANES * (j + 1)]) * (GD_DK ** -0.5)
        kn = l2n(xc[:, KW + LANES * j:KW + LANES * (j + 1)])
        qns.append(qn)
        kns.append(kn)
        vss.append(xc[:, 2 * KW + LANES * j:2 * KW + LANES * (j + 1)])
        lhs = jnp.concatenate([jnp.where(lo_r, kn, 0.0), jnp.where(lo_r, 0.0, kn),
                               jnp.where(lo_r, qn, 0.0), jnp.where(lo_r, 0.0, qn)], axis=0)
        gram = _bdot_nt(lhs, kn)
        for e in range(2):
            h = 2 * j + e
            gcol = gc_all[:, SM_GD_A + h:SM_GD_A + h + 1]
            bcol = beta_t[:, SM_GD_B + h:SM_GD_B + h + 1]
            gc_r = gc_t[SM_GD_A + h:SM_GD_A + h + 1, :]
            gam_p = jnp.exp(jnp.where(tri_p, pick(chunks(gcol)) - gc_r, NEG_BIG))
            kk_p = pick(chunks(gram[e * R:(e + 1) * R]))
            qk_p = pick(chunks(gram[(2 + e) * R:(3 + e) * R]))
            a_ps.append(jnp.where(strict_p, pick(chunks(bcol)) * gam_p * kk_p, 0.0))
            p_bs.append((gam_p * qk_p).astype(bf16))
            gcols.append(gcol)
            bcols.append(bcol)

    ads = [jnp.where(blk8_p, a, 0.0) for a in a_ps]
    a2s = [pdot(ad, ad) for ad in ads]
    a4s = [pdot(a2, a2) for a2 in a2s]
    xs = [pdot(eye_p - ad, eye_p + a2) for ad, a2 in zip(ads, a2s)]
    xs = [pdot(x, eye_p + a4) for x, a4 in zip(xs, a4s)]
    for sh in range(3, lsh):
        msk = ((pt >> (sh + 1)) == (ps >> (sh + 1))) & (((pt >> sh) & 1) == 1) & (((ps >> sh) & 1) == 0)
        ts = [pdot(jnp.where(msk, a, 0.0), x) for a, x in zip(a_ps, xs)]
        xs = [x - pdot(x, t) for x, t in zip(xs, ts)]

    us, ws, qgs, kds, gls = [], [], [], [], []
    for j in range(nsl):
        uw, egs, eds, glh = [], [], [], []
        for e in range(2):
            h = 2 * j + e
            gcol, bcol = gcols[h], bcols[h]
            eg = jnp.exp(gcol)
            rhs = jnp.concatenate([bcol * vss[j], (bcol * eg) * kns[j]], axis=1)
            uw.append(_mdot([bd(p) for p in _split_bf16(xs[h], passes)], _split_bf16(rhs, passes)))
            lasts = [gcol[c * L + L - 1:c * L + L] for c in range(nch)]
            gcl = jnp.concatenate([jnp.broadcast_to(v, (L, 1)) for v in lasts], axis=0)
            egs.append(eg)
            eds.append(jnp.exp(gcl - gcol))
            glh.append([jnp.exp(v) for v in lasts])
        us.append(jnp.where(lo_r, uw[0][:, :LANES], uw[1][:, :LANES]))
        ws.append(jnp.where(lo_r, uw[0][:, LANES:], uw[1][:, LANES:]))
        qgs.append(qns[j] * jnp.where(lo_r, egs[0], egs[1]))
        kds.append(kns[j] * jnp.where(lo_r, eds[0], eds[1]))
        gls.append([jnp.where(lane_row < GD_DK, glh[0][c], glh[1][c]) for c in range(nch)])

    states = [s_scr[j] for j in range(nsl)]
    outs = [[] for _ in range(nsl)]
    for c in range(nch):
        rs = slice(c * L, (c + 1) * L)
        for j in range(nsl):
            s_prev = states[j]
            sb = s_prev.astype(bf16)
            delta = us[j][rs] - _bdot(ws[j][rs], sb)
            db = delta.astype(bf16)
            pieces = []
            if c > 0:
                pieces.append(jnp.zeros((c * L, LANES), bf16))
            pieces.append(db)
            if c < nch - 1:
                pieces.append(jnp.zeros(((nch - 1 - c) * L, LANES), bf16))
            dpad = jnp.concatenate(pieces, axis=0) if len(pieces) > 1 else db
            intra = jnp.where(lo_l, jnp.dot(p_bs[2 * j], dpad, preferred_element_type=f32),
                              jnp.dot(p_bs[2 * j + 1], dpad, preferred_element_type=f32))
            outs[j].append(_bdot(qgs[j][rs], sb) + intra)
            upd = _bdot_tn(kds[j][rs], db)
            states[j] = gls[j][c] * s_prev + jnp.where(blockdiag, upd, 0.0)

    for j in range(nsl):
        sl = slice(LANES * j, LANES * (j + 1))
        s_scr[j] = states[j]
        o = jnp.concatenate(outs[j], axis=0) if nch > 1 else outs[j][0]
        m_lo, m_hi = half_sums(o * o, lo_r)
        inv = jnp.where(lo_r, lax.rsqrt(m_lo * (1.0 / GD_DV) + RMS_EPS), lax.rsqrt(m_hi * (1.0 / GD_DV) + RMS_EPS))
        zs = z[:, sl]
        o_ref[:, sl] = (o * inv * g_ref[:, sl] * (zs * _sigmoid(zs))).astype(o_ref.dtype)


def _gdn(x2d, w_gd, B, S, conv_w, a_log, dt_bias, norm_g):
    T = B * S
    L = min(GD_ROWS, S)
    nc = S // L
    KW = GD_K_W
    alog = jnp.zeros((1, LANES), f32).at[0, SM_GD_A:SM_GD_A + GD_HEADS].set(a_log)
    dtb = jnp.zeros((1, LANES), f32).at[0, SM_GD_A:SM_GD_A + GD_HEADS].set(dt_bias)
    g_row = jnp.tile(norm_g, GD_HEADS).reshape(1, GD_V_W)
    full = lambda shp: pl.BlockSpec(shp, lambda b, c: (0,) * len(shp))
    return pl.pallas_call(
        _gd_kernel,
        out_shape=jax.ShapeDtypeStruct((T, GD_V_W), bf16),
        grid=(B, nc),
        in_specs=[pl.BlockSpec((L, D_MODEL), lambda b, c: (b * nc + c, 0)), full(w_gd.shape),
                  full((CONV_WIDTH, 3 * KW)), full((1, LANES)), full((1, LANES)), full((1, GD_V_W))],
        out_specs=pl.BlockSpec((L, GD_V_W), lambda b, c: (b * nc + c, 0)),
        scratch_shapes=[pltpu.VMEM((L + SUBLANES, 3 * KW), f32),
                        pltpu.VMEM((KW // LANES, LANES, LANES), f32)],
        compiler_params=_cparams("parallel", "arbitrary"),
        name="gated_deltanet",
    )(x2d, w_gd, conv_w, alog, dtb, g_row)


def _merge_kernel(ha_ref, hb_ref, hc_ref, x_ref, wg_ref, wa_ref, wb_ref, wc_ref, wo_ref, lg_ref, lb_ref,
                  rw_ref, rb_ref, x1_ref, x1p_ref, pk_ref, cnt_ref, prev, carry):
    D = x_ref.shape[1]
    i = pl.program_id(0)

    @pl.when(i == 0)
    def _():
        prev[...] = jnp.zeros(prev.shape, f32)
        carry[...] = jnp.zeros(carry.shape, f32)

    route = _route_stages(prev[...], jnp.where(i > 0, 1.0, 0.0), rw_ref, rb_ref, pk_ref, cnt_ref, carry)
    xb = x_ref[...].astype(bf16)

    next(route)
    g0 = jnp.dot(xb, wg_ref[:, 0:D], preferred_element_type=f32)
    next(route)
    g1 = jnp.dot(xb, wg_ref[:, D:2 * D], preferred_element_type=f32)
    next(route)
    g2 = jnp.dot(xb, wg_ref[:, 2 * D:3 * D], preferred_element_type=f32)
    next(route)
    bra = jnp.dot(ha_ref[...], wa_ref[...], preferred_element_type=f32)
    next(route)
    brb = jnp.dot(hb_ref[...], wb_ref[...], preferred_element_type=f32)
    brc = jnp.dot(hc_ref[...], wc_ref[...], preferred_element_type=f32)
    next(route)
    merged = _sigmoid(g0) * bra + _sigmoid(g1) * brb + _sigmoid(g2) * brc
    y = jnp.dot(merged.astype(bf16), wo_ref[...], preferred_element_type=f32)
    for _ in route:
        pass
    out = _layer_norm(DN_ALPHA * x_ref[...] + y, lg_ref[...], lb_ref[...])
    x1_ref[...] = out
    x1p_ref[...] = _pack_halves(out)
    prev[...] = out


def _merge(ha, hb, hc, x2d, w_gate, wa, wb, wc, wo, lg, lb, router_w, router_b):
    T = x2d.shape[0]
    D = D_MODEL
    tm = min(512, T)
    nt = T // tm
    rw, rb = _router_params(router_w, router_b)
    full = lambda shp: pl.BlockSpec(shp, lambda i: (0,) * len(shp))
    held = lambda shp: pl.BlockSpec(shp, lambda i: (0,) * len(shp), pipeline_mode=pl.Buffered(1))
    rows = lambda w: pl.BlockSpec((tm, w), lambda i: (jnp.minimum(i, nt - 1), 0))
    routed = pl.BlockSpec((tm, LANES), lambda i: (jnp.maximum(i - 1, 0), 0))
    return pl.pallas_call(
        _merge_kernel,
        out_shape=(jax.ShapeDtypeStruct((T, D), f32), jax.ShapeDtypeStruct((T, D // 2), jnp.uint32),
                   jax.ShapeDtypeStruct((T, LANES), f32), jax.ShapeDtypeStruct((1, LANES), f32)),
        grid=(nt + 1,),
        in_specs=[rows(LRU_WIDTH), rows(ML_V_W), rows(GD_V_W), rows(D), held(w_gate.shape),
                  held((LRU_WIDTH, D)), held((ML_V_W, D)), held((GD_V_W, D)), held((D, D)),
                  full((1, D)), full((1, D)), held((D, LANES)), full((1, LANES))],
        out_specs=(rows(D), rows(D // 2), routed, full((1, LANES))),
        scratch_shapes=[pltpu.VMEM((tm, D), f32), pltpu.VMEM((1, LANES), f32)],
        compiler_params=_cparams("arbitrary"),
        name="merge_outproj_ln_route",
    )(ha, hb, hc, x2d, w_gate, wa.astype(bf16), wb.astype(bf16), wc.astype(bf16),
      wo.astype(bf16), lg.reshape(1, D), lb.reshape(1, D), rw, rb)


PK_IDX = 0
PK_GATE = 4
PK_RANK = 8


def _route_stages(x, weight, rw_ref, rb_ref, pk_ref, cnt_ref, carry):
    tr = x.shape[0]
    logits = _mdot(_split_bf16(x, 3), _split_bf16(rw_ref[...], 3)) + rb_ref[...]
    lane = lax.broadcasted_iota(jnp.int32, (tr, LANES), 1).astype(f32)
    vals = logits
    idxs, tops = [], []
    yield
    for _ in range(TOP_K):
        m = jnp.max(vals, axis=1, keepdims=True)
        idx = jnp.min(jnp.where(vals == m, lane, float(LANES)), axis=1, keepdims=True)
        idxs.append(idx)
        tops.append(m)
        vals = jnp.where(lane == idx, NEG_BIG * 2.0, vals)
        yield
    es = [jnp.exp(t - tops[0]) for t in tops]
    tot = es[0] + es[1] + es[2] + es[3]
    onehots = [lane == idx for idx in idxs]
    sel = jnp.zeros((tr, LANES), f32)
    for oh in onehots:
        sel = sel + oh.astype(f32)
    ri = lax.broadcasted_iota(jnp.int32, (tr, tr), 0)
    ci = lax.broadcasted_iota(jnp.int32, (tr, tr), 1)
    before = jnp.dot((ri > ci).astype(bf16), sel.astype(bf16), preferred_element_type=f32) + carry[...]
    carry[...] = carry[...] + weight * jnp.sum(sel, axis=0, keepdims=True)
    cnt_ref[...] = carry[...]
    yield
    packed = jnp.zeros((tr, LANES), f32)
    for kk in range(TOP_K):
        rank = jnp.sum(jnp.where(onehots[kk], before, 0.0), axis=1, keepdims=True)
        packed = jnp.where(lane == float(PK_IDX + kk), idxs[kk], packed)
        packed = jnp.where(lane == float(PK_GATE + kk), es[kk] / tot, packed)
        packed = jnp.where(lane == float(PK_RANK + kk), rank, packed)
    pk_ref[...] = packed


def _router_params(router_w, router_b):
    rw = jnp.zeros((D_MODEL, LANES), f32).at[:, :N_EXPERTS].set(router_w)
    rb = jnp.full((1, LANES), NEG_BIG, f32).at[0, :N_EXPERTS].set(router_b)
    return rw, rb


def _expert_kernel(be_ref, nu_ref, x_ref, w1_ref, b1_ref, w2_ref, b2_ref, o_ref, w1b, w2b):
    i = pl.program_id(0)
    used = i < nu_ref[0]
    new_expert = jnp.logical_or(i == 0, be_ref[i] != be_ref[jnp.maximum(i - 1, 0)])

    @pl.when(jnp.logical_and(used, new_expert))
    def _():
        w1b[...] = w1_ref[0, 0].astype(bf16)
        w2b[...] = w2_ref[0, 0].astype(bf16)

    @pl.when(used)
    def _():
        x = _unpack_halves(x_ref[...]).astype(bf16)
        hdn = jnp.dot(x, w1b[...], preferred_element_type=f32) + b1_ref[0, 0]
        glu = jnp.minimum(hdn[:, :D_FF], SWIGLU_LIMIT)
        lin = jnp.clip(hdn[:, D_FF:], -SWIGLU_LIMIT, SWIGLU_LIMIT)
        act = glu * _sigmoid(SWIGLU_ALPHA * glu) * (lin + 1.0)
        y = jnp.dot(act.astype(bf16), w2b[...], preferred_element_type=f32) + b2_ref[0, 0]
        o_ref[...] = _pack_halves(y)

    @pl.when(i >= nu_ref[0])
    def _():
        o_ref[...] = jnp.zeros(o_ref.shape, o_ref.dtype)


def _experts(xs, block_e, n_used, layer, w1, b1, w2, b2):
    P = xs.shape[0]
    nb = P // MOE_BLOCK
    D = D_MODEL
    nl = w1.shape[0]
    grid_spec = pltpu.PrefetchScalarGridSpec(
        num_scalar_prefetch=2,
        grid=(nb,),
        in_specs=[pl.BlockSpec((MOE_BLOCK, D // 2), lambda i, be, nu: (i, 0)),
                  pl.BlockSpec((1, 1, D, 2 * D_FF), lambda i, be, nu: (layer, be[i], 0, 0)),
                  pl.BlockSpec((1, 1, 1, 2 * D_FF), lambda i, be, nu: (layer, be[i], 0, 0)),
                  pl.BlockSpec((1, 1, D_FF, D), lambda i, be, nu: (layer, be[i], 0, 0)),
                  pl.BlockSpec((1, 1, 1, D), lambda i, be, nu: (layer, be[i], 0, 0))],
        out_specs=pl.BlockSpec((MOE_BLOCK, D // 2), lambda i, be, nu: (i, 0)),
        scratch_shapes=[pltpu.VMEM((D, 2 * D_FF), bf16), pltpu.VMEM((D_FF, D), bf16)],
    )
    return pl.pallas_call(
        _expert_kernel,
        out_shape=jax.ShapeDtypeStruct((P, D // 2), jnp.uint32),
        grid_spec=grid_spec,
        compiler_params=_cparams("arbitrary"),
        name="experts",
    )(block_e, n_used, xs, w1, b1.reshape(nl, N_EXPERTS, 1, 2 * D_FF), w2, b2.reshape(nl, N_EXPERTS, 1, D))


def _combine_kernel(y0_ref, y1_ref, y2_ref, y3_ref, pk_ref, x_ref, lg_ref, lb_ref, o_ref):
    pk = pk_ref[...]
    y = pk[:, PK_GATE:PK_GATE + 1] * _unpack_halves(y0_ref[...])
    for kk, y_ref in ((1, y1_ref), (2, y2_ref), (3, y3_ref)):
        y = y + pk[:, PK_GATE + kk:PK_GATE + kk + 1] * _unpack_halves(y_ref[...])
    o_ref[...] = _layer_norm(DN_ALPHA * x_ref[...] + y, lg_ref[...], lb_ref[...])


def _combine(yg, packed, x1, lg, lb):
    T, D = x1.shape
    tm = min(512, T)
    nt = T // tm
    full = lambda shp: pl.BlockSpec(shp, lambda i: (0,) * len(shp))
    choice = lambda kk: pl.BlockSpec((tm, D // 2), lambda i: (kk * nt + i, 0))
    return pl.pallas_call(
        _combine_kernel,
        out_shape=jax.ShapeDtypeStruct((T, D), f32),
        grid=(nt,),
        in_specs=[choice(0), choice(1), choice(2), choice(3), pl.BlockSpec((tm, LANES), lambda i: (i, 0)),
                  pl.BlockSpec((tm, D), lambda i: (i, 0)), full((1, D)), full((1, D))],
        out_specs=pl.BlockSpec((tm, D), lambda i: (i, 0)),
        compiler_params=_cparams("parallel"),
        name="combine_ln",
    )(yg, yg, yg, yg, packed, x1, lg.reshape(1, D), lb.reshape(1, D))


def _sc_mesh():
    return plsc.VectorSubcoreMesh(core_axis_name="c", subcore_axis_name="s")


def _sc_dispatch(x, posk, n_slots):
    t_tokens, w = x.shape

    @functools.partial(pl.kernel, out_type=jax.ShapeDtypeStruct((n_slots, w), x.dtype), mesh=_sc_mesh(),
                       name="sc_dispatch")
    def k(x_hbm, p_hbm, o_hbm):
        def body(x_vmem, p_vmem):
            for kk in range(TOP_K):
                pltpu.sync_copy(x_vmem, o_hbm.at[p_vmem.at[kk]])

        pltpu.emit_pipeline(
            body,
            grid=(t_tokens // SC_ROWS,),
            in_specs=[pl.BlockSpec((SC_ROWS, w), lambda i: (i, 0), pipeline_mode=pl.Buffered(1)),
                      pl.BlockSpec((TOP_K, SC_ROWS), lambda i: (0, i))],
            out_specs=[],
            core_axis_name=("c", "s"),
            dimension_semantics=(pltpu.PARALLEL,),
        )(x_hbm, p_hbm)

    return k(x, posk)


def _sc_gather(table, idx):
    n = idx.shape[0]
    w = table.shape[1]

    @functools.partial(pl.kernel, out_type=jax.ShapeDtypeStruct((n, w), table.dtype), mesh=_sc_mesh(),
                       name="sc_gather")
    def k(t_hbm, i_hbm, o_hbm):
        def body(i_vmem, o_vmem):
            pltpu.sync_copy(t_hbm.at[i_vmem.at[0]], o_vmem)

        pltpu.emit_pipeline(
            body,
            grid=(n // SC_ROWS,),
            in_specs=[pl.BlockSpec((1, SC_ROWS), lambda i: (0, i))],
            out_specs=[pl.BlockSpec((SC_ROWS, w), lambda i: (i, 0), pipeline_mode=pl.Buffered(1))],
            core_axis_name=("c", "s"),
            dimension_semantics=(pltpu.PARALLEL,),
        )(i_hbm, o_hbm)

    return k(table, idx.reshape(1, n))


def _moe(x1, x1p, packed, cnt, layer, w1, b1, w2, b2, lg, lb):
    T, D = x1.shape
    A = T * TOP_K
    idx = packed[:, PK_IDX:PK_IDX + TOP_K].astype(jnp.int32)
    rank = packed[:, PK_RANK:PK_RANK + TOP_K].astype(jnp.int32)
    counts = cnt[0, :N_EXPERTS].astype(jnp.int32)
    padded = ((counts + MOE_BLOCK - 1) // MOE_BLOCK) * MOE_BLOCK
    pad_end = jnp.cumsum(padded)
    pad_start = pad_end - padded
    posk = (pad_start[idx] + rank).T
    n_blocks = -(-A // MOE_BLOCK) + N_EXPERTS
    P = n_blocks * MOE_BLOCK
    starts = jnp.arange(n_blocks, dtype=jnp.int32) * MOE_BLOCK
    block_e = jnp.minimum(jnp.sum((pad_end[None, :] <= starts[:, None]).astype(jnp.int32), axis=1), N_EXPERTS - 1)
    n_used = (pad_end[-1:] // MOE_BLOCK).astype(jnp.int32)
    xs = _sc_dispatch(x1p, posk, P)
    ys = _experts(xs, block_e, n_used, layer, w1, b1, w2, b2)
    yg = _sc_gather(ys, posk.reshape(A))
    return _combine(yg, packed, x1, lg, lb)


def _layer(x2d, B, S, p, layer, stacked):
    w = _split_w_in(p['w_in'])
    ha = _lru(x2d, w['lru'], B, S, p['lru_conv_w'], p['lru_conv_b'], p['lru_wa'], p['lru_ba'], p['lru_wx'],
              p['lru_bx'], p['lru_lambda'])
    hb = _mlstm(x2d, w['ml'], B, S, p['ml_i_bias'], p['ml_f_bias'], p['ml_norm_g'])
    hc = _gdn(x2d, w['gd'], B, S, p['gd_conv_w'], p['gd_a_log'], p['gd_dt_bias'], p['gd_norm_g'])
    x1, x1p, packed, cnt = _merge(ha, hb, hc, x2d, w['gate'], p['w_br_lru'], p['w_br_ml'], p['w_br_gd'],
                                  p['w_out'], p['ln1_g'], p['ln1_b'], p['router_w'], p['router_b'])
    return _moe(x1, x1p, packed, cnt, layer, stacked['exp_w1'], stacked['exp_b1'],
                stacked['exp_w2'], stacked['exp_b2'], p['ln2_g'], p['ln2_b'])


def kernel(x, w_in, lru_conv_w, lru_conv_b, lru_wa, lru_ba, lru_wx, lru_bx, lru_lambda, ml_i_bias, ml_f_bias, ml_norm_g, gd_conv_w, gd_a_log, gd_dt_bias, gd_norm_g, w_br_lru, w_br_ml, w_br_gd, w_out, ln1_g, ln1_b, router_w, router_b, exp_w1, exp_b1, exp_w2, exp_b2, ln2_g, ln2_b):
    B, S, D = x.shape
    params = dict(w_in=w_in, lru_conv_w=lru_conv_w, lru_conv_b=lru_conv_b, lru_wa=lru_wa, lru_ba=lru_ba,
                  lru_wx=lru_wx, lru_bx=lru_bx, lru_lambda=lru_lambda, ml_i_bias=ml_i_bias,
                  ml_f_bias=ml_f_bias, ml_norm_g=ml_norm_g, gd_conv_w=gd_conv_w, gd_a_log=gd_a_log,
                  gd_dt_bias=gd_dt_bias, gd_norm_g=gd_norm_g, w_br_lru=w_br_lru, w_br_ml=w_br_ml,
                  w_br_gd=w_br_gd, w_out=w_out, ln1_g=ln1_g, ln1_b=ln1_b, router_w=router_w,
                  router_b=router_b, ln2_g=ln2_g, ln2_b=ln2_b)
    stacked = dict(exp_w1=exp_w1, exp_b1=exp_b1, exp_w2=exp_w2, exp_b2=exp_b2)
    h = x.reshape(B * S, D)
    for l in range(w_in.shape[0]):
        h = _layer(h, B, S, {k: v[l] for k, v in params.items()}, l, stacked)
    return h.reshape(B, S, D)
```

```python
import functools

import jax
import jax.numpy as jnp
from jax import lax
from jax.experimental import pallas as pl
from jax.experimental.pallas import tpu as pltpu
from jax.experimental.pallas import tpu_sc as plsc

f32 = jnp.float32
bf16 = jnp.bfloat16

D_MODEL = 1024
LRU_WIDTH = 1024
LRU_BLOCKS = 8
LRU_C = 8.0
CONV_WIDTH = 4
ML_HEADS = 4
ML_DQK = 64
ML_DV = 128
GD_HEADS = 8
GD_DK = 64
GD_DV = 64
N_EXPERTS = 32
TOP_K = 4
D_FF = 1024
SWIGLU_LIMIT = 7.0
SWIGLU_ALPHA = 1.702
MOE_BLOCK = 512
N_BRANCH = 3
DEPTH = 2
DN_ALPHA = (2.0 * DEPTH) ** 0.25
LN_EPS = 1e-5
RMS_EPS = 1e-6

ML_QK_W = ML_HEADS * ML_DQK
ML_V_W = ML_HEADS * ML_DV
GD_K_W = GD_HEADS * GD_DK
GD_V_W = GD_HEADS * GD_DV
IN_SPLITS = (LRU_WIDTH, LRU_WIDTH, ML_QK_W, ML_QK_W, ML_V_W, ML_HEADS, ML_HEADS, ML_V_W,
             GD_K_W, GD_K_W, GD_V_W, GD_HEADS, GD_HEADS, GD_V_W, N_BRANCH * D_MODEL)

LANES = 128
SUBLANES = 8
NEG_BIG = -1e30

SM_ML_I = 0
SM_ML_F = 4
SM_GD_A = 8
SM_GD_B = 16

ML_CHUNK = 256
GD_CHUNK = 64
GD_ROWS = 256
GD_SOLVE_PASSES = 1
VMEM_LIMIT = 56 * 1024 * 1024
SC_ROWS = 128


def _cparams(*sem):
    return pltpu.CompilerParams(dimension_semantics=sem, vmem_limit_bytes=VMEM_LIMIT)


def _hdot(a, b):
    return jnp.dot(a, b, precision=lax.Precision.HIGHEST, preferred_element_type=f32)


def _bdot(a, b):
    return jnp.dot(a.astype(bf16), b.astype(bf16), preferred_element_type=f32)


def _bdot_nt(a, b):
    return lax.dot_general(a.astype(bf16), b.astype(bf16), (((1,), (1,)), ((), ())),
                           preferred_element_type=f32)


def _bdot_tn(a, b):
    return lax.dot_general(a.astype(bf16), b.astype(bf16), (((0,), (0,)), ((), ())),
                           preferred_element_type=f32)


def _split_bf16(x, passes):
    hi = x.astype(bf16)
    if passes == 1:
        return (hi,)
    return (hi, (x - hi.astype(f32)).astype(bf16))


def _mdot(xs, ys):
    out = jnp.dot(xs[0], ys[0], preferred_element_type=f32)
    if len(xs) > 1:
        out = out + jnp.dot(xs[1], ys[0], preferred_element_type=f32)
        out = out + jnp.dot(xs[0], ys[1], preferred_element_type=f32)
    return out


def _sigmoid(x):
    return 1.0 / (1.0 + jnp.exp(-x))


def _softplus(x):
    return jnp.maximum(x, 0.0) + jnp.log1p(jnp.exp(-jnp.abs(x)))


def _log_sigmoid(x):
    return jnp.minimum(x, 0.0) - jnp.log1p(jnp.exp(-jnp.abs(x)))


def _pack_halves(v):
    n = v.shape[1] // 2
    hi = lax.bitcast_convert_type(v[:, :n].astype(bf16).astype(f32), jnp.uint32)
    lo = lax.bitcast_convert_type(v[:, n:].astype(bf16).astype(f32), jnp.uint32)
    return hi | (lo >> 16)


def _unpack_halves(p):
    hi = lax.bitcast_convert_type(p & jnp.uint32(0xFFFF0000), f32)
    lo = lax.bitcast_convert_type(p << 16, f32)
    return jnp.concatenate([hi, lo], axis=1)


def _layer_norm(z, g, b):
    mu = jnp.mean(z, axis=-1, keepdims=True)
    zc = z - mu
    var = jnp.mean(zc * zc, axis=-1, keepdims=True)
    return zc * lax.rsqrt(var + LN_EPS) * g + b


def _split_w_in(w):
    pts = []
    acc = 0
    for s in IN_SPLITS[:-1]:
        acc += s
        pts.append(acc)
    (lru_x, lru_y, ml_q, ml_k, ml_v, ml_i, ml_f, ml_o,
     gd_q, gd_k, gd_v, gd_a, gd_b, gd_z, gate) = jnp.split(w, pts, axis=1)
    pad = jnp.zeros((w.shape[0], LANES - 2 * ML_HEADS - 2 * GD_HEADS), w.dtype)
    small = jnp.concatenate([ml_i, ml_f, gd_a, gd_b, pad], axis=1)
    cat = lambda *cols: jnp.concatenate(cols, axis=1).astype(bf16)
    return dict(lru=cat(lru_x, lru_y), ml=cat(ml_q, ml_k, ml_v, ml_o, small),
                gd=cat(gd_q, gd_k, gd_v, gd_z, small), gate=gate.astype(bf16))


def _project(x_ref, w_ref):
    return jnp.dot(x_ref[...].astype(bf16), w_ref[...], preferred_element_type=f32)


def _lru_kernel(xin_ref, xnext_ref, w_ref, cw_ref, cb_ref, wcat_ref, ba_ref, bx_ref, lam_ref, o_ref,
                xext, pr_even, pr_odd, a_scr, u_scr, carry):
    ts = xin_ref.shape[0]
    W = a_scr.shape[1]
    bw = W // LRU_BLOCKS
    step = pl.program_id(0) * pl.num_programs(1) + pl.program_id(1)

    @pl.when(step == 0)
    def _():
        pr_even[...] = _project(xin_ref, w_ref)

    @pl.when(pl.program_id(1) == 0)
    def _():
        xext[0:SUBLANES, :] = jnp.zeros((SUBLANES, W), f32)
        carry[...] = jnp.zeros((1, W), f32)

    def tile(cur, nxt):
        x = cur[:, 0:W]
        xext[SUBLANES:SUBLANES + ts, :] = x
        xa = (cw_ref[3:4, :] * x
              + cw_ref[2:3, :] * xext[SUBLANES - 1:SUBLANES - 1 + ts, :]
              + cw_ref[1:2, :] * xext[SUBLANES - 2:SUBLANES - 2 + ts, :]
              + cw_ref[0:1, :] * xext[SUBLANES - 3:SUBLANES - 3 + ts, :]) + cb_ref[...]
        xext[0:SUBLANES, :] = xext[ts:ts + SUBLANES, :]

        xnb = xnext_ref[...].astype(bf16)
        pw = 2 * W // LRU_BLOCKS
        cdec = -LRU_C * _softplus(-lam_ref[...])
        for h in range(LRU_BLOCKS):
            sl = slice(bw * h, bw * (h + 1))
            psl = slice(pw * h, pw * (h + 1))
            nxt[:, psl] = jnp.dot(xnb, w_ref[:, psl], preferred_element_type=f32)
            xh = xa[:, sl]
            g = jnp.dot(xh.astype(bf16), wcat_ref[h], preferred_element_type=f32)
            r = _sigmoid(g[:, :bw] + ba_ref[:, sl])
            ig = _sigmoid(g[:, bw:] + bx_ref[:, sl])
            log_a = r * cdec[:, sl]
            a = jnp.exp(log_a)
            a_scr[:, sl] = a
            u_scr[:, sl] = jnp.sqrt(jnp.tanh(-log_a) * (1.0 + a * a)) * (ig * xh)

        row = lax.broadcasted_iota(jnp.int32, (SUBLANES, W), 0)

        def body(g, cr):
            off = pl.multiple_of(g * SUBLANES, SUBLANES)
            A = a_scr[pl.ds(off, SUBLANES), :]
            U = u_scr[pl.ds(off, SUBLANES), :]
            for s in (1, 2, 4):
                a_sh = pltpu.roll(A, s, 0)
                u_sh = pltpu.roll(U, s, 0)
                m = row >= s
                U = jnp.where(m, A * u_sh + U, U)
                A = jnp.where(m, A * a_sh, A)
            H = A * cr + U
            u_scr[pl.ds(off, SUBLANES), :] = H
            return H[SUBLANES - 1:SUBLANES, :]

        carry[...] = lax.fori_loop(0, ts // SUBLANES, body, carry[...], unroll=4)
        o_ref[...] = (u_scr[...] * jax.nn.gelu(cur[:, W:2 * W])).astype(o_ref.dtype)

    @pl.when(lax.rem(step, 2) == 0)
    def _():
        tile(pr_even, pr_odd)

    @pl.when(lax.rem(step, 2) == 1)
    def _():
        tile(pr_odd, pr_even)


def _lru(x2d, w_lru, B, S, cw, cb, wa, ba, wx, bx, lam):
    T = B * S
    ts = min(512, S)
    nt = S // ts
    W = LRU_WIDTH
    wcat = jnp.concatenate([wa, wx], axis=-1).astype(bf16)
    row = lambda v: v.reshape(1, W)
    full = lambda shp: pl.BlockSpec(shp, lambda b, c: (0,) * len(shp))
    last = B * nt - 1
    return pl.pallas_call(
        _lru_kernel,
        out_shape=jax.ShapeDtypeStruct((T, W), bf16),
        grid=(B, nt),
        in_specs=[pl.BlockSpec((ts, D_MODEL), lambda b, c: (b * nt + c, 0)),
                  pl.BlockSpec((ts, D_MODEL), lambda b, c: (jnp.minimum(b * nt + c + 1, last), 0)),
                  full(w_lru.shape), full((CONV_WIDTH, W)), full((1, W)), full(wcat.shape),
                  full((1, W)), full((1, W)), full((1, W))],
        out_specs=pl.BlockSpec((ts, W), lambda b, c: (b * nt + c, 0)),
        scratch_shapes=[pltpu.VMEM((ts + SUBLANES, W), f32), pltpu.VMEM((ts, 2 * W), f32),
                        pltpu.VMEM((ts, 2 * W), f32), pltpu.VMEM((ts, W), f32), pltpu.VMEM((ts, W), f32),
                        pltpu.VMEM((1, W), f32)],
        compiler_params=_cparams("arbitrary", "arbitrary"),
        name="rg_lru",
    )(x2d, x2d, w_lru, cw, row(cb), wcat, row(ba), row(bx), row(lam))


def _ml_kernel(xin_ref, w_ref, bias_ref, g_ref, o_ref, c_scr, n_scr, m_scr):
    L = xin_ref.shape[0]

    @pl.when(pl.program_id(1) == 0)
    def _():
        c_scr[...] = jnp.zeros(c_scr.shape, f32)
        n_scr[...] = jnp.zeros(n_scr.shape, f32)
        m_scr[...] = jnp.zeros(m_scr.shape, f32)

    pr = _project(xin_ref, w_ref)
    q = pr[:, 0:ML_QK_W]
    k = pr[:, ML_QK_W:2 * ML_QK_W]
    v = pr[:, 2 * ML_QK_W:2 * ML_QK_W + ML_V_W]
    og = pr[:, 2 * ML_QK_W + ML_V_W:2 * ML_QK_W + 2 * ML_V_W]
    sm = pr[:, 2 * ML_QK_W + 2 * ML_V_W:] + bias_ref[...]
    logf = _log_sigmoid(sm)
    ri = lax.broadcasted_iota(jnp.int32, (L, L), 0)
    ci = lax.broadcasted_iota(jnp.int32, (L, L), 1)
    tri = ri >= ci
    b_all = _hdot(tri.astype(f32), logf)
    sm_t = sm.T
    b_t = b_all.T
    kb = k.astype(bf16)
    lane_head = lax.broadcasted_iota(jnp.int32, q.shape, 1) // ML_DQK
    scale = ML_DQK ** -0.5
    heads = range(ML_HEADS)
    sls = [slice(ML_DV * h, ML_DV * (h + 1)) for h in heads]
    m_news, decays, w_cs, m_ts, dws, iws = [], [], [], [], [], []
    for h in heads:
        ig_c = sm[:, SM_ML_I + h:SM_ML_I + h + 1]
        b_c = b_all[:, SM_ML_F + h:SM_ML_F + h + 1]
        ig_r = sm_t[SM_ML_I + h:SM_ML_I + h + 1, :]
        b_r = b_t[SM_ML_F + h:SM_ML_F + h + 1, :]
        b_last = b_c[L - 1:L, :]
        m_prev = m_scr[h:h + 1, 0:1]
        ge_r = b_last - b_r + ig_r
        ge_c = b_last - b_c + ig_c
        m_new = jnp.maximum(b_last + m_prev, jnp.max(ge_r, axis=1, keepdims=True))
        dmat = jnp.where(tri, b_c - b_r + ig_r, NEG_BIG)
        inter = b_c + m_prev
        m_t = jnp.maximum(inter, jnp.max(dmat, axis=1, keepdims=True))
        m_news.append(m_new)
        decays.append(jnp.exp(b_last + m_prev - m_new))
        w_cs.append(jnp.exp(ge_c - m_new))
        m_ts.append(m_t)
        dws.append(jnp.exp(dmat - m_t))
        iws.append(jnp.exp(inter - m_t))
    qms = [jnp.where(lane_head == h, q, 0.0) * scale for h in heads]
    qmbs = [qm.astype(bf16) for qm in qms]
    ss = [_bdot_nt(qmbs[h], kb) * dws[h] for h in heads]
    n_rows = [n_scr[h:h + 1, :] for h in heads]
    nums = [_bdot(ss[h], v[:, sls[h]]) + iws[h] * _bdot(qmbs[h], c_scr[:, sls[h]]) for h in heads]
    dens = [jnp.sum(ss[h], axis=1, keepdims=True) + iws[h] * jnp.sum(qms[h] * n_rows[h], axis=1, keepdims=True)
            for h in heads]
    hhs = [nums[h] / jnp.maximum(jnp.abs(dens[h]), jnp.exp(-m_ts[h])) for h in heads]
    for h in heads:
        c_scr[:, sls[h]] = decays[h] * c_scr[:, sls[h]] + _bdot_tn(kb, w_cs[h] * v[:, sls[h]])
        n_scr[h:h + 1, :] = decays[h] * n_rows[h] + jnp.sum(w_cs[h] * k, axis=0, keepdims=True)
        m_scr[h:h + 1, :] = jnp.broadcast_to(m_news[h], (1, LANES))
    for h in heads:
        ms = jnp.mean(hhs[h] * hhs[h], axis=1, keepdims=True)
        hn = hhs[h] * lax.rsqrt(ms + RMS_EPS) * g_ref[:, sls[h]]
        o_ref[:, sls[h]] = (hn * _sigmoid(og[:, sls[h]])).astype(o_ref.dtype)


def _mlstm(x2d, w_ml, B, S, i_bias, f_bias, norm_g):
    T = B * S
    L = min(ML_CHUNK, S)
    nc = S // L
    bias = jnp.zeros((1, LANES), f32)
    bias = bias.at[0, SM_ML_I:SM_ML_I + ML_HEADS].set(i_bias).at[0, SM_ML_F:SM_ML_F + ML_HEADS].set(f_bias)
    full = lambda shp: pl.BlockSpec(shp, lambda b, c: (0,) * len(shp))
    return pl.pallas_call(
        _ml_kernel,
        out_shape=jax.ShapeDtypeStruct((T, ML_V_W), bf16),
        grid=(B, nc),
        in_specs=[pl.BlockSpec((L, D_MODEL), lambda b, c: (b * nc + c, 0)), full(w_ml.shape),
                  full((1, LANES)), full((1, ML_V_W))],
        out_specs=pl.BlockSpec((L, ML_V_W), lambda b, c: (b * nc + c, 0)),
        scratch_shapes=[pltpu.VMEM((ML_QK_W, ML_V_W), f32), pltpu.VMEM((SUBLANES, ML_QK_W), f32),
                        pltpu.VMEM((SUBLANES, LANES), f32)],
        compiler_params=_cparams("parallel", "arbitrary"),
        name="mlstm",
    )(x2d, w_ml, bias, norm_g.reshape(1, ML_V_W))


def _gd_kernel(xin_ref, w_ref, cw_ref, alog_ref, dtb_ref, g_ref, o_ref, ext, s_scr):
    R = xin_ref.shape[0]
    KW = GD_K_W
    L = GD_CHUNK
    nch = R // L
    lsh = L.bit_length() - 1
    PW = nch * L
    nsl = KW // LANES
    passes = GD_SOLVE_PASSES

    @pl.when(pl.program_id(1) == 0)
    def _():
        ext[0:SUBLANES, :] = jnp.zeros((SUBLANES, ext.shape[1]), f32)
        s_scr[...] = jnp.zeros(s_scr.shape, f32)

    pr = _project(xin_ref, w_ref)
    z = pr[:, 3 * KW:4 * KW]
    sm = pr[:, 4 * KW:]
    ext[SUBLANES:SUBLANES + R, :] = pr[:, 0:3 * KW]
    xc = (cw_ref[3:4, :] * ext[SUBLANES:SUBLANES + R, :]
          + cw_ref[2:3, :] * ext[SUBLANES - 1:SUBLANES - 1 + R, :]
          + cw_ref[1:2, :] * ext[SUBLANES - 2:SUBLANES - 2 + R, :]
          + cw_ref[0:1, :] * ext[SUBLANES - 3:SUBLANES - 3 + R, :])
    ext[0:SUBLANES, :] = ext[R:R + SUBLANES, :]
    xc = xc * _sigmoid(xc)

    g_t = -jnp.exp(alog_ref[...]) * _softplus(sm + dtb_ref[...])
    beta_t = _sigmoid(sm)
    rr = lax.broadcasted_iota(jnp.int32, (R, R), 0)
    cc = lax.broadcasted_iota(jnp.int32, (R, R), 1)
    cum = jnp.where(((rr >> lsh) == (cc >> lsh)) & (rr >= cc), 1.0, 0.0)
    gc_all = _hdot(cum, g_t)
    gc_t = gc_all.T

    pt = lax.broadcasted_iota(jnp.int32, (L, PW), 0)
    plane = lax.broadcasted_iota(jnp.int32, (L, PW), 1)
    ps = plane & (L - 1)
    pc = plane >> lsh
    tri_p = pt >= ps
    strict_p = pt > ps
    eye_p = jnp.where(pt == ps, 1.0, 0.0)
    blk8_p = (pt >> 3) == (ps >> 3)
    cmask_b = [jnp.where(pc == c, 1.0, 0.0).astype(bf16) for c in range(nch)]

    def pick(parts):
        out = parts[nch - 1]
        for c in range(nch - 2, -1, -1):
            out = jnp.where(pc == c, parts[c], out)
        return out

    def chunks(col):
        return [col[c * L:(c + 1) * L] for c in range(nch)]

    def bd(yb):
        return jnp.concatenate([yb * cmask_b[c] for c in range(nch)], axis=0)

    def pdot(xp, yp):
        return _mdot(_split_bf16(xp, passes), [bd(p) for p in _split_bf16(yp, passes)])

    lane_r = lax.broadcasted_iota(jnp.int32, (R, LANES), 1)
    lo_r = lane_r < GD_DK
    lane_l = lax.broadcasted_iota(jnp.int32, (L, LANES), 1)
    lo_l = lane_l < GD_DK
    lane_row = lax.broadcasted_iota(jnp.int32, (1, LANES), 1)
    r2 = lax.broadcasted_iota(jnp.int32, (LANES, LANES), 0)
    c2 = lax.broadcasted_iota(jnp.int32, (LANES, LANES), 1)
    blockdiag = (r2 < GD_DK) == (c2 < GD_DK)

    def half_sums(y, lo):
        s_lo = jnp.sum(jnp.where(lo, y, 0.0), axis=1, keepdims=True)
        s_hi = jnp.sum(jnp.where(lo, 0.0, y), axis=1, keepdims=True)
        return s_lo, s_hi

    def l2n(y):
        s_lo, s_hi = half_sums(y * y, lo_r)
        return y * jnp.where(lo_r, lax.rsqrt(s_lo + RMS_EPS), lax.rsqrt(s_hi + RMS_EPS))

    qns, kns, vss, a_ps, p_bs, gcols, bcols = [], [], [], [], [], [], []
    for j in range(nsl):
        qn = l2n(xc[:, LANES * j:LANES * (j + 1)]) * (GD_DK ** -0.5)
        kn = l2n(xc[:, KW + LANES * j:KW + LANES * (j + 1)])
        qns.append(qn)
        kns.append(kn)
        vss.append(xc[:, 2 * KW + LANES * j:2 * KW + LANES * (j + 1)])
        lhs = jnp.concatenate([jnp.where(lo_r, kn, 0.0), jnp.where(lo_r, 0.0, kn),
                               jnp.where(lo_r, qn, 0.0), jnp.where(lo_r, 0.0, qn)], axis=0)
        gram = _bdot_nt(lhs, kn)
        for e in range(2):
            h = 2 * j + e
            gcol = gc_all[:, SM_GD_A + h:SM_GD_A + h + 1]
            bcol = beta_t[:, SM_GD_B + h:SM_GD_B + h + 1]
            gc_r = gc_t[SM_GD_A + h:SM_GD_A + h + 1, :]
            gam_p = jnp.exp(jnp.where(tri_p, pick(chunks(gcol)) - gc_r, NEG_BIG))
            kk_p = pick(chunks(gram[e * R:(e + 1) * R]))
            qk_p = pick(chunks(gram[(2 + e) * R:(3 + e) * R]))
            a_ps.append(jnp.where(strict_p, pick(chunks(bcol)) * gam_p * kk_p, 0.0))
            p_bs.append((gam_p * qk_p).astype(bf16))
            gcols.append(gcol)
            bcols.append(bcol)

    ads = [jnp.where(blk8_p, a, 0.0) for a in a_ps]
    a2s = [pdot(ad, ad) for ad in ads]
    a4s = [pdot(a2, a2) for a2 in a2s]
    xs = [pdot(eye_p - ad, eye_p + a2) for ad, a2 in zip(ads, a2s)]
    xs = [pdot(x, eye_p + a4) for x, a4 in zip(xs, a4s)]
    for sh in range(3, lsh):
        msk = ((pt >> (sh + 1)) == (ps >> (sh + 1))) & (((pt >> sh) & 1) == 1) & (((ps >> sh) & 1) == 0)
        ts = [pdot(jnp.where(msk, a, 0.0), x) for a, x in zip(a_ps, xs)]
        xs = [x - pdot(x, t) for x, t in zip(xs, ts)]

    us, ws, qgs, kds, gls = [], [], [], [], []
    for j in range(nsl):
        uw, egs, eds, glh = [], [], [], []
        for e in range(2):
            h = 2 * j + e
            gcol, bcol = gcols[h], bcols[h]
            eg = jnp.exp(gcol)
            rhs = jnp.concatenate([bcol * vss[j], (bcol * eg) * kns[j]], axis=1)
            uw.append(_mdot([bd(p) for p in _split_bf16(xs[h], passes)], _split_bf16(rhs, passes)))
            lasts = [gcol[c * L + L - 1:c * L + L] for c in range(nch)]
            gcl = jnp.concatenate([jnp.broadcast_to(v, (L, 1)) for v in lasts], axis=0)
            egs.append(eg)
            eds.append(jnp.exp(gcl - gcol))
            glh.append([jnp.exp(v) for v in lasts])
        us.append(jnp.where(lo_r, uw[0][:, :LANES], uw[1][:, :LANES]))
        ws.append(jnp.where(lo_r, uw[0][:, LANES:], uw[1][:, LANES:]))
        qgs.append(qns[j] * jnp.where(lo_r, egs[0], egs[1]))
        kds.append(kns[j] * jnp.where(lo_r, eds[0], eds[1]))
        gls.append([jnp.where(lane_row < GD_DK, glh[0][c], glh[1][c]) for c in range(nch)])

    states = [s_scr[j] for j in range(nsl)]
    outs = [[] for _ in range(nsl)]
    for c in range(nch):
        rs = slice(c * L, (c + 1) * L)
        for j in range(nsl):
            s_prev = states[j]
            sb = s_prev.astype(bf16)
            delta = us[j][rs] - _bdot(ws[j][rs], sb)
            db = delta.astype(bf16)
            pieces = []
            if c > 0:
                pieces.append(jnp.zeros((c * L, LANES), bf16))
            pieces.append(db)
            if c < nch - 1:
                pieces.append(jnp.zeros(((nch - 1 - c) * L, LANES), bf16))
            dpad = jnp.concatenate(pieces, axis=0) if len(pieces) > 1 else db
            intra = jnp.where(lo_l, jnp.dot(p_bs[2 * j], dpad, preferred_element_type=f32),
                              jnp.dot(p_bs[2 * j + 1], dpad, preferred_element_type=f32))
            outs[j].append(_bdot(qgs[j][rs], sb) + intra)
            upd = _bdot_tn(kds[j][rs], db)
            states[j] = gls[j][c] * s_prev + jnp.where(blockdiag, upd, 0.0)

    for j in range(nsl):
        sl = slice(LANES * j, LANES * (j + 1))
        s_scr[j] = states[j]
        o = jnp.concatenate(outs[j], axis=0) if nch > 1 else outs[j][0]
        m_lo, m_hi = half_sums(o * o, lo_r)
        inv = jnp.where(lo_r, lax.rsqrt(m_lo * (1.0 / GD_DV) + RMS_EPS), lax.rsqrt(m_hi * (1.0 / GD_DV) + RMS_EPS))
        zs = z[:, sl]
        o_ref[:, sl] = (o * inv * g_ref[:, sl] * (zs * _sigmoid(zs))).astype(o_ref.dtype)


def _gdn(x2d, w_gd, B, S, conv_w, a_log, dt_bias, norm_g):
    T = B * S
    L = min(GD_ROWS, S)
    nc = S // L
    KW = GD_K_W
    alog = jnp.zeros((1, LANES), f32).at[0, SM_GD_A:SM_GD_A + GD_HEADS].set(a_log)
    dtb = jnp.zeros((1, LANES), f32).at[0, SM_GD_A:SM_GD_A + GD_HEADS].set(dt_bias)
    g_row = jnp.tile(norm_g, GD_HEADS).reshape(1, GD_V_W)
    full = lambda shp: pl.BlockSpec(shp, lambda b, c: (0,) * len(shp))
    return pl.pallas_call(
        _gd_kernel,
        out_shape=jax.ShapeDtypeStruct((T, GD_V_W), bf16),
        grid=(B, nc),
        in_specs=[pl.BlockSpec((L, D_MODEL), lambda b, c: (b * nc + c, 0)), full(w_gd.shape),
                  full((CONV_WIDTH, 3 * KW)), full((1, LANES)), full((1, LANES)), full((1, GD_V_W))],
        out_specs=pl.BlockSpec((L, GD_V_W), lambda b, c: (b * nc + c, 0)),
        scratch_shapes=[pltpu.VMEM((L + SUBLANES, 3 * KW), f32),
                        pltpu.VMEM((KW // LANES, LANES, LANES), f32)],
        compiler_params=_cparams("parallel", "arbitrary"),
        name="gated_deltanet",
    )(x2d, w_gd, conv_w, alog, dtb, g_row)


def _merge_kernel(ha_ref, hb_ref, hc_ref, x_ref, wg_ref, wa_ref, wb_ref, wc_ref, wo_ref, lg_ref, lb_ref,
                  rw_ref, rb_ref, x1_ref, x1p_ref, pk_ref, cnt_ref, prev, carry):
    D = x_ref.shape[1]
    i = pl.program_id(0)

    @pl.when(i == 0)
    def _():
        prev[...] = jnp.zeros(prev.shape, f32)
        carry[...] = jnp.zeros(carry.shape, f32)

    route = _route_stages(prev[...], jnp.where(i > 0, 1.0, 0.0), rw_ref, rb_ref, pk_ref, cnt_ref, carry)
    xb = x_ref[...].astype(bf16)

    next(route)
    g0 = jnp.dot(xb, wg_ref[:, 0:D], preferred_element_type=f32)
    next(route)
    g1 = jnp.dot(xb, wg_ref[:, D:2 * D], preferred_element_type=f32)
    next(route)
    g2 = jnp.dot(xb, wg_ref[:, 2 * D:3 * D], preferred_element_type=f32)
    next(route)
    bra = jnp.dot(ha_ref[...], wa_ref[...], preferred_element_type=f32)
    next(route)
    brb = jnp.dot(hb_ref[...], wb_ref[...], preferred_element_type=f32)
    brc = jnp.dot(hc_ref[...], wc_ref[...], preferred_element_type=f32)
    next(route)
    merged = _sigmoid(g0) * bra + _sigmoid(g1) * brb + _sigmoid(g2) * brc
    y = jnp.dot(merged.astype(bf16), wo_ref[...], preferred_element_type=f32)
    for _ in route:
        pass
    out = _layer_norm(DN_ALPHA * x_ref[...] + y, lg_ref[...], lb_ref[...])
    x1_ref[...] = out
    x1p_ref[...] = _pack_halves(out)
    prev[...] = out


def _merge(ha, hb, hc, x2d, w_gate, wa, wb, wc, wo, lg, lb, router_w, router_b):
    T = x2d.shape[0]
    D = D_MODEL
    tm = min(512, T)
    nt = T // tm
    rw, rb = _router_params(router_w, router_b)
    full = lambda shp: pl.BlockSpec(shp, lambda i: (0,) * len(shp))
    held = lambda shp: pl.BlockSpec(shp, lambda i: (0,) * len(shp), pipeline_mode=pl.Buffered(1))
    rows = lambda w: pl.BlockSpec((tm, w), lambda i: (jnp.minimum(i, nt - 1), 0))
    routed = pl.BlockSpec((tm, LANES), lambda i: (jnp.maximum(i - 1, 0), 0))
    return pl.pallas_call(
        _merge_kernel,
        out_shape=(jax.ShapeDtypeStruct((T, D), f32), jax.ShapeDtypeStruct((T, D // 2), jnp.uint32),
                   jax.ShapeDtypeStruct((T, LANES), f32), jax.ShapeDtypeStruct((1, LANES), f32)),
        grid=(nt + 1,),
        in_specs=[rows(LRU_WIDTH), rows(ML_V_W), rows(GD_V_W), rows(D), held(w_gate.shape),
                  held((LRU_WIDTH, D)), held((ML_V_W, D)), held((GD_V_W, D)), held((D, D)),
                  full((1, D)), full((1, D)), held((D, LANES)), full((1, LANES))],
        out_specs=(rows(D), rows(D // 2), routed, full((1, LANES))),
        scratch_shapes=[pltpu.VMEM((tm, D), f32), pltpu.VMEM((1, LANES), f32)],
        compiler_params=_cparams("arbitrary"),
        name="merge_outproj_ln_route",
    )(ha, hb, hc, x2d, w_gate, wa.astype(bf16), wb.astype(bf16), wc.astype(bf16),
      wo.astype(bf16), lg.reshape(1, D), lb.reshape(1, D), rw, rb)


PK_IDX = 0
PK_GATE = 4
PK_RANK = 8


def _route_stages(x, weight, rw_ref, rb_ref, pk_ref, cnt_ref, carry):
    tr = x.shape[0]
    logits = _mdot(_split_bf16(x, 3), _split_bf16(rw_ref[...], 3)) + rb_ref[...]
    lane = lax.broadcasted_iota(jnp.int32, (tr, LANES), 1).astype(f32)
    vals = logits
    idxs, tops = [], []
    yield
    for _ in range(TOP_K):
        m = jnp.max(vals, axis=1, keepdims=True)
        idx = jnp.min(jnp.where(vals == m, lane, float(LANES)), axis=1, keepdims=True)
        idxs.append(idx)
        tops.append(m)
        vals = jnp.where(lane == idx, NEG_BIG * 2.0, vals)
        yield
    es = [jnp.exp(t - tops[0]) for t in tops]
    tot = es[0] + es[1] + es[2] + es[3]
    onehots = [lane == idx for idx in idxs]
    sel = jnp.zeros((tr, LANES), f32)
    for oh in onehots:
        sel = sel + oh.astype(f32)
    ri = lax.broadcasted_iota(jnp.int32, (tr, tr), 0)
    ci = lax.broadcasted_iota(jnp.int32, (tr, tr), 1)
    before = jnp.dot((ri > ci).astype(bf16), sel.astype(bf16), preferred_element_type=f32) + carry[...]
    carry[...] = carry[...] + weight * jnp.sum(sel, axis=0, keepdims=True)
    cnt_ref[...] = carry[...]
    yield
    packed = jnp.zeros((tr, LANES), f32)
    for kk in range(TOP_K):
        rank = jnp.sum(jnp.where(onehots[kk], before, 0.0), axis=1, keepdims=True)
        packed = jnp.where(lane == float(PK_IDX + kk), idxs[kk], packed)
        packed = jnp.where(lane == float(PK_GATE + kk), es[kk] / tot, packed)
        packed = jnp.where(lane == float(PK_RANK + kk), rank, packed)
    pk_ref[...] = packed


def _router_params(router_w, router_b):
    rw = jnp.zeros((D_MODEL, LANES), f32).at[:, :N_EXPERTS].set(router_w)
    rb = jnp.full((1, LANES), NEG_BIG, f32).at[0, :N_EXPERTS].set(router_b)
    return rw, rb


def _expert_kernel(be_ref, nu_ref, nv_ref, x_ref, w1_ref, b1_ref, w2_ref, b2_ref, o_ref, w1b, w2b):
    i = pl.program_id(0)
    used = i < nu_ref[0]
    new_expert = jnp.logical_or(i == 0, be_ref[i] != be_ref[jnp.maximum(i - 1, 0)])

    @pl.when(jnp.logical_and(used, new_expert))
    def _():
        w1b[...] = w1_ref[0, 0].astype(bf16)
        w2b[...] = w2_ref[0, 0].astype(bf16)

    @pl.when(used)
    def _():
        rows = lax.broadcasted_iota(jnp.int32, x_ref.shape, 0)
        xp = jnp.where(rows < nv_ref[i], x_ref[...], jnp.uint32(0))
        x = _unpack_halves(xp).astype(bf16)
        hdn = jnp.dot(x, w1b[...], preferred_element_type=f32) + b1_ref[0, 0]
        glu = jnp.minimum(hdn[:, :D_FF], SWIGLU_LIMIT)
        lin = jnp.clip(hdn[:, D_FF:], -SWIGLU_LIMIT, SWIGLU_LIMIT)
        act = glu * _sigmoid(SWIGLU_ALPHA * glu) * (lin + 1.0)
        y = jnp.dot(act.astype(bf16), w2b[...], preferred_element_type=f32) + b2_ref[0, 0]
        o_ref[...] = _pack_halves(y)

    @pl.when(i >= nu_ref[0])
    def _():
        o_ref[...] = jnp.zeros(o_ref.shape, o_ref.dtype)


def _experts(xs, block_e, n_used, n_valid, layer, w1, b1, w2, b2):
    P = xs.shape[0]
    nb = P // MOE_BLOCK
    D = D_MODEL
    nl = w1.shape[0]
    grid_spec = pltpu.PrefetchScalarGridSpec(
        num_scalar_prefetch=3,
        grid=(nb,),
        in_specs=[pl.BlockSpec((MOE_BLOCK, D // 2), lambda i, be, nu, nv: (i, 0)),
                  pl.BlockSpec((1, 1, D, 2 * D_FF), lambda i, be, nu, nv: (layer, be[i], 0, 0)),
                  pl.BlockSpec((1, 1, 1, 2 * D_FF), lambda i, be, nu, nv: (layer, be[i], 0, 0)),
                  pl.BlockSpec((1, 1, D_FF, D), lambda i, be, nu, nv: (layer, be[i], 0, 0)),
                  pl.BlockSpec((1, 1, 1, D), lambda i, be, nu, nv: (layer, be[i], 0, 0))],
        out_specs=pl.BlockSpec((MOE_BLOCK, D // 2), lambda i, be, nu, nv: (i, 0)),
        scratch_shapes=[pltpu.VMEM((D, 2 * D_FF), bf16), pltpu.VMEM((D_FF, D), bf16)],
    )
    return pl.pallas_call(
        _expert_kernel,
        out_shape=jax.ShapeDtypeStruct((P, D // 2), jnp.uint32),
        grid_spec=grid_spec,
        compiler_params=_cparams("arbitrary"),
        name="experts",
    )(block_e, n_used, n_valid, xs, w1, b1.reshape(nl, N_EXPERTS, 1, 2 * D_FF), w2,
      b2.reshape(nl, N_EXPERTS, 1, D))


def _combine_kernel(y0_ref, y1_ref, y2_ref, y3_ref, pk_ref, x_ref, lg_ref, lb_ref, o_ref):
    pk = pk_ref[...]
    y = pk[:, PK_GATE:PK_GATE + 1] * _unpack_halves(y0_ref[...])
    for kk, y_ref in ((1, y1_ref), (2, y2_ref), (3, y3_ref)):
        y = y + pk[:, PK_GATE + kk:PK_GATE + kk + 1] * _unpack_halves(y_ref[...])
    o_ref[...] = _layer_norm(DN_ALPHA * x_ref[...] + y, lg_ref[...], lb_ref[...])


def _combine(yg, packed, x1, lg, lb):
    T, D = x1.shape
    tm = min(1024, T)
    nt = T // tm
    full = lambda shp: pl.BlockSpec(shp, lambda i: (0,) * len(shp))
    choice = lambda kk: pl.BlockSpec((tm, D // 2), lambda i: (kk * nt + i, 0))
    return pl.pallas_call(
        _combine_kernel,
        out_shape=jax.ShapeDtypeStruct((T, D), f32),
        grid=(nt,),
        in_specs=[choice(0), choice(1), choice(2), choice(3), pl.BlockSpec((tm, LANES), lambda i: (i, 0)),
                  pl.BlockSpec((tm, D), lambda i: (i, 0)), full((1, D)), full((1, D))],
        out_specs=pl.BlockSpec((tm, D), lambda i: (i, 0)),
        compiler_params=_cparams("parallel"),
        name="combine_ln",
    )(yg, yg, yg, yg, packed, x1, lg.reshape(1, D), lb.reshape(1, D))


def _sc_mesh():
    return plsc.VectorSubcoreMesh(core_axis_name="c", subcore_axis_name="s")


def _sc_dispatch(x, posk, n_slots):
    t_tokens, w = x.shape

    @functools.partial(pl.kernel, out_type=jax.ShapeDtypeStruct((n_slots, w), x.dtype), mesh=_sc_mesh(),
                       name="sc_dispatch")
    def k(x_hbm, p_hbm, o_hbm):
        def body(x_vmem, p_vmem):
            for kk in range(TOP_K):
                pltpu.sync_copy(x_vmem, o_hbm.at[p_vmem.at[kk]])

        pltpu.emit_pipeline(
            body,
            grid=(t_tokens // SC_ROWS,),
            in_specs=[pl.BlockSpec((SC_ROWS, w), lambda i: (i, 0), pipeline_mode=pl.Buffered(1)),
                      pl.BlockSpec((TOP_K, SC_ROWS), lambda i: (0, i))],
            out_specs=[],
            core_axis_name=("c", "s"),
            dimension_semantics=(pltpu.PARALLEL,),
        )(x_hbm, p_hbm)

    return k(x, posk)


def _sc_gather(table, idx):
    n = idx.shape[0]
    w = table.shape[1]

    @functools.partial(pl.kernel, out_type=jax.ShapeDtypeStruct((n, w), table.dtype), mesh=_sc_mesh(),
                       name="sc_gather")
    def k(t_hbm, i_hbm, o_hbm):
        def body(i_vmem, o_vmem):
            pltpu.sync_copy(t_hbm.at[i_vmem.at[0]], o_vmem)

        pltpu.emit_pipeline(
            body,
            grid=(n // SC_ROWS,),
            in_specs=[pl.BlockSpec((1, SC_ROWS), lambda i: (0, i))],
            out_specs=[pl.BlockSpec((SC_ROWS, w), lambda i: (i, 0), pipeline_mode=pl.Buffered(1))],
            core_axis_name=("c", "s"),
            dimension_semantics=(pltpu.PARALLEL,),
        )(i_hbm, o_hbm)

    return k(table, idx.reshape(1, n))


def _moe(x1, x1p, packed, cnt, layer, w1, b1, w2, b2, lg, lb):
    T, D = x1.shape
    A = T * TOP_K
    idx = packed[:, PK_IDX:PK_IDX + TOP_K].astype(jnp.int32)
    rank = packed[:, PK_RANK:PK_RANK + TOP_K].astype(jnp.int32)
    counts = cnt[0, :N_EXPERTS].astype(jnp.int32)
    padded = ((counts + MOE_BLOCK - 1) // MOE_BLOCK) * MOE_BLOCK
    pad_end = jnp.cumsum(padded)
    pad_start = pad_end - padded
    posk = (pad_start[idx] + rank).T
    n_blocks = -(-A // MOE_BLOCK) + N_EXPERTS
    P = n_blocks * MOE_BLOCK
    starts = jnp.arange(n_blocks, dtype=jnp.int32) * MOE_BLOCK
    block_e = jnp.minimum(jnp.sum((pad_end[None, :] <= starts[:, None]).astype(jnp.int32), axis=1), N_EXPERTS - 1)
    n_used = (pad_end[-1:] // MOE_BLOCK).astype(jnp.int32)
    n_valid = jnp.clip(pad_start[block_e] + counts[block_e] - starts, 0, MOE_BLOCK).astype(jnp.int32)
    xs = _sc_dispatch(x1p, posk, P)
    ys = _experts(xs, block_e, n_used, n_valid, layer, w1, b1, w2, b2)
    yg = _sc_gather(ys, posk.reshape(A))
    return _combine(yg, packed, x1, lg, lb)


def _layer(x2d, B, S, p, layer, stacked):
    w = _split_w_in(p['w_in'])
    ha = _lru(x2d, w['lru'], B, S, p['lru_conv_w'], p['lru_conv_b'], p['lru_wa'], p['lru_ba'], p['lru_wx'],
              p['lru_bx'], p['lru_lambda'])
    hb = _mlstm(x2d, w['ml'], B, S, p['ml_i_bias'], p['ml_f_bias'], p['ml_norm_g'])
    hc = _gdn(x2d, w['gd'], B, S, p['gd_conv_w'], p['gd_a_log'], p['gd_dt_bias'], p['gd_norm_g'])
    x1, x1p, packed, cnt = _merge(ha, hb, hc, x2d, w['gate'], p['w_br_lru'], p['w_br_ml'], p['w_br_gd'],
                                  p['w_out'], p['ln1_g'], p['ln1_b'], p['router_w'], p['router_b'])
    return _moe(x1, x1p, packed, cnt, layer, stacked['exp_w1'], stacked['exp_b1'],
                stacked['exp_w2'], stacked['exp_b2'], p['ln2_g'], p['ln2_b'])


def kernel(x, w_in, lru_conv_w, lru_conv_b, lru_wa, lru_ba, lru_wx, lru_bx, lru_lambda, ml_i_bias, ml_f_bias, ml_norm_g, gd_conv_w, gd_a_log, gd_dt_bias, gd_norm_g, w_br_lru, w_br_ml, w_br_gd, w_out, ln1_g, ln1_b, router_w, router_b, exp_w1, exp_b1, exp_w2, exp_b2, ln2_g, ln2_b):
    B, S, D = x.shape
    params = dict(w_in=w_in, lru_conv_w=lru_conv_w, lru_conv_b=lru_conv_b, lru_wa=lru_wa, lru_ba=lru_ba,
                  lru_wx=lru_wx, lru_bx=lru_bx, lru_lambda=lru_lambda, ml_i_bias=ml_i_bias,
                  ml_f_bias=ml_f_bias, ml_norm_g=ml_norm_g, gd_conv_w=gd_conv_w, gd_a_log=gd_a_log,
                  gd_dt_bias=gd_dt_bias, gd_norm_g=gd_norm_g, w_br_lru=w_br_lru, w_br_ml=w_br_ml,
                  w_br_gd=w_br_gd, w_out=w_out, ln1_g=ln1_g, ln1_b=ln1_b, router_w=router_w,
                  router_b=router_b, ln2_g=ln2_g, ln2_b=ln2_b)
    stacked = dict(exp_w1=exp_w1, exp_b1=exp_b1, exp_w2=exp_w2, exp_b2=exp_b2)
    h = x.reshape(B * S, D)
    for l in range(w_in.shape[0]):
        h = _layer(h, B, S, {k: v[l] for k, v in params.items()}, l, stacked)
    return h.reshape(B, S, D)
```

```python
import functools

import jax
import jax.numpy as jnp
from jax import lax
from jax.experimental import pallas as pl
from jax.experimental.pallas import tpu as pltpu
from jax.experimental.pallas import tpu_sc as plsc

f32 = jnp.float32
bf16 = jnp.bfloat16

D_MODEL = 1024
LRU_WIDTH = 1024
LRU_BLOCKS = 8
LRU_C = 8.0
CONV_WIDTH = 4
ML_HEADS = 4
ML_DQK = 64
ML_DV = 128
GD_HEADS = 8
GD_DK = 64
GD_DV = 64
N_EXPERTS = 32
TOP_K = 4
D_FF = 1024
SWIGLU_LIMIT = 7.0
SWIGLU_ALPHA = 1.702
MOE_BLOCK = 512
N_BRANCH = 3
DEPTH = 2
DN_ALPHA = (2.0 * DEPTH) ** 0.25
LN_EPS = 1e-5
RMS_EPS = 1e-6

ML_QK_W = ML_HEADS * ML_DQK
ML_V_W = ML_HEADS * ML_DV
GD_K_W = GD_HEADS * GD_DK
GD_V_W = GD_HEADS * GD_DV
IN_SPLITS = (LRU_WIDTH, LRU_WIDTH, ML_QK_W, ML_QK_W, ML_V_W, ML_HEADS, ML_HEADS, ML_V_W,
             GD_K_W, GD_K_W, GD_V_W, GD_HEADS, GD_HEADS, GD_V_W, N_BRANCH * D_MODEL)

LANES = 128
SUBLANES = 8
NEG_BIG = -1e30

SM_ML_I = 0
SM_ML_F = 4
SM_GD_A = 8
SM_GD_B = 16

ML_CHUNK = 256
GD_CHUNK = 64
GD_ROWS = 256
GD_SOLVE_PASSES = 1
VMEM_LIMIT = 56 * 1024 * 1024
SC_ROWS = 128


def _cparams(*sem):
    return pltpu.CompilerParams(dimension_semantics=sem, vmem_limit_bytes=VMEM_LIMIT)


def _cummax_rows(x, seg):
    pos = lax.broadcasted_iota(jnp.int32, x.shape, 0) & (seg - 1)
    step = 1
    while step < seg:
        x = jnp.maximum(x, jnp.where(pos >= step, pltpu.roll(x, step, 0), NEG_BIG))
        step *= 2
    return x


def _cumsum_rows(x, seg):
    pos = lax.broadcasted_iota(jnp.int32, x.shape, 0) & (seg - 1)
    step = 1
    while step < seg:
        x = x + jnp.where(pos >= step, pltpu.roll(x, step, 0), 0.0)
        step *= 2
    return x


def _bdot(a, b):
    return jnp.dot(a.astype(bf16), b.astype(bf16), preferred_element_type=f32)


def _bdot_nt(a, b):
    return lax.dot_general(a.astype(bf16), b.astype(bf16), (((1,), (1,)), ((), ())),
                           preferred_element_type=f32)


def _bdot_tn(a, b):
    return lax.dot_general(a.astype(bf16), b.astype(bf16), (((0,), (0,)), ((), ())),
                           preferred_element_type=f32)


def _split_bf16(x, passes):
    hi = x.astype(bf16)
    if passes == 1:
        return (hi,)
    return (hi, (x - hi.astype(f32)).astype(bf16))


def _mdot(xs, ys):
    out = jnp.dot(xs[0], ys[0], preferred_element_type=f32)
    if len(xs) > 1:
        out = out + jnp.dot(xs[1], ys[0], preferred_element_type=f32)
        out = out + jnp.dot(xs[0], ys[1], preferred_element_type=f32)
    return out


def _sigmoid(x):
    return 1.0 / (1.0 + jnp.exp(-x))


def _softplus(x):
    return jnp.maximum(x, 0.0) + jnp.log1p(jnp.exp(-jnp.abs(x)))


def _log_sigmoid(x):
    return jnp.minimum(x, 0.0) - jnp.log1p(jnp.exp(-jnp.abs(x)))


def _pack_halves(v):
    n = v.shape[1] // 2
    hi = lax.bitcast_convert_type(v[:, :n].astype(bf16).astype(f32), jnp.uint32)
    lo = lax.bitcast_convert_type(v[:, n:].astype(bf16).astype(f32), jnp.uint32)
    return hi | (lo >> 16)


def _unpack_halves(p):
    hi = lax.bitcast_convert_type(p & jnp.uint32(0xFFFF0000), f32)
    lo = lax.bitcast_convert_type(p << 16, f32)
    return jnp.concatenate([hi, lo], axis=1)


def _layer_norm(z, g, b):
    mu = jnp.mean(z, axis=-1, keepdims=True)
    zc = z - mu
    var = jnp.mean(zc * zc, axis=-1, keepdims=True)
    return zc * lax.rsqrt(var + LN_EPS) * g + b


def _split_w_in(w):
    pts = []
    acc = 0
    for s in IN_SPLITS[:-1]:
        acc += s
        pts.append(acc)
    (lru_x, lru_y, ml_q, ml_k, ml_v, ml_i, ml_f, ml_o,
     gd_q, gd_k, gd_v, gd_a, gd_b, gd_z, gate) = jnp.split(w, pts, axis=1)
    pad = jnp.zeros((w.shape[0], LANES - 2 * ML_HEADS - 2 * GD_HEADS), w.dtype)
    small = jnp.concatenate([ml_i, ml_f, gd_a, gd_b, pad], axis=1)
    cat = lambda *cols: jnp.concatenate(cols, axis=1).astype(bf16)
    return dict(lru=cat(lru_x, lru_y), ml=cat(ml_q, ml_k, ml_v, ml_o, small),
                gd=cat(gd_q, gd_k, gd_v, gd_z, small), gate=gate.astype(bf16))


def _project(x_ref, w_ref):
    return jnp.dot(x_ref[...].astype(bf16), w_ref[...], preferred_element_type=f32)


def _lru_kernel(xin_ref, xnext_ref, w_ref, cw_ref, cb_ref, wcat_ref, ba_ref, bx_ref, lam_ref, o_ref,
                xext, pr_even, pr_odd, a_scr, u_scr, carry):
    ts = xin_ref.shape[0]
    W = a_scr.shape[1]
    bw = W // LRU_BLOCKS
    step = pl.program_id(0) * pl.num_programs(1) + pl.program_id(1)

    @pl.when(step == 0)
    def _():
        pr_even[...] = _project(xin_ref, w_ref)

    @pl.when(pl.program_id(1) == 0)
    def _():
        xext[0:SUBLANES, :] = jnp.zeros((SUBLANES, W), f32)
        carry[...] = jnp.zeros((1, W), f32)

    def tile(cur, nxt):
        x = cur[:, 0:W]
        xext[SUBLANES:SUBLANES + ts, :] = x
        xa = (cw_ref[3:4, :] * x
              + cw_ref[2:3, :] * xext[SUBLANES - 1:SUBLANES - 1 + ts, :]
              + cw_ref[1:2, :] * xext[SUBLANES - 2:SUBLANES - 2 + ts, :]
              + cw_ref[0:1, :] * xext[SUBLANES - 3:SUBLANES - 3 + ts, :]) + cb_ref[...]
        xext[0:SUBLANES, :] = xext[ts:ts + SUBLANES, :]

        xnb = xnext_ref[...].astype(bf16)
        pw = 2 * W // LRU_BLOCKS
        cdec = -LRU_C * _softplus(-lam_ref[...])
        for h in range(LRU_BLOCKS):
            sl = slice(bw * h, bw * (h + 1))
            psl = slice(pw * h, pw * (h + 1))
            nxt[:, psl] = jnp.dot(xnb, w_ref[:, psl], preferred_element_type=f32)
            xh = xa[:, sl]
            g = jnp.dot(xh.astype(bf16), wcat_ref[h], preferred_element_type=f32)
            r = _sigmoid(g[:, :bw] + ba_ref[:, sl])
            ig = _sigmoid(g[:, bw:] + bx_ref[:, sl])
            log_a = r * cdec[:, sl]
            a = jnp.exp(log_a)
            a_scr[:, sl] = a
            u_scr[:, sl] = jnp.sqrt(jnp.tanh(-log_a) * (1.0 + a * a)) * (ig * xh)

        row = lax.broadcasted_iota(jnp.int32, (SUBLANES, W), 0)

        def body(g, cr):
            off = pl.multiple_of(g * SUBLANES, SUBLANES)
            A = a_scr[pl.ds(off, SUBLANES), :]
            U = u_scr[pl.ds(off, SUBLANES), :]
            for s in (1, 2, 4):
                a_sh = pltpu.roll(A, s, 0)
                u_sh = pltpu.roll(U, s, 0)
                m = row >= s
                U = jnp.where(m, A * u_sh + U, U)
                A = jnp.where(m, A * a_sh, A)
            H = A * cr + U
            u_scr[pl.ds(off, SUBLANES), :] = H
            return H[SUBLANES - 1:SUBLANES, :]

        carry[...] = lax.fori_loop(0, ts // SUBLANES, body, carry[...], unroll=4)
        o_ref[...] = (u_scr[...] * jax.nn.gelu(cur[:, W:2 * W])).astype(o_ref.dtype)

    @pl.when(lax.rem(step, 2) == 0)
    def _():
        tile(pr_even, pr_odd)

    @pl.when(lax.rem(step, 2) == 1)
    def _():
        tile(pr_odd, pr_even)


def _lru(x2d, w_lru, B, S, cw, cb, wa, ba, wx, bx, lam):
    T = B * S
    ts = min(512, S)
    nt = S // ts
    W = LRU_WIDTH
    wcat = jnp.concatenate([wa, wx], axis=-1).astype(bf16)
    row = lambda v: v.reshape(1, W)
    full = lambda shp: pl.BlockSpec(shp, lambda b, c: (0,) * len(shp))
    last = B * nt - 1
    return pl.pallas_call(
        _lru_kernel,
        out_shape=jax.ShapeDtypeStruct((T, W), bf16),
        grid=(B, nt),
        in_specs=[pl.BlockSpec((ts, D_MODEL), lambda b, c: (b * nt + c, 0)),
                  pl.BlockSpec((ts, D_MODEL), lambda b, c: (jnp.minimum(b * nt + c + 1, last), 0)),
                  full(w_lru.shape), full((CONV_WIDTH, W)), full((1, W)), full(wcat.shape),
                  full((1, W)), full((1, W)), full((1, W))],
        out_specs=pl.BlockSpec((ts, W), lambda b, c: (b * nt + c, 0)),
        scratch_shapes=[pltpu.VMEM((ts + SUBLANES, W), f32), pltpu.VMEM((ts, 2 * W), f32),
                        pltpu.VMEM((ts, 2 * W), f32), pltpu.VMEM((ts, W), f32), pltpu.VMEM((ts, W), f32),
                        pltpu.VMEM((1, W), f32)],
        compiler_params=_cparams("arbitrary", "arbitrary"),
        name="rg_lru",
    )(x2d, x2d, w_lru, cw, row(cb), wcat, row(ba), row(bx), row(lam))


def _ml_kernel(xin_ref, w_ref, bias_ref, g_ref, o_ref, c_scr, n_scr, m_scr):
    L = xin_ref.shape[0]

    @pl.when(pl.program_id(1) == 0)
    def _():
        c_scr[...] = jnp.zeros(c_scr.shape, f32)
        n_scr[...] = jnp.zeros(n_scr.shape, f32)
        m_scr[...] = jnp.zeros(m_scr.shape, f32)

    pr = _project(xin_ref, w_ref)
    q = pr[:, 0:ML_QK_W]
    k = pr[:, ML_QK_W:2 * ML_QK_W]
    v = pr[:, 2 * ML_QK_W:2 * ML_QK_W + ML_V_W]
    og = pr[:, 2 * ML_QK_W + ML_V_W:2 * ML_QK_W + 2 * ML_V_W]
    sm = pr[:, 2 * ML_QK_W + 2 * ML_V_W:] + bias_ref[...]
    logf = _log_sigmoid(sm)
    ri = lax.broadcasted_iota(jnp.int32, (L, L), 0)
    ci = lax.broadcasted_iota(jnp.int32, (L, L), 1)
    tri = ri >= ci
    b_all = _cumsum_rows(logf, L)
    g_all = sm - pltpu.roll(b_all, LANES - (SM_ML_F - SM_ML_I), 1)
    cm_all = _cummax_rows(g_all, L)
    g_t = g_all.T
    kb = k.astype(bf16)
    lane_head = lax.broadcasted_iota(jnp.int32, q.shape, 1) // ML_DQK
    scale = ML_DQK ** -0.5
    heads = range(ML_HEADS)
    sls = [slice(ML_DV * h, ML_DV * (h + 1)) for h in heads]
    m_news, decays, w_cs, m_ts, dws, iws = [], [], [], [], [], []
    for h in heads:
        b_c = b_all[:, SM_ML_F + h:SM_ML_F + h + 1]
        g_c = g_all[:, SM_ML_I + h:SM_ML_I + h + 1]
        g_r = g_t[SM_ML_I + h:SM_ML_I + h + 1, :]
        b_last = b_c[L - 1:L, :]
        m_prev = m_scr[h:h + 1, 0:1]
        mm_c = jnp.maximum(m_prev, cm_all[:, SM_ML_I + h:SM_ML_I + h + 1])
        mm_last = mm_c[L - 1:L, :]
        m_news.append(b_last + mm_last)
        decays.append(jnp.exp(m_prev - mm_last))
        w_cs.append(jnp.exp(g_c - mm_last))
        m_ts.append(b_c + mm_c)
        dws.append(jnp.exp(jnp.where(tri, g_r - mm_c, NEG_BIG)))
        iws.append(jnp.exp(m_prev - mm_c))
    qms = [jnp.where(lane_head == h, q, 0.0) * scale for h in heads]
    qmbs = [qm.astype(bf16) for qm in qms]
    ss = [_bdot_nt(qmbs[h], kb) * dws[h] for h in heads]
    n_rows = [n_scr[h:h + 1, :] for h in heads]
    nums = [_bdot(ss[h], v[:, sls[h]]) + iws[h] * _bdot(qmbs[h], c_scr[:, sls[h]]) for h in heads]
    dens = [jnp.sum(ss[h], axis=1, keepdims=True) + iws[h] * jnp.sum(qms[h] * n_rows[h], axis=1, keepdims=True)
            for h in heads]
    hhs = [nums[h] / jnp.maximum(jnp.abs(dens[h]), jnp.exp(-m_ts[h])) for h in heads]
    for h in heads:
        c_scr[:, sls[h]] = decays[h] * c_scr[:, sls[h]] + _bdot_tn(kb, w_cs[h] * v[:, sls[h]])
        n_scr[h:h + 1, :] = decays[h] * n_rows[h] + jnp.sum(w_cs[h] * k, axis=0, keepdims=True)
        m_scr[h:h + 1, :] = jnp.broadcast_to(m_news[h], (1, LANES))
    for h in heads:
        ms = jnp.mean(hhs[h] * hhs[h], axis=1, keepdims=True)
        hn = hhs[h] * lax.rsqrt(ms + RMS_EPS) * g_ref[:, sls[h]]
        o_ref[:, sls[h]] = (hn * _sigmoid(og[:, sls[h]])).astype(o_ref.dtype)


def _mlstm(x2d, w_ml, B, S, i_bias, f_bias, norm_g):
    T = B * S
    L = min(ML_CHUNK, S)
    nc = S // L
    bias = jnp.zeros((1, LANES), f32)
    bias = bias.at[0, SM_ML_I:SM_ML_I + ML_HEADS].set(i_bias).at[0, SM_ML_F:SM_ML_F + ML_HEADS].set(f_bias)
    full = lambda shp: pl.BlockSpec(shp, lambda b, c: (0,) * len(shp))
    return pl.pallas_call(
        _ml_kernel,
        out_shape=jax.ShapeDtypeStruct((T, ML_V_W), bf16),
        grid=(B, nc),
        in_specs=[pl.BlockSpec((L, D_MODEL), lambda b, c: (b * nc + c, 0)), full(w_ml.shape),
                  full((1, LANES)), full((1, ML_V_W))],
        out_specs=pl.BlockSpec((L, ML_V_W), lambda b, c: (b * nc + c, 0)),
        scratch_shapes=[pltpu.VMEM((ML_QK_W, ML_V_W), f32), pltpu.VMEM((SUBLANES, ML_QK_W), f32),
                        pltpu.VMEM((SUBLANES, LANES), f32)],
        compiler_params=_cparams("parallel", "arbitrary"),
        name="mlstm",
    )(x2d, w_ml, bias, norm_g.reshape(1, ML_V_W))


def _gd_kernel(xin_ref, w_ref, cw_ref, alog_ref, dtb_ref, g_ref, o_ref, ext, s_scr):
    R = xin_ref.shape[0]
    KW = GD_K_W
    L = GD_CHUNK
    nch = R // L
    lsh = L.bit_length() - 1
    PW = nch * L
    nsl = KW // LANES
    passes = GD_SOLVE_PASSES

    @pl.when(pl.program_id(1) == 0)
    def _():
        ext[0:SUBLANES, :] = jnp.zeros((SUBLANES, ext.shape[1]), f32)
        s_scr[...] = jnp.zeros(s_scr.shape, f32)

    pr = _project(xin_ref, w_ref)
    z = pr[:, 3 * KW:4 * KW]
    sm = pr[:, 4 * KW:]
    ext[SUBLANES:SUBLANES + R, :] = pr[:, 0:3 * KW]
    xc = (cw_ref[3:4, :] * ext[SUBLANES:SUBLANES + R, :]
          + cw_ref[2:3, :] * ext[SUBLANES - 1:SUBLANES - 1 + R, :]
          + cw_ref[1:2, :] * ext[SUBLANES - 2:SUBLANES - 2 + R, :]
          + cw_ref[0:1, :] * ext[SUBLANES - 3:SUBLANES - 3 + R, :])
    ext[0:SUBLANES, :] = ext[R:R + SUBLANES, :]
    xc = xc * _sigmoid(xc)

    g_t = -jnp.exp(alog_ref[...]) * _softplus(sm + dtb_ref[...])
    beta_t = _sigmoid(sm)
    gc_all = _cumsum_rows(g_t, L)
    gc_t = gc_all.T

    pt = lax.broadcasted_iota(jnp.int32, (L, PW), 0)
    plane = lax.broadcasted_iota(jnp.int32, (L, PW), 1)
    ps = plane & (L - 1)
    pc = plane >> lsh
    tri_p = pt >= ps
    strict_p = pt > ps
    eye_p = jnp.where(pt == ps, 1.0, 0.0)
    blk8_p = (pt >> 3) == (ps >> 3)
    cmask_b = [jnp.where(pc == c, 1.0, 0.0).astype(bf16) for c in range(nch)]

    def pick(parts):
        out = parts[nch - 1]
        for c in range(nch - 2, -1, -1):
            out = jnp.where(pc == c, parts[c], out)
        return out

    def chunks(col):
        return [col[c * L:(c + 1) * L] for c in range(nch)]

    def bd(yb):
        return jnp.concatenate([yb * cmask_b[c] for c in range(nch)], axis=0)

    def pdot(xp, yp):
        return _mdot(_split_bf16(xp, passes), [bd(p) for p in _split_bf16(yp, passes)])

    lane_r = lax.broadcasted_iota(jnp.int32, (R, LANES), 1)
    lo_r = lane_r < GD_DK
    lane_l = lax.broadcasted_iota(jnp.int32, (L, LANES), 1)
    lo_l = lane_l < GD_DK
    lane_row = lax.broadcasted_iota(jnp.int32, (1, LANES), 1)
    r2 = lax.broadcasted_iota(jnp.int32, (LANES, LANES), 0)
    c2 = lax.broadcasted_iota(jnp.int32, (LANES, LANES), 1)
    blockdiag = (r2 < GD_DK) == (c2 < GD_DK)

    def half_sums(y, lo):
        s_lo = jnp.sum(jnp.where(lo, y, 0.0), axis=1, keepdims=True)
        s_hi = jnp.sum(jnp.where(lo, 0.0, y), axis=1, keepdims=True)
        return s_lo, s_hi

    def l2n(y):
        s_lo, s_hi = half_sums(y * y, lo_r)
        return y * jnp.where(lo_r, lax.rsqrt(s_lo + RMS_EPS), lax.rsqrt(s_hi + RMS_EPS))

    qns, kns, vss, a_ps, p_bs, gcols, bcols = [], [], [], [], [], [], []
    for j in range(nsl):
        qn = l2n(xc[:, LANES * j:LANES * (j + 1)]) * (GD_DK ** -0.5)
        kn = l2n(xc[:, KW + LANES * j:KW + LANES * (j + 1)])
        qns.append(qn)
        kns.append(kn)
        vss.append(xc[:, 2 * KW + LANES * j:2 * KW + LANES * (j + 1)])
        lhs = jnp.concatenate([jnp.where(lo_r, kn, 0.0), jnp.where(lo_r, 0.0, kn),
                               jnp.where(lo_r, qn, 0.0), jnp.where(lo_r, 0.0, qn)], axis=0)
        gram = _bdot_nt(lhs, kn)
        for e in range(2):
            h = 2 * j + e
            gcol = gc_all[:, SM_GD_A + h:SM_GD_A + h + 1]
            bcol = beta_t[:, SM_GD_B + h:SM_GD_B + h + 1]
            gc_r = gc_t[SM_GD_A + h:SM_GD_A + h + 1, :]
            gam_p = jnp.exp(jnp.where(tri_p, pick(chunks(gcol)) - gc_r, NEG_BIG))
            kk_p = pick(chunks(gram[e * R:(e + 1) * R]))
            qk_p = pick(chunks(gram[(2 + e) * R:(3 + e) * R]))
            a_ps.append(jnp.where(strict_p, pick(chunks(bcol)) * gam_p * kk_p, 0.0))
            p_bs.append((gam_p * qk_p).astype(bf16))
            gcols.append(gcol)
            bcols.append(bcol)

    ads = [jnp.where(blk8_p, a, 0.0) for a in a_ps]
    a2s = [pdot(ad, ad) for ad in ads]
    a4s = [pdot(a2, a2) for a2 in a2s]
    xs = [pdot(eye_p - ad, eye_p + a2) for ad, a2 in zip(ads, a2s)]
    xs = [pdot(x, eye_p + a4) for x, a4 in zip(xs, a4s)]
    for sh in range(3, lsh):
        msk = ((pt >> (sh + 1)) == (ps >> (sh + 1))) & (((pt >> sh) & 1) == 1) & (((ps >> sh) & 1) == 0)
        ts = [pdot(jnp.where(msk, a, 0.0), x) for a, x in zip(a_ps, xs)]
        xs = [x - pdot(x, t) for x, t in zip(xs, ts)]

    us, ws, qgs, kds, gls = [], [], [], [], []
    for j in range(nsl):
        uw, egs, eds, glh = [], [], [], []
        for e in range(2):
            h = 2 * j + e
            gcol, bcol = gcols[h], bcols[h]
            eg = jnp.exp(gcol)
            rhs = jnp.concatenate([bcol * vss[j], (bcol * eg) * kns[j]], axis=1)
            uw.append(_mdot([bd(p) for p in _split_bf16(xs[h], passes)], _split_bf16(rhs, passes)))
            lasts = [gcol[c * L + L - 1:c * L + L] for c in range(nch)]
            gcl = jnp.concatenate([jnp.broadcast_to(v, (L, 1)) for v in lasts], axis=0)
            egs.append(eg)
            eds.append(jnp.exp(gcl - gcol))
            glh.append([jnp.exp(v) for v in lasts])
        us.append(jnp.where(lo_r, uw[0][:, :LANES], uw[1][:, :LANES]))
        ws.append(jnp.where(lo_r, uw[0][:, LANES:], uw[1][:, LANES:]))
        qgs.append(qns[j] * jnp.where(lo_r, egs[0], egs[1]))
        kds.append(kns[j] * jnp.where(lo_r, eds[0], eds[1]))
        gls.append([jnp.where(lane_row < GD_DK, glh[0][c], glh[1][c]) for c in range(nch)])

    states = [s_scr[j] for j in range(nsl)]
    outs = [[] for _ in range(nsl)]
    for c in range(nch):
        rs = slice(c * L, (c + 1) * L)
        for j in range(nsl):
            s_prev = states[j]
            sb = s_prev.astype(bf16)
            delta = us[j][rs] - _bdot(ws[j][rs], sb)
            db = delta.astype(bf16)
            pieces = []
            if c > 0:
                pieces.append(jnp.zeros((c * L, LANES), bf16))
            pieces.append(db)
            if c < nch - 1:
                pieces.append(jnp.zeros(((nch - 1 - c) * L, LANES), bf16))
            dpad = jnp.concatenate(pieces, axis=0) if len(pieces) > 1 else db
            intra = jnp.where(lo_l, jnp.dot(p_bs[2 * j], dpad, preferred_element_type=f32),
                              jnp.dot(p_bs[2 * j + 1], dpad, preferred_element_type=f32))
            outs[j].append(_bdot(qgs[j][rs], sb) + intra)
            upd = _bdot_tn(kds[j][rs], db)
            states[j] = gls[j][c] * s_prev + jnp.where(blockdiag, upd, 0.0)

    for j in range(nsl):
        sl = slice(LANES * j, LANES * (j + 1))
        s_scr[j] = states[j]
        o = jnp.concatenate(outs[j], axis=0) if nch > 1 else outs[j][0]
        m_lo, m_hi = half_sums(o * o, lo_r)
        inv = jnp.where(lo_r, lax.rsqrt(m_lo * (1.0 / GD_DV) + RMS_EPS), lax.rsqrt(m_hi * (1.0 / GD_DV) + RMS_EPS))
        zs = z[:, sl]
        o_ref[:, sl] = (o * inv * g_ref[:, sl] * (zs * _sigmoid(zs))).astype(o_ref.dtype)


def _gdn(x2d, w_gd, B, S, conv_w, a_log, dt_bias, norm_g):
    T = B * S
    L = min(GD_ROWS, S)
    nc = S // L
    KW = GD_K_W
    alog = jnp.zeros((1, LANES), f32).at[0, SM_GD_A:SM_GD_A + GD_HEADS].set(a_log)
    dtb = jnp.zeros((1, LANES), f32).at[0, SM_GD_A:SM_GD_A + GD_HEADS].set(dt_bias)
    g_row = jnp.tile(norm_g, GD_HEADS).reshape(1, GD_V_W)
    full = lambda shp: pl.BlockSpec(shp, lambda b, c: (0,) * len(shp))
    return pl.pallas_call(
        _gd_kernel,
        out_shape=jax.ShapeDtypeStruct((T, GD_V_W), bf16),
        grid=(B, nc),
        in_specs=[pl.BlockSpec((L, D_MODEL), lambda b, c: (b * nc + c, 0)), full(w_gd.shape),
                  full((CONV_WIDTH, 3 * KW)), full((1, LANES)), full((1, LANES)), full((1, GD_V_W))],
        out_specs=pl.BlockSpec((L, GD_V_W), lambda b, c: (b * nc + c, 0)),
        scratch_shapes=[pltpu.VMEM((L + SUBLANES, 3 * KW), f32),
                        pltpu.VMEM((KW // LANES, LANES, LANES), f32)],
        compiler_params=_cparams("parallel", "arbitrary"),
        name="gated_deltanet",
    )(x2d, w_gd, conv_w, alog, dtb, g_row)


def _merge_kernel(ha_ref, hb_ref, hc_ref, x_ref, wg_ref, wa_ref, wb_ref, wc_ref, wo_ref, lg_ref, lb_ref,
                  rw_ref, rb_ref, x1_ref, x1p_ref, pk_ref, cnt_ref, prev, carry):
    D = x_ref.shape[1]
    i = pl.program_id(0)

    @pl.when(i == 0)
    def _():
        prev[...] = jnp.zeros(prev.shape, f32)
        carry[...] = jnp.zeros(carry.shape, f32)

    route = _route_stages(prev[...], jnp.where(i > 0, 1.0, 0.0), rw_ref, rb_ref, pk_ref, cnt_ref, carry)
    xb = x_ref[...].astype(bf16)

    next(route)
    g0 = jnp.dot(xb, wg_ref[:, 0:D], preferred_element_type=f32)
    next(route)
    g1 = jnp.dot(xb, wg_ref[:, D:2 * D], preferred_element_type=f32)
    next(route)
    g2 = jnp.dot(xb, wg_ref[:, 2 * D:3 * D], preferred_element_type=f32)
    next(route)
    bra = jnp.dot(ha_ref[...], wa_ref[...], preferred_element_type=f32)
    next(route)
    brb = jnp.dot(hb_ref[...], wb_ref[...], preferred_element_type=f32)
    brc = jnp.dot(hc_ref[...], wc_ref[...], preferred_element_type=f32)
    next(route)
    merged = _sigmoid(g0) * bra + _sigmoid(g1) * brb + _sigmoid(g2) * brc
    y = jnp.dot(merged.astype(bf16), wo_ref[...], preferred_element_type=f32)
    for _ in route:
        pass
    out = _layer_norm(DN_ALPHA * x_ref[...] + y, lg_ref[...], lb_ref[...])
    x1_ref[...] = out
    x1p_ref[...] = _pack_halves(out)
    prev[...] = out


def _merge(ha, hb, hc, x2d, w_gate, wa, wb, wc, wo, lg, lb, router_w, router_b):
    T = x2d.shape[0]
    D = D_MODEL
    tm = min(512, T)
    nt = T // tm
    rw, rb = _router_params(router_w, router_b)
    full = lambda shp: pl.BlockSpec(shp, lambda i: (0,) * len(shp))
    held = lambda shp: pl.BlockSpec(shp, lambda i: (0,) * len(shp), pipeline_mode=pl.Buffered(1))
    rows = lambda w: pl.BlockSpec((tm, w), lambda i: (jnp.minimum(i, nt - 1), 0))
    routed = pl.BlockSpec((tm, LANES), lambda i: (jnp.maximum(i - 1, 0), 0))
    return pl.pallas_call(
        _merge_kernel,
        out_shape=(jax.ShapeDtypeStruct((T, D), f32), jax.ShapeDtypeStruct((T, D // 2), jnp.uint32),
                   jax.ShapeDtypeStruct((T, LANES), f32), jax.ShapeDtypeStruct((1, LANES), f32)),
        grid=(nt + 1,),
        in_specs=[rows(LRU_WIDTH), rows(ML_V_W), rows(GD_V_W), rows(D), held(w_gate.shape),
                  held((LRU_WIDTH, D)), held((ML_V_W, D)), held((GD_V_W, D)), held((D, D)),
                  full((1, D)), full((1, D)), held((D, LANES)), full((1, LANES))],
        out_specs=(rows(D), rows(D // 2), routed, full((1, LANES))),
        scratch_shapes=[pltpu.VMEM((tm, D), f32), pltpu.VMEM((1, LANES), f32)],
        compiler_params=_cparams("arbitrary"),
        name="merge_outproj_ln_route",
    )(ha, hb, hc, x2d, w_gate, wa.astype(bf16), wb.astype(bf16), wc.astype(bf16),
      wo.astype(bf16), lg.reshape(1, D), lb.reshape(1, D), rw, rb)


PK_IDX = 0
PK_GATE = 4
PK_RANK = 8


def _route_stages(x, weight, rw_ref, rb_ref, pk_ref, cnt_ref, carry):
    tr = x.shape[0]
    logits = _mdot(_split_bf16(x, 3), _split_bf16(rw_ref[...], 3)) + rb_ref[...]
    lane = lax.broadcasted_iota(jnp.int32, (tr, LANES), 1).astype(f32)
    vals = logits
    idxs, tops = [], []
    yield
    for _ in range(TOP_K):
        m = jnp.max(vals, axis=1, keepdims=True)
        idx = jnp.min(jnp.where(vals == m, lane, float(LANES)), axis=1, keepdims=True)
        idxs.append(idx)
        tops.append(m)
        vals = jnp.where(lane == idx, NEG_BIG * 2.0, vals)
        yield
    es = [jnp.exp(t - tops[0]) for t in tops]
    tot = es[0] + es[1] + es[2] + es[3]
    onehots = [lane == idx for idx in idxs]
    sel = jnp.zeros((tr, LANES), f32)
    for oh in onehots:
        sel = sel + oh.astype(f32)
    ri = lax.broadcasted_iota(jnp.int32, (tr, tr), 0)
    ci = lax.broadcasted_iota(jnp.int32, (tr, tr), 1)
    before = jnp.dot((ri > ci).astype(bf16), sel.astype(bf16), preferred_element_type=f32) + carry[...]
    carry[...] = carry[...] + weight * jnp.sum(sel, axis=0, keepdims=True)
    cnt_ref[...] = carry[...]
    yield
    packed = jnp.zeros((tr, LANES), f32)
    for kk in range(TOP_K):
        rank = jnp.sum(jnp.where(onehots[kk], before, 0.0), axis=1, keepdims=True)
        packed = jnp.where(lane == float(PK_IDX + kk), idxs[kk], packed)
        packed = jnp.where(lane == float(PK_GATE + kk), es[kk] / tot, packed)
        packed = jnp.where(lane == float(PK_RANK + kk), rank, packed)
    pk_ref[...] = packed


def _router_params(router_w, router_b):
    rw = jnp.zeros((D_MODEL, LANES), f32).at[:, :N_EXPERTS].set(router_w)
    rb = jnp.full((1, LANES), NEG_BIG, f32).at[0, :N_EXPERTS].set(router_b)
    return rw, rb


def _expert_kernel(be_ref, nu_ref, nv_ref, x_ref, w1_ref, b1_ref, w2_ref, b2_ref, o_ref, w1b, w2b):
    i = pl.program_id(0)
    used = i < nu_ref[0]
    new_expert = jnp.logical_or(i == 0, be_ref[i] != be_ref[jnp.maximum(i - 1, 0)])

    @pl.when(jnp.logical_and(used, new_expert))
    def _():
        w1b[...] = w1_ref[0, 0].astype(bf16)
        w2b[...] = w2_ref[0, 0].astype(bf16)

    @pl.when(used)
    def _():
        rows = lax.broadcasted_iota(jnp.int32, x_ref.shape, 0)
        xp = jnp.where(rows < nv_ref[i], x_ref[...], jnp.uint32(0))
        x = _unpack_halves(xp).astype(bf16)
        hdn = jnp.dot(x, w1b[...], preferred_element_type=f32) + b1_ref[0, 0]
        glu = jnp.minimum(hdn[:, :D_FF], SWIGLU_LIMIT)
        lin = jnp.clip(hdn[:, D_FF:], -SWIGLU_LIMIT, SWIGLU_LIMIT)
        act = glu * _sigmoid(SWIGLU_ALPHA * glu) * (lin + 1.0)
        y = jnp.dot(act.astype(bf16), w2b[...], preferred_element_type=f32) + b2_ref[0, 0]
        o_ref[...] = _pack_halves(y)

    @pl.when(i >= nu_ref[0])
    def _():
        o_ref[...] = jnp.zeros(o_ref.shape, o_ref.dtype)


def _experts(xs, block_e, n_used, n_valid, layer, w1, b1, w2, b2):
    P = xs.shape[0]
    nb = P // MOE_BLOCK
    D = D_MODEL
    nl = w1.shape[0]
    grid_spec = pltpu.PrefetchScalarGridSpec(
        num_scalar_prefetch=3,
        grid=(nb,),
        in_specs=[pl.BlockSpec((MOE_BLOCK, D // 2), lambda i, be, nu, nv: (i, 0)),
                  pl.BlockSpec((1, 1, D, 2 * D_FF), lambda i, be, nu, nv: (layer, be[i], 0, 0)),
                  pl.BlockSpec((1, 1, 1, 2 * D_FF), lambda i, be, nu, nv: (layer, be[i], 0, 0)),
                  pl.BlockSpec((1, 1, D_FF, D), lambda i, be, nu, nv: (layer, be[i], 0, 0)),
                  pl.BlockSpec((1, 1, 1, D), lambda i, be, nu, nv: (layer, be[i], 0, 0))],
        out_specs=pl.BlockSpec((MOE_BLOCK, D // 2), lambda i, be, nu, nv: (i, 0)),
        scratch_shapes=[pltpu.VMEM((D, 2 * D_FF), bf16), pltpu.VMEM((D_FF, D), bf16)],
    )
    return pl.pallas_call(
        _expert_kernel,
        out_shape=jax.ShapeDtypeStruct((P, D // 2), jnp.uint32),
        grid_spec=grid_spec,
        compiler_params=_cparams("arbitrary"),
        name="experts",
    )(block_e, n_used, n_valid, xs, w1, b1.reshape(nl, N_EXPERTS, 1, 2 * D_FF), w2,
      b2.reshape(nl, N_EXPERTS, 1, D))


def _combine_kernel(y0_ref, y1_ref, y2_ref, y3_ref, pk_ref, x_ref, lg_ref, lb_ref, o_ref):
    pk = pk_ref[...]
    y = pk[:, PK_GATE:PK_GATE + 1] * _unpack_halves(y0_ref[...])
    for kk, y_ref in ((1, y1_ref), (2, y2_ref), (3, y3_ref)):
        y = y + pk[:, PK_GATE + kk:PK_GATE + kk + 1] * _unpack_halves(y_ref[...])
    o_ref[...] = _layer_norm(DN_ALPHA * x_ref[...] + y, lg_ref[...], lb_ref[...])


def _combine(yg, packed, x1, lg, lb):
    T, D = x1.shape
    tm = min(1024, T)
    nt = T // tm
    full = lambda shp: pl.BlockSpec(shp, lambda i: (0,) * len(shp))
    choice = lambda kk: pl.BlockSpec((tm, D // 2), lambda i: (kk * nt + i, 0))
    return pl.pallas_call(
        _combine_kernel,
        out_shape=jax.ShapeDtypeStruct((T, D), f32),
        grid=(nt,),
        in_specs=[choice(0), choice(1), choice(2), choice(3), pl.BlockSpec((tm, LANES), lambda i: (i, 0)),
                  pl.BlockSpec((tm, D), lambda i: (i, 0)), full((1, D)), full((1, D))],
        out_specs=pl.BlockSpec((tm, D), lambda i: (i, 0)),
        compiler_params=_cparams("parallel"),
        name="combine_ln",
    )(yg, yg, yg, yg, packed, x1, lg.reshape(1, D), lb.reshape(1, D))


def _sc_mesh():
    return plsc.VectorSubcoreMesh(core_axis_name="c", subcore_axis_name="s")


def _sc_dispatch(x, posk, n_slots):
    t_tokens, w = x.shape

    @functools.partial(pl.kernel, out_type=jax.ShapeDtypeStruct((n_slots, w), x.dtype), mesh=_sc_mesh(),
                       name="sc_dispatch")
    def k(x_hbm, p_hbm, o_hbm):
        def body(x_vmem, p_vmem):
            for kk in range(TOP_K):
                pltpu.sync_copy(x_vmem, o_hbm.at[p_vmem.at[kk]])

        pltpu.emit_pipeline(
            body,
            grid=(t_tokens // SC_ROWS,),
            in_specs=[pl.BlockSpec((SC_ROWS, w), lambda i: (i, 0), pipeline_mode=pl.Buffered(1)),
                      pl.BlockSpec((TOP_K, SC_ROWS), lambda i: (0, i))],
            out_specs=[],
            core_axis_name=("c", "s"),
            dimension_semantics=(pltpu.PARALLEL,),
        )(x_hbm, p_hbm)

    return k(x, posk)


def _sc_gather(table, idx):
    n = idx.shape[0]
    w = table.shape[1]

    @functools.partial(pl.kernel, out_type=jax.ShapeDtypeStruct((n, w), table.dtype), mesh=_sc_mesh(),
                       name="sc_gather")
    def k(t_hbm, i_hbm, o_hbm):
        def body(i_vmem, o_vmem):
            pltpu.sync_copy(t_hbm.at[i_vmem.at[0]], o_vmem)

        pltpu.emit_pipeline(
            body,
            grid=(n // SC_ROWS,),
            in_specs=[pl.BlockSpec((1, SC_ROWS), lambda i: (0, i))],
            out_specs=[pl.BlockSpec((SC_ROWS, w), lambda i: (i, 0), pipeline_mode=pl.Buffered(1))],
            core_axis_name=("c", "s"),
            dimension_semantics=(pltpu.PARALLEL,),
        )(i_hbm, o_hbm)

    return k(table, idx.reshape(1, n))


def _moe(x1, x1p, packed, cnt, layer, w1, b1, w2, b2, lg, lb):
    T, D = x1.shape
    A = T * TOP_K
    idx = packed[:, PK_IDX:PK_IDX + TOP_K].astype(jnp.int32)
    rank = packed[:, PK_RANK:PK_RANK + TOP_K].astype(jnp.int32)
    counts = cnt[0, :N_EXPERTS].astype(jnp.int32)
    padded = ((counts + MOE_BLOCK - 1) // MOE_BLOCK) * MOE_BLOCK
    pad_end = jnp.cumsum(padded)
    pad_start = pad_end - padded
    posk = (pad_start[idx] + rank).T
    n_blocks = -(-A // MOE_BLOCK) + N_EXPERTS
    P = n_blocks * MOE_BLOCK
    starts = jnp.arange(n_blocks, dtype=jnp.int32) * MOE_BLOCK
    block_e = jnp.minimum(jnp.sum((pad_end[None, :] <= starts[:, None]).astype(jnp.int32), axis=1), N_EXPERTS - 1)
    n_used = (pad_end[-1:] // MOE_BLOCK).astype(jnp.int32)
    n_valid = jnp.clip(pad_start[block_e] + counts[block_e] - starts, 0, MOE_BLOCK).astype(jnp.int32)
    xs = _sc_dispatch(x1p, posk, P)
    ys = _experts(xs, block_e, n_used, n_valid, layer, w1, b1, w2, b2)
    yg = _sc_gather(ys, posk.reshape(A))
    return _combine(yg, packed, x1, lg, lb)


def _layer(x2d, B, S, p, layer, stacked):
    w = _split_w_in(p['w_in'])
    ha = _lru(x2d, w['lru'], B, S, p['lru_conv_w'], p['lru_conv_b'], p['lru_wa'], p['lru_ba'], p['lru_wx'],
              p['lru_bx'], p['lru_lambda'])
    hb = _mlstm(x2d, w['ml'], B, S, p['ml_i_bias'], p['ml_f_bias'], p['ml_norm_g'])
    hc = _gdn(x2d, w['gd'], B, S, p['gd_conv_w'], p['gd_a_log'], p['gd_dt_bias'], p['gd_norm_g'])
    x1, x1p, packed, cnt = _merge(ha, hb, hc, x2d, w['gate'], p['w_br_lru'], p['w_br_ml'], p['w_br_gd'],
                                  p['w_out'], p['ln1_g'], p['ln1_b'], p['router_w'], p['router_b'])
    return _moe(x1, x1p, packed, cnt, layer, stacked['exp_w1'], stacked['exp_b1'],
                stacked['exp_w2'], stacked['exp_b2'], p['ln2_g'], p['ln2_b'])


def kernel(x, w_in, lru_conv_w, lru_conv_b, lru_wa, lru_ba, lru_wx, lru_bx, lru_lambda, ml_i_bias, ml_f_bias, ml_norm_g, gd_conv_w, gd_a_log, gd_dt_bias, gd_norm_g, w_br_lru, w_br_ml, w_br_gd, w_out, ln1_g, ln1_b, router_w, router_b, exp_w1, exp_b1, exp_w2, exp_b2, ln2_g, ln2_b):
    B, S, D = x.shape
    params = dict(w_in=w_in, lru_conv_w=lru_conv_w, lru_conv_b=lru_conv_b, lru_wa=lru_wa, lru_ba=lru_ba,
                  lru_wx=lru_wx, lru_bx=lru_bx, lru_lambda=lru_lambda, ml_i_bias=ml_i_bias,
                  ml_f_bias=ml_f_bias, ml_norm_g=ml_norm_g, gd_conv_w=gd_conv_w, gd_a_log=gd_a_log,
                  gd_dt_bias=gd_dt_bias, gd_norm_g=gd_norm_g, w_br_lru=w_br_lru, w_br_ml=w_br_ml,
                  w_br_gd=w_br_gd, w_out=w_out, ln1_g=ln1_g, ln1_b=ln1_b, router_w=router_w,
                  router_b=router_b, ln2_g=ln2_g, ln2_b=ln2_b)
    stacked = dict(exp_w1=exp_w1, exp_b1=exp_b1, exp_w2=exp_w2, exp_b2=exp_b2)
    h = x.reshape(B * S, D)
    for l in range(w_in.shape[0]):
        h = _layer(h, B, S, {k: v[l] for k, v in params.items()}, l, stacked)
    return h.reshape(B, S, D)
```

```python
import functools

import jax
import jax.numpy as jnp
from jax import lax
from jax.experimental import pallas as pl
from jax.experimental.pallas import tpu as pltpu
from jax.experimental.pallas import tpu_sc as plsc

f32 = jnp.float32
bf16 = jnp.bfloat16

D_MODEL = 1024
LRU_WIDTH = 1024
LRU_BLOCKS = 8
LRU_C = 8.0
CONV_WIDTH = 4
ML_HEADS = 4
ML_DQK = 64
ML_DV = 128
GD_HEADS = 8
GD_DK = 64
GD_DV = 64
N_EXPERTS = 32
TOP_K = 4
D_FF = 1024
SWIGLU_LIMIT = 7.0
SWIGLU_ALPHA = 1.702
MOE_BLOCK = 512
N_BRANCH = 3
DEPTH = 2
DN_ALPHA = (2.0 * DEPTH) ** 0.25
LN_EPS = 1e-5
RMS_EPS = 1e-6

ML_QK_W = ML_HEADS * ML_DQK
ML_V_W = ML_HEADS * ML_DV
GD_K_W = GD_HEADS * GD_DK
GD_V_W = GD_HEADS * GD_DV
IN_SPLITS = (LRU_WIDTH, LRU_WIDTH, ML_QK_W, ML_QK_W, ML_V_W, ML_HEADS, ML_HEADS, ML_V_W,
             GD_K_W, GD_K_W, GD_V_W, GD_HEADS, GD_HEADS, GD_V_W, N_BRANCH * D_MODEL)

LANES = 128
SUBLANES = 8
NEG_BIG = -1e30

SM_ML_I = 0
SM_ML_F = 4
SM_GD_A = 8
SM_GD_B = 16

ML_CHUNK = 256
GD_CHUNK = 64
GD_ROWS = 256
GD_SOLVE_PASSES = 1
VMEM_LIMIT = 56 * 1024 * 1024
SC_ROWS = 128


def _cparams(*sem):
    return pltpu.CompilerParams(dimension_semantics=sem, vmem_limit_bytes=VMEM_LIMIT)


def _cummax_rows(x, seg):
    pos = lax.broadcasted_iota(jnp.int32, x.shape, 0) & (seg - 1)
    step = 1
    while step < seg:
        x = jnp.maximum(x, jnp.where(pos >= step, pltpu.roll(x, step, 0), NEG_BIG))
        step *= 2
    return x


def _cumsum_rows(x, seg):
    pos = lax.broadcasted_iota(jnp.int32, x.shape, 0) & (seg - 1)
    step = 1
    while step < seg:
        x = x + jnp.where(pos >= step, pltpu.roll(x, step, 0), 0.0)
        step *= 2
    return x


def _bdot(a, b):
    return jnp.dot(a.astype(bf16), b.astype(bf16), preferred_element_type=f32)


def _bdot_nt(a, b):
    return lax.dot_general(a.astype(bf16), b.astype(bf16), (((1,), (1,)), ((), ())),
                           preferred_element_type=f32)


def _bdot_tn(a, b):
    return lax.dot_general(a.astype(bf16), b.astype(bf16), (((0,), (0,)), ((), ())),
                           preferred_element_type=f32)


def _split_bf16(x, passes):
    hi = x.astype(bf16)
    if passes == 1:
        return (hi,)
    return (hi, (x - hi.astype(f32)).astype(bf16))


def _mdot(xs, ys):
    out = jnp.dot(xs[0], ys[0], preferred_element_type=f32)
    if len(xs) > 1:
        out = out + jnp.dot(xs[1], ys[0], preferred_element_type=f32)
        out = out + jnp.dot(xs[0], ys[1], preferred_element_type=f32)
    return out


def _sigmoid(x):
    return 1.0 / (1.0 + jnp.exp(-x))


def _softplus(x):
    return jnp.maximum(x, 0.0) + jnp.log1p(jnp.exp(-jnp.abs(x)))


def _log_sigmoid(x):
    return jnp.minimum(x, 0.0) - jnp.log1p(jnp.exp(-jnp.abs(x)))


def _pack_halves(v):
    n = v.shape[1] // 2
    hi = lax.bitcast_convert_type(v[:, :n].astype(bf16).astype(f32), jnp.uint32)
    lo = lax.bitcast_convert_type(v[:, n:].astype(bf16).astype(f32), jnp.uint32)
    return hi | (lo >> 16)


def _unpack_halves(p):
    hi = lax.bitcast_convert_type(p & jnp.uint32(0xFFFF0000), f32)
    lo = lax.bitcast_convert_type(p << 16, f32)
    return jnp.concatenate([hi, lo], axis=1)


def _layer_norm(z, g, b):
    mu = jnp.mean(z, axis=-1, keepdims=True)
    zc = z - mu
    var = jnp.mean(zc * zc, axis=-1, keepdims=True)
    return zc * lax.rsqrt(var + LN_EPS) * g + b


def _split_w_in(w):
    pts = []
    acc = 0
    for s in IN_SPLITS[:-1]:
        acc += s
        pts.append(acc)
    (lru_x, lru_y, ml_q, ml_k, ml_v, ml_i, ml_f, ml_o,
     gd_q, gd_k, gd_v, gd_a, gd_b, gd_z, gate) = jnp.split(w, pts, axis=1)
    pad = jnp.zeros((w.shape[0], LANES - 2 * ML_HEADS - 2 * GD_HEADS), w.dtype)
    small = jnp.concatenate([ml_i, ml_f, gd_a, gd_b, pad], axis=1)
    cat = lambda *cols: jnp.concatenate(cols, axis=1).astype(bf16)
    return dict(lru=cat(lru_x, lru_y), ml=cat(ml_q, ml_k, ml_v, ml_o, small),
                gd=cat(gd_q, gd_k, gd_v, gd_z, small), gate=gate.astype(bf16))


def _project(x_ref, w_ref):
    return jnp.dot(x_ref[...].astype(bf16), w_ref[...], preferred_element_type=f32)


def _lru_kernel(xin_ref, xnext_ref, w_ref, cw_ref, cb_ref, wcat_ref, ba_ref, bx_ref, lam_ref, o_ref,
                xext, pr_even, pr_odd, a_scr, u_scr, carry):
    ts = xin_ref.shape[0]
    W = a_scr.shape[1]
    bw = W // LRU_BLOCKS
    step = pl.program_id(0) * pl.num_programs(1) + pl.program_id(1)

    @pl.when(step == 0)
    def _():
        pr_even[...] = _project(xin_ref, w_ref)

    @pl.when(pl.program_id(1) == 0)
    def _():
        xext[0:SUBLANES, :] = jnp.zeros((SUBLANES, W), f32)
        carry[...] = jnp.zeros((1, W), f32)

    def tile(cur, nxt):
        x = cur[:, 0:W]
        xext[SUBLANES:SUBLANES + ts, :] = x
        xa = (cw_ref[3:4, :] * x
              + cw_ref[2:3, :] * xext[SUBLANES - 1:SUBLANES - 1 + ts, :]
              + cw_ref[1:2, :] * xext[SUBLANES - 2:SUBLANES - 2 + ts, :]
              + cw_ref[0:1, :] * xext[SUBLANES - 3:SUBLANES - 3 + ts, :]) + cb_ref[...]
        xext[0:SUBLANES, :] = xext[ts:ts + SUBLANES, :]

        xnb = xnext_ref[...].astype(bf16)
        pw = 2 * W // LRU_BLOCKS
        cdec = -LRU_C * _softplus(-lam_ref[...])
        for h in range(LRU_BLOCKS):
            sl = slice(bw * h, bw * (h + 1))
            psl = slice(pw * h, pw * (h + 1))
            nxt[:, psl] = jnp.dot(xnb, w_ref[:, psl], preferred_element_type=f32)
            xh = xa[:, sl]
            g = jnp.dot(xh.astype(bf16), wcat_ref[h], preferred_element_type=f32)
            r = _sigmoid(g[:, :bw] + ba_ref[:, sl])
            ig = _sigmoid(g[:, bw:] + bx_ref[:, sl])
            log_a = r * cdec[:, sl]
            a = jnp.exp(log_a)
            a_scr[:, sl] = a
            u_scr[:, sl] = jnp.sqrt(jnp.tanh(-log_a) * (1.0 + a * a)) * (ig * xh)

        row = lax.broadcasted_iota(jnp.int32, (SUBLANES, W), 0)

        def body(g, cr):
            off = pl.multiple_of(g * SUBLANES, SUBLANES)
            A = a_scr[pl.ds(off, SUBLANES), :]
            U = u_scr[pl.ds(off, SUBLANES), :]
            for s in (1, 2, 4):
                a_sh = pltpu.roll(A, s, 0)
                u_sh = pltpu.roll(U, s, 0)
                m = row >= s
                U = jnp.where(m, A * u_sh + U, U)
                A = jnp.where(m, A * a_sh, A)
            H = A * cr + U
            u_scr[pl.ds(off, SUBLANES), :] = H
            return H[SUBLANES - 1:SUBLANES, :]

        carry[...] = lax.fori_loop(0, ts // SUBLANES, body, carry[...], unroll=4)
        o_ref[...] = (u_scr[...] * jax.nn.gelu(cur[:, W:2 * W])).astype(o_ref.dtype)

    @pl.when(lax.rem(step, 2) == 0)
    def _():
        tile(pr_even, pr_odd)

    @pl.when(lax.rem(step, 2) == 1)
    def _():
        tile(pr_odd, pr_even)


def _lru(x2d, w_lru, B, S, cw, cb, wa, ba, wx, bx, lam):
    T = B * S
    ts = min(512, S)
    nt = S // ts
    W = LRU_WIDTH
    wcat = jnp.concatenate([wa, wx], axis=-1).astype(bf16)
    row = lambda v: v.reshape(1, W)
    full = lambda shp: pl.BlockSpec(shp, lambda b, c: (0,) * len(shp))
    last = B * nt - 1
    return pl.pallas_call(
        _lru_kernel,
        out_shape=jax.ShapeDtypeStruct((T, W), bf16),
        grid=(B, nt),
        in_specs=[pl.BlockSpec((ts, D_MODEL), lambda b, c: (b * nt + c, 0)),
                  pl.BlockSpec((ts, D_MODEL), lambda b, c: (jnp.minimum(b * nt + c + 1, last), 0)),
                  full(w_lru.shape), full((CONV_WIDTH, W)), full((1, W)), full(wcat.shape),
                  full((1, W)), full((1, W)), full((1, W))],
        out_specs=pl.BlockSpec((ts, W), lambda b, c: (b * nt + c, 0)),
        scratch_shapes=[pltpu.VMEM((ts + SUBLANES, W), f32), pltpu.VMEM((ts, 2 * W), f32),
                        pltpu.VMEM((ts, 2 * W), f32), pltpu.VMEM((ts, W), f32), pltpu.VMEM((ts, W), f32),
                        pltpu.VMEM((1, W), f32)],
        compiler_params=_cparams("arbitrary", "arbitrary"),
        name="rg_lru",
    )(x2d, x2d, w_lru, cw, row(cb), wcat, row(ba), row(bx), row(lam))


def _ml_kernel(xin_ref, xnext_ref, w_ref, bias_ref, g_ref, o_ref, c_scr, n_scr, m_scr, pr_even, pr_odd):
    step = pl.program_id(0) * pl.num_programs(1) + pl.program_id(1)

    @pl.when(step == 0)
    def _():
        pr_even[...] = _project(xin_ref, w_ref)

    @pl.when(pl.program_id(1) == 0)
    def _():
        c_scr[...] = jnp.zeros(c_scr.shape, f32)
        n_scr[...] = jnp.zeros(n_scr.shape, f32)
        m_scr[...] = jnp.zeros(m_scr.shape, f32)

    args = (xnext_ref, w_ref, bias_ref, g_ref, o_ref, c_scr, n_scr, m_scr)

    @pl.when(lax.rem(step, 2) == 0)
    def _():
        _ml_tile(pr_even, pr_odd, *args)

    @pl.when(lax.rem(step, 2) == 1)
    def _():
        _ml_tile(pr_odd, pr_even, *args)


def _ml_tile(cur, nxt, xnext_ref, w_ref, bias_ref, g_ref, o_ref, c_scr, n_scr, m_scr):
    L = cur.shape[0]
    xnb = xnext_ref[...].astype(bf16)
    pcols = [0, 2 * ML_QK_W, 2 * ML_QK_W + ML_V_W, 2 * ML_QK_W + 2 * ML_V_W, cur.shape[1]]

    def project_next(i):
        nxt[:, pcols[i]:pcols[i + 1]] = jnp.dot(xnb, w_ref[:, pcols[i]:pcols[i + 1]],
                                                preferred_element_type=f32)

    pr = cur[...]
    q = pr[:, 0:ML_QK_W]
    k = pr[:, ML_QK_W:2 * ML_QK_W]
    v = pr[:, 2 * ML_QK_W:2 * ML_QK_W + ML_V_W]
    og = pr[:, 2 * ML_QK_W + ML_V_W:2 * ML_QK_W + 2 * ML_V_W]
    sm = pr[:, 2 * ML_QK_W + 2 * ML_V_W:] + bias_ref[...]
    logf = _log_sigmoid(sm)
    ri = lax.broadcasted_iota(jnp.int32, (L, L), 0)
    ci = lax.broadcasted_iota(jnp.int32, (L, L), 1)
    tri = ri >= ci
    b_all = _cumsum_rows(logf, L)
    g_all = sm - pltpu.roll(b_all, LANES - (SM_ML_F - SM_ML_I), 1)
    cm_all = _cummax_rows(g_all, L)
    g_t = g_all.T
    kb = k.astype(bf16)
    lane_head = lax.broadcasted_iota(jnp.int32, q.shape, 1) // ML_DQK
    scale = ML_DQK ** -0.5
    heads = range(ML_HEADS)
    sls = [slice(ML_DV * h, ML_DV * (h + 1)) for h in heads]
    m_news, decays, w_cs, m_ts, dws, iws = [], [], [], [], [], []
    for h in heads:
        project_next(h)
        b_c = b_all[:, SM_ML_F + h:SM_ML_F + h + 1]
        g_c = g_all[:, SM_ML_I + h:SM_ML_I + h + 1]
        g_r = g_t[SM_ML_I + h:SM_ML_I + h + 1, :]
        b_last = b_c[L - 1:L, :]
        m_prev = m_scr[h:h + 1, 0:1]
        mm_c = jnp.maximum(m_prev, cm_all[:, SM_ML_I + h:SM_ML_I + h + 1])
        mm_last = mm_c[L - 1:L, :]
        m_news.append(b_last + mm_last)
        decays.append(jnp.exp(m_prev - mm_last))
        w_cs.append(jnp.exp(g_c - mm_last))
        m_ts.append(b_c + mm_c)
        dws.append(jnp.exp(jnp.where(tri, g_r - mm_c, NEG_BIG)))
        iws.append(jnp.exp(m_prev - mm_c))
    qms = [jnp.where(lane_head == h, q, 0.0) * scale for h in heads]
    qmbs = [qm.astype(bf16) for qm in qms]
    ss = [_bdot_nt(qmbs[h], kb) * dws[h] for h in heads]
    n_rows = [n_scr[h:h + 1, :] for h in heads]
    nums = [_bdot(ss[h], v[:, sls[h]]) + iws[h] * _bdot(qmbs[h], c_scr[:, sls[h]]) for h in heads]
    dens = [jnp.sum(ss[h], axis=1, keepdims=True) + iws[h] * jnp.sum(qms[h] * n_rows[h], axis=1, keepdims=True)
            for h in heads]
    hhs = [nums[h] / jnp.maximum(jnp.abs(dens[h]), jnp.exp(-m_ts[h])) for h in heads]
    for h in heads:
        c_scr[:, sls[h]] = decays[h] * c_scr[:, sls[h]] + _bdot_tn(kb, w_cs[h] * v[:, sls[h]])
        n_scr[h:h + 1, :] = decays[h] * n_rows[h] + jnp.sum(w_cs[h] * k, axis=0, keepdims=True)
        m_scr[h:h + 1, :] = jnp.broadcast_to(m_news[h], (1, LANES))
    for h in heads:
        ms = jnp.mean(hhs[h] * hhs[h], axis=1, keepdims=True)
        hn = hhs[h] * lax.rsqrt(ms + RMS_EPS) * g_ref[:, sls[h]]
        o_ref[:, sls[h]] = (hn * _sigmoid(og[:, sls[h]])).astype(o_ref.dtype)


def _mlstm(x2d, w_ml, B, S, i_bias, f_bias, norm_g):
    T = B * S
    L = min(ML_CHUNK, S)
    nc = S // L
    bias = jnp.zeros((1, LANES), f32)
    bias = bias.at[0, SM_ML_I:SM_ML_I + ML_HEADS].set(i_bias).at[0, SM_ML_F:SM_ML_F + ML_HEADS].set(f_bias)
    full = lambda shp: pl.BlockSpec(shp, lambda b, c: (0,) * len(shp))
    return pl.pallas_call(
        _ml_kernel,
        out_shape=jax.ShapeDtypeStruct((T, ML_V_W), bf16),
        grid=(B, nc),
        in_specs=[pl.BlockSpec((L, D_MODEL), lambda b, c: (b * nc + c, 0)),
                  pl.BlockSpec((L, D_MODEL), lambda b, c: (jnp.minimum(b * nc + c + 1, B * nc - 1), 0)),
                  full(w_ml.shape), full((1, LANES)), full((1, ML_V_W))],
        out_specs=pl.BlockSpec((L, ML_V_W), lambda b, c: (b * nc + c, 0)),
        scratch_shapes=[pltpu.VMEM((ML_QK_W, ML_V_W), f32), pltpu.VMEM((SUBLANES, ML_QK_W), f32),
                        pltpu.VMEM((SUBLANES, LANES), f32),
                        pltpu.VMEM((L, w_ml.shape[1]), f32), pltpu.VMEM((L, w_ml.shape[1]), f32)],
        compiler_params=_cparams("arbitrary", "arbitrary"),
        name="mlstm",
    )(x2d, x2d, w_ml, bias, norm_g.reshape(1, ML_V_W))


def _gd_kernel(xin_ref, w_ref, cw_ref, alog_ref, dtb_ref, g_ref, o_ref, ext, s_scr):
    R = xin_ref.shape[0]
    KW = GD_K_W
    L = GD_CHUNK
    nch = R // L
    lsh = L.bit_length() - 1
    PW = nch * L
    nsl = KW // LANES
    passes = GD_SOLVE_PASSES

    @pl.when(pl.program_id(1) == 0)
    def _():
        ext[0:SUBLANES, :] = jnp.zeros((SUBLANES, ext.shape[1]), f32)
        s_scr[...] = jnp.zeros(s_scr.shape, f32)

    pr = _project(xin_ref, w_ref)
    z = pr[:, 3 * KW:4 * KW]
    sm = pr[:, 4 * KW:]
    ext[SUBLANES:SUBLANES + R, :] = pr[:, 0:3 * KW]
    xc = (cw_ref[3:4, :] * ext[SUBLANES:SUBLANES + R, :]
          + cw_ref[2:3, :] * ext[SUBLANES - 1:SUBLANES - 1 + R, :]
          + cw_ref[1:2, :] * ext[SUBLANES - 2:SUBLANES - 2 + R, :]
          + cw_ref[0:1, :] * ext[SUBLANES - 3:SUBLANES - 3 + R, :])
    ext[0:SUBLANES, :] = ext[R:R + SUBLANES, :]
    xc = xc * _sigmoid(xc)

    g_t = -jnp.exp(alog_ref[...]) * _softplus(sm + dtb_ref[...])
    beta_t = _sigmoid(sm)
    gc_all = _cumsum_rows(g_t, L)
    gc_t = gc_all.T

    pt = lax.broadcasted_iota(jnp.int32, (L, PW), 0)
    plane = lax.broadcasted_iota(jnp.int32, (L, PW), 1)
    ps = plane & (L - 1)
    pc = plane >> lsh
    tri_p = pt >= ps
    strict_p = pt > ps
    eye_p = jnp.where(pt == ps, 1.0, 0.0)
    blk8_p = (pt >> 3) == (ps >> 3)
    cmask_b = [jnp.where(pc == c, 1.0, 0.0).astype(bf16) for c in range(nch)]

    def pick(parts):
        out = parts[nch - 1]
        for c in range(nch - 2, -1, -1):
            out = jnp.where(pc == c, parts[c], out)
        return out

    def chunks(col):
        return [col[c * L:(c + 1) * L] for c in range(nch)]

    def bd(yb):
        return jnp.concatenate([yb * cmask_b[c] for c in range(nch)], axis=0)

    def pdot(xp, yp):
        return _mdot(_split_bf16(xp, passes), [bd(p) for p in _split_bf16(yp, passes)])

    lane_r = lax.broadcasted_iota(jnp.int32, (R, LANES), 1)
    lo_r = lane_r < GD_DK
    lane_l = lax.broadcasted_iota(jnp.int32, (L, LANES), 1)
    lo_l = lane_l < GD_DK
    lane_row = lax.broadcasted_iota(jnp.int32, (1, LANES), 1)
    r2 = lax.broadcasted_iota(jnp.int32, (LANES, LANES), 0)
    c2 = lax.broadcasted_iota(jnp.int32, (LANES, LANES), 1)
    blockdiag = (r2 < GD_DK) == (c2 < GD_DK)

    def half_sums(y, lo):
        s_lo = jnp.sum(jnp.where(lo, y, 0.0), axis=1, keepdims=True)
        s_hi = jnp.sum(jnp.where(lo, 0.0, y), axis=1, keepdims=True)
        return s_lo, s_hi

    def l2n(y):
        s_lo, s_hi = half_sums(y * y, lo_r)
        return y * jnp.where(lo_r, lax.rsqrt(s_lo + RMS_EPS), lax.rsqrt(s_hi + RMS_EPS))

    qns, kns, vss, a_ps, p_bs, gcols, bcols = [], [], [], [], [], [], []
    for j in range(nsl):
        qn = l2n(xc[:, LANES * j:LANES * (j + 1)]) * (GD_DK ** -0.5)
        kn = l2n(xc[:, KW + LANES * j:KW + LANES * (j + 1)])
        qns.append(qn)
        kns.append(kn)
        vss.append(xc[:, 2 * KW + LANES * j:2 * KW + LANES * (j + 1)])
        lhs = jnp.concatenate([jnp.where(lo_r, kn, 0.0), jnp.where(lo_r, 0.0, kn),
                               jnp.where(lo_r, qn, 0.0), jnp.where(lo_r, 0.0, qn)], axis=0)
        gram = _bdot_nt(lhs, kn)
        for e in range(2):
            h = 2 * j + e
            gcol = gc_all[:, SM_GD_A + h:SM_GD_A + h + 1]
            bcol = beta_t[:, SM_GD_B + h:SM_GD_B + h + 1]
            gc_r = gc_t[SM_GD_A + h:SM_GD_A + h + 1, :]
            gam_p = jnp.exp(jnp.where(tri_p, pick(chunks(gcol)) - gc_r, NEG_BIG))
            kk_p = pick(chunks(gram[e * R:(e + 1) * R]))
            qk_p = pick(chunks(gram[(2 + e) * R:(3 + e) * R]))
            a_ps.append(jnp.where(strict_p, pick(chunks(bcol)) * gam_p * kk_p, 0.0))
            p_bs.append((gam_p * qk_p).astype(bf16))
            gcols.append(gcol)
            bcols.append(bcol)

    ads = [jnp.where(blk8_p, a, 0.0) for a in a_ps]
    a2s = [pdot(ad, ad) for ad in ads]
    a4s = [pdot(a2, a2) for a2 in a2s]
    xs = [pdot(eye_p - ad, eye_p + a2) for ad, a2 in zip(ads, a2s)]
    xs = [pdot(x, eye_p + a4) for x, a4 in zip(xs, a4s)]
    for sh in range(3, lsh):
        msk = ((pt >> (sh + 1)) == (ps >> (sh + 1))) & (((pt >> sh) & 1) == 1) & (((ps >> sh) & 1) == 0)
        ts = [pdot(jnp.where(msk, a, 0.0), x) for a, x in zip(a_ps, xs)]
        xs = [x - pdot(x, t) for x, t in zip(xs, ts)]

    us, ws, qgs, kds, gls = [], [], [], [], []
    for j in range(nsl):
        uw, egs, eds, glh = [], [], [], []
        for e in range(2):
            h = 2 * j + e
            gcol, bcol = gcols[h], bcols[h]
            eg = jnp.exp(gcol)
            rhs = jnp.concatenate([bcol * vss[j], (bcol * eg) * kns[j]], axis=1)
            uw.append(_mdot([bd(p) for p in _split_bf16(xs[h], passes)], _split_bf16(rhs, passes)))
            lasts = [gcol[c * L + L - 1:c * L + L] for c in range(nch)]
            gcl = jnp.concatenate([jnp.broadcast_to(v, (L, 1)) for v in lasts], axis=0)
            egs.append(eg)
            eds.append(jnp.exp(gcl - gcol))
            glh.append([jnp.exp(v) for v in lasts])
        us.append(jnp.where(lo_r, uw[0][:, :LANES], uw[1][:, :LANES]))
        ws.append(jnp.where(lo_r, uw[0][:, LANES:], uw[1][:, LANES:]))
        qgs.append(qns[j] * jnp.where(lo_r, egs[0], egs[1]))
        kds.append(kns[j] * jnp.where(lo_r, eds[0], eds[1]))
        gls.append([jnp.where(lane_row < GD_DK, glh[0][c], glh[1][c]) for c in range(nch)])

    states = [s_scr[j] for j in range(nsl)]
    outs = [[] for _ in range(nsl)]
    for c in range(nch):
        rs = slice(c * L, (c + 1) * L)
        for j in range(nsl):
            s_prev = states[j]
            sb = s_prev.astype(bf16)
            delta = us[j][rs] - _bdot(ws[j][rs], sb)
            db = delta.astype(bf16)
            pieces = []
            if c > 0:
                pieces.append(jnp.zeros((c * L, LANES), bf16))
            pieces.append(db)
            if c < nch - 1:
                pieces.append(jnp.zeros(((nch - 1 - c) * L, LANES), bf16))
            dpad = jnp.concatenate(pieces, axis=0) if len(pieces) > 1 else db
            intra = jnp.where(lo_l, jnp.dot(p_bs[2 * j], dpad, preferred_element_type=f32),
                              jnp.dot(p_bs[2 * j + 1], dpad, preferred_element_type=f32))
            outs[j].append(_bdot(qgs[j][rs], sb) + intra)
            upd = _bdot_tn(kds[j][rs], db)
            states[j] = gls[j][c] * s_prev + jnp.where(blockdiag, upd, 0.0)

    for j in range(nsl):
        sl = slice(LANES * j, LANES * (j + 1))
        s_scr[j] = states[j]
        o = jnp.concatenate(outs[j], axis=0) if nch > 1 else outs[j][0]
        m_lo, m_hi = half_sums(o * o, lo_r)
        inv = jnp.where(lo_r, lax.rsqrt(m_lo * (1.0 / GD_DV) + RMS_EPS), lax.rsqrt(m_hi * (1.0 / GD_DV) + RMS_EPS))
        zs = z[:, sl]
        o_ref[:, sl] = (o * inv * g_ref[:, sl] * (zs * _sigmoid(zs))).astype(o_ref.dtype)


def _gdn(x2d, w_gd, B, S, conv_w, a_log, dt_bias, norm_g):
    T = B * S
    L = min(GD_ROWS, S)
    nc = S // L
    KW = GD_K_W
    alog = jnp.zeros((1, LANES), f32).at[0, SM_GD_A:SM_GD_A + GD_HEADS].set(a_log)
    dtb = jnp.zeros((1, LANES), f32).at[0, SM_GD_A:SM_GD_A + GD_HEADS].set(dt_bias)
    g_row = jnp.tile(norm_g, GD_HEADS).reshape(1, GD_V_W)
    full = lambda shp: pl.BlockSpec(shp, lambda b, c: (0,) * len(shp))
    return pl.pallas_call(
        _gd_kernel,
        out_shape=jax.ShapeDtypeStruct((T, GD_V_W), bf16),
        grid=(B, nc),
        in_specs=[pl.BlockSpec((L, D_MODEL), lambda b, c: (b * nc + c, 0)), full(w_gd.shape),
                  full((CONV_WIDTH, 3 * KW)), full((1, LANES)), full((1, LANES)), full((1, GD_V_W))],
        out_specs=pl.BlockSpec((L, GD_V_W), lambda b, c: (b * nc + c, 0)),
        scratch_shapes=[pltpu.VMEM((L + SUBLANES, 3 * KW), f32),
                        pltpu.VMEM((KW // LANES, LANES, LANES), f32)],
        compiler_params=_cparams("parallel", "arbitrary"),
        name="gated_deltanet",
    )(x2d, w_gd, conv_w, alog, dtb, g_row)


def _merge_kernel(ha_ref, hb_ref, hc_ref, x_ref, wg_ref, wa_ref, wb_ref, wc_ref, wo_ref, lg_ref, lb_ref,
                  rw_ref, rb_ref, x1_ref, x1p_ref, pk_ref, cnt_ref, prev, carry):
    D = x_ref.shape[1]
    i = pl.program_id(0)

    @pl.when(i == 0)
    def _():
        prev[...] = jnp.zeros(prev.shape, f32)
        carry[...] = jnp.zeros(carry.shape, f32)

    route = _route_stages(prev[...], jnp.where(i > 0, 1.0, 0.0), rw_ref, rb_ref, pk_ref, cnt_ref, carry)
    xb = x_ref[...].astype(bf16)

    next(route)
    g0 = jnp.dot(xb, wg_ref[:, 0:D], preferred_element_type=f32)
    next(route)
    g1 = jnp.dot(xb, wg_ref[:, D:2 * D], preferred_element_type=f32)
    next(route)
    g2 = jnp.dot(xb, wg_ref[:, 2 * D:3 * D], preferred_element_type=f32)
    next(route)
    bra = jnp.dot(ha_ref[...], wa_ref[...], preferred_element_type=f32)
    next(route)
    brb = jnp.dot(hb_ref[...], wb_ref[...], preferred_element_type=f32)
    brc = jnp.dot(hc_ref[...], wc_ref[...], preferred_element_type=f32)
    next(route)
    merged = _sigmoid(g0) * bra + _sigmoid(g1) * brb + _sigmoid(g2) * brc
    y = jnp.dot(merged.astype(bf16), wo_ref[...], preferred_element_type=f32)
    for _ in route:
        pass
    out = _layer_norm(DN_ALPHA * x_ref[...] + y, lg_ref[...], lb_ref[...])
    x1_ref[...] = out
    x1p_ref[...] = _pack_halves(out)
    prev[...] = out


def _merge(ha, hb, hc, x2d, w_gate, wa, wb, wc, wo, lg, lb, router_w, router_b):
    T = x2d.shape[0]
    D = D_MODEL
    tm = min(512, T)
    nt = T // tm
    rw, rb = _router_params(router_w, router_b)
    full = lambda shp: pl.BlockSpec(shp, lambda i: (0,) * len(shp))
    held = lambda shp: pl.BlockSpec(shp, lambda i: (0,) * len(shp), pipeline_mode=pl.Buffered(1))
    rows = lambda w: pl.BlockSpec((tm, w), lambda i: (jnp.minimum(i, nt - 1), 0))
    routed = pl.BlockSpec((tm, LANES), lambda i: (jnp.maximum(i - 1, 0), 0))
    return pl.pallas_call(
        _merge_kernel,
        out_shape=(jax.ShapeDtypeStruct((T, D), f32), jax.ShapeDtypeStruct((T, D // 2), jnp.uint32),
                   jax.ShapeDtypeStruct((T, LANES), f32), jax.ShapeDtypeStruct((1, LANES), f32)),
        grid=(nt + 1,),
        in_specs=[rows(LRU_WIDTH), rows(ML_V_W), rows(GD_V_W), rows(D), held(w_gate.shape),
                  held((LRU_WIDTH, D)), held((ML_V_W, D)), held((GD_V_W, D)), held((D, D)),
                  full((1, D)), full((1, D)), held((D, LANES)), full((1, LANES))],
        out_specs=(rows(D), rows(D // 2), routed, full((1, LANES))),
        scratch_shapes=[pltpu.VMEM((tm, D), f32), pltpu.VMEM((1, LANES), f32)],
        compiler_params=_cparams("arbitrary"),
        name="merge_outproj_ln_route",
    )(ha, hb, hc, x2d, w_gate, wa.astype(bf16), wb.astype(bf16), wc.astype(bf16),
      wo.astype(bf16), lg.reshape(1, D), lb.reshape(1, D), rw, rb)


PK_IDX = 0
PK_GATE = 4
PK_RANK = 8


def _route_stages(x, weight, rw_ref, rb_ref, pk_ref, cnt_ref, carry):
    tr = x.shape[0]
    logits = _mdot(_split_bf16(x, 3), _split_bf16(rw_ref[...], 3)) + rb_ref[...]
    lane = lax.broadcasted_iota(jnp.int32, (tr, LANES), 1).astype(f32)
    vals = logits
    idxs, tops = [], []
    yield
    for _ in range(TOP_K):
        m = jnp.max(vals, axis=1, keepdims=True)
        idx = jnp.min(jnp.where(vals == m, lane, float(LANES)), axis=1, keepdims=True)
        idxs.append(idx)
        tops.append(m)
        vals = jnp.where(lane == idx, NEG_BIG * 2.0, vals)
        yield
    es = [jnp.exp(t - tops[0]) for t in tops]
    tot = es[0] + es[1] + es[2] + es[3]
    onehots = [lane == idx for idx in idxs]
    sel = jnp.zeros((tr, LANES), f32)
    for oh in onehots:
        sel = sel + oh.astype(f32)
    ri = lax.broadcasted_iota(jnp.int32, (tr, tr), 0)
    ci = lax.broadcasted_iota(jnp.int32, (tr, tr), 1)
    before = jnp.dot((ri > ci).astype(bf16), sel.astype(bf16), preferred_element_type=f32) + carry[...]
    carry[...] = carry[...] + weight * jnp.sum(sel, axis=0, keepdims=True)
    cnt_ref[...] = carry[...]
    yield
    packed = jnp.zeros((tr, LANES), f32)
    for kk in range(TOP_K):
        rank = jnp.sum(jnp.where(onehots[kk], before, 0.0), axis=1, keepdims=True)
        packed = jnp.where(lane == float(PK_IDX + kk), idxs[kk], packed)
        packed = jnp.where(lane == float(PK_GATE + kk), es[kk] / tot, packed)
        packed = jnp.where(lane == float(PK_RANK + kk), rank, packed)
    pk_ref[...] = packed


def _router_params(router_w, router_b):
    rw = jnp.zeros((D_MODEL, LANES), f32).at[:, :N_EXPERTS].set(router_w)
    rb = jnp.full((1, LANES), NEG_BIG, f32).at[0, :N_EXPERTS].set(router_b)
    return rw, rb


def _expert_kernel(be_ref, nu_ref, nv_ref, x_ref, w1_ref, b1_ref, w2_ref, b2_ref, o_ref, w1b, w2b):
    i = pl.program_id(0)
    used = i < nu_ref[0]
    new_expert = jnp.logical_or(i == 0, be_ref[i] != be_ref[jnp.maximum(i - 1, 0)])

    @pl.when(jnp.logical_and(used, new_expert))
    def _():
        w1b[...] = w1_ref[0, 0].astype(bf16)
        w2b[...] = w2_ref[0, 0].astype(bf16)

    @pl.when(used)
    def _():
        rows = lax.broadcasted_iota(jnp.int32, x_ref.shape, 0)
        xp = jnp.where(rows < nv_ref[i], x_ref[...], jnp.uint32(0))
        x = _unpack_halves(xp).astype(bf16)
        hdn = jnp.dot(x, w1b[...], preferred_element_type=f32) + b1_ref[0, 0]
        glu = jnp.minimum(hdn[:, :D_FF], SWIGLU_LIMIT)
        lin = jnp.clip(hdn[:, D_FF:], -SWIGLU_LIMIT, SWIGLU_LIMIT)
        act = glu * _sigmoid(SWIGLU_ALPHA * glu) * (lin + 1.0)
        y = jnp.dot(act.astype(bf16), w2b[...], preferred_element_type=f32) + b2_ref[0, 0]
        o_ref[...] = _pack_halves(y)

    @pl.when(i >= nu_ref[0])
    def _():
        o_ref[...] = jnp.zeros(o_ref.shape, o_ref.dtype)


def _experts(xs, block_e, n_used, n_valid, layer, w1, b1, w2, b2):
    P = xs.shape[0]
    nb = P // MOE_BLOCK
    D = D_MODEL
    nl = w1.shape[0]
    grid_spec = pltpu.PrefetchScalarGridSpec(
        num_scalar_prefetch=3,
        grid=(nb,),
        in_specs=[pl.BlockSpec((MOE_BLOCK, D // 2), lambda i, be, nu, nv: (i, 0)),
                  pl.BlockSpec((1, 1, D, 2 * D_FF), lambda i, be, nu, nv: (layer, be[i], 0, 0)),
                  pl.BlockSpec((1, 1, 1, 2 * D_FF), lambda i, be, nu, nv: (layer, be[i], 0, 0)),
                  pl.BlockSpec((1, 1, D_FF, D), lambda i, be, nu, nv: (layer, be[i], 0, 0)),
                  pl.BlockSpec((1, 1, 1, D), lambda i, be, nu, nv: (layer, be[i], 0, 0))],
        out_specs=pl.BlockSpec((MOE_BLOCK, D // 2), lambda i, be, nu, nv: (i, 0)),
        scratch_shapes=[pltpu.VMEM((D, 2 * D_FF), bf16), pltpu.VMEM((D_FF, D), bf16)],
    )
    return pl.pallas_call(
        _expert_kernel,
        out_shape=jax.ShapeDtypeStruct((P, D // 2), jnp.uint32),
        grid_spec=grid_spec,
        compiler_params=_cparams("arbitrary"),
        name="experts",
    )(block_e, n_used, n_valid, xs, w1, b1.reshape(nl, N_EXPERTS, 1, 2 * D_FF), w2,
      b2.reshape(nl, N_EXPERTS, 1, D))


def _combine_kernel(y0_ref, y1_ref, y2_ref, y3_ref, pk_ref, x_ref, lg_ref, lb_ref, o_ref):
    pk = pk_ref[...]
    y = pk[:, PK_GATE:PK_GATE + 1] * _unpack_halves(y0_ref[...])
    for kk, y_ref in ((1, y1_ref), (2, y2_ref), (3, y3_ref)):
        y = y + pk[:, PK_GATE + kk:PK_GATE + kk + 1] * _unpack_halves(y_ref[...])
    o_ref[...] = _layer_norm(DN_ALPHA * x_ref[...] + y, lg_ref[...], lb_ref[...])


def _combine(yg, packed, x1, lg, lb):
    T, D = x1.shape
    tm = min(1024, T)
    nt = T // tm
    full = lambda shp: pl.BlockSpec(shp, lambda i: (0,) * len(shp))
    choice = lambda kk: pl.BlockSpec((tm, D // 2), lambda i: (kk * nt + i, 0))
    return pl.pallas_call(
        _combine_kernel,
        out_shape=jax.ShapeDtypeStruct((T, D), f32),
        grid=(nt,),
        in_specs=[choice(0), choice(1), choice(2), choice(3), pl.BlockSpec((tm, LANES), lambda i: (i, 0)),
                  pl.BlockSpec((tm, D), lambda i: (i, 0)), full((1, D)), full((1, D))],
        out_specs=pl.BlockSpec((tm, D), lambda i: (i, 0)),
        compiler_params=_cparams("parallel"),
        name="combine_ln",
    )(yg, yg, yg, yg, packed, x1, lg.reshape(1, D), lb.reshape(1, D))


def _sc_mesh():
    return plsc.VectorSubcoreMesh(core_axis_name="c", subcore_axis_name="s")


def _sc_dispatch(x, posk, n_slots):
    t_tokens, w = x.shape

    @functools.partial(pl.kernel, out_type=jax.ShapeDtypeStruct((n_slots, w), x.dtype), mesh=_sc_mesh(),
                       name="sc_dispatch")
    def k(x_hbm, p_hbm, o_hbm):
        def body(x_vmem, p_vmem):
            for kk in range(TOP_K):
                pltpu.sync_copy(x_vmem, o_hbm.at[p_vmem.at[kk]])

        pltpu.emit_pipeline(
            body,
            grid=(t_tokens // SC_ROWS,),
            in_specs=[pl.BlockSpec((SC_ROWS, w), lambda i: (i, 0), pipeline_mode=pl.Buffered(1)),
                      pl.BlockSpec((TOP_K, SC_ROWS), lambda i: (0, i))],
            out_specs=[],
            core_axis_name=("c", "s"),
            dimension_semantics=(pltpu.PARALLEL,),
        )(x_hbm, p_hbm)

    return k(x, posk)


def _sc_gather(table, idx):
    n = idx.shape[0]
    w = table.shape[1]

    @functools.partial(pl.kernel, out_type=jax.ShapeDtypeStruct((n, w), table.dtype), mesh=_sc_mesh(),
                       name="sc_gather")
    def k(t_hbm, i_hbm, o_hbm):
        def body(i_vmem, o_vmem):
            pltpu.sync_copy(t_hbm.at[i_vmem.at[0]], o_vmem)

        pltpu.emit_pipeline(
            body,
            grid=(n // SC_ROWS,),
            in_specs=[pl.BlockSpec((1, SC_ROWS), lambda i: (0, i))],
            out_specs=[pl.BlockSpec((SC_ROWS, w), lambda i: (i, 0), pipeline_mode=pl.Buffered(1))],
            core_axis_name=("c", "s"),
            dimension_semantics=(pltpu.PARALLEL,),
        )(i_hbm, o_hbm)

    return k(table, idx.reshape(1, n))


def _moe(x1, x1p, packed, cnt, layer, w1, b1, w2, b2, lg, lb):
    T, D = x1.shape
    A = T * TOP_K
    idx = packed[:, PK_IDX:PK_IDX + TOP_K].astype(jnp.int32)
    rank = packed[:, PK_RANK:PK_RANK + TOP_K].astype(jnp.int32)
    counts = cnt[0, :N_EXPERTS].astype(jnp.int32)
    padded = ((counts + MOE_BLOCK - 1) // MOE_BLOCK) * MOE_BLOCK
    pad_end = jnp.cumsum(padded)
    pad_start = pad_end - padded
    posk = (pad_start[idx] + rank).T
    n_blocks = -(-A // MOE_BLOCK) + N_EXPERTS
    P = n_blocks * MOE_BLOCK
    starts = jnp.arange(n_blocks, dtype=jnp.int32) * MOE_BLOCK
    block_e = jnp.minimum(jnp.sum((pad_end[None, :] <= starts[:, None]).astype(jnp.int32), axis=1), N_EXPERTS - 1)
    n_used = (pad_end[-1:] // MOE_BLOCK).astype(jnp.int32)
    n_valid = jnp.clip(pad_start[block_e] + counts[block_e] - starts, 0, MOE_BLOCK).astype(jnp.int32)
    xs = _sc_dispatch(x1p, posk, P)
    ys = _experts(xs, block_e, n_used, n_valid, layer, w1, b1, w2, b2)
    yg = _sc_gather(ys, posk.reshape(A))
    return _combine(yg, packed, x1, lg, lb)


def _layer(x2d, B, S, p, layer, stacked):
    w = _split_w_in(p['w_in'])
    ha = _lru(x2d, w['lru'], B, S, p['lru_conv_w'], p['lru_conv_b'], p['lru_wa'], p['lru_ba'], p['lru_wx'],
              p['lru_bx'], p['lru_lambda'])
    hb = _mlstm(x2d, w['ml'], B, S, p['ml_i_bias'], p['ml_f_bias'], p['ml_norm_g'])
    hc = _gdn(x2d, w['gd'], B, S, p['gd_conv_w'], p['gd_a_log'], p['gd_dt_bias'], p['gd_norm_g'])
    x1, x1p, packed, cnt = _merge(ha, hb, hc, x2d, w['gate'], p['w_br_lru'], p['w_br_ml'], p['w_br_gd'],
                                  p['w_out'], p['ln1_g'], p['ln1_b'], p['router_w'], p['router_b'])
    return _moe(x1, x1p, packed, cnt, layer, stacked['exp_w1'], stacked['exp_b1'],
                stacked['exp_w2'], stacked['exp_b2'], p['ln2_g'], p['ln2_b'])


def kernel(x, w_in, lru_conv_w, lru_conv_b, lru_wa, lru_ba, lru_wx, lru_bx, lru_lambda, ml_i_bias, ml_f_bias, ml_norm_g, gd_conv_w, gd_a_log, gd_dt_bias, gd_norm_g, w_br_lru, w_br_ml, w_br_gd, w_out, ln1_g, ln1_b, router_w, router_b, exp_w1, exp_b1, exp_w2, exp_b2, ln2_g, ln2_b):
    B, S, D = x.shape
    params = dict(w_in=w_in, lru_conv_w=lru_conv_w, lru_conv_b=lru_conv_b, lru_wa=lru_wa, lru_ba=lru_ba,
                  lru_wx=lru_wx, lru_bx=lru_bx, lru_lambda=lru_lambda, ml_i_bias=ml_i_bias,
                  ml_f_bias=ml_f_bias, ml_norm_g=ml_norm_g, gd_conv_w=gd_conv_w, gd_a_log=gd_a_log,
                  gd_dt_bias=gd_dt_bias, gd_norm_g=gd_norm_g, w_br_lru=w_br_lru, w_br_ml=w_br_ml,
                  w_br_gd=w_br_gd, w_out=w_out, ln1_g=ln1_g, ln1_b=ln1_b, router_w=router_w,
                  router_b=router_b, ln2_g=ln2_g, ln2_b=ln2_b)
    stacked = dict(exp_w1=exp_w1, exp_b1=exp_b1, exp_w2=exp_w2, exp_b2=exp_b2)
    h = x.reshape(B * S, D)
    for l in range(w_in.shape[0]):
        h = _layer(h, B, S, {k: v[l] for k, v in params.items()}, l, stacked)
    return h.reshape(B, S, D)
```

```python
import functools

import jax
import jax.numpy as jnp
from jax import lax
from jax.experimental import pallas as pl
from jax.experimental.pallas import tpu as pltpu
from jax.experimental.pallas import tpu_sc as plsc

f32 = jnp.float32
bf16 = jnp.bfloat16

D_MODEL = 1024
LRU_WIDTH = 1024
LRU_BLOCKS = 8
LRU_C = 8.0
CONV_WIDTH = 4
ML_HEADS = 4
ML_DQK = 64
ML_DV = 128
GD_HEADS = 8
GD_DK = 64
GD_DV = 64
N_EXPERTS = 32
TOP_K = 4
D_FF = 1024
SWIGLU_LIMIT = 7.0
SWIGLU_ALPHA = 1.702
MOE_BLOCK = 512
N_BRANCH = 3
DEPTH = 2
DN_ALPHA = (2.0 * DEPTH) ** 0.25
LN_EPS = 1e-5
RMS_EPS = 1e-6

ML_QK_W = ML_HEADS * ML_DQK
ML_V_W = ML_HEADS * ML_DV
GD_K_W = GD_HEADS * GD_DK
GD_V_W = GD_HEADS * GD_DV
IN_SPLITS = (LRU_WIDTH, LRU_WIDTH, ML_QK_W, ML_QK_W, ML_V_W, ML_HEADS, ML_HEADS, ML_V_W,
             GD_K_W, GD_K_W, GD_V_W, GD_HEADS, GD_HEADS, GD_V_W, N_BRANCH * D_MODEL)

LANES = 128
SUBLANES = 8
NEG_BIG = -1e30

SM_ML_I = 0
SM_ML_F = 4
SM_GD_A = 8
SM_GD_B = 16

ML_CHUNK = 256
GD_CHUNK = 64
GD_ROWS = 256
GD_SOLVE_PASSES = 1
VMEM_LIMIT = 56 * 1024 * 1024
SC_ROWS = 128


def _cparams(*sem):
    return pltpu.CompilerParams(dimension_semantics=sem, vmem_limit_bytes=VMEM_LIMIT)


def _cummax_rows(x, seg):
    pos = lax.broadcasted_iota(jnp.int32, x.shape, 0) & (seg - 1)
    step = 1
    while step < seg:
        x = jnp.maximum(x, jnp.where(pos >= step, pltpu.roll(x, step, 0), NEG_BIG))
        step *= 2
    return x


def _cumsum_rows(x, seg):
    pos = lax.broadcasted_iota(jnp.int32, x.shape, 0) & (seg - 1)
    step = 1
    while step < seg:
        x = x + jnp.where(pos >= step, pltpu.roll(x, step, 0), 0.0)
        step *= 2
    return x


def _bdot(a, b):
    return jnp.dot(a.astype(bf16), b.astype(bf16), preferred_element_type=f32)


def _bdot_nt(a, b):
    return lax.dot_general(a.astype(bf16), b.astype(bf16), (((1,), (1,)), ((), ())),
                           preferred_element_type=f32)


def _bdot_tn(a, b):
    return lax.dot_general(a.astype(bf16), b.astype(bf16), (((0,), (0,)), ((), ())),
                           preferred_element_type=f32)


def _split_bf16(x, passes):
    hi = x.astype(bf16)
    if passes == 1:
        return (hi,)
    return (hi, (x - hi.astype(f32)).astype(bf16))


def _mdot(xs, ys):
    out = jnp.dot(xs[0], ys[0], preferred_element_type=f32)
    if len(xs) > 1:
        out = out + jnp.dot(xs[1], ys[0], preferred_element_type=f32)
        out = out + jnp.dot(xs[0], ys[1], preferred_element_type=f32)
    return out


def _sigmoid(x):
    return 1.0 / (1.0 + jnp.exp(-x))


def _softplus(x):
    return jnp.maximum(x, 0.0) + jnp.log1p(jnp.exp(-jnp.abs(x)))


def _log_sigmoid(x):
    return jnp.minimum(x, 0.0) - jnp.log1p(jnp.exp(-jnp.abs(x)))


def _pack_halves(v):
    n = v.shape[1] // 2
    hi = lax.bitcast_convert_type(v[:, :n].astype(bf16).astype(f32), jnp.uint32)
    lo = lax.bitcast_convert_type(v[:, n:].astype(bf16).astype(f32), jnp.uint32)
    return hi | (lo >> 16)


def _unpack_halves(p):
    hi = lax.bitcast_convert_type(p & jnp.uint32(0xFFFF0000), f32)
    lo = lax.bitcast_convert_type(p << 16, f32)
    return jnp.concatenate([hi, lo], axis=1)


def _layer_norm(z, g, b):
    mu = jnp.mean(z, axis=-1, keepdims=True)
    zc = z - mu
    var = jnp.mean(zc * zc, axis=-1, keepdims=True)
    return zc * lax.rsqrt(var + LN_EPS) * g + b


def _split_w_in(w):
    pts = []
    acc = 0
    for s in IN_SPLITS[:-1]:
        acc += s
        pts.append(acc)
    (lru_x, lru_y, ml_q, ml_k, ml_v, ml_i, ml_f, ml_o,
     gd_q, gd_k, gd_v, gd_a, gd_b, gd_z, gate) = jnp.split(w, pts, axis=1)
    pad = jnp.zeros((w.shape[0], LANES - 2 * ML_HEADS - 2 * GD_HEADS), w.dtype)
    small = jnp.concatenate([ml_i, ml_f, gd_a, gd_b, pad], axis=1)
    cat = lambda *cols: jnp.concatenate(cols, axis=1).astype(bf16)
    return dict(lru=cat(lru_x, lru_y), ml=cat(ml_q, ml_k, ml_v, ml_o, small),
                gd=cat(gd_q, gd_k, gd_v, gd_z, small), gate=gate.astype(bf16))


def _project(x_ref, w_ref):
    return jnp.dot(x_ref[...].astype(bf16), w_ref[...], preferred_element_type=f32)


def _lru_kernel(xin_ref, xnext_ref, w_ref, cw_ref, cb_ref, wcat_ref, ba_ref, bx_ref, lam_ref, o_ref,
                xext, pr_even, pr_odd, a_scr, u_scr, carry):
    ts = xin_ref.shape[0]
    W = a_scr.shape[1]
    bw = W // LRU_BLOCKS
    step = pl.program_id(0) * pl.num_programs(1) + pl.program_id(1)

    @pl.when(step == 0)
    def _():
        pr_even[...] = _project(xin_ref, w_ref)

    @pl.when(pl.program_id(1) == 0)
    def _():
        xext[0:SUBLANES, :] = jnp.zeros((SUBLANES, W), f32)
        carry[...] = jnp.zeros((1, W), f32)

    def tile(cur, nxt):
        x = cur[:, 0:W]
        xext[SUBLANES:SUBLANES + ts, :] = x
        xa = (cw_ref[3:4, :] * x
              + cw_ref[2:3, :] * xext[SUBLANES - 1:SUBLANES - 1 + ts, :]
              + cw_ref[1:2, :] * xext[SUBLANES - 2:SUBLANES - 2 + ts, :]
              + cw_ref[0:1, :] * xext[SUBLANES - 3:SUBLANES - 3 + ts, :]) + cb_ref[...]
        xext[0:SUBLANES, :] = xext[ts:ts + SUBLANES, :]

        xnb = xnext_ref[...].astype(bf16)
        pw = 2 * W // LRU_BLOCKS
        cdec = -LRU_C * _softplus(-lam_ref[...])
        for h in range(LRU_BLOCKS):
            sl = slice(bw * h, bw * (h + 1))
            psl = slice(pw * h, pw * (h + 1))
            nxt[:, psl] = jnp.dot(xnb, w_ref[:, psl], preferred_element_type=f32)
            xh = xa[:, sl]
            g = jnp.dot(xh.astype(bf16), wcat_ref[h], preferred_element_type=f32)
            r = _sigmoid(g[:, :bw] + ba_ref[:, sl])
            ig = _sigmoid(g[:, bw:] + bx_ref[:, sl])
            log_a = r * cdec[:, sl]
            a = jnp.exp(log_a)
            a_scr[:, sl] = a
            u_scr[:, sl] = jnp.sqrt(jnp.tanh(-log_a) * (1.0 + a * a)) * (ig * xh)

        row = lax.broadcasted_iota(jnp.int32, (SUBLANES, W), 0)

        def body(g, cr):
            off = pl.multiple_of(g * SUBLANES, SUBLANES)
            A = a_scr[pl.ds(off, SUBLANES), :]
            U = u_scr[pl.ds(off, SUBLANES), :]
            for s in (1, 2, 4):
                a_sh = pltpu.roll(A, s, 0)
                u_sh = pltpu.roll(U, s, 0)
                m = row >= s
                U = jnp.where(m, A * u_sh + U, U)
                A = jnp.where(m, A * a_sh, A)
            H = A * cr + U
            u_scr[pl.ds(off, SUBLANES), :] = H
            return H[SUBLANES - 1:SUBLANES, :]

        carry[...] = lax.fori_loop(0, ts // SUBLANES, body, carry[...], unroll=4)
        o_ref[...] = (u_scr[...] * jax.nn.gelu(cur[:, W:2 * W])).astype(o_ref.dtype)

    @pl.when(lax.rem(step, 2) == 0)
    def _():
        tile(pr_even, pr_odd)

    @pl.when(lax.rem(step, 2) == 1)
    def _():
        tile(pr_odd, pr_even)


def _lru(x2d, w_lru, B, S, cw, cb, wa, ba, wx, bx, lam):
    T = B * S
    ts = min(512, S)
    nt = S // ts
    W = LRU_WIDTH
    wcat = jnp.concatenate([wa, wx], axis=-1).astype(bf16)
    row = lambda v: v.reshape(1, W)
    full = lambda shp: pl.BlockSpec(shp, lambda b, c: (0,) * len(shp))
    last = B * nt - 1
    return pl.pallas_call(
        _lru_kernel,
        out_shape=jax.ShapeDtypeStruct((T, W), bf16),
        grid=(B, nt),
        in_specs=[pl.BlockSpec((ts, D_MODEL), lambda b, c: (b * nt + c, 0)),
                  pl.BlockSpec((ts, D_MODEL), lambda b, c: (jnp.minimum(b * nt + c + 1, last), 0)),
                  full(w_lru.shape), full((CONV_WIDTH, W)), full((1, W)), full(wcat.shape),
                  full((1, W)), full((1, W)), full((1, W))],
        out_specs=pl.BlockSpec((ts, W), lambda b, c: (b * nt + c, 0)),
        scratch_shapes=[pltpu.VMEM((ts + SUBLANES, W), f32), pltpu.VMEM((ts, 2 * W), f32),
                        pltpu.VMEM((ts, 2 * W), f32), pltpu.VMEM((ts, W), f32), pltpu.VMEM((ts, W), f32),
                        pltpu.VMEM((1, W), f32)],
        compiler_params=_cparams("arbitrary", "arbitrary"),
        name="rg_lru",
    )(x2d, x2d, w_lru, cw, row(cb), wcat, row(ba), row(bx), row(lam))


def _ml_kernel(xin_ref, xnext_ref, w_ref, bias_ref, g_ref, o_ref, c_scr, n_scr, m_scr, pr_even, pr_odd):
    step = pl.program_id(0) * pl.num_programs(1) + pl.program_id(1)

    @pl.when(step == 0)
    def _():
        pr_even[...] = _project(xin_ref, w_ref)

    @pl.when(pl.program_id(1) == 0)
    def _():
        c_scr[...] = jnp.zeros(c_scr.shape, f32)
        n_scr[...] = jnp.zeros(n_scr.shape, f32)
        m_scr[...] = jnp.zeros(m_scr.shape, f32)

    args = (xnext_ref, w_ref, bias_ref, g_ref, o_ref, c_scr, n_scr, m_scr)

    @pl.when(lax.rem(step, 2) == 0)
    def _():
        _ml_tile(pr_even, pr_odd, *args)

    @pl.when(lax.rem(step, 2) == 1)
    def _():
        _ml_tile(pr_odd, pr_even, *args)


def _ml_tile(cur, nxt, xnext_ref, w_ref, bias_ref, g_ref, o_ref, c_scr, n_scr, m_scr):
    L = cur.shape[0]
    xnb = xnext_ref[...].astype(bf16)
    pcols = [0, 2 * ML_QK_W, 2 * ML_QK_W + ML_V_W, 2 * ML_QK_W + 2 * ML_V_W, cur.shape[1]]

    def project_next(i):
        nxt[:, pcols[i]:pcols[i + 1]] = jnp.dot(xnb, w_ref[:, pcols[i]:pcols[i + 1]],
                                                preferred_element_type=f32)

    pr = cur[...]
    q = pr[:, 0:ML_QK_W]
    k = pr[:, ML_QK_W:2 * ML_QK_W]
    v = pr[:, 2 * ML_QK_W:2 * ML_QK_W + ML_V_W]
    og = pr[:, 2 * ML_QK_W + ML_V_W:2 * ML_QK_W + 2 * ML_V_W]
    sm = pr[:, 2 * ML_QK_W + 2 * ML_V_W:] + bias_ref[...]
    logf = _log_sigmoid(sm)
    ri = lax.broadcasted_iota(jnp.int32, (L, L), 0)
    ci = lax.broadcasted_iota(jnp.int32, (L, L), 1)
    tri = ri >= ci
    b_all = _cumsum_rows(logf, L)
    g_all = sm - pltpu.roll(b_all, LANES - (SM_ML_F - SM_ML_I), 1)
    cm_all = _cummax_rows(g_all, L)
    g_t = g_all.T
    kb = k.astype(bf16)
    lane_head = lax.broadcasted_iota(jnp.int32, q.shape, 1) // ML_DQK
    scale = ML_DQK ** -0.5
    heads = range(ML_HEADS)
    sls = [slice(ML_DV * h, ML_DV * (h + 1)) for h in heads]
    m_news, decays, w_cs, m_ts, dws, iws = [], [], [], [], [], []
    for h in heads:
        project_next(h)
        b_c = b_all[:, SM_ML_F + h:SM_ML_F + h + 1]
        g_c = g_all[:, SM_ML_I + h:SM_ML_I + h + 1]
        g_r = g_t[SM_ML_I + h:SM_ML_I + h + 1, :]
        b_last = b_c[L - 1:L, :]
        m_prev = m_scr[h:h + 1, 0:1]
        mm_c = jnp.maximum(m_prev, cm_all[:, SM_ML_I + h:SM_ML_I + h + 1])
        mm_last = mm_c[L - 1:L, :]
        m_news.append(b_last + mm_last)
        decays.append(jnp.exp(m_prev - mm_last))
        w_cs.append(jnp.exp(g_c - mm_last))
        m_ts.append(b_c + mm_c)
        dws.append(jnp.exp(jnp.where(tri, g_r - mm_c, NEG_BIG)))
        iws.append(jnp.exp(m_prev - mm_c))
    qms = [jnp.where(lane_head == h, q, 0.0) * scale for h in heads]
    qmbs = [qm.astype(bf16) for qm in qms]
    ss = [_bdot_nt(qmbs[h], kb) * dws[h] for h in heads]
    n_rows = [n_scr[h:h + 1, :] for h in heads]
    nums = [_bdot(ss[h], v[:, sls[h]]) + iws[h] * _bdot(qmbs[h], c_scr[:, sls[h]]) for h in heads]
    dens = [jnp.sum(ss[h], axis=1, keepdims=True) + iws[h] * jnp.sum(qms[h] * n_rows[h], axis=1, keepdims=True)
            for h in heads]
    hhs = [nums[h] / jnp.maximum(jnp.abs(dens[h]), jnp.exp(-m_ts[h])) for h in heads]
    for h in heads:
        c_scr[:, sls[h]] = decays[h] * c_scr[:, sls[h]] + _bdot_tn(kb, w_cs[h] * v[:, sls[h]])
        n_scr[h:h + 1, :] = decays[h] * n_rows[h] + jnp.sum(w_cs[h] * k, axis=0, keepdims=True)
        m_scr[h:h + 1, :] = jnp.broadcast_to(m_news[h], (1, LANES))
    for h in heads:
        ms = jnp.mean(hhs[h] * hhs[h], axis=1, keepdims=True)
        hn = hhs[h] * lax.rsqrt(ms + RMS_EPS) * g_ref[:, sls[h]]
        o_ref[:, sls[h]] = (hn * _sigmoid(og[:, sls[h]])).astype(o_ref.dtype)


def _mlstm(x2d, w_ml, B, S, i_bias, f_bias, norm_g):
    T = B * S
    L = min(ML_CHUNK, S)
    nc = S // L
    bias = jnp.zeros((1, LANES), f32)
    bias = bias.at[0, SM_ML_I:SM_ML_I + ML_HEADS].set(i_bias).at[0, SM_ML_F:SM_ML_F + ML_HEADS].set(f_bias)
    full = lambda shp: pl.BlockSpec(shp, lambda b, c: (0,) * len(shp))
    return pl.pallas_call(
        _ml_kernel,
        out_shape=jax.ShapeDtypeStruct((T, ML_V_W), bf16),
        grid=(B, nc),
        in_specs=[pl.BlockSpec((L, D_MODEL), lambda b, c: (b * nc + c, 0)),
                  pl.BlockSpec((L, D_MODEL), lambda b, c: (jnp.minimum(b * nc + c + 1, B * nc - 1), 0)),
                  full(w_ml.shape), full((1, LANES)), full((1, ML_V_W))],
        out_specs=pl.BlockSpec((L, ML_V_W), lambda b, c: (b * nc + c, 0)),
        scratch_shapes=[pltpu.VMEM((ML_QK_W, ML_V_W), f32), pltpu.VMEM((SUBLANES, ML_QK_W), f32),
                        pltpu.VMEM((SUBLANES, LANES), f32),
                        pltpu.VMEM((L, w_ml.shape[1]), f32), pltpu.VMEM((L, w_ml.shape[1]), f32)],
        compiler_params=_cparams("arbitrary", "arbitrary"),
        name="mlstm",
    )(x2d, x2d, w_ml, bias, norm_g.reshape(1, ML_V_W))


def _gd_kernel(xin_ref, w_ref, cw_ref, alog_ref, dtb_ref, g_ref, o_ref, ext, s_scr):
    R = xin_ref.shape[0]
    KW = GD_K_W
    L = GD_CHUNK
    nch = R // L
    lsh = L.bit_length() - 1
    PW = nch * L
    nsl = KW // LANES
    passes = GD_SOLVE_PASSES

    @pl.when(pl.program_id(1) == 0)
    def _():
        ext[0:SUBLANES, :] = jnp.zeros((SUBLANES, ext.shape[1]), f32)
        s_scr[...] = jnp.zeros(s_scr.shape, f32)

    pr = _project(xin_ref, w_ref)
    z = pr[:, 3 * KW:4 * KW]
    sm = pr[:, 4 * KW:]
    ext[SUBLANES:SUBLANES + R, :] = pr[:, 0:3 * KW]
    xc = (cw_ref[3:4, :] * ext[SUBLANES:SUBLANES + R, :]
          + cw_ref[2:3, :] * ext[SUBLANES - 1:SUBLANES - 1 + R, :]
          + cw_ref[1:2, :] * ext[SUBLANES - 2:SUBLANES - 2 + R, :]
          + cw_ref[0:1, :] * ext[SUBLANES - 3:SUBLANES - 3 + R, :])
    ext[0:SUBLANES, :] = ext[R:R + SUBLANES, :]
    xc = xc * _sigmoid(xc)

    g_t = -jnp.exp(alog_ref[...]) * _softplus(sm + dtb_ref[...])
    beta_t = _sigmoid(sm)
    gc_all = _cumsum_rows(g_t, L)
    gc_t = gc_all.T

    pt = lax.broadcasted_iota(jnp.int32, (L, PW), 0)
    plane = lax.broadcasted_iota(jnp.int32, (L, PW), 1)
    ps = plane & (L - 1)
    pc = plane >> lsh
    tri_p = pt >= ps
    strict_p = pt > ps
    eye_p = jnp.where(pt == ps, 1.0, 0.0)
    blk8_p = (pt >> 3) == (ps >> 3)
    cmask_b = [jnp.where(pc == c, 1.0, 0.0).astype(bf16) for c in range(nch)]

    def pick(parts):
        out = parts[nch - 1]
        for c in range(nch - 2, -1, -1):
            out = jnp.where(pc == c, parts[c], out)
        return out

    def chunks(col):
        return [col[c * L:(c + 1) * L] for c in range(nch)]

    def bd(yb):
        return jnp.concatenate([yb * cmask_b[c] for c in range(nch)], axis=0)

    def pdot(xp, yp):
        return _mdot(_split_bf16(xp, passes), [bd(p) for p in _split_bf16(yp, passes)])

    lane_r = lax.broadcasted_iota(jnp.int32, (R, LANES), 1)
    lo_r = lane_r < GD_DK
    lane_l = lax.broadcasted_iota(jnp.int32, (L, LANES), 1)
    lo_l = lane_l < GD_DK
    lane_row = lax.broadcasted_iota(jnp.int32, (1, LANES), 1)
    r2 = lax.broadcasted_iota(jnp.int32, (LANES, LANES), 0)
    c2 = lax.broadcasted_iota(jnp.int32, (LANES, LANES), 1)
    blockdiag = (r2 < GD_DK) == (c2 < GD_DK)

    def half_sums(y, lo):
        s_lo = jnp.sum(jnp.where(lo, y, 0.0), axis=1, keepdims=True)
        s_hi = jnp.sum(jnp.where(lo, 0.0, y), axis=1, keepdims=True)
        return s_lo, s_hi

    def l2n(y):
        s_lo, s_hi = half_sums(y * y, lo_r)
        return y * jnp.where(lo_r, lax.rsqrt(s_lo + RMS_EPS), lax.rsqrt(s_hi + RMS_EPS))

    qns, kns, vss, a_ps, p_bs, gcols, bcols = [], [], [], [], [], [], []
    for j in range(nsl):
        qn = l2n(xc[:, LANES * j:LANES * (j + 1)]) * (GD_DK ** -0.5)
        kn = l2n(xc[:, KW + LANES * j:KW + LANES * (j + 1)])
        qns.append(qn)
        kns.append(kn)
        vss.append(xc[:, 2 * KW + LANES * j:2 * KW + LANES * (j + 1)])
        lhs = jnp.concatenate([jnp.where(lo_r, kn, 0.0), jnp.where(lo_r, 0.0, kn),
                               jnp.where(lo_r, qn, 0.0), jnp.where(lo_r, 0.0, qn)], axis=0)
        gram = _bdot_nt(lhs, kn)
        for e in range(2):
            h = 2 * j + e
            gcol = gc_all[:, SM_GD_A + h:SM_GD_A + h + 1]
            bcol = beta_t[:, SM_GD_B + h:SM_GD_B + h + 1]
            gc_r = gc_t[SM_GD_A + h:SM_GD_A + h + 1, :]
            gam_p = jnp.exp(jnp.where(tri_p, pick(chunks(gcol)) - gc_r, NEG_BIG))
            kk_p = pick(chunks(gram[e * R:(e + 1) * R]))
            qk_p = pick(chunks(gram[(2 + e) * R:(3 + e) * R]))
            a_ps.append(jnp.where(strict_p, pick(chunks(bcol)) * gam_p * kk_p, 0.0))
            p_bs.append((gam_p * qk_p).astype(bf16))
            gcols.append(gcol)
            bcols.append(bcol)

    ads = [jnp.where(blk8_p, a, 0.0) for a in a_ps]
    a2s = [pdot(ad, ad) for ad in ads]
    a4s = [pdot(a2, a2) for a2 in a2s]
    xs = [pdot(eye_p - ad, eye_p + a2) for ad, a2 in zip(ads, a2s)]
    xs = [pdot(x, eye_p + a4) for x, a4 in zip(xs, a4s)]
    for sh in range(3, lsh):
        msk = ((pt >> (sh + 1)) == (ps >> (sh + 1))) & (((pt >> sh) & 1) == 1) & (((ps >> sh) & 1) == 0)
        ts = [pdot(jnp.where(msk, a, 0.0), x) for a, x in zip(a_ps, xs)]
        xs = [x - pdot(x, t) for x, t in zip(xs, ts)]

    us, ws, qgs, kds, gls = [], [], [], [], []
    for j in range(nsl):
        uw, egs, eds, glh = [], [], [], []
        for e in range(2):
            h = 2 * j + e
            gcol, bcol = gcols[h], bcols[h]
            eg = jnp.exp(gcol)
            rhs = jnp.concatenate([bcol * vss[j], (bcol * eg) * kns[j]], axis=1)
            uw.append(_mdot([bd(p) for p in _split_bf16(xs[h], passes)], _split_bf16(rhs, passes)))
            lasts = [gcol[c * L + L - 1:c * L + L] for c in range(nch)]
            gcl = jnp.concatenate([jnp.broadcast_to(v, (L, 1)) for v in lasts], axis=0)
            egs.append(eg)
            eds.append(jnp.exp(gcl - gcol))
            glh.append([jnp.exp(v) for v in lasts])
        us.append(jnp.where(lo_r, uw[0][:, :LANES], uw[1][:, :LANES]))
        ws.append(jnp.where(lo_r, uw[0][:, LANES:], uw[1][:, LANES:]))
        qgs.append(qns[j] * jnp.where(lo_r, egs[0], egs[1]))
        kds.append(kns[j] * jnp.where(lo_r, eds[0], eds[1]))
        gls.append([jnp.where(lane_row < GD_DK, glh[0][c], glh[1][c]) for c in range(nch)])

    stacked = [jnp.concatenate([p_bs[2 * j], p_bs[2 * j + 1], kds[j].T.astype(bf16)], axis=0)
               for j in range(nsl)]
    states = [s_scr[j] for j in range(nsl)]
    outs = [[] for _ in range(nsl)]
    for c in range(nch):
        rs = slice(c * L, (c + 1) * L)
        for j in range(nsl):
            s_prev = states[j]
            ws_qs = _bdot(jnp.concatenate([ws[j][rs], qgs[j][rs]], axis=0), s_prev)
            db = (us[j][rs] - ws_qs[0:L]).astype(bf16)
            pieces = []
            if c > 0:
                pieces.append(jnp.zeros((c * L, LANES), bf16))
            pieces.append(db)
            if c < nch - 1:
                pieces.append(jnp.zeros(((nch - 1 - c) * L, LANES), bf16))
            dpad = jnp.concatenate(pieces, axis=0) if len(pieces) > 1 else db
            pd_upd = jnp.dot(stacked[j], dpad, preferred_element_type=f32)
            outs[j].append(ws_qs[L:2 * L] + jnp.where(lo_l, pd_upd[0:L], pd_upd[L:2 * L]))
            states[j] = gls[j][c] * s_prev + jnp.where(blockdiag, pd_upd[2 * L:], 0.0)

    for j in range(nsl):
        sl = slice(LANES * j, LANES * (j + 1))
        s_scr[j] = states[j]
        o = jnp.concatenate(outs[j], axis=0) if nch > 1 else outs[j][0]
        m_lo, m_hi = half_sums(o * o, lo_r)
        inv = jnp.where(lo_r, lax.rsqrt(m_lo * (1.0 / GD_DV) + RMS_EPS), lax.rsqrt(m_hi * (1.0 / GD_DV) + RMS_EPS))
        zs = z[:, sl]
        o_ref[:, sl] = (o * inv * g_ref[:, sl] * (zs * _sigmoid(zs))).astype(o_ref.dtype)


def _gdn(x2d, w_gd, B, S, conv_w, a_log, dt_bias, norm_g):
    T = B * S
    L = min(GD_ROWS, S)
    nc = S // L
    KW = GD_K_W
    alog = jnp.zeros((1, LANES), f32).at[0, SM_GD_A:SM_GD_A + GD_HEADS].set(a_log)
    dtb = jnp.zeros((1, LANES), f32).at[0, SM_GD_A:SM_GD_A + GD_HEADS].set(dt_bias)
    g_row = jnp.tile(norm_g, GD_HEADS).reshape(1, GD_V_W)
    full = lambda shp: pl.BlockSpec(shp, lambda b, c: (0,) * len(shp))
    return pl.pallas_call(
        _gd_kernel,
        out_shape=jax.ShapeDtypeStruct((T, GD_V_W), bf16),
        grid=(B, nc),
        in_specs=[pl.BlockSpec((L, D_MODEL), lambda b, c: (b * nc + c, 0)), full(w_gd.shape),
                  full((CONV_WIDTH, 3 * KW)), full((1, LANES)), full((1, LANES)), full((1, GD_V_W))],
        out_specs=pl.BlockSpec((L, GD_V_W), lambda b, c: (b * nc + c, 0)),
        scratch_shapes=[pltpu.VMEM((L + SUBLANES, 3 * KW), f32),
                        pltpu.VMEM((KW // LANES, LANES, LANES), f32)],
        compiler_params=_cparams("parallel", "arbitrary"),
        name="gated_deltanet",
    )(x2d, w_gd, conv_w, alog, dtb, g_row)


def _merge_kernel(ha_ref, hb_ref, hc_ref, x_ref, wg_ref, wa_ref, wb_ref, wc_ref, wo_ref, lg_ref, lb_ref,
                  rw_ref, rb_ref, x1_ref, x1p_ref, pk_ref, cnt_ref, prev, carry):
    D = x_ref.shape[1]
    i = pl.program_id(0)

    @pl.when(i == 0)
    def _():
        prev[...] = jnp.zeros(prev.shape, f32)
        carry[...] = jnp.zeros(carry.shape, f32)

    route = _route_stages(prev[...], jnp.where(i > 0, 1.0, 0.0), rw_ref, rb_ref, pk_ref, cnt_ref, carry)
    xb = x_ref[...].astype(bf16)

    next(route)
    g0 = jnp.dot(xb, wg_ref[:, 0:D], preferred_element_type=f32)
    next(route)
    g1 = jnp.dot(xb, wg_ref[:, D:2 * D], preferred_element_type=f32)
    next(route)
    g2 = jnp.dot(xb, wg_ref[:, 2 * D:3 * D], preferred_element_type=f32)
    next(route)
    bra = jnp.dot(ha_ref[...], wa_ref[...], preferred_element_type=f32)
    next(route)
    brb = jnp.dot(hb_ref[...], wb_ref[...], preferred_element_type=f32)
    brc = jnp.dot(hc_ref[...], wc_ref[...], preferred_element_type=f32)
    next(route)
    merged = _sigmoid(g0) * bra + _sigmoid(g1) * brb + _sigmoid(g2) * brc
    y = jnp.dot(merged.astype(bf16), wo_ref[...], preferred_element_type=f32)
    for _ in route:
        pass
    out = _layer_norm(DN_ALPHA * x_ref[...] + y, lg_ref[...], lb_ref[...])
    x1_ref[...] = out
    x1p_ref[...] = _pack_halves(out)
    prev[...] = out


def _merge(ha, hb, hc, x2d, w_gate, wa, wb, wc, wo, lg, lb, router_w, router_b):
    T = x2d.shape[0]
    D = D_MODEL
    tm = min(512, T)
    nt = T // tm
    rw, rb = _router_params(router_w, router_b)
    full = lambda shp: pl.BlockSpec(shp, lambda i: (0,) * len(shp))
    held = lambda shp: pl.BlockSpec(shp, lambda i: (0,) * len(shp), pipeline_mode=pl.Buffered(1))
    rows = lambda w: pl.BlockSpec((tm, w), lambda i: (jnp.minimum(i, nt - 1), 0))
    routed = pl.BlockSpec((tm, LANES), lambda i: (jnp.maximum(i - 1, 0), 0))
    return pl.pallas_call(
        _merge_kernel,
        out_shape=(jax.ShapeDtypeStruct((T, D), f32), jax.ShapeDtypeStruct((T, D // 2), jnp.uint32),
                   jax.ShapeDtypeStruct((T, LANES), f32), jax.ShapeDtypeStruct((1, LANES), f32)),
        grid=(nt + 1,),
        in_specs=[rows(LRU_WIDTH), rows(ML_V_W), rows(GD_V_W), rows(D), held(w_gate.shape),
                  held((LRU_WIDTH, D)), held((ML_V_W, D)), held((GD_V_W, D)), held((D, D)),
                  full((1, D)), full((1, D)), held((D, LANES)), full((1, LANES))],
        out_specs=(rows(D), rows(D // 2), routed, full((1, LANES))),
        scratch_shapes=[pltpu.VMEM((tm, D), f32), pltpu.VMEM((1, LANES), f32)],
        compiler_params=_cparams("arbitrary"),
        name="merge_outproj_ln_route",
    )(ha, hb, hc, x2d, w_gate, wa.astype(bf16), wb.astype(bf16), wc.astype(bf16),
      wo.astype(bf16), lg.reshape(1, D), lb.reshape(1, D), rw, rb)


PK_IDX = 0
PK_GATE = 4
PK_RANK = 8


def _route_stages(x, weight, rw_ref, rb_ref, pk_ref, cnt_ref, carry):
    tr = x.shape[0]
    logits = _mdot(_split_bf16(x, 3), _split_bf16(rw_ref[...], 3)) + rb_ref[...]
    lane = lax.broadcasted_iota(jnp.int32, (tr, LANES), 1).astype(f32)
    vals = logits
    idxs, tops = [], []
    yield
    for _ in range(TOP_K):
        m = jnp.max(vals, axis=1, keepdims=True)
        idx = jnp.min(jnp.where(vals == m, lane, float(LANES)), axis=1, keepdims=True)
        idxs.append(idx)
        tops.append(m)
        vals = jnp.where(lane == idx, NEG_BIG * 2.0, vals)
        yield
    es = [jnp.exp(t - tops[0]) for t in tops]
    tot = es[0] + es[1] + es[2] + es[3]
    onehots = [lane == idx for idx in idxs]
    sel = jnp.zeros((tr, LANES), f32)
    for oh in onehots:
        sel = sel + oh.astype(f32)
    ri = lax.broadcasted_iota(jnp.int32, (tr, tr), 0)
    ci = lax.broadcasted_iota(jnp.int32, (tr, tr), 1)
    before = jnp.dot((ri > ci).astype(bf16), sel.astype(bf16), preferred_element_type=f32) + carry[...]
    carry[...] = carry[...] + weight * jnp.sum(sel, axis=0, keepdims=True)
    cnt_ref[...] = carry[...]
    yield
    packed = jnp.zeros((tr, LANES), f32)
    for kk in range(TOP_K):
        rank = jnp.sum(jnp.where(onehots[kk], before, 0.0), axis=1, keepdims=True)
        packed = jnp.where(lane == float(PK_IDX + kk), idxs[kk], packed)
        packed = jnp.where(lane == float(PK_GATE + kk), es[kk] / tot, packed)
        packed = jnp.where(lane == float(PK_RANK + kk), rank, packed)
    pk_ref[...] = packed


def _router_params(router_w, router_b):
    rw = jnp.zeros((D_MODEL, LANES), f32).at[:, :N_EXPERTS].set(router_w)
    rb = jnp.full((1, LANES), NEG_BIG, f32).at[0, :N_EXPERTS].set(router_b)
    return rw, rb


def _expert_kernel(be_ref, nu_ref, nv_ref, x_ref, w1_ref, b1_ref, w2_ref, b2_ref, o_ref, w1b, w2b):
    i = pl.program_id(0)
    used = i < nu_ref[0]
    new_expert = jnp.logical_or(i == 0, be_ref[i] != be_ref[jnp.maximum(i - 1, 0)])

    @pl.when(jnp.logical_and(used, new_expert))
    def _():
        w1b[...] = w1_ref[0, 0].astype(bf16)
        w2b[...] = w2_ref[0, 0].astype(bf16)

    @pl.when(used)
    def _():
        rows = lax.broadcasted_iota(jnp.int32, x_ref.shape, 0)
        xp = jnp.where(rows < nv_ref[i], x_ref[...], jnp.uint32(0))
        x = _unpack_halves(xp).astype(bf16)
        hdn = jnp.dot(x, w1b[...], preferred_element_type=f32) + b1_ref[0, 0]
        glu = jnp.minimum(hdn[:, :D_FF], SWIGLU_LIMIT)
        lin = jnp.clip(hdn[:, D_FF:], -SWIGLU_LIMIT, SWIGLU_LIMIT)
        act = glu * _sigmoid(SWIGLU_ALPHA * glu) * (lin + 1.0)
        y = jnp.dot(act.astype(bf16), w2b[...], preferred_element_type=f32) + b2_ref[0, 0]
        o_ref[...] = _pack_halves(y)

    @pl.when(i >= nu_ref[0])
    def _():
        o_ref[...] = jnp.zeros(o_ref.shape, o_ref.dtype)


def _experts(xs, block_e, n_used, n_valid, layer, w1, b1, w2, b2):
    P = xs.shape[0]
    nb = P // MOE_BLOCK
    D = D_MODEL
    nl = w1.shape[0]
    grid_spec = pltpu.PrefetchScalarGridSpec(
        num_scalar_prefetch=3,
        grid=(nb,),
        in_specs=[pl.BlockSpec((MOE_BLOCK, D // 2), lambda i, be, nu, nv: (i, 0)),
                  pl.BlockSpec((1, 1, D, 2 * D_FF), lambda i, be, nu, nv: (layer, be[i], 0, 0)),
                  pl.BlockSpec((1, 1, 1, 2 * D_FF), lambda i, be, nu, nv: (layer, be[i], 0, 0)),
                  pl.BlockSpec((1, 1, D_FF, D), lambda i, be, nu, nv: (layer, be[i], 0, 0)),
                  pl.BlockSpec((1, 1, 1, D), lambda i, be, nu, nv: (layer, be[i], 0, 0))],
        out_specs=pl.BlockSpec((MOE_BLOCK, D // 2), lambda i, be, nu, nv: (i, 0)),
        scratch_shapes=[pltpu.VMEM((D, 2 * D_FF), bf16), pltpu.VMEM((D_FF, D), bf16)],
    )
    return pl.pallas_call(
        _expert_kernel,
        out_shape=jax.ShapeDtypeStruct((P, D // 2), jnp.uint32),
        grid_spec=grid_spec,
        compiler_params=_cparams("arbitrary"),
        name="experts",
    )(block_e, n_used, n_valid, xs, w1, b1.reshape(nl, N_EXPERTS, 1, 2 * D_FF), w2,
      b2.reshape(nl, N_EXPERTS, 1, D))


def _combine_kernel(y0_ref, y1_ref, y2_ref, y3_ref, pk_ref, x_ref, lg_ref, lb_ref, o_ref):
    pk = pk_ref[...]
    y = pk[:, PK_GATE:PK_GATE + 1] * _unpack_halves(y0_ref[...])
    for kk, y_ref in ((1, y1_ref), (2, y2_ref), (3, y3_ref)):
        y = y + pk[:, PK_GATE + kk:PK_GATE + kk + 1] * _unpack_halves(y_ref[...])
    o_ref[...] = _layer_norm(DN_ALPHA * x_ref[...] + y, lg_ref[...], lb_ref[...])


def _combine(yg, packed, x1, lg, lb):
    T, D = x1.shape
    tm = min(1024, T)
    nt = T // tm
    full = lambda shp: pl.BlockSpec(shp, lambda i: (0,) * len(shp))
    choice = lambda kk: pl.BlockSpec((tm, D // 2), lambda i: (kk * nt + i, 0))
    return pl.pallas_call(
        _combine_kernel,
        out_shape=jax.ShapeDtypeStruct((T, D), f32),
        grid=(nt,),
        in_specs=[choice(0), choice(1), choice(2), choice(3), pl.BlockSpec((tm, LANES), lambda i: (i, 0)),
                  pl.BlockSpec((tm, D), lambda i: (i, 0)), full((1, D)), full((1, D))],
        out_specs=pl.BlockSpec((tm, D), lambda i: (i, 0)),
        compiler_params=_cparams("parallel"),
        name="combine_ln",
    )(yg, yg, yg, yg, packed, x1, lg.reshape(1, D), lb.reshape(1, D))


def _sc_mesh():
    return plsc.VectorSubcoreMesh(core_axis_name="c", subcore_axis_name="s")


def _sc_dispatch(x, posk, n_slots):
    t_tokens, w = x.shape

    @functools.partial(pl.kernel, out_type=jax.ShapeDtypeStruct((n_slots, w), x.dtype), mesh=_sc_mesh(),
                       name="sc_dispatch")
    def k(x_hbm, p_hbm, o_hbm):
        def body(x_vmem, p_vmem):
            for kk in range(TOP_K):
                pltpu.sync_copy(x_vmem, o_hbm.at[p_vmem.at[kk]])

        pltpu.emit_pipeline(
            body,
            grid=(t_tokens // SC_ROWS,),
            in_specs=[pl.BlockSpec((SC_ROWS, w), lambda i: (i, 0), pipeline_mode=pl.Buffered(1)),
                      pl.BlockSpec((TOP_K, SC_ROWS), lambda i: (0, i))],
            out_specs=[],
            core_axis_name=("c", "s"),
            dimension_semantics=(pltpu.PARALLEL,),
        )(x_hbm, p_hbm)

    return k(x, posk)


def _sc_gather(table, idx):
    n = idx.shape[0]
    w = table.shape[1]

    @functools.partial(pl.kernel, out_type=jax.ShapeDtypeStruct((n, w), table.dtype), mesh=_sc_mesh(),
                       name="sc_gather")
    def k(t_hbm, i_hbm, o_hbm):
        def body(i_vmem, o_vmem):
            pltpu.sync_copy(t_hbm.at[i_vmem.at[0]], o_vmem)

        pltpu.emit_pipeline(
            body,
            grid=(n // SC_ROWS,),
            in_specs=[pl.BlockSpec((1, SC_ROWS), lambda i: (0, i))],
            out_specs=[pl.BlockSpec((SC_ROWS, w), lambda i: (i, 0), pipeline_mode=pl.Buffered(1))],
            core_axis_name=("c", "s"),
            dimension_semantics=(pltpu.PARALLEL,),
        )(i_hbm, o_hbm)

    return k(table, idx.reshape(1, n))


def _moe(x1, x1p, packed, cnt, layer, w1, b1, w2, b2, lg, lb):
    T, D = x1.shape
    A = T * TOP_K
    idx = packed[:, PK_IDX:PK_IDX + TOP_K].astype(jnp.int32)
    rank = packed[:, PK_RANK:PK_RANK + TOP_K].astype(jnp.int32)
    counts = cnt[0, :N_EXPERTS].astype(jnp.int32)
    padded = ((counts + MOE_BLOCK - 1) // MOE_BLOCK) * MOE_BLOCK
    pad_end = jnp.cumsum(padded)
    pad_start = pad_end - padded
    posk = (pad_start[idx] + rank).T
    n_blocks = -(-A // MOE_BLOCK) + N_EXPERTS
    P = n_blocks * MOE_BLOCK
    starts = jnp.arange(n_blocks, dtype=jnp.int32) * MOE_BLOCK
    block_e = jnp.minimum(jnp.sum((pad_end[None, :] <= starts[:, None]).astype(jnp.int32), axis=1), N_EXPERTS - 1)
    n_used = (pad_end[-1:] // MOE_BLOCK).astype(jnp.int32)
    n_valid = jnp.clip(pad_start[block_e] + counts[block_e] - starts, 0, MOE_BLOCK).astype(jnp.int32)
    xs = _sc_dispatch(x1p, posk, P)
    ys = _experts(xs, block_e, n_used, n_valid, layer, w1, b1, w2, b2)
    yg = _sc_gather(ys, posk.reshape(A))
    return _combine(yg, packed, x1, lg, lb)


def _layer(x2d, B, S, p, layer, stacked):
    w = _split_w_in(p['w_in'])
    ha = _lru(x2d, w['lru'], B, S, p['lru_conv_w'], p['lru_conv_b'], p['lru_wa'], p['lru_ba'], p['lru_wx'],
              p['lru_bx'], p['lru_lambda'])
    hb = _mlstm(x2d, w['ml'], B, S, p['ml_i_bias'], p['ml_f_bias'], p['ml_norm_g'])
    hc = _gdn(x2d, w['gd'], B, S, p['gd_conv_w'], p['gd_a_log'], p['gd_dt_bias'], p['gd_norm_g'])
    x1, x1p, packed, cnt = _merge(ha, hb, hc, x2d, w['gate'], p['w_br_lru'], p['w_br_ml'], p['w_br_gd'],
                                  p['w_out'], p['ln1_g'], p['ln1_b'], p['router_w'], p['router_b'])
    return _moe(x1, x1p, packed, cnt, layer, stacked['exp_w1'], stacked['exp_b1'],
                stacked['exp_w2'], stacked['exp_b2'], p['ln2_g'], p['ln2_b'])


def kernel(x, w_in, lru_conv_w, lru_conv_b, lru_wa, lru_ba, lru_wx, lru_bx, lru_lambda, ml_i_bias, ml_f_bias, ml_norm_g, gd_conv_w, gd_a_log, gd_dt_bias, gd_norm_g, w_br_lru, w_br_ml, w_br_gd, w_out, ln1_g, ln1_b, router_w, router_b, exp_w1, exp_b1, exp_w2, exp_b2, ln2_g, ln2_b):
    B, S, D = x.shape
    params = dict(w_in=w_in, lru_conv_w=lru_conv_w, lru_conv_b=lru_conv_b, lru_wa=lru_wa, lru_ba=lru_ba,
                  lru_wx=lru_wx, lru_bx=lru_bx, lru_lambda=lru_lambda, ml_i_bias=ml_i_bias,
                  ml_f_bias=ml_f_bias, ml_norm_g=ml_norm_g, gd_conv_w=gd_conv_w, gd_a_log=gd_a_log,
                  gd_dt_bias=gd_dt_bias, gd_norm_g=gd_norm_g, w_br_lru=w_br_lru, w_br_ml=w_br_ml,
                  w_br_gd=w_br_gd, w_out=w_out, ln1_g=ln1_g, ln1_b=ln1_b, router_w=router_w,
                  router_b=router_b, ln2_g=ln2_g, ln2_b=ln2_b)
    stacked = dict(exp_w1=exp_w1, exp_b1=exp_b1, exp_w2=exp_w2, exp_b2=exp_b2)
    h = x.reshape(B * S, D)
    for l in range(w_in.shape[0]):
        h = _layer(h, B, S, {k: v[l] for k, v in params.items()}, l, stacked)
    return h.reshape(B, S, D)
```

```python
import functools

import jax
import jax.numpy as jnp
from jax import lax
from jax.experimental import pallas as pl
from jax.experimental.pallas import tpu as pltpu
from jax.experimental.pallas import tpu_sc as plsc

f32 = jnp.float32
bf16 = jnp.bfloat16

D_MODEL = 1024
LRU_WIDTH = 1024
LRU_BLOCKS = 8
LRU_C = 8.0
CONV_WIDTH = 4
ML_HEADS = 4
ML_DQK = 64
ML_DV = 128
GD_HEADS = 8
GD_DK = 64
GD_DV = 64
N_EXPERTS = 32
TOP_K = 4
D_FF = 1024
SWIGLU_LIMIT = 7.0
SWIGLU_ALPHA = 1.702
MOE_BLOCK = 512
N_BRANCH = 3
DEPTH = 2
DN_ALPHA = (2.0 * DEPTH) ** 0.25
LN_EPS = 1e-5
RMS_EPS = 1e-6

ML_QK_W = ML_HEADS * ML_DQK
ML_V_W = ML_HEADS * ML_DV
GD_K_W = GD_HEADS * GD_DK
GD_V_W = GD_HEADS * GD_DV
IN_SPLITS = (LRU_WIDTH, LRU_WIDTH, ML_QK_W, ML_QK_W, ML_V_W, ML_HEADS, ML_HEADS, ML_V_W,
             GD_K_W, GD_K_W, GD_V_W, GD_HEADS, GD_HEADS, GD_V_W, N_BRANCH * D_MODEL)

LANES = 128
SUBLANES = 8
NEG_BIG = -1e30

SM_ML_I = 0
SM_ML_F = 4
SM_GD_A = 8
SM_GD_B = 16

ML_CHUNK = 256
GD_CHUNK = 64
GD_ROWS = 256
GD_SOLVE_PASSES = 1
VMEM_LIMIT = 56 * 1024 * 1024
SC_ROWS = 128


def _cparams(*sem):
    return pltpu.CompilerParams(dimension_semantics=sem, vmem_limit_bytes=VMEM_LIMIT)


def _cummax_rows(x, seg):
    pos = lax.broadcasted_iota(jnp.int32, x.shape, 0) & (seg - 1)
    step = 1
    while step < seg:
        x = jnp.maximum(x, jnp.where(pos >= step, pltpu.roll(x, step, 0), NEG_BIG))
        step *= 2
    return x


def _cumsum_rows(x, seg):
    pos = lax.broadcasted_iota(jnp.int32, x.shape, 0) & (seg - 1)
    step = 1
    while step < seg:
        x = x + jnp.where(pos >= step, pltpu.roll(x, step, 0), 0.0)
        step *= 2
    return x


def _bdot(a, b):
    return jnp.dot(a.astype(bf16), b.astype(bf16), preferred_element_type=f32)


def _bdot_nt(a, b):
    return lax.dot_general(a.astype(bf16), b.astype(bf16), (((1,), (1,)), ((), ())),
                           preferred_element_type=f32)


def _bdot_tn(a, b):
    return lax.dot_general(a.astype(bf16), b.astype(bf16), (((0,), (0,)), ((), ())),
                           preferred_element_type=f32)


def _split_bf16(x, passes):
    hi = x.astype(bf16)
    if passes == 1:
        return (hi,)
    return (hi, (x - hi.astype(f32)).astype(bf16))


def _mdot(xs, ys):
    out = jnp.dot(xs[0], ys[0], preferred_element_type=f32)
    if len(xs) > 1:
        out = out + jnp.dot(xs[1], ys[0], preferred_element_type=f32)
        out = out + jnp.dot(xs[0], ys[1], preferred_element_type=f32)
    return out


def _sigmoid(x):
    return 1.0 / (1.0 + jnp.exp(-x))


def _softplus(x):
    return jnp.maximum(x, 0.0) + jnp.log1p(jnp.exp(-jnp.abs(x)))


def _log_sigmoid(x):
    return jnp.minimum(x, 0.0) - jnp.log1p(jnp.exp(-jnp.abs(x)))


def _pack_halves(v):
    n = v.shape[1] // 2
    hi = lax.bitcast_convert_type(v[:, :n].astype(bf16).astype(f32), jnp.uint32)
    lo = lax.bitcast_convert_type(v[:, n:].astype(bf16).astype(f32), jnp.uint32)
    return hi | (lo >> 16)


def _unpack_halves(p):
    hi = lax.bitcast_convert_type(p & jnp.uint32(0xFFFF0000), f32)
    lo = lax.bitcast_convert_type(p << 16, f32)
    return jnp.concatenate([hi, lo], axis=1)


def _layer_norm(z, g, b):
    mu = jnp.mean(z, axis=-1, keepdims=True)
    zc = z - mu
    var = jnp.mean(zc * zc, axis=-1, keepdims=True)
    return zc * lax.rsqrt(var + LN_EPS) * g + b


def _split_w_in(w):
    pts = []
    acc = 0
    for s in IN_SPLITS[:-1]:
        acc += s
        pts.append(acc)
    (lru_x, lru_y, ml_q, ml_k, ml_v, ml_i, ml_f, ml_o,
     gd_q, gd_k, gd_v, gd_a, gd_b, gd_z, gate) = jnp.split(w, pts, axis=1)
    pad = jnp.zeros((w.shape[0], LANES - 2 * ML_HEADS - 2 * GD_HEADS), w.dtype)
    small = jnp.concatenate([ml_i, ml_f, gd_a, gd_b, pad], axis=1)
    cat = lambda *cols: jnp.concatenate(cols, axis=1).astype(bf16)
    return dict(lru=cat(lru_x, lru_y), ml=cat(ml_q, ml_k, ml_v, ml_o, small),
                gd=cat(gd_q, gd_k, gd_v, gd_z, small), gate=gate.astype(bf16))


def _project(x_ref, w_ref):
    return jnp.dot(x_ref[...].astype(bf16), w_ref[...], preferred_element_type=f32)


def _lru_kernel(xin_ref, xnext_ref, w_ref, cw_ref, cb_ref, wcat_ref, ba_ref, bx_ref, lam_ref, o_ref,
                xext, pr_even, pr_odd, a_scr, u_scr, carry):
    ts = xin_ref.shape[0]
    W = a_scr.shape[1]
    bw = W // LRU_BLOCKS
    step = pl.program_id(0) * pl.num_programs(1) + pl.program_id(1)

    @pl.when(step == 0)
    def _():
        pr_even[...] = _project(xin_ref, w_ref)

    @pl.when(pl.program_id(1) == 0)
    def _():
        xext[0:SUBLANES, :] = jnp.zeros((SUBLANES, W), f32)
        carry[...] = jnp.zeros((1, W), f32)

    def tile(cur, nxt):
        x = cur[:, 0:W]
        xext[SUBLANES:SUBLANES + ts, :] = x
        xa = (cw_ref[3:4, :] * x
              + cw_ref[2:3, :] * xext[SUBLANES - 1:SUBLANES - 1 + ts, :]
              + cw_ref[1:2, :] * xext[SUBLANES - 2:SUBLANES - 2 + ts, :]
              + cw_ref[0:1, :] * xext[SUBLANES - 3:SUBLANES - 3 + ts, :]) + cb_ref[...]
        xext[0:SUBLANES, :] = xext[ts:ts + SUBLANES, :]

        xnb = xnext_ref[...].astype(bf16)
        pw = 2 * W // LRU_BLOCKS
        cdec = -LRU_C * _softplus(-lam_ref[...])
        for h in range(LRU_BLOCKS):
            sl = slice(bw * h, bw * (h + 1))
            psl = slice(pw * h, pw * (h + 1))
            nxt[:, psl] = jnp.dot(xnb, w_ref[:, psl], preferred_element_type=f32)
            xh = xa[:, sl]
            g = jnp.dot(xh.astype(bf16), wcat_ref[h], preferred_element_type=f32)
            r = _sigmoid(g[:, :bw] + ba_ref[:, sl])
            ig = _sigmoid(g[:, bw:] + bx_ref[:, sl])
            log_a = r * cdec[:, sl]
            a = jnp.exp(log_a)
            a_scr[:, sl] = a
            u_scr[:, sl] = jnp.sqrt(jnp.tanh(-log_a) * (1.0 + a * a)) * (ig * xh)

        row = lax.broadcasted_iota(jnp.int32, (SUBLANES, W), 0)

        def body(g, cr):
            off = pl.multiple_of(g * SUBLANES, SUBLANES)
            A = a_scr[pl.ds(off, SUBLANES), :]
            U = u_scr[pl.ds(off, SUBLANES), :]
            for s in (1, 2, 4):
                a_sh = pltpu.roll(A, s, 0)
                u_sh = pltpu.roll(U, s, 0)
                m = row >= s
                U = jnp.where(m, A * u_sh + U, U)
                A = jnp.where(m, A * a_sh, A)
            H = A * cr + U
            u_scr[pl.ds(off, SUBLANES), :] = H
            return H[SUBLANES - 1:SUBLANES, :]

        carry[...] = lax.fori_loop(0, ts // SUBLANES, body, carry[...], unroll=4)
        o_ref[...] = (u_scr[...] * jax.nn.gelu(cur[:, W:2 * W])).astype(o_ref.dtype)

    @pl.when(lax.rem(step, 2) == 0)
    def _():
        tile(pr_even, pr_odd)

    @pl.when(lax.rem(step, 2) == 1)
    def _():
        tile(pr_odd, pr_even)


def _lru(x2d, w_lru, B, S, cw, cb, wa, ba, wx, bx, lam):
    T = B * S
    ts = min(512, S)
    nt = S // ts
    W = LRU_WIDTH
    wcat = jnp.concatenate([wa, wx], axis=-1).astype(bf16)
    row = lambda v: v.reshape(1, W)
    full = lambda shp: pl.BlockSpec(shp, lambda b, c: (0,) * len(shp))
    last = B * nt - 1
    return pl.pallas_call(
        _lru_kernel,
        out_shape=jax.ShapeDtypeStruct((T, W), bf16),
        grid=(B, nt),
        in_specs=[pl.BlockSpec((ts, D_MODEL), lambda b, c: (b * nt + c, 0)),
                  pl.BlockSpec((ts, D_MODEL), lambda b, c: (jnp.minimum(b * nt + c + 1, last), 0)),
                  full(w_lru.shape), full((CONV_WIDTH, W)), full((1, W)), full(wcat.shape),
                  full((1, W)), full((1, W)), full((1, W))],
        out_specs=pl.BlockSpec((ts, W), lambda b, c: (b * nt + c, 0)),
        scratch_shapes=[pltpu.VMEM((ts + SUBLANES, W), f32), pltpu.VMEM((ts, 2 * W), f32),
                        pltpu.VMEM((ts, 2 * W), f32), pltpu.VMEM((ts, W), f32), pltpu.VMEM((ts, W), f32),
                        pltpu.VMEM((1, W), f32)],
        compiler_params=_cparams("arbitrary", "arbitrary"),
        name="rg_lru",
    )(x2d, x2d, w_lru, cw, row(cb), wcat, row(ba), row(bx), row(lam))


def _ml_kernel(xin_ref, xnext_ref, w_ref, bias_ref, g_ref, o_ref, c_scr, n_scr, m_scr, pr_even, pr_odd):
    step = pl.program_id(0) * pl.num_programs(1) + pl.program_id(1)

    @pl.when(step == 0)
    def _():
        pr_even[...] = _project(xin_ref, w_ref)

    @pl.when(pl.program_id(1) == 0)
    def _():
        c_scr[...] = jnp.zeros(c_scr.shape, f32)
        n_scr[...] = jnp.zeros(n_scr.shape, f32)
        m_scr[...] = jnp.zeros(m_scr.shape, f32)

    args = (xnext_ref, w_ref, bias_ref, g_ref, o_ref, c_scr, n_scr, m_scr)

    @pl.when(lax.rem(step, 2) == 0)
    def _():
        _ml_tile(pr_even, pr_odd, *args)

    @pl.when(lax.rem(step, 2) == 1)
    def _():
        _ml_tile(pr_odd, pr_even, *args)


def _ml_tile(cur, nxt, xnext_ref, w_ref, bias_ref, g_ref, o_ref, c_scr, n_scr, m_scr):
    L = cur.shape[0]
    xnb = xnext_ref[...].astype(bf16)
    pcols = [0, 2 * ML_QK_W, 2 * ML_QK_W + ML_V_W, 2 * ML_QK_W + 2 * ML_V_W, cur.shape[1]]

    def project_next(i):
        nxt[:, pcols[i]:pcols[i + 1]] = jnp.dot(xnb, w_ref[:, pcols[i]:pcols[i + 1]],
                                                preferred_element_type=f32)

    pr = cur[...]
    q = pr[:, 0:ML_QK_W]
    k = pr[:, ML_QK_W:2 * ML_QK_W]
    v = pr[:, 2 * ML_QK_W:2 * ML_QK_W + ML_V_W]
    og = pr[:, 2 * ML_QK_W + ML_V_W:2 * ML_QK_W + 2 * ML_V_W]
    sm = pr[:, 2 * ML_QK_W + 2 * ML_V_W:] + bias_ref[...]
    logf = _log_sigmoid(sm)
    ri = lax.broadcasted_iota(jnp.int32, (L, L), 0)
    ci = lax.broadcasted_iota(jnp.int32, (L, L), 1)
    tri = ri >= ci
    b_all = _cumsum_rows(logf, L)
    g_all = sm - pltpu.roll(b_all, LANES - (SM_ML_F - SM_ML_I), 1)
    cm_all = _cummax_rows(g_all, L)
    g_t = g_all.T
    kb = k.astype(bf16)
    lane_head = lax.broadcasted_iota(jnp.int32, q.shape, 1) // ML_DQK
    scale = ML_DQK ** -0.5
    heads = range(ML_HEADS)
    sls = [slice(ML_DV * h, ML_DV * (h + 1)) for h in heads]
    m_news, decays, w_cs, m_ts, dws, iws = [], [], [], [], [], []
    for h in heads:
        project_next(h)
        b_c = b_all[:, SM_ML_F + h:SM_ML_F + h + 1]
        g_c = g_all[:, SM_ML_I + h:SM_ML_I + h + 1]
        g_r = g_t[SM_ML_I + h:SM_ML_I + h + 1, :]
        b_last = b_c[L - 1:L, :]
        m_prev = m_scr[h:h + 1, 0:1]
        mm_c = jnp.maximum(m_prev, cm_all[:, SM_ML_I + h:SM_ML_I + h + 1])
        mm_last = mm_c[L - 1:L, :]
        m_news.append(b_last + mm_last)
        decays.append(jnp.exp(m_prev - mm_last))
        w_cs.append(jnp.exp(g_c - mm_last))
        m_ts.append(b_c + mm_c)
        dws.append(jnp.exp(jnp.where(tri, g_r - mm_c, NEG_BIG)))
        iws.append(jnp.exp(m_prev - mm_c))
    qms = [jnp.where(lane_head == h, q, 0.0) * scale for h in heads]
    qmbs = [qm.astype(bf16) for qm in qms]
    ss = [_bdot_nt(qmbs[h], kb) * dws[h] for h in heads]
    n_rows = [n_scr[h:h + 1, :] for h in heads]
    nums = [_bdot(ss[h], v[:, sls[h]]) + iws[h] * _bdot(qmbs[h], c_scr[:, sls[h]]) for h in heads]
    dens = [jnp.sum(ss[h], axis=1, keepdims=True) + iws[h] * jnp.sum(qms[h] * n_rows[h], axis=1, keepdims=True)
            for h in heads]
    hhs = [nums[h] / jnp.maximum(jnp.abs(dens[h]), jnp.exp(-m_ts[h])) for h in heads]
    for h in heads:
        c_scr[:, sls[h]] = decays[h] * c_scr[:, sls[h]] + _bdot_tn(kb, w_cs[h] * v[:, sls[h]])
        n_scr[h:h + 1, :] = decays[h] * n_rows[h] + jnp.sum(w_cs[h] * k, axis=0, keepdims=True)
        m_scr[h:h + 1, :] = jnp.broadcast_to(m_news[h], (1, LANES))
    for h in heads:
        ms = jnp.mean(hhs[h] * hhs[h], axis=1, keepdims=True)
        hn = hhs[h] * lax.rsqrt(ms + RMS_EPS) * g_ref[:, sls[h]]
        o_ref[:, sls[h]] = (hn * _sigmoid(og[:, sls[h]])).astype(o_ref.dtype)


def _mlstm(x2d, w_ml, B, S, i_bias, f_bias, norm_g):
    T = B * S
    L = min(ML_CHUNK, S)
    nc = S // L
    bias = jnp.zeros((1, LANES), f32)
    bias = bias.at[0, SM_ML_I:SM_ML_I + ML_HEADS].set(i_bias).at[0, SM_ML_F:SM_ML_F + ML_HEADS].set(f_bias)
    full = lambda shp: pl.BlockSpec(shp, lambda b, c: (0,) * len(shp))
    return pl.pallas_call(
        _ml_kernel,
        out_shape=jax.ShapeDtypeStruct((T, ML_V_W), bf16),
        grid=(B, nc),
        in_specs=[pl.BlockSpec((L, D_MODEL), lambda b, c: (b * nc + c, 0)),
                  pl.BlockSpec((L, D_MODEL), lambda b, c: (jnp.minimum(b * nc + c + 1, B * nc - 1), 0)),
                  full(w_ml.shape), full((1, LANES)), full((1, ML_V_W))],
        out_specs=pl.BlockSpec((L, ML_V_W), lambda b, c: (b * nc + c, 0)),
        scratch_shapes=[pltpu.VMEM((ML_QK_W, ML_V_W), f32), pltpu.VMEM((SUBLANES, ML_QK_W), f32),
                        pltpu.VMEM((SUBLANES, LANES), f32),
                        pltpu.VMEM((L, w_ml.shape[1]), f32), pltpu.VMEM((L, w_ml.shape[1]), f32)],
        compiler_params=_cparams("arbitrary", "arbitrary"),
        name="mlstm",
    )(x2d, x2d, w_ml, bias, norm_g.reshape(1, ML_V_W))


def _gd_kernel(xin_ref, w_ref, cw_ref, alog_ref, dtb_ref, g_ref, o_ref, ext, s_scr):
    R = xin_ref.shape[0]
    KW = GD_K_W
    L = GD_CHUNK
    nch = R // L
    lsh = L.bit_length() - 1
    PW = nch * L
    nsl = KW // LANES
    passes = GD_SOLVE_PASSES

    @pl.when(pl.program_id(1) == 0)
    def _():
        ext[0:SUBLANES, :] = jnp.zeros((SUBLANES, ext.shape[1]), f32)
        s_scr[...] = jnp.zeros(s_scr.shape, f32)

    pr = _project(xin_ref, w_ref)
    z = pr[:, 3 * KW:4 * KW]
    sm = pr[:, 4 * KW:]
    ext[SUBLANES:SUBLANES + R, :] = pr[:, 0:3 * KW]
    xc = (cw_ref[3:4, :] * ext[SUBLANES:SUBLANES + R, :]
          + cw_ref[2:3, :] * ext[SUBLANES - 1:SUBLANES - 1 + R, :]
          + cw_ref[1:2, :] * ext[SUBLANES - 2:SUBLANES - 2 + R, :]
          + cw_ref[0:1, :] * ext[SUBLANES - 3:SUBLANES - 3 + R, :])
    ext[0:SUBLANES, :] = ext[R:R + SUBLANES, :]
    xc = xc * _sigmoid(xc)

    g_t = -jnp.exp(alog_ref[...]) * _softplus(sm + dtb_ref[...])
    beta_t = _sigmoid(sm)
    gc_all = _cumsum_rows(g_t, L)
    gc_t = gc_all.T

    pt = lax.broadcasted_iota(jnp.int32, (L, PW), 0)
    plane = lax.broadcasted_iota(jnp.int32, (L, PW), 1)
    ps = plane & (L - 1)
    pc = plane >> lsh
    tri_p = pt >= ps
    strict_p = pt > ps
    eye_p = jnp.where(pt == ps, 1.0, 0.0)
    blk8_p = (pt >> 3) == (ps >> 3)
    cmask_b = [jnp.where(pc == c, 1.0, 0.0).astype(bf16) for c in range(nch)]

    def pick(parts):
        out = parts[nch - 1]
        for c in range(nch - 2, -1, -1):
            out = jnp.where(pc == c, parts[c], out)
        return out

    def chunks(col):
        return [col[c * L:(c + 1) * L] for c in range(nch)]

    def bd(yb):
        return jnp.concatenate([yb * cmask_b[c] for c in range(nch)], axis=0)

    def pdot(xp, yp):
        return _mdot(_split_bf16(xp, passes), [bd(p) for p in _split_bf16(yp, passes)])

    lane_r = lax.broadcasted_iota(jnp.int32, (R, LANES), 1)
    lo_r = lane_r < GD_DK
    lane_l = lax.broadcasted_iota(jnp.int32, (L, LANES), 1)
    lo_l = lane_l < GD_DK
    lane_row = lax.broadcasted_iota(jnp.int32, (1, LANES), 1)
    r2 = lax.broadcasted_iota(jnp.int32, (LANES, LANES), 0)
    c2 = lax.broadcasted_iota(jnp.int32, (LANES, LANES), 1)
    blockdiag = (r2 < GD_DK) == (c2 < GD_DK)

    def half_sums(y, lo):
        s_lo = jnp.sum(jnp.where(lo, y, 0.0), axis=1, keepdims=True)
        s_hi = jnp.sum(jnp.where(lo, 0.0, y), axis=1, keepdims=True)
        return s_lo, s_hi

    def l2n(y):
        s_lo, s_hi = half_sums(y * y, lo_r)
        return y * jnp.where(lo_r, lax.rsqrt(s_lo + RMS_EPS), lax.rsqrt(s_hi + RMS_EPS))

    qns, kns, vss, a_ps, p_bs, gcols, bcols = [], [], [], [], [], [], []
    for j in range(nsl):
        qn = l2n(xc[:, LANES * j:LANES * (j + 1)]) * (GD_DK ** -0.5)
        kn = l2n(xc[:, KW + LANES * j:KW + LANES * (j + 1)])
        qns.append(qn)
        kns.append(kn)
        vss.append(xc[:, 2 * KW + LANES * j:2 * KW + LANES * (j + 1)])
        lhs = jnp.concatenate([jnp.where(lo_r, kn, 0.0), jnp.where(lo_r, 0.0, kn),
                               jnp.where(lo_r, qn, 0.0), jnp.where(lo_r, 0.0, qn)], axis=0)
        gram = _bdot_nt(lhs, kn)
        for e in range(2):
            h = 2 * j + e
            gcol = gc_all[:, SM_GD_A + h:SM_GD_A + h + 1]
            bcol = beta_t[:, SM_GD_B + h:SM_GD_B + h + 1]
            gc_r = gc_t[SM_GD_A + h:SM_GD_A + h + 1, :]
            gam_p = jnp.exp(jnp.where(tri_p, pick(chunks(gcol)) - gc_r, NEG_BIG))
            kk_p = pick(chunks(gram[e * R:(e + 1) * R]))
            qk_p = pick(chunks(gram[(2 + e) * R:(3 + e) * R]))
            a_ps.append(jnp.where(strict_p, pick(chunks(bcol)) * gam_p * kk_p, 0.0))
            p_bs.append((gam_p * qk_p).astype(bf16))
            gcols.append(gcol)
            bcols.append(bcol)

    ads = [jnp.where(blk8_p, a, 0.0) for a in a_ps]
    a2s = [pdot(ad, ad) for ad in ads]
    a4s = [pdot(a2, a2) for a2 in a2s]
    xs = [pdot(eye_p - ad, eye_p + a2) for ad, a2 in zip(ads, a2s)]
    xs = [pdot(x, eye_p + a4) for x, a4 in zip(xs, a4s)]
    for sh in range(3, lsh):
        msk = ((pt >> (sh + 1)) == (ps >> (sh + 1))) & (((pt >> sh) & 1) == 1) & (((ps >> sh) & 1) == 0)
        ts = [pdot(jnp.where(msk, a, 0.0), x) for a, x in zip(a_ps, xs)]
        xs = [x - pdot(x, t) for x, t in zip(xs, ts)]

    us, ws, qgs, kds, gls = [], [], [], [], []
    for j in range(nsl):
        uw, egs, eds, glh = [], [], [], []
        for e in range(2):
            h = 2 * j + e
            gcol, bcol = gcols[h], bcols[h]
            eg = jnp.exp(gcol)
            rhs = jnp.concatenate([bcol * vss[j], (bcol * eg) * kns[j]], axis=1)
            uw.append(_mdot([bd(p) for p in _split_bf16(xs[h], passes)], _split_bf16(rhs, passes)))
            lasts = [gcol[c * L + L - 1:c * L + L] for c in range(nch)]
            gcl = jnp.concatenate([jnp.broadcast_to(v, (L, 1)) for v in lasts], axis=0)
            egs.append(eg)
            eds.append(jnp.exp(gcl - gcol))
            glh.append([jnp.exp(v) for v in lasts])
        us.append(jnp.where(lo_r, uw[0][:, :LANES], uw[1][:, :LANES]))
        ws.append(jnp.where(lo_r, uw[0][:, LANES:], uw[1][:, LANES:]))
        qgs.append(qns[j] * jnp.where(lo_r, egs[0], egs[1]))
        kds.append(kns[j] * jnp.where(lo_r, eds[0], eds[1]))
        gls.append([jnp.where(lane_row < GD_DK, glh[0][c], glh[1][c]) for c in range(nch)])

    stacked = [jnp.concatenate([p_bs[2 * j], p_bs[2 * j + 1], kds[j].T.astype(bf16)], axis=0)
               for j in range(nsl)]
    states = [s_scr[j] for j in range(nsl)]
    outs = [[] for _ in range(nsl)]
    for c in range(nch):
        rs = slice(c * L, (c + 1) * L)
        for j in range(nsl):
            s_prev = states[j]
            ws_qs = _bdot(jnp.concatenate([ws[j][rs], qgs[j][rs]], axis=0), s_prev)
            db = (us[j][rs] - ws_qs[0:L]).astype(bf16)
            pieces = []
            if c > 0:
                pieces.append(jnp.zeros((c * L, LANES), bf16))
            pieces.append(db)
            if c < nch - 1:
                pieces.append(jnp.zeros(((nch - 1 - c) * L, LANES), bf16))
            dpad = jnp.concatenate(pieces, axis=0) if len(pieces) > 1 else db
            pd_upd = jnp.dot(stacked[j], dpad, preferred_element_type=f32)
            outs[j].append(ws_qs[L:2 * L] + jnp.where(lo_l, pd_upd[0:L], pd_upd[L:2 * L]))
            states[j] = gls[j][c] * s_prev + jnp.where(blockdiag, pd_upd[2 * L:], 0.0)

    for j in range(nsl):
        sl = slice(LANES * j, LANES * (j + 1))
        s_scr[j] = states[j]
        o = jnp.concatenate(outs[j], axis=0) if nch > 1 else outs[j][0]
        m_lo, m_hi = half_sums(o * o, lo_r)
        inv = jnp.where(lo_r, lax.rsqrt(m_lo * (1.0 / GD_DV) + RMS_EPS), lax.rsqrt(m_hi * (1.0 / GD_DV) + RMS_EPS))
        zs = z[:, sl]
        o_ref[:, sl] = (o * inv * g_ref[:, sl] * (zs * _sigmoid(zs))).astype(o_ref.dtype)


def _gdn(x2d, w_gd, B, S, conv_w, a_log, dt_bias, norm_g):
    T = B * S
    L = min(GD_ROWS, S)
    nc = S // L
    KW = GD_K_W
    alog = jnp.zeros((1, LANES), f32).at[0, SM_GD_A:SM_GD_A + GD_HEADS].set(a_log)
    dtb = jnp.zeros((1, LANES), f32).at[0, SM_GD_A:SM_GD_A + GD_HEADS].set(dt_bias)
    g_row = jnp.tile(norm_g, GD_HEADS).reshape(1, GD_V_W)
    full = lambda shp: pl.BlockSpec(shp, lambda b, c: (0,) * len(shp))
    return pl.pallas_call(
        _gd_kernel,
        out_shape=jax.ShapeDtypeStruct((T, GD_V_W), bf16),
        grid=(B, nc),
        in_specs=[pl.BlockSpec((L, D_MODEL), lambda b, c: (b * nc + c, 0)), full(w_gd.shape),
                  full((CONV_WIDTH, 3 * KW)), full((1, LANES)), full((1, LANES)), full((1, GD_V_W))],
        out_specs=pl.BlockSpec((L, GD_V_W), lambda b, c: (b * nc + c, 0)),
        scratch_shapes=[pltpu.VMEM((L + SUBLANES, 3 * KW), f32),
                        pltpu.VMEM((KW // LANES, LANES, LANES), f32)],
        compiler_params=_cparams("parallel", "arbitrary"),
        name="gated_deltanet",
    )(x2d, w_gd, conv_w, alog, dtb, g_row)


def _merge_kernel(ha_ref, hb_ref, hc_ref, x_ref, wg_ref, wa_ref, wb_ref, wc_ref, wo_ref, lg_ref, lb_ref,
                  rw_ref, rb_ref, x1_ref, x1p_ref, pk_ref, cnt_ref, prev, carry):
    D = x_ref.shape[1]
    i = pl.program_id(0)

    @pl.when(i == 0)
    def _():
        prev[...] = jnp.zeros(prev.shape, f32)
        carry[...] = jnp.zeros(carry.shape, f32)

    route = _route_stages(prev[...], jnp.where(i > 0, 1.0, 0.0), rw_ref, rb_ref, pk_ref, cnt_ref, carry)
    xb = x_ref[...].astype(bf16)

    next(route)
    g0 = jnp.dot(xb, wg_ref[:, 0:D], preferred_element_type=f32)
    next(route)
    g1 = jnp.dot(xb, wg_ref[:, D:2 * D], preferred_element_type=f32)
    next(route)
    g2 = jnp.dot(xb, wg_ref[:, 2 * D:3 * D], preferred_element_type=f32)
    next(route)
    bra = jnp.dot(ha_ref[...], wa_ref[...], preferred_element_type=f32)
    next(route)
    brb = jnp.dot(hb_ref[...], wb_ref[...], preferred_element_type=f32)
    brc = jnp.dot(hc_ref[...], wc_ref[...], preferred_element_type=f32)
    next(route)
    merged = _sigmoid(g0) * bra + _sigmoid(g1) * brb + _sigmoid(g2) * brc
    y = jnp.dot(merged.astype(bf16), wo_ref[...], preferred_element_type=f32)
    for _ in route:
        pass
    out = _layer_norm(DN_ALPHA * x_ref[...] + y, lg_ref[...], lb_ref[...])
    x1_ref[...] = out
    x1p_ref[...] = _pack_halves(out)
    prev[...] = out


def _merge(ha, hb, hc, x2d, w_gate, wa, wb, wc, wo, lg, lb, router_w, router_b):
    T = x2d.shape[0]
    D = D_MODEL
    tm = min(512, T)
    nt = T // tm
    rw, rb = _router_params(router_w, router_b)
    full = lambda shp: pl.BlockSpec(shp, lambda i: (0,) * len(shp))
    held = lambda shp: pl.BlockSpec(shp, lambda i: (0,) * len(shp), pipeline_mode=pl.Buffered(1))
    rows = lambda w: pl.BlockSpec((tm, w), lambda i: (jnp.minimum(i, nt - 1), 0))
    routed = pl.BlockSpec((tm, LANES), lambda i: (jnp.maximum(i - 1, 0), 0))
    return pl.pallas_call(
        _merge_kernel,
        out_shape=(jax.ShapeDtypeStruct((T, D), f32), jax.ShapeDtypeStruct((T, D // 2), jnp.uint32),
                   jax.ShapeDtypeStruct((T, LANES), f32), jax.ShapeDtypeStruct((1, LANES), f32)),
        grid=(nt + 1,),
        in_specs=[rows(LRU_WIDTH), rows(ML_V_W), rows(GD_V_W), rows(D), held(w_gate.shape),
                  held((LRU_WIDTH, D)), held((ML_V_W, D)), held((GD_V_W, D)), held((D, D)),
                  full((1, D)), full((1, D)), held((D, LANES)), full((1, LANES))],
        out_specs=(rows(D), rows(D // 2), routed, full((1, LANES))),
        scratch_shapes=[pltpu.VMEM((tm, D), f32), pltpu.VMEM((1, LANES), f32)],
        compiler_params=_cparams("arbitrary"),
        name="merge_outproj_ln_route",
    )(ha, hb, hc, x2d, w_gate, wa.astype(bf16), wb.astype(bf16), wc.astype(bf16),
      wo.astype(bf16), lg.reshape(1, D), lb.reshape(1, D), rw, rb)


PK_IDX = 0
PK_GATE = 4
PK_RANK = 8


def _route_stages(x, weight, rw_ref, rb_ref, pk_ref, cnt_ref, carry):
    tr = x.shape[0]
    x_hi, x_lo = _split_bf16(x, 3)
    w_hi, w_lo = _split_bf16(rw_ref[...], 3)
    prod = jnp.dot(jnp.concatenate([x_hi, x_lo], axis=0), jnp.concatenate([w_hi, w_lo], axis=1),
                   preferred_element_type=f32)
    logits = prod[0:tr, 0:LANES] + (prod[tr:, 0:LANES] + prod[0:tr, LANES:]) + rb_ref[...]
    lane = lax.broadcasted_iota(jnp.int32, (tr, LANES), 1).astype(f32)
    vals = logits
    idxs, tops = [], []
    yield
    for _ in range(TOP_K):
        m = jnp.max(vals, axis=1, keepdims=True)
        idx = jnp.min(jnp.where(vals == m, lane, float(LANES)), axis=1, keepdims=True)
        idxs.append(idx)
        tops.append(m)
        vals = jnp.where(lane == idx, NEG_BIG * 2.0, vals)
        yield
    es = [jnp.exp(t - tops[0]) for t in tops]
    tot = es[0] + es[1] + es[2] + es[3]
    onehots = [lane == idx for idx in idxs]
    sel = jnp.zeros((tr, LANES), f32)
    for oh in onehots:
        sel = sel + oh.astype(f32)
    ri = lax.broadcasted_iota(jnp.int32, (tr, tr), 0)
    ci = lax.broadcasted_iota(jnp.int32, (tr, tr), 1)
    before = jnp.dot((ri > ci).astype(bf16), sel.astype(bf16), preferred_element_type=f32) + carry[...]
    carry[...] = carry[...] + weight * jnp.sum(sel, axis=0, keepdims=True)
    cnt_ref[...] = carry[...]
    yield
    packed = jnp.zeros((tr, LANES), f32)
    for kk in range(TOP_K):
        rank = jnp.sum(jnp.where(onehots[kk], before, 0.0), axis=1, keepdims=True)
        packed = jnp.where(lane == float(PK_IDX + kk), idxs[kk], packed)
        packed = jnp.where(lane == float(PK_GATE + kk), es[kk] / tot, packed)
        packed = jnp.where(lane == float(PK_RANK + kk), rank, packed)
    pk_ref[...] = packed


def _router_params(router_w, router_b):
    rw = jnp.zeros((D_MODEL, LANES), f32).at[:, :N_EXPERTS].set(router_w)
    rb = jnp.full((1, LANES), NEG_BIG, f32).at[0, :N_EXPERTS].set(router_b)
    return rw, rb


def _expert_kernel(be_ref, nu_ref, nv_ref, x_ref, w1_ref, b1_ref, w2_ref, b2_ref, o_ref, w1b, w2b):
    i = pl.program_id(0)
    used = i < nu_ref[0]
    new_expert = jnp.logical_or(i == 0, be_ref[i] != be_ref[jnp.maximum(i - 1, 0)])

    @pl.when(jnp.logical_and(used, new_expert))
    def _():
        w1b[...] = w1_ref[0, 0].astype(bf16)
        w2b[...] = w2_ref[0, 0].astype(bf16)

    @pl.when(used)
    def _():
        rows = lax.broadcasted_iota(jnp.int32, x_ref.shape, 0)
        xp = jnp.where(rows < nv_ref[i], x_ref[...], jnp.uint32(0))
        x = _unpack_halves(xp).astype(bf16)
        hdn = jnp.dot(x, w1b[...], preferred_element_type=f32) + b1_ref[0, 0]
        glu = jnp.minimum(hdn[:, :D_FF], SWIGLU_LIMIT)
        lin = jnp.clip(hdn[:, D_FF:], -SWIGLU_LIMIT, SWIGLU_LIMIT)
        act = glu * _sigmoid(SWIGLU_ALPHA * glu) * (lin + 1.0)
        y = jnp.dot(act.astype(bf16), w2b[...], preferred_element_type=f32) + b2_ref[0, 0]
        o_ref[...] = _pack_halves(y)

    @pl.when(i >= nu_ref[0])
    def _():
        o_ref[...] = jnp.zeros(o_ref.shape, o_ref.dtype)


def _experts(xs, block_e, n_used, n_valid, layer, w1, b1, w2, b2):
    P = xs.shape[0]
    nb = P // MOE_BLOCK
    D = D_MODEL
    nl = w1.shape[0]
    grid_spec = pltpu.PrefetchScalarGridSpec(
        num_scalar_prefetch=3,
        grid=(nb,),
        in_specs=[pl.BlockSpec((MOE_BLOCK, D // 2), lambda i, be, nu, nv: (i, 0)),
                  pl.BlockSpec((1, 1, D, 2 * D_FF), lambda i, be, nu, nv: (layer, be[i], 0, 0)),
                  pl.BlockSpec((1, 1, 1, 2 * D_FF), lambda i, be, nu, nv: (layer, be[i], 0, 0)),
                  pl.BlockSpec((1, 1, D_FF, D), lambda i, be, nu, nv: (layer, be[i], 0, 0)),
                  pl.BlockSpec((1, 1, 1, D), lambda i, be, nu, nv: (layer, be[i], 0, 0))],
        out_specs=pl.BlockSpec((MOE_BLOCK, D // 2), lambda i, be, nu, nv: (i, 0)),
        scratch_shapes=[pltpu.VMEM((D, 2 * D_FF), bf16), pltpu.VMEM((D_FF, D), bf16)],
    )
    return pl.pallas_call(
        _expert_kernel,
        out_shape=jax.ShapeDtypeStruct((P, D // 2), jnp.uint32),
        grid_spec=grid_spec,
        compiler_params=_cparams("arbitrary"),
        name="experts",
    )(block_e, n_used, n_valid, xs, w1, b1.reshape(nl, N_EXPERTS, 1, 2 * D_FF), w2,
      b2.reshape(nl, N_EXPERTS, 1, D))


def _combine_kernel(y0_ref, y1_ref, y2_ref, y3_ref, pk_ref, x_ref, lg_ref, lb_ref, o_ref):
    pk = pk_ref[...]
    y = pk[:, PK_GATE:PK_GATE + 1] * _unpack_halves(y0_ref[...])
    for kk, y_ref in ((1, y1_ref), (2, y2_ref), (3, y3_ref)):
        y = y + pk[:, PK_GATE + kk:PK_GATE + kk + 1] * _unpack_halves(y_ref[...])
    o_ref[...] = _layer_norm(DN_ALPHA * x_ref[...] + y, lg_ref[...], lb_ref[...])


def _combine(yg, packed, x1, lg, lb):
    T, D = x1.shape
    tm = min(1024, T)
    nt = T // tm
    full = lambda shp: pl.BlockSpec(shp, lambda i: (0,) * len(shp))
    choice = lambda kk: pl.BlockSpec((tm, D // 2), lambda i: (kk * nt + i, 0))
    return pl.pallas_call(
        _combine_kernel,
        out_shape=jax.ShapeDtypeStruct((T, D), f32),
        grid=(nt,),
        in_specs=[choice(0), choice(1), choice(2), choice(3), pl.BlockSpec((tm, LANES), lambda i: (i, 0)),
                  pl.BlockSpec((tm, D), lambda i: (i, 0)), full((1, D)), full((1, D))],
        out_specs=pl.BlockSpec((tm, D), lambda i: (i, 0)),
        compiler_params=_cparams("parallel"),
        name="combine_ln",
    )(yg, yg, yg, yg, packed, x1, lg.reshape(1, D), lb.reshape(1, D))


def _sc_mesh():
    return plsc.VectorSubcoreMesh(core_axis_name="c", subcore_axis_name="s")


def _sc_dispatch(x, posk, n_slots):
    t_tokens, w = x.shape

    @functools.partial(pl.kernel, out_type=jax.ShapeDtypeStruct((n_slots, w), x.dtype), mesh=_sc_mesh(),
                       name="sc_dispatch")
    def k(x_hbm, p_hbm, o_hbm):
        def body(x_vmem, p_vmem):
            for kk in range(TOP_K):
                pltpu.sync_copy(x_vmem, o_hbm.at[p_vmem.at[kk]])

        pltpu.emit_pipeline(
            body,
            grid=(t_tokens // SC_ROWS,),
            in_specs=[pl.BlockSpec((SC_ROWS, w), lambda i: (i, 0), pipeline_mode=pl.Buffered(1)),
                      pl.BlockSpec((TOP_K, SC_ROWS), lambda i: (0, i))],
            out_specs=[],
            core_axis_name=("c", "s"),
            dimension_semantics=(pltpu.PARALLEL,),
        )(x_hbm, p_hbm)

    return k(x, posk)


def _sc_gather(table, idx):
    n = idx.shape[0]
    w = table.shape[1]

    @functools.partial(pl.kernel, out_type=jax.ShapeDtypeStruct((n, w), table.dtype), mesh=_sc_mesh(),
                       name="sc_gather")
    def k(t_hbm, i_hbm, o_hbm):
        def body(i_vmem, o_vmem):
            pltpu.sync_copy(t_hbm.at[i_vmem.at[0]], o_vmem)

        pltpu.emit_pipeline(
            body,
            grid=(n // SC_ROWS,),
            in_specs=[pl.BlockSpec((1, SC_ROWS), lambda i: (0, i))],
            out_specs=[pl.BlockSpec((SC_ROWS, w), lambda i: (i, 0), pipeline_mode=pl.Buffered(1))],
            core_axis_name=("c", "s"),
            dimension_semantics=(pltpu.PARALLEL,),
        )(i_hbm, o_hbm)

    return k(table, idx.reshape(1, n))


def _moe(x1, x1p, packed, cnt, layer, w1, b1, w2, b2, lg, lb):
    T, D = x1.shape
    A = T * TOP_K
    idx = packed[:, PK_IDX:PK_IDX + TOP_K].astype(jnp.int32)
    rank = packed[:, PK_RANK:PK_RANK + TOP_K].astype(jnp.int32)
    counts = cnt[0, :N_EXPERTS].astype(jnp.int32)
    padded = ((counts + MOE_BLOCK - 1) // MOE_BLOCK) * MOE_BLOCK
    pad_end = jnp.cumsum(padded)
    pad_start = pad_end - padded
    posk = (pad_start[idx] + rank).T
    n_blocks = -(-A // MOE_BLOCK) + N_EXPERTS
    P = n_blocks * MOE_BLOCK
    starts = jnp.arange(n_blocks, dtype=jnp.int32) * MOE_BLOCK
    block_e = jnp.minimum(jnp.sum((pad_end[None, :] <= starts[:, None]).astype(jnp.int32), axis=1), N_EXPERTS - 1)
    n_used = (pad_end[-1:] // MOE_BLOCK).astype(jnp.int32)
    n_valid = jnp.clip(pad_start[block_e] + counts[block_e] - starts, 0, MOE_BLOCK).astype(jnp.int32)
    xs = _sc_dispatch(x1p, posk, P)
    ys = _experts(xs, block_e, n_used, n_valid, layer, w1, b1, w2, b2)
    yg = _sc_gather(ys, posk.reshape(A))
    return _combine(yg, packed, x1, lg, lb)


def _layer(x2d, B, S, p, layer, stacked):
    w = _split_w_in(p['w_in'])
    ha = _lru(x2d, w['lru'], B, S, p['lru_conv_w'], p['lru_conv_b'], p['lru_wa'], p['lru_ba'], p['lru_wx'],
              p['lru_bx'], p['lru_lambda'])
    hb = _mlstm(x2d, w['ml'], B, S, p['ml_i_bias'], p['ml_f_bias'], p['ml_norm_g'])
    hc = _gdn(x2d, w['gd'], B, S, p['gd_conv_w'], p['gd_a_log'], p['gd_dt_bias'], p['gd_norm_g'])
    x1, x1p, packed, cnt = _merge(ha, hb, hc, x2d, w['gate'], p['w_br_lru'], p['w_br_ml'], p['w_br_gd'],
                                  p['w_out'], p['ln1_g'], p['ln1_b'], p['router_w'], p['router_b'])
    return _moe(x1, x1p, packed, cnt, layer, stacked['exp_w1'], stacked['exp_b1'],
                stacked['exp_w2'], stacked['exp_b2'], p['ln2_g'], p['ln2_b'])


def kernel(x, w_in, lru_conv_w, lru_conv_b, lru_wa, lru_ba, lru_wx, lru_bx, lru_lambda, ml_i_bias, ml_f_bias, ml_norm_g, gd_conv_w, gd_a_log, gd_dt_bias, gd_norm_g, w_br_lru, w_br_ml, w_br_gd, w_out, ln1_g, ln1_b, router_w, router_b, exp_w1, exp_b1, exp_w2, exp_b2, ln2_g, ln2_b):
    B, S, D = x.shape
    params = dict(w_in=w_in, lru_conv_w=lru_conv_w, lru_conv_b=lru_conv_b, lru_wa=lru_wa, lru_ba=lru_ba,
                  lru_wx=lru_wx, lru_bx=lru_bx, lru_lambda=lru_lambda, ml_i_bias=ml_i_bias,
                  ml_f_bias=ml_f_bias, ml_norm_g=ml_norm_g, gd_conv_w=gd_conv_w, gd_a_log=gd_a_log,
                  gd_dt_bias=gd_dt_bias, gd_norm_g=gd_norm_g, w_br_lru=w_br_lru, w_br_ml=w_br_ml,
                  w_br_gd=w_br_gd, w_out=w_out, ln1_g=ln1_g, ln1_b=ln1_b, router_w=router_w,
                  router_b=router_b, ln2_g=ln2_g, ln2_b=ln2_b)
    stacked = dict(exp_w1=exp_w1, exp_b1=exp_b1, exp_w2=exp_w2, exp_b2=exp_b2)
    h = x.reshape(B * S, D)
    for l in range(w_in.shape[0]):
        h = _layer(h, B, S, {k: v[l] for k, v in params.items()}, l, stacked)
    return h.reshape(B, S, D)
```

```python
import functools

import jax
import jax.numpy as jnp
from jax import lax
from jax.experimental import pallas as pl
from jax.experimental.pallas import tpu as pltpu
from jax.experimental.pallas import tpu_sc as plsc

f32 = jnp.float32
bf16 = jnp.bfloat16

D_MODEL = 1024
LRU_WIDTH = 1024
LRU_BLOCKS = 8
LRU_C = 8.0
CONV_WIDTH = 4
ML_HEADS = 4
ML_DQK = 64
ML_DV = 128
GD_HEADS = 8
GD_DK = 64
GD_DV = 64
N_EXPERTS = 32
TOP_K = 4
D_FF = 1024
SWIGLU_LIMIT = 7.0
SWIGLU_ALPHA = 1.702
MOE_BLOCK = 512
N_BRANCH = 3
DEPTH = 2
DN_ALPHA = (2.0 * DEPTH) ** 0.25
LN_EPS = 1e-5
RMS_EPS = 1e-6

ML_QK_W = ML_HEADS * ML_DQK
ML_V_W = ML_HEADS * ML_DV
GD_K_W = GD_HEADS * GD_DK
GD_V_W = GD_HEADS * GD_DV
IN_SPLITS = (LRU_WIDTH, LRU_WIDTH, ML_QK_W, ML_QK_W, ML_V_W, ML_HEADS, ML_HEADS, ML_V_W,
             GD_K_W, GD_K_W, GD_V_W, GD_HEADS, GD_HEADS, GD_V_W, N_BRANCH * D_MODEL)

LANES = 128
SUBLANES = 8
NEG_BIG = -1e30

SM_ML_I = 0
SM_ML_F = 4
SM_GD_A = 8
SM_GD_B = 16

ML_CHUNK = 256
GD_CHUNK = 64
GD_ROWS = 256
GD_SOLVE_PASSES = 1
VMEM_LIMIT = 56 * 1024 * 1024
SC_ROWS = 128


def _cparams(*sem):
    return pltpu.CompilerParams(dimension_semantics=sem, vmem_limit_bytes=VMEM_LIMIT)


def _cummax_rows(x, seg):
    pos = lax.broadcasted_iota(jnp.int32, x.shape, 0) & (seg - 1)
    step = 1
    while step < seg:
        x = jnp.maximum(x, jnp.where(pos >= step, pltpu.roll(x, step, 0), NEG_BIG))
        step *= 2
    return x


def _cumsum_rows(x, seg):
    pos = lax.broadcasted_iota(jnp.int32, x.shape, 0) & (seg - 1)
    step = 1
    while step < seg:
        x = x + jnp.where(pos >= step, pltpu.roll(x, step, 0), 0.0)
        step *= 2
    return x


def _bdot(a, b):
    return jnp.dot(a.astype(bf16), b.astype(bf16), preferred_element_type=f32)


def _bdot_nt(a, b):
    return lax.dot_general(a.astype(bf16), b.astype(bf16), (((1,), (1,)), ((), ())),
                           preferred_element_type=f32)


def _bdot_tn(a, b):
    return lax.dot_general(a.astype(bf16), b.astype(bf16), (((0,), (0,)), ((), ())),
                           preferred_element_type=f32)


def _split_bf16(x, passes):
    hi = x.astype(bf16)
    if passes == 1:
        return (hi,)
    return (hi, (x - hi.astype(f32)).astype(bf16))


def _mdot(xs, ys):
    out = jnp.dot(xs[0], ys[0], preferred_element_type=f32)
    if len(xs) > 1:
        out = out + jnp.dot(xs[1], ys[0], preferred_element_type=f32)
        out = out + jnp.dot(xs[0], ys[1], preferred_element_type=f32)
    return out


def _sigmoid(x):
    return 1.0 / (1.0 + jnp.exp(-x))


def _softplus(x):
    return jnp.maximum(x, 0.0) + jnp.log1p(jnp.exp(-jnp.abs(x)))


def _log_sigmoid(x):
    return jnp.minimum(x, 0.0) - jnp.log1p(jnp.exp(-jnp.abs(x)))


def _pack_halves(v):
    n = v.shape[1] // 2
    hi = lax.bitcast_convert_type(v[:, :n].astype(bf16).astype(f32), jnp.uint32)
    lo = lax.bitcast_convert_type(v[:, n:].astype(bf16).astype(f32), jnp.uint32)
    return hi | (lo >> 16)


def _unpack_halves(p):
    hi = lax.bitcast_convert_type(p & jnp.uint32(0xFFFF0000), f32)
    lo = lax.bitcast_convert_type(p << 16, f32)
    return jnp.concatenate([hi, lo], axis=1)


def _layer_norm(z, g, b):
    mu = jnp.mean(z, axis=-1, keepdims=True)
    zc = z - mu
    var = jnp.mean(zc * zc, axis=-1, keepdims=True)
    return zc * lax.rsqrt(var + LN_EPS) * g + b


def _split_w_in(w):
    pts = []
    acc = 0
    for s in IN_SPLITS[:-1]:
        acc += s
        pts.append(acc)
    (lru_x, lru_y, ml_q, ml_k, ml_v, ml_i, ml_f, ml_o,
     gd_q, gd_k, gd_v, gd_a, gd_b, gd_z, gate) = jnp.split(w, pts, axis=1)
    pad = jnp.zeros((w.shape[0], LANES - 2 * ML_HEADS - 2 * GD_HEADS), w.dtype)
    small = jnp.concatenate([ml_i, ml_f, gd_a, gd_b, pad], axis=1)
    cat = lambda *cols: jnp.concatenate(cols, axis=1).astype(bf16)
    return dict(lru=cat(lru_x, lru_y), ml=cat(ml_q, ml_k, ml_v, ml_o, small),
                gd=cat(gd_q, gd_k, gd_v, gd_z, small), gate=gate.astype(bf16))


def _project(x_ref, w_ref):
    return jnp.dot(x_ref[...].astype(bf16), w_ref[...], preferred_element_type=f32)


def _lru_kernel(xin_ref, xnext_ref, w_ref, cw_ref, cb_ref, wcat_ref, ba_ref, bx_ref, lam_ref, o_ref,
                xext, pr_even, pr_odd, a_scr, u_scr, carry):
    ts = xin_ref.shape[0]
    W = a_scr.shape[1]
    bw = W // LRU_BLOCKS
    step = pl.program_id(0) * pl.num_programs(1) + pl.program_id(1)

    @pl.when(step == 0)
    def _():
        pr_even[...] = _project(xin_ref, w_ref)

    @pl.when(pl.program_id(1) == 0)
    def _():
        xext[0:SUBLANES, :] = jnp.zeros((SUBLANES, W), f32)
        carry[...] = jnp.zeros((1, W), f32)

    def tile(cur, nxt):
        x = cur[:, 0:W]
        xext[SUBLANES:SUBLANES + ts, :] = x
        xa = (cw_ref[3:4, :] * x
              + cw_ref[2:3, :] * xext[SUBLANES - 1:SUBLANES - 1 + ts, :]
              + cw_ref[1:2, :] * xext[SUBLANES - 2:SUBLANES - 2 + ts, :]
              + cw_ref[0:1, :] * xext[SUBLANES - 3:SUBLANES - 3 + ts, :]) + cb_ref[...]
        xext[0:SUBLANES, :] = xext[ts:ts + SUBLANES, :]

        xnb = xnext_ref[...].astype(bf16)
        pw = 2 * W // LRU_BLOCKS
        cdec = -LRU_C * _softplus(-lam_ref[...])
        for h in range(LRU_BLOCKS):
            sl = slice(bw * h, bw * (h + 1))
            psl = slice(pw * h, pw * (h + 1))
            nxt[:, psl] = jnp.dot(xnb, w_ref[:, psl], preferred_element_type=f32)
            xh = xa[:, sl]
            g = jnp.dot(xh.astype(bf16), wcat_ref[h], preferred_element_type=f32)
            r = _sigmoid(g[:, :bw] + ba_ref[:, sl])
            ig = _sigmoid(g[:, bw:] + bx_ref[:, sl])
            log_a = r * cdec[:, sl]
            a = jnp.exp(log_a)
            a_scr[:, sl] = a
            u_scr[:, sl] = jnp.sqrt(jnp.tanh(-log_a) * (1.0 + a * a)) * (ig * xh)

        row = lax.broadcasted_iota(jnp.int32, (SUBLANES, W), 0)

        def body(g, cr):
            off = pl.multiple_of(g * SUBLANES, SUBLANES)
            A = a_scr[pl.ds(off, SUBLANES), :]
            U = u_scr[pl.ds(off, SUBLANES), :]
            for s in (1, 2, 4):
                a_sh = pltpu.roll(A, s, 0)
                u_sh = pltpu.roll(U, s, 0)
                m = row >= s
                U = jnp.where(m, A * u_sh + U, U)
                A = jnp.where(m, A * a_sh, A)
            H = A * cr + U
            u_scr[pl.ds(off, SUBLANES), :] = H
            return H[SUBLANES - 1:SUBLANES, :]

        carry[...] = lax.fori_loop(0, ts // SUBLANES, body, carry[...], unroll=4)
        o_ref[...] = (u_scr[...] * jax.nn.gelu(cur[:, W:2 * W])).astype(o_ref.dtype)

    @pl.when(lax.rem(step, 2) == 0)
    def _():
        tile(pr_even, pr_odd)

    @pl.when(lax.rem(step, 2) == 1)
    def _():
        tile(pr_odd, pr_even)


def _lru(x2d, w_lru, B, S, cw, cb, wa, ba, wx, bx, lam):
    T = B * S
    ts = min(512, S)
    nt = S // ts
    W = LRU_WIDTH
    wcat = jnp.concatenate([wa, wx], axis=-1).astype(bf16)
    row = lambda v: v.reshape(1, W)
    full = lambda shp: pl.BlockSpec(shp, lambda b, c: (0,) * len(shp))
    last = B * nt - 1
    return pl.pallas_call(
        _lru_kernel,
        out_shape=jax.ShapeDtypeStruct((T, W), bf16),
        grid=(B, nt),
        in_specs=[pl.BlockSpec((ts, D_MODEL), lambda b, c: (b * nt + c, 0)),
                  pl.BlockSpec((ts, D_MODEL), lambda b, c: (jnp.minimum(b * nt + c + 1, last), 0)),
                  full(w_lru.shape), full((CONV_WIDTH, W)), full((1, W)), full(wcat.shape),
                  full((1, W)), full((1, W)), full((1, W))],
        out_specs=pl.BlockSpec((ts, W), lambda b, c: (b * nt + c, 0)),
        scratch_shapes=[pltpu.VMEM((ts + SUBLANES, W), f32), pltpu.VMEM((ts, 2 * W), f32),
                        pltpu.VMEM((ts, 2 * W), f32), pltpu.VMEM((ts, W), f32), pltpu.VMEM((ts, W), f32),
                        pltpu.VMEM((1, W), f32)],
        compiler_params=_cparams("arbitrary", "arbitrary"),
        name="rg_lru",
    )(x2d, x2d, w_lru, cw, row(cb), wcat, row(ba), row(bx), row(lam))


def _ml_kernel(xin_ref, xnext_ref, w_ref, bias_ref, g_ref, o_ref, c_scr, n_scr, m_scr, pr_even, pr_odd):
    step = pl.program_id(0) * pl.num_programs(1) + pl.program_id(1)

    @pl.when(step == 0)
    def _():
        pr_even[...] = _project(xin_ref, w_ref)

    @pl.when(pl.program_id(1) == 0)
    def _():
        c_scr[...] = jnp.zeros(c_scr.shape, f32)
        n_scr[...] = jnp.zeros(n_scr.shape, f32)
        m_scr[...] = jnp.zeros(m_scr.shape, f32)

    args = (xnext_ref, w_ref, bias_ref, g_ref, o_ref, c_scr, n_scr, m_scr)

    @pl.when(lax.rem(step, 2) == 0)
    def _():
        _ml_tile(pr_even, pr_odd, *args)

    @pl.when(lax.rem(step, 2) == 1)
    def _():
        _ml_tile(pr_odd, pr_even, *args)


def _ml_tile(cur, nxt, xnext_ref, w_ref, bias_ref, g_ref, o_ref, c_scr, n_scr, m_scr):
    L = cur.shape[0]
    xnb = xnext_ref[...].astype(bf16)
    pcols = [0, 2 * ML_QK_W, 2 * ML_QK_W + ML_V_W, 2 * ML_QK_W + 2 * ML_V_W, cur.shape[1]]

    def project_next(i):
        nxt[:, pcols[i]:pcols[i + 1]] = jnp.dot(xnb, w_ref[:, pcols[i]:pcols[i + 1]],
                                                preferred_element_type=f32)

    pr = cur[...]
    q = pr[:, 0:ML_QK_W]
    k = pr[:, ML_QK_W:2 * ML_QK_W]
    v = pr[:, 2 * ML_QK_W:2 * ML_QK_W + ML_V_W]
    og = pr[:, 2 * ML_QK_W + ML_V_W:2 * ML_QK_W + 2 * ML_V_W]
    sm = pr[:, 2 * ML_QK_W + 2 * ML_V_W:] + bias_ref[...]
    logf = _log_sigmoid(sm)
    ri = lax.broadcasted_iota(jnp.int32, (L, L), 0)
    ci = lax.broadcasted_iota(jnp.int32, (L, L), 1)
    tri = ri >= ci
    b_all = _cumsum_rows(logf, L)
    g_all = sm - pltpu.roll(b_all, LANES - (SM_ML_F - SM_ML_I), 1)
    cm_all = _cummax_rows(g_all, L)
    g_t = g_all.T
    kb = k.astype(bf16)
    lane_head = lax.broadcasted_iota(jnp.int32, q.shape, 1) // ML_DQK
    scale = ML_DQK ** -0.5
    heads = range(ML_HEADS)
    sls = [slice(ML_DV * h, ML_DV * (h + 1)) for h in heads]
    m_news, decays, w_cs, m_ts, dws, iws = [], [], [], [], [], []
    for h in heads:
        project_next(h)
        b_c = b_all[:, SM_ML_F + h:SM_ML_F + h + 1]
        g_c = g_all[:, SM_ML_I + h:SM_ML_I + h + 1]
        g_r = g_t[SM_ML_I + h:SM_ML_I + h + 1, :]
        b_last = b_c[L - 1:L, :]
        m_prev = m_scr[h:h + 1, 0:1]
        mm_c = jnp.maximum(m_prev, cm_all[:, SM_ML_I + h:SM_ML_I + h + 1])
        mm_last = mm_c[L - 1:L, :]
        m_news.append(b_last + mm_last)
        decays.append(jnp.exp(m_prev - mm_last))
        w_cs.append(jnp.exp(g_c - mm_last))
        m_ts.append(b_c + mm_c)
        dws.append(jnp.exp(jnp.where(tri, g_r - mm_c, NEG_BIG)))
        iws.append(jnp.exp(m_prev - mm_c))
    qms = [jnp.where(lane_head == h, q, 0.0) * scale for h in heads]
    qmbs = [qm.astype(bf16) for qm in qms]
    ss = [_bdot_nt(qmbs[h], kb) * dws[h] for h in heads]
    n_rows = [n_scr[h:h + 1, :] for h in heads]
    nums = [_bdot(ss[h], v[:, sls[h]]) + iws[h] * _bdot(qmbs[h], c_scr[:, sls[h]]) for h in heads]
    dens = [jnp.sum(ss[h], axis=1, keepdims=True) + iws[h] * jnp.sum(qms[h] * n_rows[h], axis=1, keepdims=True)
            for h in heads]
    hhs = [nums[h] / jnp.maximum(jnp.abs(dens[h]), jnp.exp(-m_ts[h])) for h in heads]
    for h in heads:
        c_scr[:, sls[h]] = decays[h] * c_scr[:, sls[h]] + _bdot_tn(kb, w_cs[h] * v[:, sls[h]])
        n_scr[h:h + 1, :] = decays[h] * n_rows[h] + jnp.sum(w_cs[h] * k, axis=0, keepdims=True)
        m_scr[h:h + 1, :] = jnp.broadcast_to(m_news[h], (1, LANES))
    for h in heads:
        ms = jnp.mean(hhs[h] * hhs[h], axis=1, keepdims=True)
        hn = hhs[h] * lax.rsqrt(ms + RMS_EPS) * g_ref[:, sls[h]]
        o_ref[:, sls[h]] = (hn * _sigmoid(og[:, sls[h]])).astype(o_ref.dtype)


def _mlstm(x2d, w_ml, B, S, i_bias, f_bias, norm_g):
    T = B * S
    L = min(ML_CHUNK, S)
    nc = S // L
    bias = jnp.zeros((1, LANES), f32)
    bias = bias.at[0, SM_ML_I:SM_ML_I + ML_HEADS].set(i_bias).at[0, SM_ML_F:SM_ML_F + ML_HEADS].set(f_bias)
    full = lambda shp: pl.BlockSpec(shp, lambda b, c: (0,) * len(shp))
    return pl.pallas_call(
        _ml_kernel,
        out_shape=jax.ShapeDtypeStruct((T, ML_V_W), bf16),
        grid=(B, nc),
        in_specs=[pl.BlockSpec((L, D_MODEL), lambda b, c: (b * nc + c, 0)),
                  pl.BlockSpec((L, D_MODEL), lambda b, c: (jnp.minimum(b * nc + c + 1, B * nc - 1), 0)),
                  full(w_ml.shape), full((1, LANES)), full((1, ML_V_W))],
        out_specs=pl.BlockSpec((L, ML_V_W), lambda b, c: (b * nc + c, 0)),
        scratch_shapes=[pltpu.VMEM((ML_QK_W, ML_V_W), f32), pltpu.VMEM((SUBLANES, ML_QK_W), f32),
                        pltpu.VMEM((SUBLANES, LANES), f32),
                        pltpu.VMEM((L, w_ml.shape[1]), f32), pltpu.VMEM((L, w_ml.shape[1]), f32)],
        compiler_params=_cparams("arbitrary", "arbitrary"),
        name="mlstm",
    )(x2d, x2d, w_ml, bias, norm_g.reshape(1, ML_V_W))


def _gd_kernel(xin_ref, w_ref, cw_ref, alog_ref, dtb_ref, g_ref, o_ref, ext, s_scr):
    R = xin_ref.shape[0]
    KW = GD_K_W
    L = GD_CHUNK
    nch = R // L
    lsh = L.bit_length() - 1
    PW = nch * L
    nsl = KW // LANES
    passes = GD_SOLVE_PASSES

    @pl.when(pl.program_id(1) == 0)
    def _():
        ext[0:SUBLANES, :] = jnp.zeros((SUBLANES, ext.shape[1]), f32)
        s_scr[...] = jnp.zeros(s_scr.shape, f32)

    pr = _project(xin_ref, w_ref)
    z = pr[:, 3 * KW:4 * KW]
    sm = pr[:, 4 * KW:]
    ext[SUBLANES:SUBLANES + R, :] = pr[:, 0:3 * KW]
    xc = (cw_ref[3:4, :] * ext[SUBLANES:SUBLANES + R, :]
          + cw_ref[2:3, :] * ext[SUBLANES - 1:SUBLANES - 1 + R, :]
          + cw_ref[1:2, :] * ext[SUBLANES - 2:SUBLANES - 2 + R, :]
          + cw_ref[0:1, :] * ext[SUBLANES - 3:SUBLANES - 3 + R, :])
    ext[0:SUBLANES, :] = ext[R:R + SUBLANES, :]
    xc = xc * _sigmoid(xc)

    g_t = -jnp.exp(alog_ref[...]) * _softplus(sm + dtb_ref[...])
    beta_t = _sigmoid(sm)
    gc_all = _cumsum_rows(g_t, L)
    gc_t = gc_all.T

    pt = lax.broadcasted_iota(jnp.int32, (L, PW), 0)
    plane = lax.broadcasted_iota(jnp.int32, (L, PW), 1)
    ps = plane & (L - 1)
    pc = plane >> lsh
    tri_p = pt >= ps
    strict_p = pt > ps
    eye_p = jnp.where(pt == ps, 1.0, 0.0)
    blk8_p = (pt >> 3) == (ps >> 3)
    cmask_b = [jnp.where(pc == c, 1.0, 0.0).astype(bf16) for c in range(nch)]

    def pick(parts):
        out = parts[nch - 1]
        for c in range(nch - 2, -1, -1):
            out = jnp.where(pc == c, parts[c], out)
        return out

    def chunks(col):
        return [col[c * L:(c + 1) * L] for c in range(nch)]

    def bd(yb):
        return jnp.concatenate([yb * cmask_b[c] for c in range(nch)], axis=0)

    def pdot(xp, yp):
        return _mdot(_split_bf16(xp, passes), [bd(p) for p in _split_bf16(yp, passes)])

    lane_r = lax.broadcasted_iota(jnp.int32, (R, LANES), 1)
    lo_r = lane_r < GD_DK
    lane_l = lax.broadcasted_iota(jnp.int32, (L, LANES), 1)
    lo_l = lane_l < GD_DK
    lane_row = lax.broadcasted_iota(jnp.int32, (1, LANES), 1)
    r2 = lax.broadcasted_iota(jnp.int32, (LANES, LANES), 0)
    c2 = lax.broadcasted_iota(jnp.int32, (LANES, LANES), 1)
    blockdiag = (r2 < GD_DK) == (c2 < GD_DK)

    def half_sums(y, lo):
        s_lo = jnp.sum(jnp.where(lo, y, 0.0), axis=1, keepdims=True)
        s_hi = jnp.sum(jnp.where(lo, 0.0, y), axis=1, keepdims=True)
        return s_lo, s_hi

    def l2n(y):
        s_lo, s_hi = half_sums(y * y, lo_r)
        return y * jnp.where(lo_r, lax.rsqrt(s_lo + RMS_EPS), lax.rsqrt(s_hi + RMS_EPS))

    qns, kns, vss, a_ps, p_bs, gcols, bcols = [], [], [], [], [], [], []
    for j in range(nsl):
        qn = l2n(xc[:, LANES * j:LANES * (j + 1)]) * (GD_DK ** -0.5)
        kn = l2n(xc[:, KW + LANES * j:KW + LANES * (j + 1)])
        qns.append(qn)
        kns.append(kn)
        vss.append(xc[:, 2 * KW + LANES * j:2 * KW + LANES * (j + 1)])
        lhs = jnp.concatenate([jnp.where(lo_r, kn, 0.0), jnp.where(lo_r, 0.0, kn),
                               jnp.where(lo_r, qn, 0.0), jnp.where(lo_r, 0.0, qn)], axis=0)
        gram = _bdot_nt(lhs, kn)
        for e in range(2):
            h = 2 * j + e
            gcol = gc_all[:, SM_GD_A + h:SM_GD_A + h + 1]
            bcol = beta_t[:, SM_GD_B + h:SM_GD_B + h + 1]
            gc_r = gc_t[SM_GD_A + h:SM_GD_A + h + 1, :]
            gam_p = jnp.exp(jnp.where(tri_p, pick(chunks(gcol)) - gc_r, NEG_BIG))
            kk_p = pick(chunks(gram[e * R:(e + 1) * R]))
            qk_p = pick(chunks(gram[(2 + e) * R:(3 + e) * R]))
            a_ps.append(jnp.where(strict_p, pick(chunks(bcol)) * gam_p * kk_p, 0.0))
            p_bs.append((gam_p * qk_p).astype(bf16))
            gcols.append(gcol)
            bcols.append(bcol)

    ads = [jnp.where(blk8_p, a, 0.0) for a in a_ps]
    a2s = [pdot(ad, ad) for ad in ads]
    a4s = [pdot(a2, a2) for a2 in a2s]
    xs = [pdot(eye_p - ad, eye_p + a2) for ad, a2 in zip(ads, a2s)]
    xs = [pdot(x, eye_p + a4) for x, a4 in zip(xs, a4s)]
    for sh in range(3, lsh):
        msk = ((pt >> (sh + 1)) == (ps >> (sh + 1))) & (((pt >> sh) & 1) == 1) & (((ps >> sh) & 1) == 0)
        ts = [pdot(jnp.where(msk, a, 0.0), x) for a, x in zip(a_ps, xs)]
        xs = [x - pdot(x, t) for x, t in zip(xs, ts)]

    us, ws, qgs, kds, gls = [], [], [], [], []
    for j in range(nsl):
        uw, egs, eds, glh = [], [], [], []
        for e in range(2):
            h = 2 * j + e
            gcol, bcol = gcols[h], bcols[h]
            eg = jnp.exp(gcol)
            rhs = jnp.concatenate([bcol * vss[j], (bcol * eg) * kns[j]], axis=1)
            uw.append(_mdot([bd(p) for p in _split_bf16(xs[h], passes)], _split_bf16(rhs, passes)))
            lasts = [gcol[c * L + L - 1:c * L + L] for c in range(nch)]
            gcl = jnp.concatenate([jnp.broadcast_to(v, (L, 1)) for v in lasts], axis=0)
            egs.append(eg)
            eds.append(jnp.exp(gcl - gcol))
            glh.append([jnp.exp(v) for v in lasts])
        us.append(jnp.where(lo_r, uw[0][:, :LANES], uw[1][:, :LANES]))
        ws.append(jnp.where(lo_r, uw[0][:, LANES:], uw[1][:, LANES:]))
        qgs.append(qns[j] * jnp.where(lo_r, egs[0], egs[1]))
        kds.append(kns[j] * jnp.where(lo_r, eds[0], eds[1]))
        gls.append([jnp.where(lane_row < GD_DK, glh[0][c], glh[1][c]) for c in range(nch)])

    stacked = [jnp.concatenate([p_bs[2 * j], p_bs[2 * j + 1], kds[j].T.astype(bf16)], axis=0)
               for j in range(nsl)]
    states = [s_scr[j] for j in range(nsl)]
    outs = [[] for _ in range(nsl)]
    for c in range(nch):
        rs = slice(c * L, (c + 1) * L)
        for j in range(nsl):
            s_prev = states[j]
            ws_qs = _bdot(jnp.concatenate([ws[j][rs], qgs[j][rs]], axis=0), s_prev)
            db = (us[j][rs] - ws_qs[0:L]).astype(bf16)
            pd_upd = jnp.dot(stacked[j][:, rs], db, preferred_element_type=f32)
            outs[j].append(ws_qs[L:2 * L] + jnp.where(lo_l, pd_upd[0:L], pd_upd[L:2 * L]))
            states[j] = gls[j][c] * s_prev + jnp.where(blockdiag, pd_upd[2 * L:], 0.0)

    for j in range(nsl):
        sl = slice(LANES * j, LANES * (j + 1))
        s_scr[j] = states[j]
        o = jnp.concatenate(outs[j], axis=0) if nch > 1 else outs[j][0]
        m_lo, m_hi = half_sums(o * o, lo_r)
        inv = jnp.where(lo_r, lax.rsqrt(m_lo * (1.0 / GD_DV) + RMS_EPS), lax.rsqrt(m_hi * (1.0 / GD_DV) + RMS_EPS))
        zs = z[:, sl]
        o_ref[:, sl] = (o * inv * g_ref[:, sl] * (zs * _sigmoid(zs))).astype(o_ref.dtype)


def _gdn(x2d, w_gd, B, S, conv_w, a_log, dt_bias, norm_g):
    T = B * S
    L = min(GD_ROWS, S)
    nc = S // L
    KW = GD_K_W
    alog = jnp.zeros((1, LANES), f32).at[0, SM_GD_A:SM_GD_A + GD_HEADS].set(a_log)
    dtb = jnp.zeros((1, LANES), f32).at[0, SM_GD_A:SM_GD_A + GD_HEADS].set(dt_bias)
    g_row = jnp.tile(norm_g, GD_HEADS).reshape(1, GD_V_W)
    full = lambda shp: pl.BlockSpec(shp, lambda b, c: (0,) * len(shp))
    return pl.pallas_call(
        _gd_kernel,
        out_shape=jax.ShapeDtypeStruct((T, GD_V_W), bf16),
        grid=(B, nc),
        in_specs=[pl.BlockSpec((L, D_MODEL), lambda b, c: (b * nc + c, 0)), full(w_gd.shape),
                  full((CONV_WIDTH, 3 * KW)), full((1, LANES)), full((1, LANES)), full((1, GD_V_W))],
        out_specs=pl.BlockSpec((L, GD_V_W), lambda b, c: (b * nc + c, 0)),
        scratch_shapes=[pltpu.VMEM((L + SUBLANES, 3 * KW), f32),
                        pltpu.VMEM((KW // LANES, LANES, LANES), f32)],
        compiler_params=_cparams("parallel", "arbitrary"),
        name="gated_deltanet",
    )(x2d, w_gd, conv_w, alog, dtb, g_row)


def _merge_kernel(ha_ref, hb_ref, hc_ref, x_ref, wg_ref, wa_ref, wb_ref, wc_ref, wo_ref, lg_ref, lb_ref,
                  rw_ref, rb_ref, x1_ref, x1p_ref, pk_ref, cnt_ref, prev, carry):
    D = x_ref.shape[1]
    i = pl.program_id(0)

    @pl.when(i == 0)
    def _():
        prev[...] = jnp.zeros(prev.shape, f32)
        carry[...] = jnp.zeros(carry.shape, f32)

    route = _route_stages(prev[...], jnp.where(i > 0, 1.0, 0.0), rw_ref, rb_ref, pk_ref, cnt_ref, carry)
    xb = x_ref[...].astype(bf16)

    next(route)
    g0 = jnp.dot(xb, wg_ref[:, 0:D], preferred_element_type=f32)
    next(route)
    g1 = jnp.dot(xb, wg_ref[:, D:2 * D], preferred_element_type=f32)
    next(route)
    g2 = jnp.dot(xb, wg_ref[:, 2 * D:3 * D], preferred_element_type=f32)
    next(route)
    bra = jnp.dot(ha_ref[...], wa_ref[...], preferred_element_type=f32)
    next(route)
    brb = jnp.dot(hb_ref[...], wb_ref[...], preferred_element_type=f32)
    brc = jnp.dot(hc_ref[...], wc_ref[...], preferred_element_type=f32)
    next(route)
    merged = _sigmoid(g0) * bra + _sigmoid(g1) * brb + _sigmoid(g2) * brc
    y = jnp.dot(merged.astype(bf16), wo_ref[...], preferred_element_type=f32)
    for _ in route:
        pass
    out = _layer_norm(DN_ALPHA * x_ref[...] + y, lg_ref[...], lb_ref[...])
    x1_ref[...] = out
    x1p_ref[...] = _pack_halves(out)
    prev[...] = out


def _merge(ha, hb, hc, x2d, w_gate, wa, wb, wc, wo, lg, lb, router_w, router_b):
    T = x2d.shape[0]
    D = D_MODEL
    tm = min(512, T)
    nt = T // tm
    rw, rb = _router_params(router_w, router_b)
    full = lambda shp: pl.BlockSpec(shp, lambda i: (0,) * len(shp))
    held = lambda shp: pl.BlockSpec(shp, lambda i: (0,) * len(shp), pipeline_mode=pl.Buffered(1))
    rows = lambda w: pl.BlockSpec((tm, w), lambda i: (jnp.minimum(i, nt - 1), 0))
    routed = pl.BlockSpec((tm, LANES), lambda i: (jnp.maximum(i - 1, 0), 0))
    return pl.pallas_call(
        _merge_kernel,
        out_shape=(jax.ShapeDtypeStruct((T, D), f32), jax.ShapeDtypeStruct((T, D // 2), jnp.uint32),
                   jax.ShapeDtypeStruct((T, LANES), f32), jax.ShapeDtypeStruct((1, LANES), f32)),
        grid=(nt + 1,),
        in_specs=[rows(LRU_WIDTH), rows(ML_V_W), rows(GD_V_W), rows(D), held(w_gate.shape),
                  held((LRU_WIDTH, D)), held((ML_V_W, D)), held((GD_V_W, D)), held((D, D)),
                  full((1, D)), full((1, D)), held((D, LANES)), full((1, LANES))],
        out_specs=(rows(D), rows(D // 2), routed, full((1, LANES))),
        scratch_shapes=[pltpu.VMEM((tm, D), f32), pltpu.VMEM((1, LANES), f32)],
        compiler_params=_cparams("arbitrary"),
        name="merge_outproj_ln_route",
    )(ha, hb, hc, x2d, w_gate, wa.astype(bf16), wb.astype(bf16), wc.astype(bf16),
      wo.astype(bf16), lg.reshape(1, D), lb.reshape(1, D), rw, rb)


PK_IDX = 0
PK_GATE = 4
PK_RANK = 8


def _route_stages(x, weight, rw_ref, rb_ref, pk_ref, cnt_ref, carry):
    tr = x.shape[0]
    x_hi, x_lo = _split_bf16(x, 3)
    w_hi, w_lo = _split_bf16(rw_ref[...], 3)
    prod = jnp.dot(jnp.concatenate([x_hi, x_lo], axis=0), jnp.concatenate([w_hi, w_lo], axis=1),
                   preferred_element_type=f32)
    logits = prod[0:tr, 0:LANES] + (prod[tr:, 0:LANES] + prod[0:tr, LANES:]) + rb_ref[...]
    lane = lax.broadcasted_iota(jnp.int32, (tr, LANES), 1).astype(f32)
    vals = logits
    idxs, tops = [], []
    yield
    for _ in range(TOP_K):
        m = jnp.max(vals, axis=1, keepdims=True)
        idx = jnp.min(jnp.where(vals == m, lane, float(LANES)), axis=1, keepdims=True)
        idxs.append(idx)
        tops.append(m)
        vals = jnp.where(lane == idx, NEG_BIG * 2.0, vals)
        yield
    es = [jnp.exp(t - tops[0]) for t in tops]
    tot = es[0] + es[1] + es[2] + es[3]
    onehots = [lane == idx for idx in idxs]
    sel = jnp.zeros((tr, LANES), f32)
    for oh in onehots:
        sel = sel + oh.astype(f32)
    ri = lax.broadcasted_iota(jnp.int32, (tr, tr), 0)
    ci = lax.broadcasted_iota(jnp.int32, (tr, tr), 1)
    before = jnp.dot((ri > ci).astype(bf16), sel.astype(bf16), preferred_element_type=f32) + carry[...]
    carry[...] = carry[...] + weight * jnp.sum(sel, axis=0, keepdims=True)
    cnt_ref[...] = carry[...]
    yield
    packed = jnp.zeros((tr, LANES), f32)
    for kk in range(TOP_K):
        rank = jnp.sum(jnp.where(onehots[kk], before, 0.0), axis=1, keepdims=True)
        packed = jnp.where(lane == float(PK_IDX + kk), idxs[kk], packed)
        packed = jnp.where(lane == float(PK_GATE + kk), es[kk] / tot, packed)
        packed = jnp.where(lane == float(PK_RANK + kk), rank, packed)
    pk_ref[...] = packed


def _router_params(router_w, router_b):
    rw = jnp.zeros((D_MODEL, LANES), f32).at[:, :N_EXPERTS].set(router_w)
    rb = jnp.full((1, LANES), NEG_BIG, f32).at[0, :N_EXPERTS].set(router_b)
    return rw, rb


def _expert_kernel(be_ref, nu_ref, nv_ref, x_ref, w1_ref, b1_ref, w2_ref, b2_ref, o_ref, w1b, w2b):
    i = pl.program_id(0)
    used = i < nu_ref[0]
    new_expert = jnp.logical_or(i == 0, be_ref[i] != be_ref[jnp.maximum(i - 1, 0)])

    @pl.when(jnp.logical_and(used, new_expert))
    def _():
        w1b[...] = w1_ref[0, 0].astype(bf16)
        w2b[...] = w2_ref[0, 0].astype(bf16)

    @pl.when(used)
    def _():
        rows = lax.broadcasted_iota(jnp.int32, x_ref.shape, 0)
        xp = jnp.where(rows < nv_ref[i], x_ref[...], jnp.uint32(0))
        x = _unpack_halves(xp).astype(bf16)
        hdn = jnp.dot(x, w1b[...], preferred_element_type=f32) + b1_ref[0, 0]
        glu = jnp.minimum(hdn[:, :D_FF], SWIGLU_LIMIT)
        lin = jnp.clip(hdn[:, D_FF:], -SWIGLU_LIMIT, SWIGLU_LIMIT)
        act = glu * _sigmoid(SWIGLU_ALPHA * glu) * (lin + 1.0)
        y = jnp.dot(act.astype(bf16), w2b[...], preferred_element_type=f32) + b2_ref[0, 0]
        o_ref[...] = _pack_halves(y)

    @pl.when(i >= nu_ref[0])
    def _():
        o_ref[...] = jnp.zeros(o_ref.shape, o_ref.dtype)


def _experts(xs, block_e, n_used, n_valid, layer, w1, b1, w2, b2):
    P = xs.shape[0]
    nb = P // MOE_BLOCK
    D = D_MODEL
    nl = w1.shape[0]
    grid_spec = pltpu.PrefetchScalarGridSpec(
        num_scalar_prefetch=3,
        grid=(nb,),
        in_specs=[pl.BlockSpec((MOE_BLOCK, D // 2), lambda i, be, nu, nv: (i, 0)),
                  pl.BlockSpec((1, 1, D, 2 * D_FF), lambda i, be, nu, nv: (layer, be[i], 0, 0)),
                  pl.BlockSpec((1, 1, 1, 2 * D_FF), lambda i, be, nu, nv: (layer, be[i], 0, 0)),
                  pl.BlockSpec((1, 1, D_FF, D), lambda i, be, nu, nv: (layer, be[i], 0, 0)),
                  pl.BlockSpec((1, 1, 1, D), lambda i, be, nu, nv: (layer, be[i], 0, 0))],
        out_specs=pl.BlockSpec((MOE_BLOCK, D // 2), lambda i, be, nu, nv: (i, 0)),
        scratch_shapes=[pltpu.VMEM((D, 2 * D_FF), bf16), pltpu.VMEM((D_FF, D), bf16)],
    )
    return pl.pallas_call(
        _expert_kernel,
        out_shape=jax.ShapeDtypeStruct((P, D // 2), jnp.uint32),
        grid_spec=grid_spec,
        compiler_params=_cparams("arbitrary"),
        name="experts",
    )(block_e, n_used, n_valid, xs, w1, b1.reshape(nl, N_EXPERTS, 1, 2 * D_FF), w2,
      b2.reshape(nl, N_EXPERTS, 1, D))


def _combine_kernel(y0_ref, y1_ref, y2_ref, y3_ref, pk_ref, x_ref, lg_ref, lb_ref, o_ref):
    pk = pk_ref[...]
    y = pk[:, PK_GATE:PK_GATE + 1] * _unpack_halves(y0_ref[...])
    for kk, y_ref in ((1, y1_ref), (2, y2_ref), (3, y3_ref)):
        y = y + pk[:, PK_GATE + kk:PK_GATE + kk + 1] * _unpack_halves(y_ref[...])
    o_ref[...] = _layer_norm(DN_ALPHA * x_ref[...] + y, lg_ref[...], lb_ref[...])


def _combine(yg, packed, x1, lg, lb):
    T, D = x1.shape
    tm = min(1024, T)
    nt = T // tm
    full = lambda shp: pl.BlockSpec(shp, lambda i: (0,) * len(shp))
    choice = lambda kk: pl.BlockSpec((tm, D // 2), lambda i: (kk * nt + i, 0))
    return pl.pallas_call(
        _combine_kernel,
        out_shape=jax.ShapeDtypeStruct((T, D), f32),
        grid=(nt,),
        in_specs=[choice(0), choice(1), choice(2), choice(3), pl.BlockSpec((tm, LANES), lambda i: (i, 0)),
                  pl.BlockSpec((tm, D), lambda i: (i, 0)), full((1, D)), full((1, D))],
        out_specs=pl.BlockSpec((tm, D), lambda i: (i, 0)),
        compiler_params=_cparams("parallel"),
        name="combine_ln",
    )(yg, yg, yg, yg, packed, x1, lg.reshape(1, D), lb.reshape(1, D))


def _sc_mesh():
    return plsc.VectorSubcoreMesh(core_axis_name="c", subcore_axis_name="s")


def _sc_dispatch(x, posk, n_slots):
    t_tokens, w = x.shape

    @functools.partial(pl.kernel, out_type=jax.ShapeDtypeStruct((n_slots, w), x.dtype), mesh=_sc_mesh(),
                       name="sc_dispatch")
    def k(x_hbm, p_hbm, o_hbm):
        def body(x_vmem, p_vmem):
            for kk in range(TOP_K):
                pltpu.sync_copy(x_vmem, o_hbm.at[p_vmem.at[kk]])

        pltpu.emit_pipeline(
            body,
            grid=(t_tokens // SC_ROWS,),
            in_specs=[pl.BlockSpec((SC_ROWS, w), lambda i: (i, 0), pipeline_mode=pl.Buffered(1)),
                      pl.BlockSpec((TOP_K, SC_ROWS), lambda i: (0, i))],
            out_specs=[],
            core_axis_name=("c", "s"),
            dimension_semantics=(pltpu.PARALLEL,),
        )(x_hbm, p_hbm)

    return k(x, posk)


def _sc_gather(table, idx):
    n = idx.shape[0]
    w = table.shape[1]

    @functools.partial(pl.kernel, out_type=jax.ShapeDtypeStruct((n, w), table.dtype), mesh=_sc_mesh(),
                       name="sc_gather")
    def k(t_hbm, i_hbm, o_hbm):
        def body(i_vmem, o_vmem):
            pltpu.sync_copy(t_hbm.at[i_vmem.at[0]], o_vmem)

        pltpu.emit_pipeline(
            body,
            grid=(n // SC_ROWS,),
            in_specs=[pl.BlockSpec((1, SC_ROWS), lambda i: (0, i))],
            out_specs=[pl.BlockSpec((SC_ROWS, w), lambda i: (i, 0), pipeline_mode=pl.Buffered(1))],
            core_axis_name=("c", "s"),
            dimension_semantics=(pltpu.PARALLEL,),
        )(i_hbm, o_hbm)

    return k(table, idx.reshape(1, n))


def _moe(x1, x1p, packed, cnt, layer, w1, b1, w2, b2, lg, lb):
    T, D = x1.shape
    A = T * TOP_K
    idx = packed[:, PK_IDX:PK_IDX + TOP_K].astype(jnp.int32)
    rank = packed[:, PK_RANK:PK_RANK + TOP_K].astype(jnp.int32)
    counts = cnt[0, :N_EXPERTS].astype(jnp.int32)
    padded = ((counts + MOE_BLOCK - 1) // MOE_BLOCK) * MOE_BLOCK
    pad_end = jnp.cumsum(padded)
    pad_start = pad_end - padded
    posk = (pad_start[idx] + rank).T
    n_blocks = -(-A // MOE_BLOCK) + N_EXPERTS
    P = n_blocks * MOE_BLOCK
    starts = jnp.arange(n_blocks, dtype=jnp.int32) * MOE_BLOCK
    block_e = jnp.minimum(jnp.sum((pad_end[None, :] <= starts[:, None]).astype(jnp.int32), axis=1), N_EXPERTS - 1)
    n_used = (pad_end[-1:] // MOE_BLOCK).astype(jnp.int32)
    n_valid = jnp.clip(pad_start[block_e] + counts[block_e] - starts, 0, MOE_BLOCK).astype(jnp.int32)
    xs = _sc_dispatch(x1p, posk, P)
    ys = _experts(xs, block_e, n_used, n_valid, layer, w1, b1, w2, b2)
    yg = _sc_gather(ys, posk.reshape(A))
    return _combine(yg, packed, x1, lg, lb)


def _layer(x2d, B, S, p, layer, stacked):
    w = _split_w_in(p['w_in'])
    ha = _lru(x2d, w['lru'], B, S, p['lru_conv_w'], p['lru_conv_b'], p['lru_wa'], p['lru_ba'], p['lru_wx'],
              p['lru_bx'], p['lru_lambda'])
    hb = _mlstm(x2d, w['ml'], B, S, p['ml_i_bias'], p['ml_f_bias'], p['ml_norm_g'])
    hc = _gdn(x2d, w['gd'], B, S, p['gd_conv_w'], p['gd_a_log'], p['gd_dt_bias'], p['gd_norm_g'])
    x1, x1p, packed, cnt = _merge(ha, hb, hc, x2d, w['gate'], p['w_br_lru'], p['w_br_ml'], p['w_br_gd'],
                                  p['w_out'], p['ln1_g'], p['ln1_b'], p['router_w'], p['router_b'])
    return _moe(x1, x1p, packed, cnt, layer, stacked['exp_w1'], stacked['exp_b1'],
                stacked['exp_w2'], stacked['exp_b2'], p['ln2_g'], p['ln2_b'])


def kernel(x, w_in, lru_conv_w, lru_conv_b, lru_wa, lru_ba, lru_wx, lru_bx, lru_lambda, ml_i_bias, ml_f_bias, ml_norm_g, gd_conv_w, gd_a_log, gd_dt_bias, gd_norm_g, w_br_lru, w_br_ml, w_br_gd, w_out, ln1_g, ln1_b, router_w, router_b, exp_w1, exp_b1, exp_w2, exp_b2, ln2_g, ln2_b):
    B, S, D = x.shape
    params = dict(w_in=w_in, lru_conv_w=lru_conv_w, lru_conv_b=lru_conv_b, lru_wa=lru_wa, lru_ba=lru_ba,
                  lru_wx=lru_wx, lru_bx=lru_bx, lru_lambda=lru_lambda, ml_i_bias=ml_i_bias,
                  ml_f_bias=ml_f_bias, ml_norm_g=ml_norm_g, gd_conv_w=gd_conv_w, gd_a_log=gd_a_log,
                  gd_dt_bias=gd_dt_bias, gd_norm_g=gd_norm_g, w_br_lru=w_br_lru, w_br_ml=w_br_ml,
                  w_br_gd=w_br_gd, w_out=w_out, ln1_g=ln1_g, ln1_b=ln1_b, router_w=router_w,
                  router_b=router_b, ln2_g=ln2_g, ln2_b=ln2_b)
    stacked = dict(exp_w1=exp_w1, exp_b1=exp_b1, exp_w2=exp_w2, exp_b2=exp_b2)
    h = x.reshape(B * S, D)
    for l in range(w_in.shape[0]):
        h = _layer(h, B, S, {k: v[l] for k, v in params.items()}, l, stacked)
    return h.reshape(B, S, D)
```

```python
import functools

import jax
import jax.numpy as jnp
from jax import lax
from jax.experimental import pallas as pl
from jax.experimental.pallas import tpu as pltpu
from jax.experimental.pallas import tpu_sc as plsc

f32 = jnp.float32
bf16 = jnp.bfloat16

D_MODEL = 1024
LRU_WIDTH = 1024
LRU_BLOCKS = 8
LRU_C = 8.0
CONV_WIDTH = 4
ML_HEADS = 4
ML_DQK = 64
ML_DV = 128
GD_HEADS = 8
GD_DK = 64
GD_DV = 64
N_EXPERTS = 32
TOP_K = 4
D_FF = 1024
SWIGLU_LIMIT = 7.0
SWIGLU_ALPHA = 1.702
MOE_BLOCK = 512
N_BRANCH = 3
DEPTH = 2
DN_ALPHA = (2.0 * DEPTH) ** 0.25
LN_EPS = 1e-5
RMS_EPS = 1e-6

ML_QK_W = ML_HEADS * ML_DQK
ML_V_W = ML_HEADS * ML_DV
GD_K_W = GD_HEADS * GD_DK
GD_V_W = GD_HEADS * GD_DV
IN_SPLITS = (LRU_WIDTH, LRU_WIDTH, ML_QK_W, ML_QK_W, ML_V_W, ML_HEADS, ML_HEADS, ML_V_W,
             GD_K_W, GD_K_W, GD_V_W, GD_HEADS, GD_HEADS, GD_V_W, N_BRANCH * D_MODEL)

LANES = 128
SUBLANES = 8
NEG_BIG = -1e30

SM_ML_I = 0
SM_ML_F = 4
SM_GD_A = 8
SM_GD_B = 16

ML_CHUNK = 256
GD_CHUNK = 64
GD_ROWS = 256
GD_SOLVE_PASSES = 1
VMEM_LIMIT = 56 * 1024 * 1024
SC_ROWS = 128
COMBINE_PARTS = 2


def _cparams(*sem):
    return pltpu.CompilerParams(dimension_semantics=sem, vmem_limit_bytes=VMEM_LIMIT)


def _cummax_rows(x, seg):
    pos = lax.broadcasted_iota(jnp.int32, x.shape, 0) & (seg - 1)
    step = 1
    while step < seg:
        x = jnp.maximum(x, jnp.where(pos >= step, pltpu.roll(x, step, 0), NEG_BIG))
        step *= 2
    return x


def _cumsum_rows(x, seg):
    pos = lax.broadcasted_iota(jnp.int32, x.shape, 0) & (seg - 1)
    step = 1
    while step < seg:
        x = x + jnp.where(pos >= step, pltpu.roll(x, step, 0), 0.0)
        step *= 2
    return x


def _bdot(a, b):
    return jnp.dot(a.astype(bf16), b.astype(bf16), preferred_element_type=f32)


def _bdot_nt(a, b):
    return lax.dot_general(a.astype(bf16), b.astype(bf16), (((1,), (1,)), ((), ())),
                           preferred_element_type=f32)


def _bdot_tn(a, b):
    return lax.dot_general(a.astype(bf16), b.astype(bf16), (((0,), (0,)), ((), ())),
                           preferred_element_type=f32)


def _split_bf16(x, passes):
    hi = x.astype(bf16)
    if passes == 1:
        return (hi,)
    return (hi, (x - hi.astype(f32)).astype(bf16))


def _mdot(xs, ys):
    out = jnp.dot(xs[0], ys[0], preferred_element_type=f32)
    if len(xs) > 1:
        out = out + jnp.dot(xs[1], ys[0], preferred_element_type=f32)
        out = out + jnp.dot(xs[0], ys[1], preferred_element_type=f32)
    return out


def _sigmoid(x):
    return 1.0 / (1.0 + jnp.exp(-x))


def _softplus(x):
    return jnp.maximum(x, 0.0) + jnp.log1p(jnp.exp(-jnp.abs(x)))


def _log_sigmoid(x):
    return jnp.minimum(x, 0.0) - jnp.log1p(jnp.exp(-jnp.abs(x)))


def _pack_halves(v):
    n = v.shape[1] // 2
    hi = lax.bitcast_convert_type(v[:, :n].astype(bf16).astype(f32), jnp.uint32)
    lo = lax.bitcast_convert_type(v[:, n:].astype(bf16).astype(f32), jnp.uint32)
    return hi | (lo >> 16)


def _unpack_halves(p):
    hi = lax.bitcast_convert_type(p & jnp.uint32(0xFFFF0000), f32)
    lo = lax.bitcast_convert_type(p << 16, f32)
    return jnp.concatenate([hi, lo], axis=1)


def _layer_norm(z, g, b):
    mu = jnp.mean(z, axis=-1, keepdims=True)
    zc = z - mu
    var = jnp.mean(zc * zc, axis=-1, keepdims=True)
    return zc * lax.rsqrt(var + LN_EPS) * g + b


def _split_w_in(w):
    pts = []
    acc = 0
    for s in IN_SPLITS[:-1]:
        acc += s
        pts.append(acc)
    (lru_x, lru_y, ml_q, ml_k, ml_v, ml_i, ml_f, ml_o,
     gd_q, gd_k, gd_v, gd_a, gd_b, gd_z, gate) = jnp.split(w, pts, axis=1)
    pad = jnp.zeros((w.shape[0], LANES - 2 * ML_HEADS - 2 * GD_HEADS), w.dtype)
    small = jnp.concatenate([ml_i, ml_f, gd_a, gd_b, pad], axis=1)
    cat = lambda *cols: jnp.concatenate(cols, axis=1).astype(bf16)
    return dict(lru=cat(lru_x, lru_y), ml=cat(ml_q, ml_k, ml_v, ml_o, small),
                gd=cat(gd_q, gd_k, gd_v, gd_z, small), gate=gate.astype(bf16))


def _project(x_ref, w_ref):
    return jnp.dot(x_ref[...].astype(bf16), w_ref[...], preferred_element_type=f32)


def _lru_kernel(xin_ref, xnext_ref, w_ref, cw_ref, cb_ref, wcat_ref, ba_ref, bx_ref, lam_ref, o_ref,
                xext, pr_even, pr_odd, a_scr, u_scr, carry):
    ts = xin_ref.shape[0]
    W = a_scr.shape[1]
    bw = W // LRU_BLOCKS
    step = pl.program_id(0) * pl.num_programs(1) + pl.program_id(1)

    @pl.when(step == 0)
    def _():
        pr_even[...] = _project(xin_ref, w_ref)

    @pl.when(pl.program_id(1) == 0)
    def _():
        xext[0:SUBLANES, :] = jnp.zeros((SUBLANES, W), f32)
        carry[...] = jnp.zeros((1, W), f32)

    def tile(cur, nxt):
        x = cur[:, 0:W]
        xext[SUBLANES:SUBLANES + ts, :] = x
        xa = (cw_ref[3:4, :] * x
              + cw_ref[2:3, :] * xext[SUBLANES - 1:SUBLANES - 1 + ts, :]
              + cw_ref[1:2, :] * xext[SUBLANES - 2:SUBLANES - 2 + ts, :]
              + cw_ref[0:1, :] * xext[SUBLANES - 3:SUBLANES - 3 + ts, :]) + cb_ref[...]
        xext[0:SUBLANES, :] = xext[ts:ts + SUBLANES, :]

        xnb = xnext_ref[...].astype(bf16)
        pw = 2 * W // LRU_BLOCKS
        cdec = -LRU_C * _softplus(-lam_ref[...])
        for h in range(LRU_BLOCKS):
            sl = slice(bw * h, bw * (h + 1))
            psl = slice(pw * h, pw * (h + 1))
            nxt[:, psl] = jnp.dot(xnb, w_ref[:, psl], preferred_element_type=f32)
            xh = xa[:, sl]
            g = jnp.dot(xh.astype(bf16), wcat_ref[h], preferred_element_type=f32)
            r = _sigmoid(g[:, :bw] + ba_ref[:, sl])
            ig = _sigmoid(g[:, bw:] + bx_ref[:, sl])
            log_a = r * cdec[:, sl]
            a = jnp.exp(log_a)
            a_scr[:, sl] = a
            u_scr[:, sl] = jnp.sqrt(jnp.tanh(-log_a) * (1.0 + a * a)) * (ig * xh)

        row = lax.broadcasted_iota(jnp.int32, (SUBLANES, W), 0)

        def body(g, cr):
            off = pl.multiple_of(g * SUBLANES, SUBLANES)
            A = a_scr[pl.ds(off, SUBLANES), :]
            U = u_scr[pl.ds(off, SUBLANES), :]
            for s in (1, 2, 4):
                a_sh = pltpu.roll(A, s, 0)
                u_sh = pltpu.roll(U, s, 0)
                m = row >= s
                U = jnp.where(m, A * u_sh + U, U)
                A = jnp.where(m, A * a_sh, A)
            H = A * cr + U
            u_scr[pl.ds(off, SUBLANES), :] = H
            return H[SUBLANES - 1:SUBLANES, :]

        carry[...] = lax.fori_loop(0, ts // SUBLANES, body, carry[...], unroll=4)
        o_ref[...] = (u_scr[...] * jax.nn.gelu(cur[:, W:2 * W])).astype(o_ref.dtype)

    @pl.when(lax.rem(step, 2) == 0)
    def _():
        tile(pr_even, pr_odd)

    @pl.when(lax.rem(step, 2) == 1)
    def _():
        tile(pr_odd, pr_even)


def _lru(x2d, w_lru, B, S, cw, cb, wa, ba, wx, bx, lam):
    T = B * S
    ts = min(512, S)
    nt = S // ts
    W = LRU_WIDTH
    wcat = jnp.concatenate([wa, wx], axis=-1).astype(bf16)
    row = lambda v: v.reshape(1, W)
    full = lambda shp: pl.BlockSpec(shp, lambda b, c: (0,) * len(shp))
    last = B * nt - 1
    return pl.pallas_call(
        _lru_kernel,
        out_shape=jax.ShapeDtypeStruct((T, W), bf16),
        grid=(B, nt),
        in_specs=[pl.BlockSpec((ts, D_MODEL), lambda b, c: (b * nt + c, 0)),
                  pl.BlockSpec((ts, D_MODEL), lambda b, c: (jnp.minimum(b * nt + c + 1, last), 0)),
                  full(w_lru.shape), full((CONV_WIDTH, W)), full((1, W)), full(wcat.shape),
                  full((1, W)), full((1, W)), full((1, W))],
        out_specs=pl.BlockSpec((ts, W), lambda b, c: (b * nt + c, 0)),
        scratch_shapes=[pltpu.VMEM((ts + SUBLANES, W), f32), pltpu.VMEM((ts, 2 * W), f32),
                        pltpu.VMEM((ts, 2 * W), f32), pltpu.VMEM((ts, W), f32), pltpu.VMEM((ts, W), f32),
                        pltpu.VMEM((1, W), f32)],
        compiler_params=_cparams("arbitrary", "arbitrary"),
        name="rg_lru",
    )(x2d, x2d, w_lru, cw, row(cb), wcat, row(ba), row(bx), row(lam))


def _ml_kernel(xin_ref, xnext_ref, w_ref, bias_ref, g_ref, o_ref, c_scr, n_scr, m_scr, pr_even, pr_odd):
    step = pl.program_id(0) * pl.num_programs(1) + pl.program_id(1)

    @pl.when(step == 0)
    def _():
        pr_even[...] = _project(xin_ref, w_ref)

    @pl.when(pl.program_id(1) == 0)
    def _():
        c_scr[...] = jnp.zeros(c_scr.shape, f32)
        n_scr[...] = jnp.zeros(n_scr.shape, f32)
        m_scr[...] = jnp.zeros(m_scr.shape, f32)

    args = (xnext_ref, w_ref, bias_ref, g_ref, o_ref, c_scr, n_scr, m_scr)

    @pl.when(lax.rem(step, 2) == 0)
    def _():
        _ml_tile(pr_even, pr_odd, *args)

    @pl.when(lax.rem(step, 2) == 1)
    def _():
        _ml_tile(pr_odd, pr_even, *args)


def _ml_tile(cur, nxt, xnext_ref, w_ref, bias_ref, g_ref, o_ref, c_scr, n_scr, m_scr):
    L = cur.shape[0]
    xnb = xnext_ref[...].astype(bf16)
    pcols = [0, 2 * ML_QK_W, 2 * ML_QK_W + ML_V_W, 2 * ML_QK_W + 2 * ML_V_W, cur.shape[1]]

    def project_next(i):
        nxt[:, pcols[i]:pcols[i + 1]] = jnp.dot(xnb, w_ref[:, pcols[i]:pcols[i + 1]],
                                                preferred_element_type=f32)

    pr = cur[...]
    q = pr[:, 0:ML_QK_W]
    k = pr[:, ML_QK_W:2 * ML_QK_W]
    v = pr[:, 2 * ML_QK_W:2 * ML_QK_W + ML_V_W]
    og = pr[:, 2 * ML_QK_W + ML_V_W:2 * ML_QK_W + 2 * ML_V_W]
    sm = pr[:, 2 * ML_QK_W + 2 * ML_V_W:] + bias_ref[...]
    logf = _log_sigmoid(sm)
    ri = lax.broadcasted_iota(jnp.int32, (L, L), 0)
    ci = lax.broadcasted_iota(jnp.int32, (L, L), 1)
    tri = ri >= ci
    b_all = _cumsum_rows(logf, L)
    g_all = sm - pltpu.roll(b_all, LANES - (SM_ML_F - SM_ML_I), 1)
    cm_all = _cummax_rows(g_all, L)
    g_t = g_all.T
    kb = k.astype(bf16)
    lane_head = lax.broadcasted_iota(jnp.int32, q.shape, 1) // ML_DQK
    scale = ML_DQK ** -0.5
    heads = range(ML_HEADS)
    sls = [slice(ML_DV * h, ML_DV * (h + 1)) for h in heads]
    m_news, decays, w_cs, m_ts, dws, iws = [], [], [], [], [], []
    for h in heads:
        project_next(h)
        b_c = b_all[:, SM_ML_F + h:SM_ML_F + h + 1]
        g_c = g_all[:, SM_ML_I + h:SM_ML_I + h + 1]
        g_r = g_t[SM_ML_I + h:SM_ML_I + h + 1, :]
        b_last = b_c[L - 1:L, :]
        m_prev = m_scr[h:h + 1, 0:1]
        mm_c = jnp.maximum(m_prev, cm_all[:, SM_ML_I + h:SM_ML_I + h + 1])
        mm_last = mm_c[L - 1:L, :]
        m_news.append(b_last + mm_last)
        decays.append(jnp.exp(m_prev - mm_last))
        w_cs.append(jnp.exp(g_c - mm_last))
        m_ts.append(b_c + mm_c)
        dws.append(jnp.exp(jnp.where(tri, g_r - mm_c, NEG_BIG)))
        iws.append(jnp.exp(m_prev - mm_c))
    qms = [jnp.where(lane_head == h, q, 0.0) * scale for h in heads]
    qmbs = [qm.astype(bf16) for qm in qms]
    ss = [_bdot_nt(qmbs[h], kb) * dws[h] for h in heads]
    n_rows = [n_scr[h:h + 1, :] for h in heads]
    nums = [_bdot(ss[h], v[:, sls[h]]) + iws[h] * _bdot(qmbs[h], c_scr[:, sls[h]]) for h in heads]
    dens = [jnp.sum(ss[h], axis=1, keepdims=True) + iws[h] * jnp.sum(qms[h] * n_rows[h], axis=1, keepdims=True)
            for h in heads]
    hhs = [nums[h] / jnp.maximum(jnp.abs(dens[h]), jnp.exp(-m_ts[h])) for h in heads]
    for h in heads:
        c_scr[:, sls[h]] = decays[h] * c_scr[:, sls[h]] + _bdot_tn(kb, w_cs[h] * v[:, sls[h]])
        n_scr[h:h + 1, :] = decays[h] * n_rows[h] + jnp.sum(w_cs[h] * k, axis=0, keepdims=True)
        m_scr[h:h + 1, :] = jnp.broadcast_to(m_news[h], (1, LANES))
    for h in heads:
        ms = jnp.mean(hhs[h] * hhs[h], axis=1, keepdims=True)
        hn = hhs[h] * lax.rsqrt(ms + RMS_EPS) * g_ref[:, sls[h]]
        o_ref[:, sls[h]] = (hn * _sigmoid(og[:, sls[h]])).astype(o_ref.dtype)


def _mlstm(x2d, w_ml, B, S, i_bias, f_bias, norm_g):
    T = B * S
    L = min(ML_CHUNK, S)
    nc = S // L
    bias = jnp.zeros((1, LANES), f32)
    bias = bias.at[0, SM_ML_I:SM_ML_I + ML_HEADS].set(i_bias).at[0, SM_ML_F:SM_ML_F + ML_HEADS].set(f_bias)
    full = lambda shp: pl.BlockSpec(shp, lambda b, c: (0,) * len(shp))
    return pl.pallas_call(
        _ml_kernel,
        out_shape=jax.ShapeDtypeStruct((T, ML_V_W), bf16),
        grid=(B, nc),
        in_specs=[pl.BlockSpec((L, D_MODEL), lambda b, c: (b * nc + c, 0)),
                  pl.BlockSpec((L, D_MODEL), lambda b, c: (jnp.minimum(b * nc + c + 1, B * nc - 1), 0)),
                  full(w_ml.shape), full((1, LANES)), full((1, ML_V_W))],
        out_specs=pl.BlockSpec((L, ML_V_W), lambda b, c: (b * nc + c, 0)),
        scratch_shapes=[pltpu.VMEM((ML_QK_W, ML_V_W), f32), pltpu.VMEM((SUBLANES, ML_QK_W), f32),
                        pltpu.VMEM((SUBLANES, LANES), f32),
                        pltpu.VMEM((L, w_ml.shape[1]), f32), pltpu.VMEM((L, w_ml.shape[1]), f32)],
        compiler_params=_cparams("arbitrary", "arbitrary"),
        name="mlstm",
    )(x2d, x2d, w_ml, bias, norm_g.reshape(1, ML_V_W))


def _gd_kernel(xin_ref, w_ref, cw_ref, alog_ref, dtb_ref, g_ref, o_ref, ext, s_scr):
    R = xin_ref.shape[0]
    KW = GD_K_W
    L = GD_CHUNK
    nch = R // L
    lsh = L.bit_length() - 1
    PW = nch * L
    nsl = KW // LANES
    passes = GD_SOLVE_PASSES

    @pl.when(pl.program_id(1) == 0)
    def _():
        ext[0:SUBLANES, :] = jnp.zeros((SUBLANES, ext.shape[1]), f32)
        s_scr[...] = jnp.zeros(s_scr.shape, f32)

    pr = _project(xin_ref, w_ref)
    z = pr[:, 3 * KW:4 * KW]
    sm = pr[:, 4 * KW:]
    ext[SUBLANES:SUBLANES + R, :] = pr[:, 0:3 * KW]
    xc = (cw_ref[3:4, :] * ext[SUBLANES:SUBLANES + R, :]
          + cw_ref[2:3, :] * ext[SUBLANES - 1:SUBLANES - 1 + R, :]
          + cw_ref[1:2, :] * ext[SUBLANES - 2:SUBLANES - 2 + R, :]
          + cw_ref[0:1, :] * ext[SUBLANES - 3:SUBLANES - 3 + R, :])
    ext[0:SUBLANES, :] = ext[R:R + SUBLANES, :]
    xc = xc * _sigmoid(xc)

    g_t = -jnp.exp(alog_ref[...]) * _softplus(sm + dtb_ref[...])
    beta_t = _sigmoid(sm)
    gc_all = _cumsum_rows(g_t, L)
    gc_t = gc_all.T

    pt = lax.broadcasted_iota(jnp.int32, (L, PW), 0)
    plane = lax.broadcasted_iota(jnp.int32, (L, PW), 1)
    ps = plane & (L - 1)
    pc = plane >> lsh
    tri_p = pt >= ps
    strict_p = pt > ps
    eye_p = jnp.where(pt == ps, 1.0, 0.0)
    blk8_p = (pt >> 3) == (ps >> 3)
    cmask_b = [jnp.where(pc == c, 1.0, 0.0).astype(bf16) for c in range(nch)]

    def pick(parts):
        out = parts[nch - 1]
        for c in range(nch - 2, -1, -1):
            out = jnp.where(pc == c, parts[c], out)
        return out

    def chunks(col):
        return [col[c * L:(c + 1) * L] for c in range(nch)]

    def bd(yb):
        return jnp.concatenate([yb * cmask_b[c] for c in range(nch)], axis=0)

    def pdot(xp, yp):
        return _mdot(_split_bf16(xp, passes), [bd(p) for p in _split_bf16(yp, passes)])

    lane_r = lax.broadcasted_iota(jnp.int32, (R, LANES), 1)
    lo_r = lane_r < GD_DK
    lane_l = lax.broadcasted_iota(jnp.int32, (L, LANES), 1)
    lo_l = lane_l < GD_DK
    lane_row = lax.broadcasted_iota(jnp.int32, (1, LANES), 1)
    r2 = lax.broadcasted_iota(jnp.int32, (LANES, LANES), 0)
    c2 = lax.broadcasted_iota(jnp.int32, (LANES, LANES), 1)
    blockdiag = (r2 < GD_DK) == (c2 < GD_DK)

    def half_sums(y, lo):
        s_lo = jnp.sum(jnp.where(lo, y, 0.0), axis=1, keepdims=True)
        s_hi = jnp.sum(jnp.where(lo, 0.0, y), axis=1, keepdims=True)
        return s_lo, s_hi

    def l2n(y):
        s_lo, s_hi = half_sums(y * y, lo_r)
        return y * jnp.where(lo_r, lax.rsqrt(s_lo + RMS_EPS), lax.rsqrt(s_hi + RMS_EPS))

    qns, kns, vss, a_ps, p_bs, gcols, bcols = [], [], [], [], [], [], []
    for j in range(nsl):
        qn = l2n(xc[:, LANES * j:LANES * (j + 1)]) * (GD_DK ** -0.5)
        kn = l2n(xc[:, KW + LANES * j:KW + LANES * (j + 1)])
        qns.append(qn)
        kns.append(kn)
        vss.append(xc[:, 2 * KW + LANES * j:2 * KW + LANES * (j + 1)])
        lhs = jnp.concatenate([jnp.where(lo_r, kn, 0.0), jnp.where(lo_r, 0.0, kn),
                               jnp.where(lo_r, qn, 0.0), jnp.where(lo_r, 0.0, qn)], axis=0)
        gram = _bdot_nt(lhs, kn)
        for e in range(2):
            h = 2 * j + e
            gcol = gc_all[:, SM_GD_A + h:SM_GD_A + h + 1]
            bcol = beta_t[:, SM_GD_B + h:SM_GD_B + h + 1]
            gc_r = gc_t[SM_GD_A + h:SM_GD_A + h + 1, :]
            gam_p = jnp.exp(jnp.where(tri_p, pick(chunks(gcol)) - gc_r, NEG_BIG))
            kk_p = pick(chunks(gram[e * R:(e + 1) * R]))
            qk_p = pick(chunks(gram[(2 + e) * R:(3 + e) * R]))
            a_ps.append(jnp.where(strict_p, pick(chunks(bcol)) * gam_p * kk_p, 0.0))
            p_bs.append((gam_p * qk_p).astype(bf16))
            gcols.append(gcol)
            bcols.append(bcol)

    ads = [jnp.where(blk8_p, a, 0.0) for a in a_ps]
    a2s = [pdot(ad, ad) for ad in ads]
    a4s = [pdot(a2, a2) for a2 in a2s]
    xs = [pdot(eye_p - ad, eye_p + a2) for ad, a2 in zip(ads, a2s)]
    xs = [pdot(x, eye_p + a4) for x, a4 in zip(xs, a4s)]
    for sh in range(3, lsh):
        msk = ((pt >> (sh + 1)) == (ps >> (sh + 1))) & (((pt >> sh) & 1) == 1) & (((ps >> sh) & 1) == 0)
        ts = [pdot(jnp.where(msk, a, 0.0), x) for a, x in zip(a_ps, xs)]
        xs = [x - pdot(x, t) for x, t in zip(xs, ts)]

    us, ws, qgs, kds, gls = [], [], [], [], []
    for j in range(nsl):
        uw, egs, eds, glh = [], [], [], []
        for e in range(2):
            h = 2 * j + e
            gcol, bcol = gcols[h], bcols[h]
            eg = jnp.exp(gcol)
            rhs = jnp.concatenate([bcol * vss[j], (bcol * eg) * kns[j]], axis=1)
            uw.append(_mdot([bd(p) for p in _split_bf16(xs[h], passes)], _split_bf16(rhs, passes)))
            lasts = [gcol[c * L + L - 1:c * L + L] for c in range(nch)]
            gcl = jnp.concatenate([jnp.broadcast_to(v, (L, 1)) for v in lasts], axis=0)
            egs.append(eg)
            eds.append(jnp.exp(gcl - gcol))
            glh.append([jnp.exp(v) for v in lasts])
        us.append(jnp.where(lo_r, uw[0][:, :LANES], uw[1][:, :LANES]))
        ws.append(jnp.where(lo_r, uw[0][:, LANES:], uw[1][:, LANES:]))
        qgs.append(qns[j] * jnp.where(lo_r, egs[0], egs[1]))
        kds.append(kns[j] * jnp.where(lo_r, eds[0], eds[1]))
        gls.append([jnp.where(lane_row < GD_DK, glh[0][c], glh[1][c]) for c in range(nch)])

    stacked = [jnp.concatenate([p_bs[2 * j], p_bs[2 * j + 1], kds[j].T.astype(bf16)], axis=0)
               for j in range(nsl)]
    states = [s_scr[j] for j in range(nsl)]
    outs = [[] for _ in range(nsl)]
    for c in range(nch):
        rs = slice(c * L, (c + 1) * L)
        for j in range(nsl):
            s_prev = states[j]
            ws_qs = _bdot(jnp.concatenate([ws[j][rs], qgs[j][rs]], axis=0), s_prev)
            db = (us[j][rs] - ws_qs[0:L]).astype(bf16)
            pd_upd = jnp.dot(stacked[j][:, rs], db, preferred_element_type=f32)
            outs[j].append(ws_qs[L:2 * L] + jnp.where(lo_l, pd_upd[0:L], pd_upd[L:2 * L]))
            states[j] = gls[j][c] * s_prev + jnp.where(blockdiag, pd_upd[2 * L:], 0.0)

    for j in range(nsl):
        sl = slice(LANES * j, LANES * (j + 1))
        s_scr[j] = states[j]
        o = jnp.concatenate(outs[j], axis=0) if nch > 1 else outs[j][0]
        m_lo, m_hi = half_sums(o * o, lo_r)
        inv = jnp.where(lo_r, lax.rsqrt(m_lo * (1.0 / GD_DV) + RMS_EPS), lax.rsqrt(m_hi * (1.0 / GD_DV) + RMS_EPS))
        zs = z[:, sl]
        o_ref[:, sl] = (o * inv * g_ref[:, sl] * (zs * _sigmoid(zs))).astype(o_ref.dtype)


def _gdn(x2d, w_gd, B, S, conv_w, a_log, dt_bias, norm_g):
    T = B * S
    L = min(GD_ROWS, S)
    nc = S // L
    KW = GD_K_W
    alog = jnp.zeros((1, LANES), f32).at[0, SM_GD_A:SM_GD_A + GD_HEADS].set(a_log)
    dtb = jnp.zeros((1, LANES), f32).at[0, SM_GD_A:SM_GD_A + GD_HEADS].set(dt_bias)
    g_row = jnp.tile(norm_g, GD_HEADS).reshape(1, GD_V_W)
    full = lambda shp: pl.BlockSpec(shp, lambda b, c: (0,) * len(shp))
    return pl.pallas_call(
        _gd_kernel,
        out_shape=jax.ShapeDtypeStruct((T, GD_V_W), bf16),
        grid=(B, nc),
        in_specs=[pl.BlockSpec((L, D_MODEL), lambda b, c: (b * nc + c, 0)), full(w_gd.shape),
                  full((CONV_WIDTH, 3 * KW)), full((1, LANES)), full((1, LANES)), full((1, GD_V_W))],
        out_specs=pl.BlockSpec((L, GD_V_W), lambda b, c: (b * nc + c, 0)),
        scratch_shapes=[pltpu.VMEM((L + SUBLANES, 3 * KW), f32),
                        pltpu.VMEM((KW // LANES, LANES, LANES), f32)],
        compiler_params=_cparams("parallel", "arbitrary"),
        name="gated_deltanet",
    )(x2d, w_gd, conv_w, alog, dtb, g_row)


def _merge_kernel(ha_ref, hb_ref, hc_ref, x_ref, wg_ref, wa_ref, wb_ref, wc_ref, wo_ref, lg_ref, lb_ref,
                  rw_ref, rb_ref, x1_ref, x1p_ref, pk_ref, cnt_ref, prev, carry):
    D = x_ref.shape[1]
    i = pl.program_id(0)

    @pl.when(i == 0)
    def _():
        prev[...] = jnp.zeros(prev.shape, f32)
        carry[...] = jnp.zeros(carry.shape, f32)

    route = _route_stages(prev[...], jnp.where(i > 0, 1.0, 0.0), rw_ref, rb_ref, pk_ref, cnt_ref, carry)
    xb = x_ref[...].astype(bf16)

    next(route)
    g0 = jnp.dot(xb, wg_ref[:, 0:D], preferred_element_type=f32)
    next(route)
    g1 = jnp.dot(xb, wg_ref[:, D:2 * D], preferred_element_type=f32)
    next(route)
    g2 = jnp.dot(xb, wg_ref[:, 2 * D:3 * D], preferred_element_type=f32)
    next(route)
    bra = jnp.dot(ha_ref[...], wa_ref[...], preferred_element_type=f32)
    next(route)
    brb = jnp.dot(hb_ref[...], wb_ref[...], preferred_element_type=f32)
    brc = jnp.dot(hc_ref[...], wc_ref[...], preferred_element_type=f32)
    next(route)
    merged = _sigmoid(g0) * bra + _sigmoid(g1) * brb + _sigmoid(g2) * brc
    y = jnp.dot(merged.astype(bf16), wo_ref[...], preferred_element_type=f32)
    for _ in route:
        pass
    out = _layer_norm(DN_ALPHA * x_ref[...] + y, lg_ref[...], lb_ref[...])
    x1_ref[...] = out
    x1p_ref[...] = _pack_halves(out)
    prev[...] = out


def _merge(ha, hb, hc, x2d, w_gate, wa, wb, wc, wo, lg, lb, router_w, router_b):
    T = x2d.shape[0]
    D = D_MODEL
    tm = min(512, T)
    nt = T // tm
    rw, rb = _router_params(router_w, router_b)
    full = lambda shp: pl.BlockSpec(shp, lambda i: (0,) * len(shp))
    held = lambda shp: pl.BlockSpec(shp, lambda i: (0,) * len(shp), pipeline_mode=pl.Buffered(1))
    rows = lambda w: pl.BlockSpec((tm, w), lambda i: (jnp.minimum(i, nt - 1), 0))
    routed = pl.BlockSpec((tm, LANES), lambda i: (jnp.maximum(i - 1, 0), 0))
    return pl.pallas_call(
        _merge_kernel,
        out_shape=(jax.ShapeDtypeStruct((T, D), f32), jax.ShapeDtypeStruct((T, D // 2), jnp.uint32),
                   jax.ShapeDtypeStruct((T, LANES), f32), jax.ShapeDtypeStruct((1, LANES), f32)),
        grid=(nt + 1,),
        in_specs=[rows(LRU_WIDTH), rows(ML_V_W), rows(GD_V_W), rows(D), held(w_gate.shape),
                  held((LRU_WIDTH, D)), held((ML_V_W, D)), held((GD_V_W, D)), held((D, D)),
                  full((1, D)), full((1, D)), held((D, LANES)), full((1, LANES))],
        out_specs=(rows(D), rows(D // 2), routed, full((1, LANES))),
        scratch_shapes=[pltpu.VMEM((tm, D), f32), pltpu.VMEM((1, LANES), f32)],
        compiler_params=_cparams("arbitrary"),
        name="merge_outproj_ln_route",
    )(ha, hb, hc, x2d, w_gate, wa.astype(bf16), wb.astype(bf16), wc.astype(bf16),
      wo.astype(bf16), lg.reshape(1, D), lb.reshape(1, D), rw, rb)


PK_IDX = 0
PK_GATE = 4
PK_RANK = 8


def _route_stages(x, weight, rw_ref, rb_ref, pk_ref, cnt_ref, carry):
    tr = x.shape[0]
    x_hi, x_lo = _split_bf16(x, 3)
    w_hi, w_lo = _split_bf16(rw_ref[...], 3)
    prod = jnp.dot(jnp.concatenate([x_hi, x_lo], axis=0), jnp.concatenate([w_hi, w_lo], axis=1),
                   preferred_element_type=f32)
    logits = prod[0:tr, 0:LANES] + (prod[tr:, 0:LANES] + prod[0:tr, LANES:]) + rb_ref[...]
    lane = lax.broadcasted_iota(jnp.int32, (tr, LANES), 1).astype(f32)
    vals = logits
    idxs, tops = [], []
    yield
    for _ in range(TOP_K):
        m = jnp.max(vals, axis=1, keepdims=True)
        idx = jnp.min(jnp.where(vals == m, lane, float(LANES)), axis=1, keepdims=True)
        idxs.append(idx)
        tops.append(m)
        vals = jnp.where(lane == idx, NEG_BIG * 2.0, vals)
        yield
    es = [jnp.exp(t - tops[0]) for t in tops]
    tot = es[0] + es[1] + es[2] + es[3]
    onehots = [lane == idx for idx in idxs]
    sel = jnp.zeros((tr, LANES), f32)
    for oh in onehots:
        sel = sel + oh.astype(f32)
    ri = lax.broadcasted_iota(jnp.int32, (tr, tr), 0)
    ci = lax.broadcasted_iota(jnp.int32, (tr, tr), 1)
    before = jnp.dot((ri > ci).astype(bf16), sel.astype(bf16), preferred_element_type=f32) + carry[...]
    carry[...] = carry[...] + weight * jnp.sum(sel, axis=0, keepdims=True)
    cnt_ref[...] = carry[...]
    yield
    packed = jnp.zeros((tr, LANES), f32)
    for kk in range(TOP_K):
        rank = jnp.sum(jnp.where(onehots[kk], before, 0.0), axis=1, keepdims=True)
        packed = jnp.where(lane == float(PK_IDX + kk), idxs[kk], packed)
        packed = jnp.where(lane == float(PK_GATE + kk), es[kk] / tot, packed)
        packed = jnp.where(lane == float(PK_RANK + kk), rank, packed)
    pk_ref[...] = packed


def _router_params(router_w, router_b):
    rw = jnp.zeros((D_MODEL, LANES), f32).at[:, :N_EXPERTS].set(router_w)
    rb = jnp.full((1, LANES), NEG_BIG, f32).at[0, :N_EXPERTS].set(router_b)
    return rw, rb


def _expert_kernel(be_ref, nu_ref, nv_ref, x_ref, w1_ref, b1_ref, w2_ref, b2_ref, o_ref, w1b, w2b):
    i = pl.program_id(0)
    used = i < nu_ref[0]
    new_expert = jnp.logical_or(i == 0, be_ref[i] != be_ref[jnp.maximum(i - 1, 0)])

    @pl.when(jnp.logical_and(used, new_expert))
    def _():
        w1b[...] = w1_ref[0, 0].astype(bf16)
        w2b[...] = w2_ref[0, 0].astype(bf16)

    @pl.when(used)
    def _():
        rows = lax.broadcasted_iota(jnp.int32, x_ref.shape, 0)
        xp = jnp.where(rows < nv_ref[i], x_ref[...], jnp.uint32(0))
        x = _unpack_halves(xp).astype(bf16)
        hdn = jnp.dot(x, w1b[...], preferred_element_type=f32) + b1_ref[0, 0]
        glu = jnp.minimum(hdn[:, :D_FF], SWIGLU_LIMIT)
        lin = jnp.clip(hdn[:, D_FF:], -SWIGLU_LIMIT, SWIGLU_LIMIT)
        act = glu * _sigmoid(SWIGLU_ALPHA * glu) * (lin + 1.0)
        y = jnp.dot(act.astype(bf16), w2b[...], preferred_element_type=f32) + b2_ref[0, 0]
        o_ref[...] = _pack_halves(y)

    @pl.when(i >= nu_ref[0])
    def _():
        o_ref[...] = jnp.zeros(o_ref.shape, o_ref.dtype)


def _experts(xs, block_e, n_used, n_valid, layer, w1, b1, w2, b2):
    P = xs.shape[0]
    nb = P // MOE_BLOCK
    D = D_MODEL
    nl = w1.shape[0]
    grid_spec = pltpu.PrefetchScalarGridSpec(
        num_scalar_prefetch=3,
        grid=(nb,),
        in_specs=[pl.BlockSpec((MOE_BLOCK, D // 2), lambda i, be, nu, nv: (i, 0)),
                  pl.BlockSpec((1, 1, D, 2 * D_FF), lambda i, be, nu, nv: (layer, be[i], 0, 0)),
                  pl.BlockSpec((1, 1, 1, 2 * D_FF), lambda i, be, nu, nv: (layer, be[i], 0, 0)),
                  pl.BlockSpec((1, 1, D_FF, D), lambda i, be, nu, nv: (layer, be[i], 0, 0)),
                  pl.BlockSpec((1, 1, 1, D), lambda i, be, nu, nv: (layer, be[i], 0, 0))],
        out_specs=pl.BlockSpec((MOE_BLOCK, D // 2), lambda i, be, nu, nv: (i, 0)),
        scratch_shapes=[pltpu.VMEM((D, 2 * D_FF), bf16), pltpu.VMEM((D_FF, D), bf16)],
    )
    return pl.pallas_call(
        _expert_kernel,
        out_shape=jax.ShapeDtypeStruct((P, D // 2), jnp.uint32),
        grid_spec=grid_spec,
        compiler_params=_cparams("arbitrary"),
        name="experts",
    )(block_e, n_used, n_valid, xs, w1, b1.reshape(nl, N_EXPERTS, 1, 2 * D_FF), w2,
      b2.reshape(nl, N_EXPERTS, 1, D))


def _combine_kernel(y0_ref, y1_ref, y2_ref, y3_ref, pk_ref, x_ref, lg_ref, lb_ref, *out_refs):
    o_ref = out_refs[-1]
    pk = pk_ref[...]
    y = pk[:, PK_GATE:PK_GATE + 1] * _unpack_halves(y0_ref[...])
    for kk, y_ref in ((1, y1_ref), (2, y2_ref), (3, y3_ref)):
        y = y + pk[:, PK_GATE + kk:PK_GATE + kk + 1] * _unpack_halves(y_ref[...])
    o_ref[...] = _layer_norm(DN_ALPHA * x_ref[...] + y, lg_ref[...], lb_ref[...])


def _combine(yg, packed, x1, lg, lb, part, out_prev):
    T, D = x1.shape
    tp = yg.shape[0] // TOP_K
    tm = min(1024, tp)
    nt = tp // tm
    off = part * nt
    full = lambda shp: pl.BlockSpec(shp, lambda i: (0,) * len(shp))
    choice = lambda kk: pl.BlockSpec((tm, D // 2), lambda i: (kk * nt + i, 0))
    in_specs = [choice(0), choice(1), choice(2), choice(3), pl.BlockSpec((tm, LANES), lambda i: (off + i, 0)),
                pl.BlockSpec((tm, D), lambda i: (off + i, 0)), full((1, D)), full((1, D))]
    args = [yg, yg, yg, yg, packed, x1, lg.reshape(1, D), lb.reshape(1, D)]
    aliases = {}
    if out_prev is not None:
        in_specs.append(pl.BlockSpec(memory_space=pl.ANY))
        aliases = {len(args): 0}
        args.append(out_prev)
    return pl.pallas_call(
        _combine_kernel,
        out_shape=jax.ShapeDtypeStruct((T, D), f32),
        grid=(nt,),
        in_specs=in_specs,
        out_specs=pl.BlockSpec((tm, D), lambda i: (off + i, 0)),
        input_output_aliases=aliases,
        compiler_params=_cparams("parallel"),
        name="combine_ln",
    )(*args)


def _sc_mesh():
    return plsc.VectorSubcoreMesh(core_axis_name="c", subcore_axis_name="s")


def _sc_dispatch(x, posk, n_slots):
    t_tokens, w = x.shape

    @functools.partial(pl.kernel, out_type=jax.ShapeDtypeStruct((n_slots, w), x.dtype), mesh=_sc_mesh(),
                       name="sc_dispatch")
    def k(x_hbm, p_hbm, o_hbm):
        def body(x_vmem, p_vmem):
            for kk in range(TOP_K):
                pltpu.sync_copy(x_vmem, o_hbm.at[p_vmem.at[kk]])

        pltpu.emit_pipeline(
            body,
            grid=(t_tokens // SC_ROWS,),
            in_specs=[pl.BlockSpec((SC_ROWS, w), lambda i: (i, 0), pipeline_mode=pl.Buffered(1)),
                      pl.BlockSpec((TOP_K, SC_ROWS), lambda i: (0, i))],
            out_specs=[],
            core_axis_name=("c", "s"),
            dimension_semantics=(pltpu.PARALLEL,),
        )(x_hbm, p_hbm)

    return k(x, posk)


def _sc_gather(table, idx):
    n = idx.shape[0]
    w = table.shape[1]

    @functools.partial(pl.kernel, out_type=jax.ShapeDtypeStruct((n, w), table.dtype), mesh=_sc_mesh(),
                       name="sc_gather")
    def k(t_hbm, i_hbm, o_hbm):
        def body(i_vmem, o_vmem):
            pltpu.sync_copy(t_hbm.at[i_vmem.at[0]], o_vmem)

        pltpu.emit_pipeline(
            body,
            grid=(n // SC_ROWS,),
            in_specs=[pl.BlockSpec((1, SC_ROWS), lambda i: (0, i))],
            out_specs=[pl.BlockSpec((SC_ROWS, w), lambda i: (i, 0), pipeline_mode=pl.Buffered(1))],
            core_axis_name=("c", "s"),
            dimension_semantics=(pltpu.PARALLEL,),
        )(i_hbm, o_hbm)

    return k(table, idx.reshape(1, n))


def _moe(x1, x1p, packed, cnt, layer, w1, b1, w2, b2, lg, lb):
    T, D = x1.shape
    A = T * TOP_K
    idx = packed[:, PK_IDX:PK_IDX + TOP_K].astype(jnp.int32)
    rank = packed[:, PK_RANK:PK_RANK + TOP_K].astype(jnp.int32)
    counts = cnt[0, :N_EXPERTS].astype(jnp.int32)
    padded = ((counts + MOE_BLOCK - 1) // MOE_BLOCK) * MOE_BLOCK
    pad_end = jnp.cumsum(padded)
    pad_start = pad_end - padded
    posk = (pad_start[idx] + rank).T
    n_blocks = -(-A // MOE_BLOCK) + N_EXPERTS
    P = n_blocks * MOE_BLOCK
    starts = jnp.arange(n_blocks, dtype=jnp.int32) * MOE_BLOCK
    block_e = jnp.minimum(jnp.sum((pad_end[None, :] <= starts[:, None]).astype(jnp.int32), axis=1), N_EXPERTS - 1)
    n_used = (pad_end[-1:] // MOE_BLOCK).astype(jnp.int32)
    n_valid = jnp.clip(pad_start[block_e] + counts[block_e] - starts, 0, MOE_BLOCK).astype(jnp.int32)
    xs = _sc_dispatch(x1p, posk, P)
    ys = _experts(xs, block_e, n_used, n_valid, layer, w1, b1, w2, b2)
    tp = T // COMBINE_PARTS
    out = None
    for part in range(COMBINE_PARTS):
        yg = _sc_gather(ys, posk[:, part * tp:(part + 1) * tp].reshape(TOP_K * tp))
        out = _combine(yg, packed, x1, lg, lb, part, out)
    return out


def _layer(x2d, B, S, p, layer, stacked):
    w = _split_w_in(p['w_in'])
    ha = _lru(x2d, w['lru'], B, S, p['lru_conv_w'], p['lru_conv_b'], p['lru_wa'], p['lru_ba'], p['lru_wx'],
              p['lru_bx'], p['lru_lambda'])
    hb = _mlstm(x2d, w['ml'], B, S, p['ml_i_bias'], p['ml_f_bias'], p['ml_norm_g'])
    hc = _gdn(x2d, w['gd'], B, S, p['gd_conv_w'], p['gd_a_log'], p['gd_dt_bias'], p['gd_norm_g'])
    x1, x1p, packed, cnt = _merge(ha, hb, hc, x2d, w['gate'], p['w_br_lru'], p['w_br_ml'], p['w_br_gd'],
                                  p['w_out'], p['ln1_g'], p['ln1_b'], p['router_w'], p['router_b'])
    return _moe(x1, x1p, packed, cnt, layer, stacked['exp_w1'], stacked['exp_b1'],
                stacked['exp_w2'], stacked['exp_b2'], p['ln2_g'], p['ln2_b'])


def kernel(x, w_in, lru_conv_w, lru_conv_b, lru_wa, lru_ba, lru_wx, lru_bx, lru_lambda, ml_i_bias, ml_f_bias, ml_norm_g, gd_conv_w, gd_a_log, gd_dt_bias, gd_norm_g, w_br_lru, w_br_ml, w_br_gd, w_out, ln1_g, ln1_b, router_w, router_b, exp_w1, exp_b1, exp_w2, exp_b2, ln2_g, ln2_b):
    B, S, D = x.shape
    params = dict(w_in=w_in, lru_conv_w=lru_conv_w, lru_conv_b=lru_conv_b, lru_wa=lru_wa, lru_ba=lru_ba,
                  lru_wx=lru_wx, lru_bx=lru_bx, lru_lambda=lru_lambda, ml_i_bias=ml_i_bias,
                  ml_f_bias=ml_f_bias, ml_norm_g=ml_norm_g, gd_conv_w=gd_conv_w, gd_a_log=gd_a_log,
                  gd_dt_bias=gd_dt_bias, gd_norm_g=gd_norm_g, w_br_lru=w_br_lru, w_br_ml=w_br_ml,
                  w_br_gd=w_br_gd, w_out=w_out, ln1_g=ln1_g, ln1_b=ln1_b, router_w=router_w,
                  router_b=router_b, ln2_g=ln2_g, ln2_b=ln2_b)
    stacked = dict(exp_w1=exp_w1, exp_b1=exp_b1, exp_w2=exp_w2, exp_b2=exp_b2)
    h = x.reshape(B * S, D)
    for l in range(w_in.shape[0]):
        h = _layer(h, B, S, {k: v[l] for k, v in params.items()}, l, stacked)
    return h.reshape(B, S, D)
```

```python
import functools

import jax
import jax.numpy as jnp
from jax import lax
from jax.experimental import pallas as pl
from jax.experimental.pallas import tpu as pltpu
from jax.experimental.pallas import tpu_sc as plsc

f32 = jnp.float32
bf16 = jnp.bfloat16

D_MODEL = 1024
LRU_WIDTH = 1024
LRU_BLOCKS = 8
LRU_C = 8.0
CONV_WIDTH = 4
ML_HEADS = 4
ML_DQK = 64
ML_DV = 128
GD_HEADS = 8
GD_DK = 64
GD_DV = 64
N_EXPERTS = 32
TOP_K = 4
D_FF = 1024
SWIGLU_LIMIT = 7.0
SWIGLU_ALPHA = 1.702
MOE_BLOCK = 512
N_BRANCH = 3
DEPTH = 2
DN_ALPHA = (2.0 * DEPTH) ** 0.25
LN_EPS = 1e-5
RMS_EPS = 1e-6

ML_QK_W = ML_HEADS * ML_DQK
ML_V_W = ML_HEADS * ML_DV
GD_K_W = GD_HEADS * GD_DK
GD_V_W = GD_HEADS * GD_DV
IN_SPLITS = (LRU_WIDTH, LRU_WIDTH, ML_QK_W, ML_QK_W, ML_V_W, ML_HEADS, ML_HEADS, ML_V_W,
             GD_K_W, GD_K_W, GD_V_W, GD_HEADS, GD_HEADS, GD_V_W, N_BRANCH * D_MODEL)

LANES = 128
SUBLANES = 8
NEG_BIG = -1e30

SM_ML_I = 0
SM_ML_F = 4
SM_GD_A = 8
SM_GD_B = 16

ML_CHUNK = 256
GD_CHUNK = 64
GD_ROWS = 256
GD_GROUP_SLABS = 2
GD_SOLVE_PASSES = 1
VMEM_LIMIT = 56 * 1024 * 1024
SC_ROWS = 128


def _cparams(*sem):
    return pltpu.CompilerParams(dimension_semantics=sem, vmem_limit_bytes=VMEM_LIMIT)


def _cummax_rows(x, seg):
    pos = lax.broadcasted_iota(jnp.int32, x.shape, 0) & (seg - 1)
    step = 1
    while step < seg:
        x = jnp.maximum(x, jnp.where(pos >= step, pltpu.roll(x, step, 0), NEG_BIG))
        step *= 2
    return x


def _cumsum_rows(x, seg):
    pos = lax.broadcasted_iota(jnp.int32, x.shape, 0) & (seg - 1)
    step = 1
    while step < seg:
        x = x + jnp.where(pos >= step, pltpu.roll(x, step, 0), 0.0)
        step *= 2
    return x


def _bdot(a, b):
    return jnp.dot(a.astype(bf16), b.astype(bf16), preferred_element_type=f32)


def _bdot_nt(a, b):
    return lax.dot_general(a.astype(bf16), b.astype(bf16), (((1,), (1,)), ((), ())),
                           preferred_element_type=f32)


def _bdot_tn(a, b):
    return lax.dot_general(a.astype(bf16), b.astype(bf16), (((0,), (0,)), ((), ())),
                           preferred_element_type=f32)


def _split_bf16(x, passes):
    hi = x.astype(bf16)
    if passes == 1:
        return (hi,)
    return (hi, (x - hi.astype(f32)).astype(bf16))


def _mdot(xs, ys):
    out = jnp.dot(xs[0], ys[0], preferred_element_type=f32)
    if len(xs) > 1:
        out = out + jnp.dot(xs[1], ys[0], preferred_element_type=f32)
        out = out + jnp.dot(xs[0], ys[1], preferred_element_type=f32)
    return out


def _sigmoid(x):
    return 1.0 / (1.0 + jnp.exp(-x))


def _softplus(x):
    return jnp.maximum(x, 0.0) + jnp.log1p(jnp.exp(-jnp.abs(x)))


def _log_sigmoid(x):
    return jnp.minimum(x, 0.0) - jnp.log1p(jnp.exp(-jnp.abs(x)))


def _pack_halves(v):
    n = v.shape[1] // 2
    hi = lax.bitcast_convert_type(v[:, :n].astype(bf16).astype(f32), jnp.uint32)
    lo = lax.bitcast_convert_type(v[:, n:].astype(bf16).astype(f32), jnp.uint32)
    return hi | (lo >> 16)


def _unpack_halves(p):
    hi = lax.bitcast_convert_type(p & jnp.uint32(0xFFFF0000), f32)
    lo = lax.bitcast_convert_type(p << 16, f32)
    return jnp.concatenate([hi, lo], axis=1)


def _layer_norm(z, g, b):
    mu = jnp.mean(z, axis=-1, keepdims=True)
    zc = z - mu
    var = jnp.mean(zc * zc, axis=-1, keepdims=True)
    return zc * lax.rsqrt(var + LN_EPS) * g + b


def _split_w_in(w):
    pts = []
    acc = 0
    for s in IN_SPLITS[:-1]:
        acc += s
        pts.append(acc)
    (lru_x, lru_y, ml_q, ml_k, ml_v, ml_i, ml_f, ml_o,
     gd_q, gd_k, gd_v, gd_a, gd_b, gd_z, gate) = jnp.split(w, pts, axis=1)
    pad = jnp.zeros((w.shape[0], LANES - 2 * ML_HEADS - 2 * GD_HEADS), w.dtype)
    small = jnp.concatenate([ml_i, ml_f, gd_a, gd_b, pad], axis=1)
    cat = lambda *cols: jnp.concatenate(cols, axis=1).astype(bf16)
    return dict(lru=cat(lru_x, lru_y), ml=cat(ml_q, ml_k, ml_v, ml_o, small),
                gd=cat(gd_q, gd_k, gd_v, gd_z, small), gate=gate.astype(bf16))


def _project(x_ref, w_ref):
    return jnp.dot(x_ref[...].astype(bf16), w_ref[...], preferred_element_type=f32)


def _lru_kernel(xin_ref, xnext_ref, w_ref, cw_ref, cb_ref, wcat_ref, ba_ref, bx_ref, lam_ref, o_ref,
                xext, pr_even, pr_odd, a_scr, u_scr, carry):
    ts = xin_ref.shape[0]
    W = a_scr.shape[1]
    bw = W // LRU_BLOCKS
    step = pl.program_id(0) * pl.num_programs(1) + pl.program_id(1)

    @pl.when(step == 0)
    def _():
        pr_even[...] = _project(xin_ref, w_ref)

    @pl.when(pl.program_id(1) == 0)
    def _():
        xext[0:SUBLANES, :] = jnp.zeros((SUBLANES, W), f32)
        carry[...] = jnp.zeros((1, W), f32)

    def tile(cur, nxt):
        x = cur[:, 0:W]
        xext[SUBLANES:SUBLANES + ts, :] = x
        xa = (cw_ref[3:4, :] * x
              + cw_ref[2:3, :] * xext[SUBLANES - 1:SUBLANES - 1 + ts, :]
              + cw_ref[1:2, :] * xext[SUBLANES - 2:SUBLANES - 2 + ts, :]
              + cw_ref[0:1, :] * xext[SUBLANES - 3:SUBLANES - 3 + ts, :]) + cb_ref[...]
        xext[0:SUBLANES, :] = xext[ts:ts + SUBLANES, :]

        xnb = xnext_ref[...].astype(bf16)
        pw = 2 * W // LRU_BLOCKS
        cdec = -LRU_C * _softplus(-lam_ref[...])
        for h in range(LRU_BLOCKS):
            sl = slice(bw * h, bw * (h + 1))
            psl = slice(pw * h, pw * (h + 1))
            nxt[:, psl] = jnp.dot(xnb, w_ref[:, psl], preferred_element_type=f32)
            xh = xa[:, sl]
            g = jnp.dot(xh.astype(bf16), wcat_ref[h], preferred_element_type=f32)
            r = _sigmoid(g[:, :bw] + ba_ref[:, sl])
            ig = _sigmoid(g[:, bw:] + bx_ref[:, sl])
            log_a = r * cdec[:, sl]
            a = jnp.exp(log_a)
            a_scr[:, sl] = a
            u_scr[:, sl] = jnp.sqrt(jnp.tanh(-log_a) * (1.0 + a * a)) * (ig * xh)

        row = lax.broadcasted_iota(jnp.int32, (SUBLANES, W), 0)

        def body(g, cr):
            off = pl.multiple_of(g * SUBLANES, SUBLANES)
            A = a_scr[pl.ds(off, SUBLANES), :]
            U = u_scr[pl.ds(off, SUBLANES), :]
            for s in (1, 2, 4):
                a_sh = pltpu.roll(A, s, 0)
                u_sh = pltpu.roll(U, s, 0)
                m = row >= s
                U = jnp.where(m, A * u_sh + U, U)
                A = jnp.where(m, A * a_sh, A)
            H = A * cr + U
            u_scr[pl.ds(off, SUBLANES), :] = H
            return H[SUBLANES - 1:SUBLANES, :]

        carry[...] = lax.fori_loop(0, ts // SUBLANES, body, carry[...], unroll=4)
        o_ref[...] = (u_scr[...] * jax.nn.gelu(cur[:, W:2 * W])).astype(o_ref.dtype)

    @pl.when(lax.rem(step, 2) == 0)
    def _():
        tile(pr_even, pr_odd)

    @pl.when(lax.rem(step, 2) == 1)
    def _():
        tile(pr_odd, pr_even)


def _lru(x2d, w_lru, B, S, cw, cb, wa, ba, wx, bx, lam):
    T = B * S
    ts = min(512, S)
    nt = S // ts
    W = LRU_WIDTH
    wcat = jnp.concatenate([wa, wx], axis=-1).astype(bf16)
    row = lambda v: v.reshape(1, W)
    full = lambda shp: pl.BlockSpec(shp, lambda b, c: (0,) * len(shp))
    last = B * nt - 1
    return pl.pallas_call(
        _lru_kernel,
        out_shape=jax.ShapeDtypeStruct((T, W), bf16),
        grid=(B, nt),
        in_specs=[pl.BlockSpec((ts, D_MODEL), lambda b, c: (b * nt + c, 0)),
                  pl.BlockSpec((ts, D_MODEL), lambda b, c: (jnp.minimum(b * nt + c + 1, last), 0)),
                  full(w_lru.shape), full((CONV_WIDTH, W)), full((1, W)), full(wcat.shape),
                  full((1, W)), full((1, W)), full((1, W))],
        out_specs=pl.BlockSpec((ts, W), lambda b, c: (b * nt + c, 0)),
        scratch_shapes=[pltpu.VMEM((ts + SUBLANES, W), f32), pltpu.VMEM((ts, 2 * W), f32),
                        pltpu.VMEM((ts, 2 * W), f32), pltpu.VMEM((ts, W), f32), pltpu.VMEM((ts, W), f32),
                        pltpu.VMEM((1, W), f32)],
        compiler_params=_cparams("arbitrary", "arbitrary"),
        name="rg_lru",
    )(x2d, x2d, w_lru, cw, row(cb), wcat, row(ba), row(bx), row(lam))


def _ml_kernel(xin_ref, xnext_ref, w_ref, bias_ref, g_ref, o_ref, c_scr, n_scr, m_scr, pr_even, pr_odd):
    step = pl.program_id(0) * pl.num_programs(1) + pl.program_id(1)

    @pl.when(step == 0)
    def _():
        pr_even[...] = _project(xin_ref, w_ref)

    @pl.when(pl.program_id(1) == 0)
    def _():
        c_scr[...] = jnp.zeros(c_scr.shape, f32)
        n_scr[...] = jnp.zeros(n_scr.shape, f32)
        m_scr[...] = jnp.zeros(m_scr.shape, f32)

    args = (xnext_ref, w_ref, bias_ref, g_ref, o_ref, c_scr, n_scr, m_scr)

    @pl.when(lax.rem(step, 2) == 0)
    def _():
        _ml_tile(pr_even, pr_odd, *args)

    @pl.when(lax.rem(step, 2) == 1)
    def _():
        _ml_tile(pr_odd, pr_even, *args)


def _ml_tile(cur, nxt, xnext_ref, w_ref, bias_ref, g_ref, o_ref, c_scr, n_scr, m_scr):
    L = cur.shape[0]
    xnb = xnext_ref[...].astype(bf16)
    pcols = [0, 2 * ML_QK_W, 2 * ML_QK_W + ML_V_W, 2 * ML_QK_W + 2 * ML_V_W, cur.shape[1]]

    def project_next(i):
        nxt[:, pcols[i]:pcols[i + 1]] = jnp.dot(xnb, w_ref[:, pcols[i]:pcols[i + 1]],
                                                preferred_element_type=f32)

    pr = cur[...]
    q = pr[:, 0:ML_QK_W]
    k = pr[:, ML_QK_W:2 * ML_QK_W]
    v = pr[:, 2 * ML_QK_W:2 * ML_QK_W + ML_V_W]
    og = pr[:, 2 * ML_QK_W + ML_V_W:2 * ML_QK_W + 2 * ML_V_W]
    sm = pr[:, 2 * ML_QK_W + 2 * ML_V_W:] + bias_ref[...]
    logf = _log_sigmoid(sm)
    ri = lax.broadcasted_iota(jnp.int32, (L, L), 0)
    ci = lax.broadcasted_iota(jnp.int32, (L, L), 1)
    tri = ri >= ci
    b_all = _cumsum_rows(logf, L)
    g_all = sm - pltpu.roll(b_all, LANES - (SM_ML_F - SM_ML_I), 1)
    cm_all = _cummax_rows(g_all, L)
    g_t = g_all.T
    kb = k.astype(bf16)
    lane_head = lax.broadcasted_iota(jnp.int32, q.shape, 1) // ML_DQK
    scale = ML_DQK ** -0.5
    heads = range(ML_HEADS)
    sls = [slice(ML_DV * h, ML_DV * (h + 1)) for h in heads]
    m_news, decays, w_cs, m_ts, dws, iws = [], [], [], [], [], []
    for h in heads:
        project_next(h)
        b_c = b_all[:, SM_ML_F + h:SM_ML_F + h + 1]
        g_c = g_all[:, SM_ML_I + h:SM_ML_I + h + 1]
        g_r = g_t[SM_ML_I + h:SM_ML_I + h + 1, :]
        b_last = b_c[L - 1:L, :]
        m_prev = m_scr[h:h + 1, 0:1]
        mm_c = jnp.maximum(m_prev, cm_all[:, SM_ML_I + h:SM_ML_I + h + 1])
        mm_last = mm_c[L - 1:L, :]
        m_news.append(b_last + mm_last)
        decays.append(jnp.exp(m_prev - mm_last))
        w_cs.append(jnp.exp(g_c - mm_last))
        m_ts.append(b_c + mm_c)
        dws.append(jnp.exp(jnp.where(tri, g_r - mm_c, NEG_BIG)))
        iws.append(jnp.exp(m_prev - mm_c))
    qms = [jnp.where(lane_head == h, q, 0.0) * scale for h in heads]
    qmbs = [qm.astype(bf16) for qm in qms]
    ss = [_bdot_nt(qmbs[h], kb) * dws[h] for h in heads]
    n_rows = [n_scr[h:h + 1, :] for h in heads]
    nums = [_bdot(ss[h], v[:, sls[h]]) + iws[h] * _bdot(qmbs[h], c_scr[:, sls[h]]) for h in heads]
    dens = [jnp.sum(ss[h], axis=1, keepdims=True) + iws[h] * jnp.sum(qms[h] * n_rows[h], axis=1, keepdims=True)
            for h in heads]
    hhs = [nums[h] / jnp.maximum(jnp.abs(dens[h]), jnp.exp(-m_ts[h])) for h in heads]
    for h in heads:
        c_scr[:, sls[h]] = decays[h] * c_scr[:, sls[h]] + _bdot_tn(kb, w_cs[h] * v[:, sls[h]])
        n_scr[h:h + 1, :] = decays[h] * n_rows[h] + jnp.sum(w_cs[h] * k, axis=0, keepdims=True)
        m_scr[h:h + 1, :] = jnp.broadcast_to(m_news[h], (1, LANES))
    for h in heads:
        ms = jnp.mean(hhs[h] * hhs[h], axis=1, keepdims=True)
        hn = hhs[h] * lax.rsqrt(ms + RMS_EPS) * g_ref[:, sls[h]]
        o_ref[:, sls[h]] = (hn * _sigmoid(og[:, sls[h]])).astype(o_ref.dtype)


def _mlstm(x2d, w_ml, B, S, i_bias, f_bias, norm_g):
    T = B * S
    L = min(ML_CHUNK, S)
    nc = S // L
    bias = jnp.zeros((1, LANES), f32)
    bias = bias.at[0, SM_ML_I:SM_ML_I + ML_HEADS].set(i_bias).at[0, SM_ML_F:SM_ML_F + ML_HEADS].set(f_bias)
    full = lambda shp: pl.BlockSpec(shp, lambda b, c: (0,) * len(shp))
    return pl.pallas_call(
        _ml_kernel,
        out_shape=jax.ShapeDtypeStruct((T, ML_V_W), bf16),
        grid=(B, nc),
        in_specs=[pl.BlockSpec((L, D_MODEL), lambda b, c: (b * nc + c, 0)),
                  pl.BlockSpec((L, D_MODEL), lambda b, c: (jnp.minimum(b * nc + c + 1, B * nc - 1), 0)),
                  full(w_ml.shape), full((1, LANES)), full((1, ML_V_W))],
        out_specs=pl.BlockSpec((L, ML_V_W), lambda b, c: (b * nc + c, 0)),
        scratch_shapes=[pltpu.VMEM((ML_QK_W, ML_V_W), f32), pltpu.VMEM((SUBLANES, ML_QK_W), f32),
                        pltpu.VMEM((SUBLANES, LANES), f32),
                        pltpu.VMEM((L, w_ml.shape[1]), f32), pltpu.VMEM((L, w_ml.shape[1]), f32)],
        compiler_params=_cparams("arbitrary", "arbitrary"),
        name="mlstm",
    )(x2d, x2d, w_ml, bias, norm_g.reshape(1, ML_V_W))


def _gd_kernel(xin_ref, w_ref, cw_ref, alog_ref, dtb_ref, g_ref, o_ref, ext, s_scr):
    R = xin_ref.shape[0]
    KW = GD_K_W
    L = GD_CHUNK
    nch = R // L
    lsh = L.bit_length() - 1
    PW = nch * L
    nsl = KW // LANES
    passes = GD_SOLVE_PASSES

    @pl.when(pl.program_id(1) == 0)
    def _():
        ext[0:SUBLANES, :] = jnp.zeros((SUBLANES, ext.shape[1]), f32)
        s_scr[...] = jnp.zeros(s_scr.shape, f32)

    pr = _project(xin_ref, w_ref)
    z = pr[:, 3 * KW:4 * KW]
    sm = pr[:, 4 * KW:]
    ext[SUBLANES:SUBLANES + R, :] = pr[:, 0:3 * KW]
    xc = (cw_ref[3:4, :] * ext[SUBLANES:SUBLANES + R, :]
          + cw_ref[2:3, :] * ext[SUBLANES - 1:SUBLANES - 1 + R, :]
          + cw_ref[1:2, :] * ext[SUBLANES - 2:SUBLANES - 2 + R, :]
          + cw_ref[0:1, :] * ext[SUBLANES - 3:SUBLANES - 3 + R, :])
    ext[0:SUBLANES, :] = ext[R:R + SUBLANES, :]
    xc = xc * _sigmoid(xc)

    g_t = -jnp.exp(alog_ref[...]) * _softplus(sm + dtb_ref[...])
    beta_t = _sigmoid(sm)
    gc_all = _cumsum_rows(g_t, L)
    gc_t = gc_all.T

    pt = lax.broadcasted_iota(jnp.int32, (L, PW), 0)
    plane = lax.broadcasted_iota(jnp.int32, (L, PW), 1)
    ps = plane & (L - 1)
    pc = plane >> lsh
    tri_p = pt >= ps
    strict_p = pt > ps
    eye_p = jnp.where(pt == ps, 1.0, 0.0)
    blk8_p = (pt >> 3) == (ps >> 3)
    cmask_b = [jnp.where(pc == c, 1.0, 0.0).astype(bf16) for c in range(nch)]

    def pick(parts):
        out = parts[nch - 1]
        for c in range(nch - 2, -1, -1):
            out = jnp.where(pc == c, parts[c], out)
        return out

    def chunks(col):
        return [col[c * L:(c + 1) * L] for c in range(nch)]

    def bd(yb):
        return jnp.concatenate([yb * cmask_b[c] for c in range(nch)], axis=0)

    def pdot(xp, yp):
        return _mdot(_split_bf16(xp, passes), [bd(p) for p in _split_bf16(yp, passes)])

    lane_r = lax.broadcasted_iota(jnp.int32, (R, LANES), 1)
    lo_r = lane_r < GD_DK
    lane_l = lax.broadcasted_iota(jnp.int32, (L, LANES), 1)
    lo_l = lane_l < GD_DK
    lane_row = lax.broadcasted_iota(jnp.int32, (1, LANES), 1)
    r2 = lax.broadcasted_iota(jnp.int32, (LANES, LANES), 0)
    c2 = lax.broadcasted_iota(jnp.int32, (LANES, LANES), 1)
    blockdiag = (r2 < GD_DK) == (c2 < GD_DK)

    def half_sums(y, lo):
        s_lo = jnp.sum(jnp.where(lo, y, 0.0), axis=1, keepdims=True)
        s_hi = jnp.sum(jnp.where(lo, 0.0, y), axis=1, keepdims=True)
        return s_lo, s_hi

    def l2n(y):
        s_lo, s_hi = half_sums(y * y, lo_r)
        return y * jnp.where(lo_r, lax.rsqrt(s_lo + RMS_EPS), lax.rsqrt(s_hi + RMS_EPS))

    def prepare(slabs):
        qns, kns, vss, a_ps, p_bs, gcols, bcols = [], [], [], [], [], [], []
        for j in slabs:
            qn = l2n(xc[:, LANES * j:LANES * (j + 1)]) * (GD_DK ** -0.5)
            kn = l2n(xc[:, KW + LANES * j:KW + LANES * (j + 1)])
            qns.append(qn)
            kns.append(kn)
            vss.append(xc[:, 2 * KW + LANES * j:2 * KW + LANES * (j + 1)])
            lhs = jnp.concatenate([jnp.where(lo_r, kn, 0.0), jnp.where(lo_r, 0.0, kn),
                                   jnp.where(lo_r, qn, 0.0), jnp.where(lo_r, 0.0, qn)], axis=0)
            gram = _bdot_nt(lhs, kn)
            for e in range(2):
                h = 2 * j + e
                gcol = gc_all[:, SM_GD_A + h:SM_GD_A + h + 1]
                bcol = beta_t[:, SM_GD_B + h:SM_GD_B + h + 1]
                gc_r = gc_t[SM_GD_A + h:SM_GD_A + h + 1, :]
                gam_p = jnp.exp(jnp.where(tri_p, pick(chunks(gcol)) - gc_r, NEG_BIG))
                kk_p = pick(chunks(gram[e * R:(e + 1) * R]))
                qk_p = pick(chunks(gram[(2 + e) * R:(3 + e) * R]))
                a_ps.append(jnp.where(strict_p, pick(chunks(bcol)) * gam_p * kk_p, 0.0))
                p_bs.append((gam_p * qk_p).astype(bf16))
                gcols.append(gcol)
                bcols.append(bcol)

        ads = [jnp.where(blk8_p, a, 0.0) for a in a_ps]
        a2s = [pdot(ad, ad) for ad in ads]
        a4s = [pdot(a2, a2) for a2 in a2s]
        xs = [pdot(eye_p - ad, eye_p + a2) for ad, a2 in zip(ads, a2s)]
        xs = [pdot(x, eye_p + a4) for x, a4 in zip(xs, a4s)]
        for sh in range(3, lsh):
            msk = ((pt >> (sh + 1)) == (ps >> (sh + 1))) & (((pt >> sh) & 1) == 1) & (((ps >> sh) & 1) == 0)
            ts = [pdot(jnp.where(msk, a, 0.0), x) for a, x in zip(a_ps, xs)]
            xs = [x - pdot(x, t) for x, t in zip(xs, ts)]

        us, ws, qgs, kds, gls = [], [], [], [], []
        for jj in range(len(slabs)):
            uw, egs, eds, glh = [], [], [], []
            for e in range(2):
                h = 2 * jj + e
                gcol, bcol = gcols[h], bcols[h]
                eg = jnp.exp(gcol)
                rhs = jnp.concatenate([bcol * vss[jj], (bcol * eg) * kns[jj]], axis=1)
                uw.append(_mdot([bd(p) for p in _split_bf16(xs[h], passes)], _split_bf16(rhs, passes)))
                lasts = [gcol[c * L + L - 1:c * L + L] for c in range(nch)]
                gcl = jnp.concatenate([jnp.broadcast_to(v, (L, 1)) for v in lasts], axis=0)
                egs.append(eg)
                eds.append(jnp.exp(gcl - gcol))
                glh.append([jnp.exp(v) for v in lasts])
            us.append(jnp.where(lo_r, uw[0][:, :LANES], uw[1][:, :LANES]))
            ws.append(jnp.where(lo_r, uw[0][:, LANES:], uw[1][:, LANES:]))
            qgs.append(qns[jj] * jnp.where(lo_r, egs[0], egs[1]))
            kds.append(kns[jj] * jnp.where(lo_r, eds[0], eds[1]))
            gls.append([jnp.where(lane_row < GD_DK, glh[0][c], glh[1][c]) for c in range(nch)])
        return us, ws, qgs, kds, gls, p_bs

    us, ws, qgs, kds, gls, p_bs = [], [], [], [], [], []
    for g in range(0, nsl, GD_GROUP_SLABS):
        group = prepare(list(range(g, min(g + GD_GROUP_SLABS, nsl))))
        for acc, part in zip((us, ws, qgs, kds, gls, p_bs), group):
            acc.extend(part)


    stacked = [jnp.concatenate([p_bs[2 * j], p_bs[2 * j + 1], kds[j].T.astype(bf16)], axis=0)
               for j in range(nsl)]
    states = [s_scr[j] for j in range(nsl)]
    outs = [[] for _ in range(nsl)]
    for c in range(nch):
        rs = slice(c * L, (c + 1) * L)
        for j in range(nsl):
            s_prev = states[j]
            ws_qs = _bdot(jnp.concatenate([ws[j][rs], qgs[j][rs]], axis=0), s_prev)
            db = (us[j][rs] - ws_qs[0:L]).astype(bf16)
            pd_upd = jnp.dot(stacked[j][:, rs], db, preferred_element_type=f32)
            outs[j].append(ws_qs[L:2 * L] + jnp.where(lo_l, pd_upd[0:L], pd_upd[L:2 * L]))
            states[j] = gls[j][c] * s_prev + jnp.where(blockdiag, pd_upd[2 * L:], 0.0)

    for j in range(nsl):
        sl = slice(LANES * j, LANES * (j + 1))
        s_scr[j] = states[j]
        o = jnp.concatenate(outs[j], axis=0) if nch > 1 else outs[j][0]
        m_lo, m_hi = half_sums(o * o, lo_r)
        inv = jnp.where(lo_r, lax.rsqrt(m_lo * (1.0 / GD_DV) + RMS_EPS), lax.rsqrt(m_hi * (1.0 / GD_DV) + RMS_EPS))
        zs = z[:, sl]
        o_ref[:, sl] = (o * inv * g_ref[:, sl] * (zs * _sigmoid(zs))).astype(o_ref.dtype)


def _gdn(x2d, w_gd, B, S, conv_w, a_log, dt_bias, norm_g):
    T = B * S
    L = min(GD_ROWS, S)
    nc = S // L
    KW = GD_K_W
    alog = jnp.zeros((1, LANES), f32).at[0, SM_GD_A:SM_GD_A + GD_HEADS].set(a_log)
    dtb = jnp.zeros((1, LANES), f32).at[0, SM_GD_A:SM_GD_A + GD_HEADS].set(dt_bias)
    g_row = jnp.tile(norm_g, GD_HEADS).reshape(1, GD_V_W)
    full = lambda shp: pl.BlockSpec(shp, lambda b, c: (0,) * len(shp))
    return pl.pallas_call(
        _gd_kernel,
        out_shape=jax.ShapeDtypeStruct((T, GD_V_W), bf16),
        grid=(B, nc),
        in_specs=[pl.BlockSpec((L, D_MODEL), lambda b, c: (b * nc + c, 0)), full(w_gd.shape),
                  full((CONV_WIDTH, 3 * KW)), full((1, LANES)), full((1, LANES)), full((1, GD_V_W))],
        out_specs=pl.BlockSpec((L, GD_V_W), lambda b, c: (b * nc + c, 0)),
        scratch_shapes=[pltpu.VMEM((L + SUBLANES, 3 * KW), f32),
                        pltpu.VMEM((KW // LANES, LANES, LANES), f32)],
        compiler_params=_cparams("parallel", "arbitrary"),
        name="gated_deltanet",
    )(x2d, w_gd, conv_w, alog, dtb, g_row)


def _merge_kernel(ha_ref, hb_ref, hc_ref, x_ref, wg_ref, wa_ref, wb_ref, wc_ref, wo_ref, lg_ref, lb_ref,
                  rw_ref, rb_ref, x1_ref, x1p_ref, pk_ref, cnt_ref, prev, carry):
    D = x_ref.shape[1]
    i = pl.program_id(0)

    @pl.when(i == 0)
    def _():
        prev[...] = jnp.zeros(prev.shape, f32)
        carry[...] = jnp.zeros(carry.shape, f32)

    route = _route_stages(prev[...], jnp.where(i > 0, 1.0, 0.0), rw_ref, rb_ref, pk_ref, cnt_ref, carry)
    xb = x_ref[...].astype(bf16)

    next(route)
    g0 = jnp.dot(xb, wg_ref[:, 0:D], preferred_element_type=f32)
    next(route)
    g1 = jnp.dot(xb, wg_ref[:, D:2 * D], preferred_element_type=f32)
    next(route)
    g2 = jnp.dot(xb, wg_ref[:, 2 * D:3 * D], preferred_element_type=f32)
    next(route)
    bra = jnp.dot(ha_ref[...], wa_ref[...], preferred_element_type=f32)
    next(route)
    brb = jnp.dot(hb_ref[...], wb_ref[...], preferred_element_type=f32)
    brc = jnp.dot(hc_ref[...], wc_ref[...], preferred_element_type=f32)
    next(route)
    merged = _sigmoid(g0) * bra + _sigmoid(g1) * brb + _sigmoid(g2) * brc
    y = jnp.dot(merged.astype(bf16), wo_ref[...], preferred_element_type=f32)
    for _ in route:
        pass
    out = _layer_norm(DN_ALPHA * x_ref[...] + y, lg_ref[...], lb_ref[...])
    x1_ref[...] = out
    x1p_ref[...] = _pack_halves(out)
    prev[...] = out


def _merge(ha, hb, hc, x2d, w_gate, wa, wb, wc, wo, lg, lb, router_w, router_b):
    T = x2d.shape[0]
    D = D_MODEL
    tm = min(512, T)
    nt = T // tm
    rw, rb = _router_params(router_w, router_b)
    full = lambda shp: pl.BlockSpec(shp, lambda i: (0,) * len(shp))
    held = lambda shp: pl.BlockSpec(shp, lambda i: (0,) * len(shp), pipeline_mode=pl.Buffered(1))
    rows = lambda w: pl.BlockSpec((tm, w), lambda i: (jnp.minimum(i, nt - 1), 0))
    routed = pl.BlockSpec((tm, LANES), lambda i: (jnp.maximum(i - 1, 0), 0))
    return pl.pallas_call(
        _merge_kernel,
        out_shape=(jax.ShapeDtypeStruct((T, D), f32), jax.ShapeDtypeStruct((T, D // 2), jnp.uint32),
                   jax.ShapeDtypeStruct((T, LANES), f32), jax.ShapeDtypeStruct((1, LANES), f32)),
        grid=(nt + 1,),
        in_specs=[rows(LRU_WIDTH), rows(ML_V_W), rows(GD_V_W), rows(D), held(w_gate.shape),
                  held((LRU_WIDTH, D)), held((ML_V_W, D)), held((GD_V_W, D)), held((D, D)),
                  full((1, D)), full((1, D)), held((D, LANES)), full((1, LANES))],
        out_specs=(rows(D), rows(D // 2), routed, full((1, LANES))),
        scratch_shapes=[pltpu.VMEM((tm, D), f32), pltpu.VMEM((1, LANES), f32)],
        compiler_params=_cparams("arbitrary"),
        name="merge_outproj_ln_route",
    )(ha, hb, hc, x2d, w_gate, wa.astype(bf16), wb.astype(bf16), wc.astype(bf16),
      wo.astype(bf16), lg.reshape(1, D), lb.reshape(1, D), rw, rb)


PK_IDX = 0
PK_GATE = 4
PK_RANK = 8


def _route_stages(x, weight, rw_ref, rb_ref, pk_ref, cnt_ref, carry):
    tr = x.shape[0]
    x_hi, x_lo = _split_bf16(x, 3)
    w_hi, w_lo = _split_bf16(rw_ref[...], 3)
    prod = jnp.dot(jnp.concatenate([x_hi, x_lo], axis=0), jnp.concatenate([w_hi, w_lo], axis=1),
                   preferred_element_type=f32)
    logits = prod[0:tr, 0:LANES] + (prod[tr:, 0:LANES] + prod[0:tr, LANES:]) + rb_ref[...]
    lane = lax.broadcasted_iota(jnp.int32, (tr, LANES), 1).astype(f32)
    vals = logits
    idxs, tops = [], []
    yield
    for _ in range(TOP_K):
        m = jnp.max(vals, axis=1, keepdims=True)
        idx = jnp.min(jnp.where(vals == m, lane, float(LANES)), axis=1, keepdims=True)
        idxs.append(idx)
        tops.append(m)
        vals = jnp.where(lane == idx, NEG_BIG * 2.0, vals)
        yield
    es = [jnp.exp(t - tops[0]) for t in tops]
    tot = es[0] + es[1] + es[2] + es[3]
    onehots = [lane == idx for idx in idxs]
    sel = jnp.zeros((tr, LANES), f32)
    for oh in onehots:
        sel = sel + oh.astype(f32)
    ri = lax.broadcasted_iota(jnp.int32, (tr, tr), 0)
    ci = lax.broadcasted_iota(jnp.int32, (tr, tr), 1)
    before = jnp.dot((ri > ci).astype(bf16), sel.astype(bf16), preferred_element_type=f32) + carry[...]
    carry[...] = carry[...] + weight * jnp.sum(sel, axis=0, keepdims=True)
    cnt_ref[...] = carry[...]
    yield
    packed = jnp.zeros((tr, LANES), f32)
    for kk in range(TOP_K):
        rank = jnp.sum(jnp.where(onehots[kk], before, 0.0), axis=1, keepdims=True)
        packed = jnp.where(lane == float(PK_IDX + kk), idxs[kk], packed)
        packed = jnp.where(lane == float(PK_GATE + kk), es[kk] / tot, packed)
        packed = jnp.where(lane == float(PK_RANK + kk), rank, packed)
    pk_ref[...] = packed


def _router_params(router_w, router_b):
    rw = jnp.zeros((D_MODEL, LANES), f32).at[:, :N_EXPERTS].set(router_w)
    rb = jnp.full((1, LANES), NEG_BIG, f32).at[0, :N_EXPERTS].set(router_b)
    return rw, rb


def _expert_kernel(be_ref, nu_ref, nv_ref, x_ref, w1_ref, b1_ref, w2_ref, b2_ref, o_ref, w1b, w2b):
    i = pl.program_id(0)
    used = i < nu_ref[0]
    new_expert = jnp.logical_or(i == 0, be_ref[i] != be_ref[jnp.maximum(i - 1, 0)])

    @pl.when(jnp.logical_and(used, new_expert))
    def _():
        w1b[...] = w1_ref[0, 0].astype(bf16)
        w2b[...] = w2_ref[0, 0].astype(bf16)

    @pl.when(used)
    def _():
        rows = lax.broadcasted_iota(jnp.int32, x_ref.shape, 0)
        xp = jnp.where(rows < nv_ref[i], x_ref[...], jnp.uint32(0))
        x = _unpack_halves(xp).astype(bf16)
        hdn = jnp.dot(x, w1b[...], preferred_element_type=f32) + b1_ref[0, 0]
        glu = jnp.minimum(hdn[:, :D_FF], SWIGLU_LIMIT)
        lin = jnp.clip(hdn[:, D_FF:], -SWIGLU_LIMIT, SWIGLU_LIMIT)
        act = glu * _sigmoid(SWIGLU_ALPHA * glu) * (lin + 1.0)
        y = jnp.dot(act.astype(bf16), w2b[...], preferred_element_type=f32) + b2_ref[0, 0]
        o_ref[...] = _pack_halves(y)

    @pl.when(i >= nu_ref[0])
    def _():
        o_ref[...] = jnp.zeros(o_ref.shape, o_ref.dtype)


def _experts(xs, block_e, n_used, n_valid, layer, w1, b1, w2, b2):
    P = xs.shape[0]
    nb = P // MOE_BLOCK
    D = D_MODEL
    nl = w1.shape[0]
    grid_spec = pltpu.PrefetchScalarGridSpec(
        num_scalar_prefetch=3,
        grid=(nb,),
        in_specs=[pl.BlockSpec((MOE_BLOCK, D // 2), lambda i, be, nu, nv: (i, 0)),
                  pl.BlockSpec((1, 1, D, 2 * D_FF), lambda i, be, nu, nv: (layer, be[i], 0, 0)),
                  pl.BlockSpec((1, 1, 1, 2 * D_FF), lambda i, be, nu, nv: (layer, be[i], 0, 0)),
                  pl.BlockSpec((1, 1, D_FF, D), lambda i, be, nu, nv: (layer, be[i], 0, 0)),
                  pl.BlockSpec((1, 1, 1, D), lambda i, be, nu, nv: (layer, be[i], 0, 0))],
        out_specs=pl.BlockSpec((MOE_BLOCK, D // 2), lambda i, be, nu, nv: (i, 0)),
        scratch_shapes=[pltpu.VMEM((D, 2 * D_FF), bf16), pltpu.VMEM((D_FF, D), bf16)],
    )
    return pl.pallas_call(
        _expert_kernel,
        out_shape=jax.ShapeDtypeStruct((P, D // 2), jnp.uint32),
        grid_spec=grid_spec,
        compiler_params=_cparams("arbitrary"),
        name="experts",
    )(block_e, n_used, n_valid, xs, w1, b1.reshape(nl, N_EXPERTS, 1, 2 * D_FF), w2,
      b2.reshape(nl, N_EXPERTS, 1, D))


def _combine_kernel(y0_ref, y1_ref, y2_ref, y3_ref, pk_ref, x_ref, lg_ref, lb_ref, o_ref):
    pk = pk_ref[...]
    y = pk[:, PK_GATE:PK_GATE + 1] * _unpack_halves(y0_ref[...])
    for kk, y_ref in ((1, y1_ref), (2, y2_ref), (3, y3_ref)):
        y = y + pk[:, PK_GATE + kk:PK_GATE + kk + 1] * _unpack_halves(y_ref[...])
    o_ref[...] = _layer_norm(DN_ALPHA * x_ref[...] + y, lg_ref[...], lb_ref[...])


def _combine(yg, packed, x1, lg, lb):
    T, D = x1.shape
    tm = min(1024, T)
    nt = T // tm
    full = lambda shp: pl.BlockSpec(shp, lambda i: (0,) * len(shp))
    choice = lambda kk: pl.BlockSpec((tm, D // 2), lambda i: (kk * nt + i, 0))
    return pl.pallas_call(
        _combine_kernel,
        out_shape=jax.ShapeDtypeStruct((T, D), f32),
        grid=(nt,),
        in_specs=[choice(0), choice(1), choice(2), choice(3), pl.BlockSpec((tm, LANES), lambda i: (i, 0)),
                  pl.BlockSpec((tm, D), lambda i: (i, 0)), full((1, D)), full((1, D))],
        out_specs=pl.BlockSpec((tm, D), lambda i: (i, 0)),
        compiler_params=_cparams("parallel"),
        name="combine_ln",
    )(yg, yg, yg, yg, packed, x1, lg.reshape(1, D), lb.reshape(1, D))


def _sc_mesh():
    return plsc.VectorSubcoreMesh(core_axis_name="c", subcore_axis_name="s")


def _sc_dispatch(x, posk, n_slots):
    t_tokens, w = x.shape

    @functools.partial(pl.kernel, out_type=jax.ShapeDtypeStruct((n_slots, w), x.dtype), mesh=_sc_mesh(),
                       name="sc_dispatch")
    def k(x_hbm, p_hbm, o_hbm):
        def body(x_vmem, p_vmem):
            for kk in range(TOP_K):
                pltpu.sync_copy(x_vmem, o_hbm.at[p_vmem.at[kk]])

        pltpu.emit_pipeline(
            body,
            grid=(t_tokens // SC_ROWS,),
            in_specs=[pl.BlockSpec((SC_ROWS, w), lambda i: (i, 0), pipeline_mode=pl.Buffered(1)),
                      pl.BlockSpec((TOP_K, SC_ROWS), lambda i: (0, i))],
            out_specs=[],
            core_axis_name=("c", "s"),
            dimension_semantics=(pltpu.PARALLEL,),
        )(x_hbm, p_hbm)

    return k(x, posk)


def _sc_gather(table, idx):
    n = idx.shape[0]
    w = table.shape[1]

    @functools.partial(pl.kernel, out_type=jax.ShapeDtypeStruct((n, w), table.dtype), mesh=_sc_mesh(),
                       name="sc_gather")
    def k(t_hbm, i_hbm, o_hbm):
        def body(i_vmem, o_vmem):
            pltpu.sync_copy(t_hbm.at[i_vmem.at[0]], o_vmem)

        pltpu.emit_pipeline(
            body,
            grid=(n // SC_ROWS,),
            in_specs=[pl.BlockSpec((1, SC_ROWS), lambda i: (0, i))],
            out_specs=[pl.BlockSpec((SC_ROWS, w), lambda i: (i, 0), pipeline_mode=pl.Buffered(1))],
            core_axis_name=("c", "s"),
            dimension_semantics=(pltpu.PARALLEL,),
        )(i_hbm, o_hbm)

    return k(table, idx.reshape(1, n))


def _moe(x1, x1p, packed, cnt, layer, w1, b1, w2, b2, lg, lb):
    T, D = x1.shape
    A = T * TOP_K
    idx = packed[:, PK_IDX:PK_IDX + TOP_K].astype(jnp.int32)
    rank = packed[:, PK_RANK:PK_RANK + TOP_K].astype(jnp.int32)
    counts = cnt[0, :N_EXPERTS].astype(jnp.int32)
    padded = ((counts + MOE_BLOCK - 1) // MOE_BLOCK) * MOE_BLOCK
    pad_end = jnp.cumsum(padded)
    pad_start = pad_end - padded
    posk = (pad_start[idx] + rank).T
    n_blocks = -(-A // MOE_BLOCK) + N_EXPERTS
    P = n_blocks * MOE_BLOCK
    starts = jnp.arange(n_blocks, dtype=jnp.int32) * MOE_BLOCK
    block_e = jnp.minimum(jnp.sum((pad_end[None, :] <= starts[:, None]).astype(jnp.int32), axis=1), N_EXPERTS - 1)
    n_used = (pad_end[-1:] // MOE_BLOCK).astype(jnp.int32)
    n_valid = jnp.clip(pad_start[block_e] + counts[block_e] - starts, 0, MOE_BLOCK).astype(jnp.int32)
    xs = _sc_dispatch(x1p, posk, P)
    ys = _experts(xs, block_e, n_used, n_valid, layer, w1, b1, w2, b2)
    yg = _sc_gather(ys, posk.reshape(A))
    return _combine(yg, packed, x1, lg, lb)


def _layer(x2d, B, S, p, layer, stacked):
    w = _split_w_in(p['w_in'])
    ha = _lru(x2d, w['lru'], B, S, p['lru_conv_w'], p['lru_conv_b'], p['lru_wa'], p['lru_ba'], p['lru_wx'],
              p['lru_bx'], p['lru_lambda'])
    hb = _mlstm(x2d, w['ml'], B, S, p['ml_i_bias'], p['ml_f_bias'], p['ml_norm_g'])
    hc = _gdn(x2d, w['gd'], B, S, p['gd_conv_w'], p['gd_a_log'], p['gd_dt_bias'], p['gd_norm_g'])
    x1, x1p, packed, cnt = _merge(ha, hb, hc, x2d, w['gate'], p['w_br_lru'], p['w_br_ml'], p['w_br_gd'],
                                  p['w_out'], p['ln1_g'], p['ln1_b'], p['router_w'], p['router_b'])
    return _moe(x1, x1p, packed, cnt, layer, stacked['exp_w1'], stacked['exp_b1'],
                stacked['exp_w2'], stacked['exp_b2'], p['ln2_g'], p['ln2_b'])


def kernel(x, w_in, lru_conv_w, lru_conv_b, lru_wa, lru_ba, lru_wx, lru_bx, lru_lambda, ml_i_bias, ml_f_bias, ml_norm_g, gd_conv_w, gd_a_log, gd_dt_bias, gd_norm_g, w_br_lru, w_br_ml, w_br_gd, w_out, ln1_g, ln1_b, router_w, router_b, exp_w1, exp_b1, exp_w2, exp_b2, ln2_g, ln2_b):
    B, S, D = x.shape
    params = dict(w_in=w_in, lru_conv_w=lru_conv_w, lru_conv_b=lru_conv_b, lru_wa=lru_wa, lru_ba=lru_ba,
                  lru_wx=lru_wx, lru_bx=lru_bx, lru_lambda=lru_lambda, ml_i_bias=ml_i_bias,
                  ml_f_bias=ml_f_bias, ml_norm_g=ml_norm_g, gd_conv_w=gd_conv_w, gd_a_log=gd_a_log,
                  gd_dt_bias=gd_dt_bias, gd_norm_g=gd_norm_g, w_br_lru=w_br_lru, w_br_ml=w_br_ml,
                  w_br_gd=w_br_gd, w_out=w_out, ln1_g=ln1_g, ln1_b=ln1_b, router_w=router_w,
                  router_b=router_b, ln2_g=ln2_g, ln2_b=ln2_b)
    stacked = dict(exp_w1=exp_w1, exp_b1=exp_b1, exp_w2=exp_w2, exp_b2=exp_b2)
    h = x.reshape(B * S, D)
    for l in range(w_in.shape[0]):
        h = _layer(h, B, S, {k: v[l] for k, v in params.items()}, l, stacked)
    return h.reshape(B, S, D)
```
